```python
import math
import jax, jax.numpy as jnp
from jax import lax
import numpy as np

D_MODEL = 1024
BATCH = 8
SEQ = 2048
DEPTH = 2

HEAD_DIM = 64
Q_BLOCK = 128
A_HEADS = 8
A_KV_HEADS = 2
A_WINDOW = 128
B_HEADS = 8
C_HEADS = 8
C_BLOCK = 256
C_TOPK = 3
MOBA_Q_CHUNK = 16
D_HEADS = 8
D_KV_HEADS = 2
NSA_CMP_LEN = 32
NSA_CMP_STRIDE = 16
NSA_SLC_BLOCK = 64
NSA_TOPN = 16
NSA_WINDOW = 512
NSA_Q_CHUNK = 64
FORCE_BONUS = 1000.0
N_GROUPS = 4
EXPERTS_PER_GROUP = 8
N_EXPERTS = N_GROUPS * EXPERTS_PER_GROUP
EXPERT_FF = 256
TOP_K_IN_GROUP = 2

EPS = 1e-6
NEG = -1e30
MIX_WIDTH = (A_HEADS + B_HEADS) * HEAD_DIM
EVEN_WIDTHS = (A_HEADS * HEAD_DIM, A_KV_HEADS * HEAD_DIM, A_KV_HEADS * HEAD_DIM,
               B_HEADS * HEAD_DIM, B_HEADS * HEAD_DIM, B_HEADS * HEAD_DIM)
ODD_WIDTHS = (C_HEADS * HEAD_DIM, C_HEADS * HEAD_DIM, C_HEADS * HEAD_DIM,
              D_HEADS * HEAD_DIM) + (D_KV_HEADS * HEAD_DIM,) * 6 + (D_HEADS * 3,)
EVEN_IN = sum(EVEN_WIDTHS)
ODD_IN = sum(ODD_WIDTHS)

kernel_name = 'hybrid_swa_sink_stickbreak_moba_nsa_hmoe'


def rms_norm(x, g):
    xf = x.astype(jnp.float32)
    y = xf * lax.rsqrt(jnp.mean(xf * xf, axis=-1, keepdims=True) + EPS)
    return (y * g.astype(jnp.float32)).astype(x.dtype)


def _split(x, widths):
    offs = np.cumsum([0] + list(widths))
    return [x[..., int(offs[i]):int(offs[i + 1])] for i in range(len(widths))]


def _alibi_slopes(n_heads):
    return jnp.asarray(2.0 ** (-8.0 * np.arange(1, n_heads + 1) / n_heads), dtype=jnp.float32)


def banded_attention(q, k, v, window, slopes, sink=None):
    b, s, h, d = q.shape
    hkv = k.shape[2]
    g = h // hkv
    nb = s // Q_BLOCK
    npre = window // Q_BLOCK
    pad = ((0, 0), (npre * Q_BLOCK, 0), (0, 0), (0, 0))
    kp = jnp.pad(k, pad).reshape(b, nb + npre, Q_BLOCK, hkv, d)
    vp = jnp.pad(v, pad).reshape(b, nb + npre, Q_BLOCK, hkv, d)
    kw = jnp.concatenate([kp[:, j:j + nb] for j in range(npre + 1)], axis=2)
    vw = jnp.concatenate([vp[:, j:j + nb] for j in range(npre + 1)], axis=2)
    qb = q.reshape(b, nb, Q_BLOCK, hkv, g, d)
    sc = jnp.einsum('bnqhgd,bnkhd->bnhgqk', qb, kw).astype(jnp.float32) / math.sqrt(d)
    qpos = jnp.arange(nb)[:, None] * Q_BLOCK + jnp.arange(Q_BLOCK)[None, :]
    kpos = (jnp.arange(nb)[:, None] - npre) * Q_BLOCK + jnp.arange((npre + 1) * Q_BLOCK)[None, :]
    diff = qpos[:, :, None] - kpos[:, None, :]
    mask = (diff >= 0) & (diff < window) & (kpos[:, None, :] >= 0)
    m_h = slopes.reshape(hkv, g)[:, :, None, None]
    diff_f = diff.astype(jnp.float32)[None, :, None, None]
    sc = jnp.where(mask[None, :, None, None], sc - m_h * diff_f, NEG)
    if sink is None:
        p = jax.nn.softmax(sc, axis=-1)
    else:
        sk = sink.astype(jnp.float32).reshape(hkv, g)[None, None, :, :, None, None]
        mx = jnp.maximum(sc.max(axis=-1, keepdims=True), sk)
        e = jnp.exp(sc - mx)
        p = e / (e.sum(axis=-1, keepdims=True) + jnp.exp(sk - mx))
    o = jnp.einsum('bnhgqk,bnkhd->bnqhgd', p, vw.astype(jnp.float32))
    return o.reshape(b, s, h, d).astype(q.dtype)


def stick_breaking_attention(q, k, v):
    b, s, h, d = q.shape
    nb = s // Q_BLOCK
    qb = q.reshape(b, nb, Q_BLOCK, h, d).transpose(1, 0, 2, 3, 4)
    kpos = jnp.arange(s)
    vf = v.astype(jnp.float32)

    def block(args):
        qi, i = args
        z = jnp.einsum('bqhd,bkhd->bhqk', qi, k).astype(jnp.float32) / math.sqrt(d)
        qpos = i * Q_BLOCK + jnp.arange(Q_BLOCK)
        mask = kpos[None, :] < qpos[:, None]
        log_keep = jnp.where(mask, jax.nn.log_sigmoid(-z), 0.0)
        suffix = lax.cumsum(log_keep, axis=3, reverse=True) - log_keep
        a = jnp.where(mask, jnp.exp(jax.nn.log_sigmoid(z) + suffix), 0.0)
        return jnp.einsum('bhqk,bkhd->bqhd', a, vf)

    o = lax.map(block, (qb, jnp.arange(nb)))
    return o.transpose(1, 0, 2, 3, 4).reshape(b, s, h, d).astype(q.dtype)


def moba_attention(q, k, v, slopes):
    b, s, h, d = q.shape
    nblk = -(-s // C_BLOCK)
    s_pad = nblk * C_BLOCK
    pad = ((0, 0), (0, s_pad - s), (0, 0), (0, 0))
    q, k, v = jnp.pad(q, pad), jnp.pad(k, pad), jnp.pad(v, pad)
    kb = k.reshape(b, nblk, C_BLOCK, h, d).transpose(0, 3, 1, 2, 4)
    vb = v.reshape(b, nblk, C_BLOCK, h, d).transpose(0, 3, 1, 2, 4)
    gate = jnp.einsum('bshd,bhnd->bhsn', q, kb.mean(axis=3)).astype(jnp.float32)
    own = jnp.arange(s_pad) // C_BLOCK
    past = jnp.arange(nblk)[None, :] < own[:, None]
    gate = jnp.where(past, gate, NEG)
    _, top_idx = lax.top_k(gate, min(C_TOPK, nblk))
    top_valid = top_idx < own[:, None]
    idx = jnp.concatenate([top_idx, jnp.broadcast_to(own[:, None], (b, h, s_pad, 1))], axis=-1)
    valid = jnp.concatenate([top_valid, jnp.ones((b, h, s_pad, 1), dtype=bool)], axis=-1)
    nc = s_pad // MOBA_Q_CHUNK

    def to_chunks(a):
        return jnp.moveaxis(a.reshape(b, h, nc, MOBA_Q_CHUNK, *a.shape[3:]), 2, 0)

    bi = jnp.arange(b)[:, None, None, None]
    hi = jnp.arange(h)[None, :, None, None]

    def chunk(args):
        qi, ii, vi, ci = args
        ks = kb[bi, hi, ii]
        vs = vb[bi, hi, ii]
        sc = jnp.einsum('bhqd,bhqnkd->bhqnk', qi, ks).astype(jnp.float32) / math.sqrt(d)
        qpos = ci * MOBA_Q_CHUNK + jnp.arange(MOBA_Q_CHUNK)
        kpos = ii[..., None] * C_BLOCK + jnp.arange(C_BLOCK)
        diff = (qpos[:, None, None] - kpos).astype(jnp.float32)
        mask = vi[..., None] & (diff >= 0)
        sc = jnp.where(mask, sc - slopes[:, None, None, None] * diff, NEG)
        p = jax.nn.softmax(sc.reshape(b, h, MOBA_Q_CHUNK, -1), axis=-1).reshape(sc.shape)
        return jnp.einsum('bhqnk,bhqnkd->bhqd', p, vs.astype(jnp.float32))

    o = lax.map(chunk, (to_chunks(q.transpose(0, 2, 1, 3)), to_chunks(idx), to_chunks(valid), jnp.arange(nc)))
    o = jnp.moveaxis(o, 0, 2).reshape(b, h, s_pad, d).transpose(0, 2, 1, 3)[:, :s]
    return o.astype(q.dtype)


def compress_blocks(x, pos_emb, w):
    s = x.shape[1]
    n_cmp = (s - NSA_CMP_LEN) // NSA_CMP_STRIDE + 1
    idx = np.arange(n_cmp)[:, None] * NSA_CMP_STRIDE + np.arange(NSA_CMP_LEN)[None, :]
    blocks = x[:, idx] + pos_emb[None, None, :, None, :]
    return jnp.einsum('bnlhd,lde->bnhe', blocks, w)


def nsa_attention(q, k_cmp, v_cmp, k_slc, v_slc, k_win, v_win, gates, slopes):
    b, s, h, d = q.shape
    hkv = k_slc.shape[2]
    g = h // hkv
    n_cmp = k_cmp.shape[1]
    pos = jnp.arange(s)
    m_h = slopes.reshape(hkv, g)
    qg = q.reshape(b, s, hkv, g, d)
    cmp_end = jnp.arange(n_cmp) * NSA_CMP_STRIDE + NSA_CMP_LEN - 1
    diff_c = (pos[:, None] - cmp_end[None, :]).astype(jnp.float32)
    mask_c = diff_c >= 0
    sc = jnp.einsum('bshgd,bnhd->bhgsn', qg, k_cmp).astype(jnp.float32) / math.sqrt(d)
    sc = jnp.where(mask_c, sc - m_h[:, :, None, None] * diff_c, NEG)
    e = jnp.where(mask_c, jnp.exp(sc - sc.max(axis=-1, keepdims=True)), 0.0)
    den = e.sum(axis=-1, keepdims=True)
    p_cmp = e / jnp.where(den > 0, den, 1.0)
    o_cmp = jnp.einsum('bhgsn,bnhd->bshgd', p_cmp, v_cmp.astype(jnp.float32)).reshape(b, s, h, d)
    n_slc = s // NSA_SLC_BLOCK
    cst = np.arange(n_cmp)[:, None] * NSA_CMP_STRIDE
    sst = np.arange(n_slc)[None, :] * NSA_SLC_BLOCK
    overlap = jnp.asarray(((cst < sst + NSA_SLC_BLOCK) & (cst + NSA_CMP_LEN > sst)).astype(np.float32))
    p_slc = jnp.einsum('bhgsn,nj->bhsj', p_cmp, overlap)
    cur = pos // NSA_SLC_BLOCK
    j = jnp.arange(n_slc)[None, :]
    forced = (j == 0) | (j == cur[:, None]) | (j == cur[:, None] - 1)
    score = jnp.where(j <= cur[:, None], p_slc + jnp.where(forced, FORCE_BONUS, 0.0), NEG)
    _, sel = lax.top_k(score, min(NSA_TOPN, n_slc))
    sel_valid = sel <= cur[:, None]
    ksb = k_slc.reshape(b, n_slc, NSA_SLC_BLOCK, hkv, d).transpose(0, 3, 1, 2, 4)
    vsb = v_slc.reshape(b, n_slc, NSA_SLC_BLOCK, hkv, d).transpose(0, 3, 1, 2, 4)
    nc = s // NSA_Q_CHUNK
    qc = jnp.moveaxis(qg.transpose(0, 2, 3, 1, 4).reshape(b, hkv, g, nc, NSA_Q_CHUNK, d), 3, 0)
    selc = jnp.moveaxis(sel.reshape(b, hkv, nc, NSA_Q_CHUNK, -1), 2, 0)
    valc = jnp.moveaxis(sel_valid.reshape(b, hkv, nc, NSA_Q_CHUNK, -1), 2, 0)
    bi = jnp.arange(b)[:, None, None, None]
    hi = jnp.arange(hkv)[None, :, None, None]

    def chunk(args):
        qi, si, vi, c = args
        ks = ksb[bi, hi, si]
        vs = vsb[bi, hi, si]
        scs = jnp.einsum('bhgqd,bhqnkd->bhgqnk', qi, ks).astype(jnp.float32) / math.sqrt(d)
        qpos = c * NSA_Q_CHUNK + jnp.arange(NSA_Q_CHUNK)
        kpos = si[..., None] * NSA_SLC_BLOCK + jnp.arange(NSA_SLC_BLOCK)
        diff = (qpos[:, None, None] - kpos).astype(jnp.float32)
        mask = vi[..., None] & (diff >= 0)
        scs = jnp.where(mask[:, :, None], scs - m_h[None, :, :, None, None, None] * diff[:, :, None], NEG)
        p = jax.nn.softmax(scs.reshape(b, hkv, g, NSA_Q_CHUNK, -1), axis=-1).reshape(scs.shape)
        return jnp.einsum('bhgqnk,bhqnkd->bhgqd', p, vs.astype(jnp.float32))

    o_slc = lax.map(chunk, (qc, selc, valc, jnp.arange(nc)))
    o_slc = jnp.moveaxis(o_slc, 0, 3).reshape(b, hkv, g, s, d).transpose(0, 3, 1, 2, 4).reshape(b, s, h, d)
    o_win = banded_attention(q, k_win, v_win, NSA_WINDOW, slopes).astype(jnp.float32)
    gf = gates.astype(jnp.float32)
    o = gf[..., 0:1] * o_cmp + gf[..., 1:2] * o_slc + gf[..., 2:3] * o_win
    return o.astype(q.dtype)


def even_mixer(x, norm_g, w_in, q_norm, k_norm, sink, w_out):
    b, s, _ = x.shape
    proj = rms_norm(x, norm_g) @ w_in
    qa, ka, va, qb, kb, vb = _split(proj, EVEN_WIDTHS)
    qa = rms_norm(qa.reshape(b, s, A_HEADS, HEAD_DIM), q_norm)
    ka = rms_norm(ka.reshape(b, s, A_KV_HEADS, HEAD_DIM), k_norm)
    va = va.reshape(b, s, A_KV_HEADS, HEAD_DIM)
    oa = banded_attention(qa, ka, va, A_WINDOW, _alibi_slopes(A_HEADS), sink)
    ob = stick_breaking_attention(qb.reshape(b, s, B_HEADS, HEAD_DIM),
                                  kb.reshape(b, s, B_HEADS, HEAD_DIM),
                                  vb.reshape(b, s, B_HEADS, HEAD_DIM))
    o = jnp.concatenate([oa.reshape(b, s, -1), ob.reshape(b, s, -1)], axis=-1)
    return x + o @ w_out


def odd_mixer(x, norm_g, w_in, c_q_norm, c_k_norm, d_q_norm, d_k_norm, cmp_pos, cmp_w, w_out):
    b, s, _ = x.shape
    proj = rms_norm(x, norm_g) @ w_in
    qc, kc, vc, qd, kcmp, vcmp, kslc, vslc, kwin, vwin, gd = _split(proj, ODD_WIDTHS)

    def heads(t, n):
        return t.reshape(b, s, n, HEAD_DIM)

    oc = moba_attention(rms_norm(heads(qc, C_HEADS), c_q_norm), rms_norm(heads(kc, C_HEADS), c_k_norm),
                        heads(vc, C_HEADS), _alibi_slopes(C_HEADS))
    k_cmp = rms_norm(compress_blocks(heads(kcmp, D_KV_HEADS), cmp_pos[0], cmp_w[0]), d_k_norm[0])
    v_cmp = compress_blocks(heads(vcmp, D_KV_HEADS), cmp_pos[1], cmp_w[1])
    od = nsa_attention(rms_norm(heads(qd, D_HEADS), d_q_norm), k_cmp, v_cmp,
                       rms_norm(heads(kslc, D_KV_HEADS), d_k_norm[1]), heads(vslc, D_KV_HEADS),
                       rms_norm(heads(kwin, D_KV_HEADS), d_k_norm[2]), heads(vwin, D_KV_HEADS),
                       jax.nn.sigmoid(gd.reshape(b, s, D_HEADS, 3)), _alibi_slopes(D_HEADS))
    o = jnp.concatenate([oc.reshape(b, s, -1), od.reshape(b, s, -1)], axis=-1)
    return x + o @ w_out


def hier_moe(x, norm_g, w_grp, b_grp, w_exp, b_exp, w_gate, w_up, w_down):
    shp = x.shape
    h = rms_norm(x, norm_g).reshape(-1, D_MODEL)
    t = h.shape[0]
    grp_logits = (h @ w_grp).astype(jnp.float32) + b_grp.astype(jnp.float32)
    grp_prob = jax.nn.softmax(grp_logits, axis=-1)
    g_sel = jnp.argmax(grp_logits, axis=-1)
    p_g = jnp.take_along_axis(grp_prob, g_sel[:, None], axis=1)
    exp_logits = ((h @ w_exp).astype(jnp.float32) + b_exp.astype(jnp.float32)).reshape(t, N_GROUPS, EXPERTS_PER_GROUP)
    within = jnp.take_along_axis(exp_logits, g_sel[:, None, None], axis=1)[:, 0]
    top_p, top_i = lax.top_k(jax.nn.softmax(within, axis=-1), TOP_K_IN_GROUP)
    wts = p_g * top_p / top_p.sum(axis=-1, keepdims=True)
    expert_id = g_sel[:, None] * EXPERTS_PER_GROUP + top_i
    comb = (jax.nn.one_hot(expert_id, N_EXPERTS, dtype=jnp.float32) * wts[..., None]).sum(axis=1)

    def step(acc, params):
        wg, wu, wd, c = params
        y = (jax.nn.silu(h @ wg) * (h @ wu)) @ wd
        return acc + c[:, None].astype(y.dtype) * y, None

    acc, _ = lax.scan(step, jnp.zeros_like(h), (w_gate, w_up, w_down, comb.T))
    return x + acc.reshape(shp)


def setup_inputs(seed: int = 0) -> dict:
    key = jax.random.key(seed)
    ks = iter(jax.random.split(key, 32))
    ne, no = (DEPTH + 1) // 2, DEPTH // 2

    def nrm(shape, scale):
        return scale * jax.random.normal(next(ks), shape, jnp.float32)

    def gain(shape):
        return 1.0 + nrm(shape, 0.02)

    return {
        'x': nrm((BATCH, SEQ, D_MODEL), 1.0),
        'ev_norm': gain((ne, D_MODEL)),
        'ev_w_in': nrm((ne, D_MODEL, EVEN_IN), D_MODEL ** -0.5),
        'ev_q_norm': gain((ne, HEAD_DIM)),
        'ev_k_norm': gain((ne, HEAD_DIM)),
        'ev_sink': nrm((ne, A_HEADS), 0.5),
        'ev_w_out': nrm((ne, MIX_WIDTH, D_MODEL), MIX_WIDTH ** -0.5),
        'od_norm': gain((no, D_MODEL)),
        'od_w_in': nrm((no, D_MODEL, ODD_IN), D_MODEL ** -0.5),
        'od_c_q_norm': gain((no, HEAD_DIM)),
        'od_c_k_norm': gain((no, HEAD_DIM)),
        'od_d_q_norm': gain((no, HEAD_DIM)),
        'od_d_k_norm': gain((no, 3, HEAD_DIM)),
        'od_cmp_pos': nrm((no, 2, NSA_CMP_LEN, HEAD_DIM), 0.1),
        'od_cmp_w': nrm((no, 2, NSA_CMP_LEN, HEAD_DIM, HEAD_DIM), (NSA_CMP_LEN * HEAD_DIM) ** -0.5),
        'od_w_out': nrm((no, MIX_WIDTH, D_MODEL), MIX_WIDTH ** -0.5),
        'moe_norm': gain((DEPTH, D_MODEL)),
        'moe_w_grp': nrm((DEPTH, D_MODEL, N_GROUPS), D_MODEL ** -0.5),
        'moe_b_grp': nrm((DEPTH, N_GROUPS), 0.01),
        'moe_w_exp': nrm((DEPTH, D_MODEL, N_EXPERTS), D_MODEL ** -0.5),
        'moe_b_exp': nrm((DEPTH, N_EXPERTS), 0.01),
        'moe_w_gate': nrm((DEPTH, N_EXPERTS, D_MODEL, EXPERT_FF), D_MODEL ** -0.5),
        'moe_w_up': nrm((DEPTH, N_EXPERTS, D_MODEL, EXPERT_FF), D_MODEL ** -0.5),
        'moe_w_down': nrm((DEPTH, N_EXPERTS, EXPERT_FF, D_MODEL), EXPERT_FF ** -0.5),
    }


def reference(x, ev_norm, ev_w_in, ev_q_norm, ev_k_norm, ev_sink, ev_w_out,
              od_norm, od_w_in, od_c_q_norm, od_c_k_norm, od_d_q_norm, od_d_k_norm,
              od_cmp_pos, od_cmp_w, od_w_out,
              moe_norm, moe_w_grp, moe_b_grp, moe_w_exp, moe_b_exp, moe_w_gate, moe_w_up, moe_w_down):
    for layer in range(DEPTH):
        i = layer // 2
        if layer % 2 == 0:
            x = even_mixer(x, ev_norm[i], ev_w_in[i], ev_q_norm[i], ev_k_norm[i], ev_sink[i], ev_w_out[i])
        else:
            x = odd_mixer(x, od_norm[i], od_w_in[i], od_c_q_norm[i], od_c_k_norm[i], od_d_q_norm[i],
                          od_d_k_norm[i], od_cmp_pos[i], od_cmp_w[i], od_w_out[i])
        x = hier_moe(x, moe_norm[layer], moe_w_grp[layer], moe_b_grp[layer], moe_w_exp[layer],
                     moe_b_exp[layer], moe_w_gate[layer], moe_w_up[layer], moe_w_down[layer])
    return x
```

```python
import functools
import math

import numpy as np
import jax
import jax.numpy as jnp
from jax import lax
from jax.experimental import pallas as pl
from jax.experimental.pallas import tpu as pltpu

F32 = jnp.float32
BF16 = jnp.bfloat16

LANES = 128
HEAD_DIM = 64
HEADS_PER_CHUNK = LANES // HEAD_DIM
D_MODEL = 1024
N_HEADS = 8
N_KV = 2
GROUP = N_HEADS // N_KV
N_CHUNKS = N_HEADS // HEADS_PER_CHUNK
MIX = N_HEADS * HEAD_DIM
SCALE = 1.0 / math.sqrt(HEAD_DIM)
EPS = 1e-6
NEG = -1e30

A_WINDOW = 128
C_BLOCK = 256
C_TOPK = 3
CMP_LEN = 32
CMP_STRIDE = 16
SLC_BLOCK = 64
SLC_TOPN = 16
D_WINDOW = 512
FORCE_BONUS = 1000.0

N_GROUPS = 4
EXPERTS_PER_GROUP = 8
N_EXPERTS = N_GROUPS * EXPERTS_PER_GROUP
EXPERT_FF = 256

VMEM_LIMIT = 48 * 1024 * 1024

GQA_PERM = tuple(h for c in range(N_CHUNKS) for h in (c, c + GROUP))
MHA_PERM = tuple(range(N_HEADS))


def _alibi(n_heads):
    return [float(2.0 ** (-8.0 * (i + 1) / n_heads)) for i in range(n_heads)]


def _head_cols(perm):
    return np.concatenate([np.arange(h * HEAD_DIM, (h + 1) * HEAD_DIM) for h in perm])


def _cparams(*sem):
    return pltpu.CompilerParams(dimension_semantics=sem, vmem_limit_bytes=VMEM_LIMIT)


def _dot(a, b):
    return jnp.dot(a, b, preferred_element_type=F32)


def _dot_nt(a, b):
    return lax.dot_general(a, b, (((1,), (1,)), ((), ())), preferred_element_type=F32)


def _split(x):
    hi = x.astype(BF16)
    lo = (x - hi.astype(F32)).astype(BF16)
    return hi, lo


def _dot_hilo(a, b):
    hi, lo = _split(a)
    return _dot(hi, b) + _dot(lo, b)


def _iota(shape, dim):
    return lax.broadcasted_iota(jnp.int32, shape, dim)


def _half_masks(dtype):
    lane = _iota((1, LANES), 1)
    return [(lane // HEAD_DIM == p).astype(dtype) for p in range(HEADS_PER_CHUNK)]


def _head_mean_sq(y):
    same = (_iota((LANES, LANES), 0) // HEAD_DIM == _iota((LANES, LANES), 1) // HEAD_DIM)
    return _dot_hilo(y * y, same.astype(BF16)) * (1.0 / HEAD_DIM)


def _proj_kernel(x_ref, g_ref, w_ref, gain_ref, *out_refs, plan):
    x = x_ref[...]
    ms = jnp.mean(x * x, axis=-1, keepdims=True)
    xn = (x * lax.rsqrt(ms + EPS) * g_ref[...]).astype(BF16)
    for o_ref, (col0, width, op) in zip(out_refs, plan):
        for a in range(0, width, 2 * LANES):
            wd = min(2 * LANES, width - a)
            y = _dot(xn, w_ref[:, col0 + a:col0 + a + wd])
            for c in range(0, wd, LANES):
                yc = y[:, c:c + LANES]
                if op == "norm":
                    gain = gain_ref[:, col0 + a + c:col0 + a + c + LANES]
                    yc = yc * lax.rsqrt(_head_mean_sq(yc) + EPS) * gain
                elif op == "sigmoid":
                    yc = jax.nn.sigmoid(yc)
                o_ref[:, a + c:a + c + LANES] = yc.astype(o_ref.dtype)


def _proj(x2, norm_g, w, gain, plan, out_dtypes, tm=512):
    t, d = x2.shape
    n = w.shape[1]
    out_shape = [jax.ShapeDtypeStruct((t, width), dt) for (_, width, _), dt in zip(plan, out_dtypes)]
    return pl.pallas_call(
        functools.partial(_proj_kernel, plan=plan),
        out_shape=out_shape,
        grid=(t // tm,),
        in_specs=[
            pl.BlockSpec((tm, d), lambda i: (i, 0)),
            pl.BlockSpec((1, d), lambda i: (0, 0)),
            pl.BlockSpec((d, n), lambda i: (0, 0)),
            pl.BlockSpec((1, n), lambda i: (0, 0)),
        ],
        out_specs=[pl.BlockSpec((tm, width), lambda i: (i, 0)) for (_, width, _) in plan],
        compiler_params=_cparams("parallel"),
        name="norm_proj",
    )(x2, norm_g.reshape(1, d), w, gain)


def _stack_group(q_ref, halfmask):
    return jnp.concatenate(
        [q_ref[:, c * LANES:(c + 1) * LANES] * halfmask for c in range(N_CHUNKS)], axis=0)


def _per_chunk_rows(rows, tq, values):
    r = _iota((rows, 1), 0) // tq
    out = jnp.full((rows, 1), values[N_CHUNKS - 1], F32)
    for c in range(N_CHUNKS - 2, -1, -1):
        out = jnp.where(r == c, values[c], out)
    return out


def _band_kernel(*refs, tq, window, wpad, slopes, has_sink):
    if has_sink:
        sink_ref, q_ref, k_ref, v_ref, o_ref = refs
    else:
        q_ref, k_ref, v_ref, o_ref = refs
    q0 = pl.program_id(1) * tq
    kw = tq + wpad
    kstart = pl.multiple_of(jnp.maximum(q0 - wpad, 0), LANES)
    ks = k_ref[pl.ds(kstart, kw), :]
    vs = v_ref[pl.ds(kstart, kw), :]
    rows = N_CHUNKS * tq
    qpos = q0 + _iota((rows, kw), 0) % tq
    kpos = kstart + _iota((rows, kw), 1)
    diff = qpos - kpos
    mask = (diff >= 0) & (diff < window)
    diff_f = diff.astype(F32)
    lane = _iota((1, LANES), 1)
    outs = [None] * N_CHUNKS
    for p, hm in enumerate(_half_masks(BF16)):
        qs = _stack_group(q_ref, hm * SCALE)
        slope = _per_chunk_rows(rows, tq, [slopes[2 * c + p] for c in range(N_CHUNKS)])
        s = jnp.where(mask, _dot_nt(qs, ks) - slope * diff_f, NEG)
        mx = jnp.max(s, axis=-1, keepdims=True)
        if has_sink:
            sk = _per_chunk_rows(rows, tq, [sink_ref[2 * c + p] for c in range(N_CHUNKS)])
            mx = jnp.maximum(mx, sk)
        e = jnp.exp(s - mx)
        den = jnp.sum(e, axis=-1, keepdims=True)
        if has_sink:
            den = den + jnp.exp(sk - mx)
        o = _dot(e.astype(BF16), vs) / den
        for c in range(N_CHUNKS):
            oc = o[c * tq:(c + 1) * tq]
            outs[c] = oc if p == 0 else jnp.where(lane // HEAD_DIM == p, oc, outs[c])
    for c in range(N_CHUNKS):
        o_ref[:, c * LANES:(c + 1) * LANES] = outs[c].astype(o_ref.dtype)


def _band_attention(q, k, v, *, window, slopes, sink=None, tq=128):
    b, s, _ = q.shape
    wpad = -(-window // LANES) * LANES
    assert s >= tq + wpad and s % tq == 0
    kern = functools.partial(_band_kernel, tq=tq, window=window, wpad=wpad,
                             slopes=slopes, has_sink=sink is not None)
    in_specs = [
        pl.BlockSpec((None, tq, MIX), lambda bi, n: (bi, n, 0)),
        pl.BlockSpec((None, s, LANES), lambda bi, n: (bi, 0, 0)),
        pl.BlockSpec((None, s, LANES), lambda bi, n: (bi, 0, 0)),
    ]
    args = [q, k, v]
    if sink is not None:
        in_specs = [pl.BlockSpec(memory_space=pltpu.SMEM)] + in_specs
        args = [sink] + args
    return pl.pallas_call(
        kern,
        out_shape=jax.ShapeDtypeStruct((b, s, MIX), BF16),
        grid=(b, s // tq),
        in_specs=in_specs,
        out_specs=pl.BlockSpec((None, tq, MIX), lambda bi, n: (bi, n, 0)),
        compiler_params=_cparams("parallel", "parallel"),
        name="band_attention",
    )(*args)


def _stick_kernel(q_ref, k_ref, v_ref, o_ref, *, tq):
    i = pl.program_id(2)
    q0 = pl.multiple_of(i * tq, tq)
    hms = _half_masks(BF16)
    qs = [q_ref[...] * (hm * SCALE) for hm in hms]
    upper = (_iota((tq, tq), 0) > _iota((tq, tq), 1)).astype(BF16)
    causal = _iota((tq, tq), 1) < _iota((tq, tq), 0)

    def block(kstart, carry, diag):
        kj = k_ref[pl.ds(kstart, tq), :]
        vj = v_ref[pl.ds(kstart, tq), :]
        new = []
        for p in range(HEADS_PER_CHUNK):
            acc, run = carry[p]
            z = _dot_nt(qs[p], kj)
            ls = jnp.minimum(z, 0.0) - jnp.log(1.0 + jnp.exp(-jnp.abs(z)))
            lk = ls - z
            if diag:
                lk = jnp.where(causal, lk, 0.0)
            a = jnp.exp(ls + _dot_hilo(lk, upper) + run)
            if diag:
                a = jnp.where(causal, a, 0.0)
            acc = acc + _dot(a.astype(BF16), vj)
            run = run + jnp.sum(lk, axis=-1, keepdims=True)
            new.append((acc, run))
        return tuple(new)

    zero = (jnp.zeros((tq, LANES), F32), jnp.zeros((tq, 1), F32))
    carry = block(q0, (zero, zero), True)

    def body(t, carry):
        return block(pl.multiple_of((i - 1 - t) * tq, tq), carry, False)

    carry = lax.fori_loop(0, i, body, carry)
    lane = _iota((1, LANES), 1)
    o_ref[...] = jnp.where(lane < HEAD_DIM, carry[0][0], carry[1][0]).astype(o_ref.dtype)


def _stick_attention(q, k, v, tq=128):
    b, s, _ = q.shape
    spec_q = pl.BlockSpec((None, tq, LANES), lambda bi, c, i: (bi, i, c))
    spec_kv = pl.BlockSpec((None, s, LANES), lambda bi, c, i: (bi, 0, c))
    return pl.pallas_call(
        functools.partial(_stick_kernel, tq=tq),
        out_shape=jax.ShapeDtypeStruct((b, s, MIX), BF16),
        grid=(b, N_CHUNKS, s // tq),
        in_specs=[spec_q, spec_kv, spec_kv],
        out_specs=spec_q,
        compiler_params=_cparams("parallel", "parallel", "parallel"),
        name="stick_breaking",
    )(q, k, v)


def _route(logits):
    lane = _iota(logits.shape, 1)
    lane_f = lane.astype(F32)
    ninf = -jnp.inf
    is_g = (lane >= N_EXPERTS) & (lane < N_EXPERTS + N_GROUPS)
    gmax = jnp.max(jnp.where(is_g, logits, ninf), axis=-1, keepdims=True)
    gidx = jnp.min(jnp.where(is_g & (logits == gmax), lane_f - N_EXPERTS, 1e9), axis=-1, keepdims=True)
    p_g = 1.0 / jnp.sum(jnp.where(is_g, jnp.exp(logits - gmax), 0.0), axis=-1, keepdims=True)
    in_grp = (lane < N_EXPERTS) & ((lane // EXPERTS_PER_GROUP).astype(F32) == gidx)
    le = jnp.where(in_grp, logits, ninf)
    m1 = jnp.max(le, axis=-1, keepdims=True)
    i1 = jnp.min(jnp.where(le == m1, lane_f, 1e9), axis=-1, keepdims=True)
    le2 = jnp.where(lane_f == i1, ninf, le)
    m2 = jnp.max(le2, axis=-1, keepdims=True)
    i2 = jnp.min(jnp.where(le2 == m2, lane_f, 1e9), axis=-1, keepdims=True)
    e2 = jnp.exp(m2 - m1)
    w1 = p_g / (1.0 + e2)
    w2 = p_g * e2 / (1.0 + e2)
    return jnp.where(lane_f == i1, w1, 0.0) + jnp.where(lane_f == i2, w2, 0.0)


def _gate_expand(branch):
    r = _iota((LANES, MIX), 0)
    col = _iota((LANES, MIX), 1)
    head = col // LANES + GROUP * ((col % LANES) // HEAD_DIM)
    return (r == 3 * head + branch).astype(BF16)


def _out_kernel(*refs, nsa):
    if nsa:
        (x_ref, o1_ref, ocmp_ref, oslc_ref, owin_ref, gates_ref, w1_ref, w2_ref,
         ng_ref, wr_ref, br_ref, x1_ref, h_ref, comb_ref) = refs
        g = gates_ref[...]
        o2 = (_dot_hilo(g, _gate_expand(0)) * ocmp_ref[...]
              + _dot_hilo(g, _gate_expand(1)) * oslc_ref[...]
              + _dot_hilo(g, _gate_expand(2)) * owin_ref[...]).astype(BF16)
    else:
        (x_ref, o1_ref, o2_ref, w1_ref, w2_ref,
         ng_ref, wr_ref, br_ref, x1_ref, h_ref, comb_ref) = refs
        o2 = o2_ref[...]
    x1 = x_ref[...] + _dot(o1_ref[...], w1_ref[...]) + _dot(o2, w2_ref[...])
    x1_ref[...] = x1
    ms = jnp.mean(x1 * x1, axis=-1, keepdims=True)
    h = x1 * lax.rsqrt(ms + EPS) * ng_ref[...]
    h_ref[...] = h.astype(BF16)
    h_hi, h_lo = _split(h)
    w_hi, w_lo = _split(wr_ref[...])
    logits = _dot(h_hi, w_hi) + (_dot(h_hi, w_lo) + _dot(h_lo, w_hi)) + br_ref[...]
    comb_ref[...] = _route(logits)


def _out_proj_route(x2, attn, w1, w2, moe_g, w_route, b_route, tm=512):
    t, d = x2.shape
    nsa = len(attn) > 2
    row = lambda width: pl.BlockSpec((tm, width), lambda i: (i, 0))
    full = lambda a: pl.BlockSpec(a.shape, lambda i: (0, 0))
    consts = [w1, w2, moe_g.reshape(1, d), w_route, b_route]
    return pl.pallas_call(
        functools.partial(_out_kernel, nsa=nsa),
        out_shape=[jax.ShapeDtypeStruct((t, d), F32), jax.ShapeDtypeStruct((t, d), BF16),
                   jax.ShapeDtypeStruct((t, LANES), F32)],
        grid=(t // tm,),
        in_specs=[row(d)] + [row(a.shape[1]) for a in attn] + [full(a) for a in consts],
        out_specs=[row(d), row(d), row(LANES)],
        compiler_params=_cparams("parallel"),
        name="out_proj_route",
    )(x2, *attn, *consts)


def _moe_kernel(h_ref, comb_ref, x1_ref, wgu_ref, wd_ref, o_ref):
    e = pl.program_id(1)

    @pl.when(e == 0)
    def _():
        o_ref[...] = x1_ref[...]

    gu = _dot(h_ref[...], wgu_ref[...])
    g = gu[:, :EXPERT_FF]
    u = gu[:, EXPERT_FF:]
    act = (g * jax.nn.sigmoid(g) * u).astype(BF16)
    y = _dot(act, wd_ref[...])
    comb = comb_ref[...]
    c = jnp.sum(jnp.where(_iota(comb.shape, 1) == e, comb, 0.0), axis=-1, keepdims=True)
    o_ref[...] += c * y


def _moe(h, comb, x1, wgu, wd, tm=1024):
    t, d = h.shape
    assert t % tm == 0
    return pl.pallas_call(
        _moe_kernel,
        out_shape=jax.ShapeDtypeStruct((t, d), F32),
        grid=(t // tm, N_EXPERTS),
        in_specs=[
            pl.BlockSpec((tm, d), lambda i, e: (i, 0)),
            pl.BlockSpec((tm, LANES), lambda i, e: (i, 0)),
            pl.BlockSpec((tm, d), lambda i, e: (i, 0)),
            pl.BlockSpec((None, d, 2 * EXPERT_FF), lambda i, e: (e, 0, 0)),
            pl.BlockSpec((None, EXPERT_FF, d), lambda i, e: (e, 0, 0)),
        ],
        out_specs=pl.BlockSpec((tm, d), lambda i, e: (i, 0)),
        compiler_params=_cparams("parallel", "arbitrary"),
        name="moe_experts",
    )(h, comb, x1, wgu, wd)


def _moe_block(x2, attn, w1, w2, moe_g, w_grp, b_grp, w_exp, b_exp, w_gate, w_up, w_down):
    d = x2.shape[1]
    pad = LANES - N_EXPERTS - N_GROUPS
    w_route = jnp.concatenate([w_exp, w_grp, jnp.zeros((d, pad), F32)], axis=1)
    b_route = jnp.concatenate([b_exp, b_grp, jnp.zeros((pad,), F32)]).reshape(1, LANES)
    x1, h, comb = _out_proj_route(x2, attn, w1, w2, moe_g, w_route, b_route)
    wgu = jnp.concatenate([w_gate, w_up], axis=2).astype(BF16)
    return _moe(h, comb, x1, wgu, w_down.astype(BF16))


def _even_attn(x2, b, s, norm_g, w_in, q_norm, k_norm, sink, w_out):
    gq = _head_cols(GQA_PERM)
    w = jnp.concatenate([w_in[:, :MIX][:, gq], w_in[:, MIX:]], axis=1).astype(BF16)
    n = w.shape[1]
    kvw = N_KV * HEAD_DIM
    gain = jnp.concatenate([jnp.tile(q_norm, N_HEADS), jnp.tile(k_norm, N_KV),
                            jnp.ones((n - MIX - kvw,), F32)]).reshape(1, n)
    plan, col = [], 0
    for width, op in ((MIX, "norm"), (kvw, "norm"), (kvw, None), (MIX, None), (MIX, None), (MIX, None)):
        plan.append((col, width, op))
        col += width
    qa, ka, va, qb, kb, vb = _proj(x2, norm_g, w, gain, tuple(plan), [BF16] * 6)
    r3 = lambda a: a.reshape(b, s, a.shape[-1])
    slopes = _alibi(N_HEADS)
    oa = _band_attention(r3(qa), r3(ka), r3(va), window=A_WINDOW,
                         slopes=[slopes[h] for h in GQA_PERM], sink=sink[np.asarray(GQA_PERM)])
    ob = _stick_attention(r3(qb), r3(kb), r3(vb))
    w1 = w_out[:MIX][gq].astype(BF16)
    w2 = w_out[MIX:].astype(BF16)
    t = b * s
    return [oa.reshape(t, MIX), ob.reshape(t, MIX)], w1, w2


def _rank_before(score, col, n_cols, seg):
    lane = _iota(score.shape, 1)
    rank = jnp.zeros(score.shape, F32)
    n_seg = LANES // seg
    for jj in range(n_cols):
        other = score[:, jj:jj + 1]
        for sgm in range(1, n_seg):
            other = jnp.where(lane // seg == sgm, score[:, sgm * seg + jj:sgm * seg + jj + 1], other)
        beats = (other > score) | ((other == score) & (jj < col))
        rank = rank + jnp.where(beats, 1.0, 0.0)
    return rank


def _moba_kernel(slope_ref, q_ref, k_ref, v_ref, o_ref, kmean_ref, *, blk, nblk, topk):
    c = pl.program_id(1)
    i = pl.program_id(2)
    s_len = k_ref.shape[0]

    @pl.when(i == 0)
    def _():
        member = (_iota((LANES, s_len), 1) // blk == _iota((LANES, s_len), 0)).astype(BF16)
        kmean_ref[...] = _dot(member, k_ref[...]) * (1.0 / blk)

    km_hi, km_lo = _split(kmean_ref[...])
    q = q_ref[...]
    q0 = pl.multiple_of(i * blk, blk)
    col = _iota((blk, LANES), 1)
    rel = (_iota((blk, blk), 0) - _iota((blk, blk), 1)).astype(F32)
    k_own = k_ref[pl.ds(q0, blk), :]
    v_own = v_ref[pl.ds(q0, blk), :]
    lane = _iota((1, LANES), 1)
    out = None
    for p, hm in enumerate(_half_masks(BF16)):
        slope = slope_ref[HEADS_PER_CHUNK * c + p]
        qh = q * hm
        gate = jnp.where(col < i, _dot_nt(qh, km_hi) + _dot_nt(qh, km_lo), NEG)
        rank = _rank_before(gate, col, nblk, LANES)
        sel = jnp.where((rank < topk) & (col < i), 1.0, 0.0)
        qs = q * (hm * SCALE)
        s = jnp.where(rel >= 0, _dot_nt(qs, k_own) - slope * rel, NEG)
        m = jnp.max(s, axis=-1, keepdims=True)
        e = jnp.exp(s - m)
        l = jnp.sum(e, axis=-1, keepdims=True)
        acc = _dot(e.astype(BF16), v_own)

        def body(j, carry, qs=qs, sel=sel, slope=slope):
            m, l, acc = carry
            k0 = pl.multiple_of(j * blk, blk)
            kj = k_ref[pl.ds(k0, blk), :]
            vj = v_ref[pl.ds(k0, blk), :]
            picked = jnp.sum(jnp.where(col == j, sel, 0.0), axis=-1, keepdims=True)
            diff = rel + ((i - j) * blk).astype(F32)
            s = jnp.where(picked > 0.5, _dot_nt(qs, kj) - slope * diff, NEG)
            m_new = jnp.maximum(m, jnp.max(s, axis=-1, keepdims=True))
            alpha = jnp.exp(m - m_new)
            e = jnp.exp(s - m_new)
            l = alpha * l + jnp.sum(e, axis=-1, keepdims=True)
            acc = alpha * acc + _dot(e.astype(BF16), vj)
            return m_new, l, acc

        m, l, acc = lax.fori_loop(0, i, body, (m, l, acc))
        o = acc / l
        out = o if p == 0 else jnp.where(lane // HEAD_DIM == p, o, out)
    o_ref[...] = out.astype(o_ref.dtype)


def _moba_attention(q, k, v, slopes):
    b, s, _ = q.shape
    assert s % C_BLOCK == 0
    nblk = s // C_BLOCK
    assert nblk <= LANES
    spec_q = pl.BlockSpec((None, C_BLOCK, LANES), lambda bi, c, i, sl: (bi, i, c))
    spec_kv = pl.BlockSpec((None, s, LANES), lambda bi, c, i, sl: (bi, 0, c))
    return pl.pallas_call(
        functools.partial(_moba_kernel, blk=C_BLOCK, nblk=nblk, topk=min(C_TOPK, nblk)),
        out_shape=jax.ShapeDtypeStruct((b, s, MIX), BF16),
        grid_spec=pltpu.PrefetchScalarGridSpec(
            num_scalar_prefetch=1,
            grid=(b, N_CHUNKS, nblk),
            in_specs=[spec_q, spec_kv, spec_kv],
            out_specs=spec_q,
            scratch_shapes=[pltpu.VMEM((LANES, LANES), F32)],
        ),
        compiler_params=_cparams("parallel", "parallel", "arbitrary"),
        name="moba_attention",
    )(jnp.asarray(slopes, F32), q, k, v)


def _compress_kernel(xk_ref, xv_ref, wk_lo_ref, wk_hi_ref, wv_lo_ref, wv_hi_ref,
                     pk_lo_ref, pk_hi_ref, pv_lo_ref, pv_hi_ref, gain_ref, kc_ref, vc_ref):
    def compress(x_ref, w_lo_ref, w_hi_ref, p_lo_ref, p_hi_ref):
        x = x_ref[...]
        nrow = x.shape[0]
        first = _dot(x, w_lo_ref[...])
        second = pltpu.roll(_dot(x, w_hi_ref[...]), nrow - 1, 0)
        p_lo = jnp.broadcast_to(p_lo_ref[...], (8, p_lo_ref.shape[1]))
        p_hi = jnp.broadcast_to(p_hi_ref[...], (8, p_hi_ref.shape[1]))
        bias = _dot_hilo(p_lo, w_lo_ref[...]) + _dot_hilo(p_hi, w_hi_ref[...])
        return first + second + bias[0:1]

    kc = compress(xk_ref, wk_lo_ref, wk_hi_ref, pk_lo_ref, pk_hi_ref)
    kc = kc * lax.rsqrt(_head_mean_sq(kc) + EPS) * gain_ref[...]
    kc_ref[...] = kc.astype(kc_ref.dtype)
    vc_ref[...] = compress(xv_ref, wv_lo_ref, wv_hi_ref, pv_lo_ref, pv_hi_ref).astype(vc_ref.dtype)


def _compress_weights(pos, w):
    half = CMP_LEN // 2
    eye = jnp.eye(N_KV, dtype=F32)
    wd = jnp.einsum("gh,lde->lgdhe", eye, w).reshape(CMP_LEN, LANES, LANES)
    w_lo = wd[:half].reshape(half * LANES, LANES).astype(BF16)
    w_hi = wd[half:].reshape(half * LANES, LANES).astype(BF16)
    pt = jnp.tile(pos, (1, N_KV))
    return w_lo, w_hi, pt[:half].reshape(1, half * LANES), pt[half:].reshape(1, half * LANES)


def _compress(kcmp, vcmp, cmp_pos, cmp_w, k_gain):
    b, s, _ = kcmp.shape
    assert CMP_LEN == 2 * CMP_STRIDE and s % CMP_STRIDE == 0
    nrow = s // CMP_STRIDE
    wide = CMP_STRIDE * LANES
    xk = kcmp.reshape(b, nrow, wide)
    xv = vcmp.reshape(b, nrow, wide)
    wk = _compress_weights(cmp_pos[0], cmp_w[0])
    wv = _compress_weights(cmp_pos[1], cmp_w[1])
    consts = [wk[0], wk[1], wv[0], wv[1], wk[2], wk[3], wv[2], wv[3],
              jnp.tile(k_gain, N_KV).reshape(1, LANES)]
    spec_x = pl.BlockSpec((None, nrow, wide), lambda bi: (bi, 0, 0))
    spec_o = pl.BlockSpec((None, nrow, LANES), lambda bi: (bi, 0, 0))
    return pl.pallas_call(
        _compress_kernel,
        out_shape=[jax.ShapeDtypeStruct((b, nrow, LANES), BF16)] * 2,
        grid=(b,),
        in_specs=[spec_x, spec_x] + [pl.BlockSpec(a.shape, lambda bi: (0, 0)) for a in consts],
        out_specs=[spec_o, spec_o],
        compiler_params=_cparams("parallel"),
        name="nsa_compress",
    )(xk, xv, *consts)


def _nsa_cmp_kernel(q_ref, kc_ref, vc_ref, o_ref, sel_ref, *, tq, n_cmp, n_slc, topn, slopes):
    q0 = pl.program_id(1) * tq
    kc = kc_ref[...]
    vc = vc_ref[...]
    ncp = kc.shape[0]
    rows = N_CHUNKS * tq
    seg = LANES // N_KV
    qpos = q0 + _iota((rows, ncp), 0) % tq
    ncol = _iota((rows, ncp), 1)
    diff = qpos - (ncol * CMP_STRIDE + CMP_LEN - 1)
    mask = (diff >= 0) & (ncol < n_cmp)
    diff_f = diff.astype(F32)
    lane = _iota((1, LANES), 1)
    cst = _iota((ncp, LANES), 0) * CMP_STRIDE
    ocol = _iota((ncp, LANES), 1)
    sst = (ocol % seg) * SLC_BLOCK
    overlap = (cst < sst + SLC_BLOCK) & (cst + CMP_LEN > sst) & (ocol % seg < n_slc)
    outs = [None] * N_CHUNKS
    p_slc = jnp.zeros((tq, LANES), F32)
    for p, hm in enumerate(_half_masks(BF16)):
        qs = _stack_group(q_ref, hm * SCALE)
        slope = _per_chunk_rows(rows, tq, [slopes[2 * c + p] for c in range(N_CHUNKS)])
        sc = jnp.where(mask, _dot_nt(qs, kc) - slope * diff_f, NEG)
        mx = jnp.max(sc, axis=-1, keepdims=True)
        e = jnp.where(mask, jnp.exp(sc - mx), 0.0)
        den = jnp.sum(e, axis=-1, keepdims=True)
        pc = e / jnp.where(den > 0, den, 1.0)
        o = _dot(pc.astype(BF16), vc)
        pg = pc[0:tq]
        for c in range(N_CHUNKS):
            oc = o[c * tq:(c + 1) * tq]
            outs[c] = oc if p == 0 else jnp.where(lane // HEAD_DIM == p, oc, outs[c])
            if c > 0:
                pg = pg + pc[c * tq:(c + 1) * tq]
        ov = (overlap & (ocol // seg == p)).astype(BF16)
        p_slc = p_slc + _dot_hilo(pg, ov)
    for c in range(N_CHUNKS):
        o_ref[:, c * LANES:(c + 1) * LANES] = outs[c].astype(o_ref.dtype)
    lanes = _iota((tq, LANES), 1)
    j = lanes % seg
    cur = (q0 + _iota((tq, LANES), 0)) // SLC_BLOCK
    forced = (j == 0) | (j == cur) | (j == cur - 1)
    usable = (j <= cur) & (j < n_slc)
    score = jnp.where(usable, p_slc + jnp.where(forced, FORCE_BONUS, 0.0), NEG)
    rank = _rank_before(score, j, n_slc, seg)
    sel_ref[...] = jnp.where((rank < topn) & usable, 1.0, 0.0).astype(sel_ref.dtype)


def _nsa_cmp(q, kc, vc, slopes, tq=128):
    b, s, _ = q.shape
    ncp = kc.shape[1]
    n_slc = s // SLC_BLOCK
    assert n_slc <= LANES // N_KV and ncp % LANES == 0
    kern = functools.partial(_nsa_cmp_kernel, tq=tq, n_cmp=ncp - 1, n_slc=n_slc,
                             topn=min(SLC_TOPN, n_slc), slopes=slopes)
    spec_c = pl.BlockSpec((None, ncp, LANES), lambda bi, n: (bi, 0, 0))
    return pl.pallas_call(
        kern,
        out_shape=[jax.ShapeDtypeStruct((b, s, MIX), BF16), jax.ShapeDtypeStruct((b, s, LANES), BF16)],
        grid=(b, s // tq),
        in_specs=[pl.BlockSpec((None, tq, MIX), lambda bi, n: (bi, n, 0)), spec_c, spec_c],
        out_specs=[pl.BlockSpec((None, tq, MIX), lambda bi, n: (bi, n, 0)),
                   pl.BlockSpec((None, tq, LANES), lambda bi, n: (bi, n, 0))],
        compiler_params=_cparams("parallel", "parallel"),
        name="nsa_compressed",
    )(q, kc, vc)


def _nsa_slc_kernel(q_ref, k_ref, v_ref, sel_ref, o_ref, *, tq, slopes):
    n = pl.program_id(1)
    q0 = n * tq
    rows = N_CHUNKS * tq
    seg = LANES // N_KV
    sel = sel_ref[...]
    rel = _iota((tq, tq), 0) - _iota((tq, tq), 1)
    lane = _iota((1, LANES), 1)
    outs = [None] * N_CHUNKS
    for p, hm in enumerate(_half_masks(BF16)):
        qs = _stack_group(q_ref, hm * SCALE)
        slope = _per_chunk_rows(rows, tq, [slopes[2 * c + p] for c in range(N_CHUNKS)])

        def body(kt, carry, qs=qs, slope=slope, p=p):
            m, l, acc = carry
            k0 = pl.multiple_of(kt * tq, tq)
            kj = k_ref[pl.ds(k0, tq), :]
            vj = v_ref[pl.ds(k0, tq), :]
            expand = (_iota((LANES, tq), 0) == p * seg + (k0 + _iota((LANES, tq), 1)) // SLC_BLOCK)
            picked = _dot(sel, expand.astype(BF16))
            diff = rel + (q0 - k0)
            ok = jnp.where((picked > 0.5) & (diff >= 0), 1.0, 0.0)
            ok4 = jnp.concatenate([ok] * N_CHUNKS, axis=0)
            diff4 = jnp.concatenate([diff.astype(F32)] * N_CHUNKS, axis=0)
            s = jnp.where(ok4 > 0.5, _dot_nt(qs, kj) - slope * diff4, NEG)
            m_new = jnp.maximum(m, jnp.max(s, axis=-1, keepdims=True))
            alpha = jnp.exp(m - m_new)
            e = jnp.exp(s - m_new)
            l = alpha * l + jnp.sum(e, axis=-1, keepdims=True)
            acc = alpha * acc + _dot(e.astype(BF16), vj)
            return m_new, l, acc

        init = (jnp.full((rows, 1), 0.1 * NEG, F32), jnp.zeros((rows, 1), F32), jnp.zeros((rows, LANES), F32))
        m, l, acc = lax.fori_loop(0, n + 1, body, init)
        o = acc / l
        for c in range(N_CHUNKS):
            oc = o[c * tq:(c + 1) * tq]
            outs[c] = oc if p == 0 else jnp.where(lane // HEAD_DIM == p, oc, outs[c])
    for c in range(N_CHUNKS):
        o_ref[:, c * LANES:(c + 1) * LANES] = outs[c].astype(o_ref.dtype)


def _nsa_slc(q, k, v, sel, slopes, tq=128):
    b, s, _ = q.shape
    assert tq % SLC_BLOCK == 0 and s % tq == 0
    spec_q = pl.BlockSpec((None, tq, MIX), lambda bi, n: (bi, n, 0))
    spec_kv = pl.BlockSpec((None, s, LANES), lambda bi, n: (bi, 0, 0))
    return pl.pallas_call(
        functools.partial(_nsa_slc_kernel, tq=tq, slopes=slopes),
        out_shape=jax.ShapeDtypeStruct((b, s, MIX), BF16),
        grid=(b, s // tq),
        in_specs=[spec_q, spec_kv, spec_kv, pl.BlockSpec((None, tq, LANES), lambda bi, n: (bi, n, 0))],
        out_specs=spec_q,
        compiler_params=_cparams("parallel", "parallel"),
        name="nsa_selected",
    )(q, k, v, sel)


def _odd_attn(x2, b, s, norm_g, w_in, c_q_norm, c_k_norm, d_q_norm, d_k_norm, cmp_pos, cmp_w, w_out):
    gq = _head_cols(GQA_PERM)
    kvw = N_KV * HEAD_DIM
    n_gate = N_HEADS * 3
    qd0 = 3 * MIX
    w = jnp.concatenate([w_in[:, :qd0], w_in[:, qd0:qd0 + MIX][:, gq], w_in[:, qd0 + MIX:],
                         jnp.zeros((w_in.shape[0], LANES - n_gate), F32)], axis=1).astype(BF16)
    n = w.shape[1]
    ones = lambda width: jnp.ones((width,), F32)
    gain = jnp.concatenate([
        jnp.tile(c_q_norm, N_HEADS), jnp.tile(c_k_norm, N_HEADS), ones(MIX), jnp.tile(d_q_norm, N_HEADS),
        ones(2 * kvw), jnp.tile(d_k_norm[1], N_KV), ones(kvw), jnp.tile(d_k_norm[2], N_KV), ones(kvw),
        ones(LANES)]).reshape(1, n)
    plan, col = [], 0
    for width, op in ((MIX, "norm"), (MIX, "norm"), (MIX, None), (MIX, "norm"), (kvw, None), (kvw, None),
                      (kvw, "norm"), (kvw, None), (kvw, "norm"), (kvw, None), (LANES, "sigmoid")):
        plan.append((col, width, op))
        col += width
    outs = _proj(x2, norm_g, w, gain, tuple(plan), [BF16] * 10 + [F32])
    r3 = lambda a: a.reshape(b, s, a.shape[-1])
    qc, kc, vc, qd, kcmp, vcmp, kslc, vslc, kwin, vwin = [r3(a) for a in outs[:10]]
    gates = outs[10]
    slopes = _alibi(N_HEADS)
    gslopes = [slopes[h] for h in GQA_PERM]
    oc = _moba_attention(qc, kc, vc, slopes)
    k_cmp, v_cmp = _compress(kcmp, vcmp, cmp_pos, cmp_w, d_k_norm[0])
    o_cmp, sel = _nsa_cmp(qd, k_cmp, v_cmp, gslopes)
    o_slc = _nsa_slc(qd, kslc, vslc, sel, gslopes)
    o_win = _band_attention(qd, kwin, vwin, window=D_WINDOW, slopes=gslopes)
    t = b * s
    w1 = w_out[:MIX].astype(BF16)
    w2 = w_out[MIX:][gq].astype(BF16)
    flat = lambda a: a.reshape(t, a.shape[-1])
    return [flat(oc), flat(o_cmp), flat(o_slc), flat(o_win), gates], w1, w2


def kernel(x, ev_norm, ev_w_in, ev_q_norm, ev_k_norm, ev_sink, ev_w_out, od_norm, od_w_in, od_c_q_norm,
           od_c_k_norm, od_d_q_norm, od_d_k_norm, od_cmp_pos, od_cmp_w, od_w_out, moe_norm, moe_w_grp,
           moe_b_grp, moe_w_exp, moe_b_exp, moe_w_gate, moe_w_up, moe_w_down):
    b, s, d = x.shape
    x2 = x.reshape(b * s, d)
    depth = moe_norm.shape[0]
    for layer in range(depth):
        i = layer // 2
        if layer % 2 == 0:
            attn, w1, w2 = _even_attn(x2, b, s, ev_norm[i], ev_w_in[i], ev_q_norm[i], ev_k_norm[i],
                                      ev_sink[i], ev_w_out[i])
        else:
            attn, w1, w2 = _odd_attn(x2, b, s, od_norm[i], od_w_in[i], od_c_q_norm[i], od_c_k_norm[i],
                                     od_d_q_norm[i], od_d_k_norm[i], od_cmp_pos[i], od_cmp_w[i], od_w_out[i])
        x2 = _moe_block(x2, attn, w1, w2, moe_norm[layer], moe_w_grp[layer], moe_b_grp[layer],
                        moe_w_exp[layer], moe_b_exp[layer], moe_w_gate[layer], moe_w_up[layer],
                        moe_w_down[layer])
    return x2.reshape(b, s, d)
```

```python
import functools
import math

import numpy as np
import jax
import jax.numpy as jnp
from jax import lax
from jax.experimental import pallas as pl
from jax.experimental.pallas import tpu as pltpu

F32 = jnp.float32
BF16 = jnp.bfloat16

LANES = 128
HEAD_DIM = 64
HEADS_PER_CHUNK = LANES // HEAD_DIM
D_MODEL = 1024
N_HEADS = 8
N_KV = 2
GROUP = N_HEADS // N_KV
N_CHUNKS = N_HEADS // HEADS_PER_CHUNK
MIX = N_HEADS * HEAD_DIM
SCALE = 1.0 / math.sqrt(HEAD_DIM)
EPS = 1e-6
NEG = -1e30

A_WINDOW = 128
C_BLOCK = 256
C_TOPK = 3
CMP_LEN = 32
CMP_STRIDE = 16
SLC_BLOCK = 64
SLC_TOPN = 16
D_WINDOW = 512
FORCE_BONUS = 1000.0

N_GROUPS = 4
EXPERTS_PER_GROUP = 8
N_EXPERTS = N_GROUPS * EXPERTS_PER_GROUP
EXPERT_FF = 256

VMEM_LIMIT = 48 * 1024 * 1024

GQA_PERM = tuple(h for c in range(N_CHUNKS) for h in (c, c + GROUP))
MHA_PERM = tuple(range(N_HEADS))


def _alibi(n_heads):
    return [float(2.0 ** (-8.0 * (i + 1) / n_heads)) for i in range(n_heads)]


def _head_cols(perm):
    return np.concatenate([np.arange(h * HEAD_DIM, (h + 1) * HEAD_DIM) for h in perm])


def _cparams(*sem):
    return pltpu.CompilerParams(dimension_semantics=sem, vmem_limit_bytes=VMEM_LIMIT)


def _dot(a, b):
    return jnp.dot(a, b, preferred_element_type=F32)


def _dot_nt(a, b):
    return lax.dot_general(a, b, (((1,), (1,)), ((), ())), preferred_element_type=F32)


def _split(x):
    hi = x.astype(BF16)
    lo = (x - hi.astype(F32)).astype(BF16)
    return hi, lo


def _dot_hilo(a, b):
    hi, lo = _split(a)
    return _dot(hi, b) + _dot(lo, b)


def _iota(shape, dim):
    return lax.broadcasted_iota(jnp.int32, shape, dim)


def _half_masks(dtype):
    lane = _iota((1, LANES), 1)
    return [(lane // HEAD_DIM == p).astype(dtype) for p in range(HEADS_PER_CHUNK)]


def _head_mean_sq(y):
    same = (_iota((LANES, LANES), 0) // HEAD_DIM == _iota((LANES, LANES), 1) // HEAD_DIM)
    return _dot_hilo(y * y, same.astype(BF16)) * (1.0 / HEAD_DIM)


def _proj_kernel(x_ref, g_ref, w_ref, gain_ref, *out_refs, plan):
    x = x_ref[...]
    ms = jnp.mean(x * x, axis=-1, keepdims=True)
    xn = (x * lax.rsqrt(ms + EPS) * g_ref[...]).astype(BF16)
    for o_ref, (col0, width, op) in zip(out_refs, plan):
        for a in range(0, width, 2 * LANES):
            wd = min(2 * LANES, width - a)
            y = _dot(xn, w_ref[:, col0 + a:col0 + a + wd])
            for c in range(0, wd, LANES):
                yc = y[:, c:c + LANES]
                if op == "norm":
                    gain = gain_ref[:, col0 + a + c:col0 + a + c + LANES]
                    yc = yc * lax.rsqrt(_head_mean_sq(yc) + EPS) * gain
                elif op == "sigmoid":
                    yc = jax.nn.sigmoid(yc)
                o_ref[:, a + c:a + c + LANES] = yc.astype(o_ref.dtype)


def _proj(x2, norm_g, w, gain, plan, out_dtypes, tm=512):
    t, d = x2.shape
    n = w.shape[1]
    out_shape = [jax.ShapeDtypeStruct((t, width), dt) for (_, width, _), dt in zip(plan, out_dtypes)]
    return pl.pallas_call(
        functools.partial(_proj_kernel, plan=plan),
        out_shape=out_shape,
        grid=(t // tm,),
        in_specs=[
            pl.BlockSpec((tm, d), lambda i: (i, 0)),
            pl.BlockSpec((1, d), lambda i: (0, 0)),
            pl.BlockSpec((d, n), lambda i: (0, 0)),
            pl.BlockSpec((1, n), lambda i: (0, 0)),
        ],
        out_specs=[pl.BlockSpec((tm, width), lambda i: (i, 0)) for (_, width, _) in plan],
        compiler_params=_cparams("parallel"),
        name="norm_proj",
    )(x2, norm_g.reshape(1, d), w, gain)


def _stack_group(q_ref, halfmask):
    return jnp.concatenate(
        [q_ref[:, c * LANES:(c + 1) * LANES] * halfmask for c in range(N_CHUNKS)], axis=0)


def _per_chunk_rows(rows, tq, values):
    r = _iota((rows, 1), 0) // tq
    out = jnp.full((rows, 1), values[N_CHUNKS - 1], F32)
    for c in range(N_CHUNKS - 2, -1, -1):
        out = jnp.where(r == c, values[c], out)
    return out


def _band_kernel(*refs, tq, window, wpad, slopes, has_sink):
    if has_sink:
        sink_ref, q_ref, k_ref, v_ref, o_ref = refs
    else:
        q_ref, k_ref, v_ref, o_ref = refs
    q0 = pl.program_id(1) * tq
    kw = tq + wpad
    kstart = pl.multiple_of(jnp.maximum(q0 - wpad, 0), LANES)
    ks = k_ref[pl.ds(kstart, kw), :]
    vs = v_ref[pl.ds(kstart, kw), :]
    rows = N_CHUNKS * tq
    qpos = q0 + _iota((rows, kw), 0) % tq
    kpos = kstart + _iota((rows, kw), 1)
    diff = qpos - kpos
    mask = (diff >= 0) & (diff < window)
    diff_f = diff.astype(F32)
    halves = range(HEADS_PER_CHUNK)
    hms = _half_masks(BF16)
    raw = [_dot_nt(_stack_group(q_ref, hms[p] * SCALE), ks) for p in halves]
    es, dens = [], []
    for p in halves:
        slope = _per_chunk_rows(rows, tq, [slopes[2 * c + p] for c in range(N_CHUNKS)])
        s = jnp.where(mask, raw[p] - slope * diff_f, NEG)
        mx = jnp.max(s, axis=-1, keepdims=True)
        if has_sink:
            sk = _per_chunk_rows(rows, tq, [sink_ref[2 * c + p] for c in range(N_CHUNKS)])
            mx = jnp.maximum(mx, sk)
        e = jnp.exp(s - mx)
        den = jnp.sum(e, axis=-1, keepdims=True)
        if has_sink:
            den = den + jnp.exp(sk - mx)
        es.append(e.astype(BF16))
        dens.append(den)
    os_ = [_dot(es[p], vs) / dens[p] for p in halves]
    _store_group(o_ref, os_, tq)


def _store_group(o_ref, os_, tq):
    lane = _iota((1, LANES), 1)
    for c in range(N_CHUNKS):
        o = jnp.where(lane < HEAD_DIM, os_[0][c * tq:(c + 1) * tq], os_[1][c * tq:(c + 1) * tq])
        o_ref[:, c * LANES:(c + 1) * LANES] = o.astype(o_ref.dtype)


def _band_attention(q, k, v, *, window, slopes, sink=None, tq=128):
    b, s, _ = q.shape
    wpad = -(-window // LANES) * LANES
    assert s >= tq + wpad and s % tq == 0
    kern = functools.partial(_band_kernel, tq=tq, window=window, wpad=wpad,
                             slopes=slopes, has_sink=sink is not None)
    in_specs = [
        pl.BlockSpec((None, tq, MIX), lambda bi, n: (bi, n, 0)),
        pl.BlockSpec((None, s, LANES), lambda bi, n: (bi, 0, 0)),
        pl.BlockSpec((None, s, LANES), lambda bi, n: (bi, 0, 0)),
    ]
    args = [q, k, v]
    if sink is not None:
        in_specs = [pl.BlockSpec(memory_space=pltpu.SMEM)] + in_specs
        args = [sink] + args
    return pl.pallas_call(
        kern,
        out_shape=jax.ShapeDtypeStruct((b, s, MIX), BF16),
        grid=(b, s // tq),
        in_specs=in_specs,
        out_specs=pl.BlockSpec((None, tq, MIX), lambda bi, n: (bi, n, 0)),
        compiler_params=_cparams("parallel", "parallel"),
        name="band_attention",
    )(*args)


def _stick_kernel(q_ref, k_ref, v_ref, o_ref, acc_ref, run_ref, *, tq, cpb):
    i = pl.program_id(2)
    q0 = pl.multiple_of(i * tq, tq)
    hms = _half_masks(BF16)
    heads = [(cc, p) for cc in range(cpb) for p in range(HEADS_PER_CHUNK)]
    qs = [q_ref[:, cc * LANES:(cc + 1) * LANES] * (hms[p] * SCALE) for cc, p in heads]
    upper = (_iota((tq, tq), 0) > _iota((tq, tq), 1)).astype(BF16)

    def block(kstart, diag):
        if diag:
            causal = _iota((tq, tq), 1) < _iota((tq, tq), 0)
        kjs = [k_ref[pl.ds(kstart, tq), cc * LANES:(cc + 1) * LANES] for cc in range(cpb)]
        vjs = [v_ref[pl.ds(kstart, tq), cc * LANES:(cc + 1) * LANES] for cc in range(cpb)]
        zs = [_dot_nt(qs[h], kjs[cc]) for h, (cc, p) in enumerate(heads)]
        lss, lks = [], []
        for z in zs:
            ls = jnp.minimum(z, 0.0) - jnp.log(1.0 + jnp.exp(-jnp.abs(z)))
            lk = ls - z
            if diag:
                lk = jnp.where(causal, lk, 0.0)
            lss.append(ls)
            lks.append(lk)
        sufs = [_dot_hilo(lk, upper) for lk in lks]
        ws = []
        for h in range(len(heads)):
            if diag:
                a = jnp.where(causal, jnp.exp(lss[h] + sufs[h]), 0.0)
            else:
                a = jnp.exp(lss[h] + sufs[h] + run_ref[h])
            ws.append(a.astype(BF16))
        pvs = [_dot(ws[h], vjs[cc]) for h, (cc, p) in enumerate(heads)]
        for h in range(len(heads)):
            rowsum = jnp.sum(lks[h], axis=-1, keepdims=True)
            if diag:
                acc_ref[h] = pvs[h]
                run_ref[h] = rowsum
            else:
                acc_ref[h] += pvs[h]
                run_ref[h] += rowsum

    block(q0, True)

    def body(t, carry):
        block(pl.multiple_of((i - 1 - t) * tq, tq), False)
        return carry

    lax.fori_loop(0, i, body, 0)
    lane = _iota((1, LANES), 1)
    for cc in range(cpb):
        o = jnp.where(lane < HEAD_DIM, acc_ref[HEADS_PER_CHUNK * cc], acc_ref[HEADS_PER_CHUNK * cc + 1])
        o_ref[:, cc * LANES:(cc + 1) * LANES] = o.astype(o_ref.dtype)


def _stick_attention(q, k, v, tq=256, cpb=2):
    b, s, _ = q.shape
    wide = cpb * LANES
    n_heads = cpb * HEADS_PER_CHUNK
    spec_q = pl.BlockSpec((None, tq, wide), lambda bi, c, i: (bi, i, c))
    spec_kv = pl.BlockSpec((None, s, wide), lambda bi, c, i: (bi, 0, c))
    return pl.pallas_call(
        functools.partial(_stick_kernel, tq=tq, cpb=cpb),
        out_shape=jax.ShapeDtypeStruct((b, s, MIX), BF16),
        grid=(b, N_CHUNKS // cpb, s // tq),
        in_specs=[spec_q, spec_kv, spec_kv],
        out_specs=spec_q,
        scratch_shapes=[pltpu.VMEM((n_heads, tq, LANES), F32), pltpu.VMEM((n_heads, tq, 1), F32)],
        compiler_params=_cparams("parallel", "parallel", "parallel"),
        name="stick_breaking",
    )(q, k, v)


def _route(logits):
    lane = _iota(logits.shape, 1)
    lane_f = lane.astype(F32)
    ninf = -jnp.inf
    is_g = (lane >= N_EXPERTS) & (lane < N_EXPERTS + N_GROUPS)
    gmax = jnp.max(jnp.where(is_g, logits, ninf), axis=-1, keepdims=True)
    gidx = jnp.min(jnp.where(is_g & (logits == gmax), lane_f - N_EXPERTS, 1e9), axis=-1, keepdims=True)
    p_g = 1.0 / jnp.sum(jnp.where(is_g, jnp.exp(logits - gmax), 0.0), axis=-1, keepdims=True)
    in_grp = (lane < N_EXPERTS) & ((lane // EXPERTS_PER_GROUP).astype(F32) == gidx)
    le = jnp.where(in_grp, logits, ninf)
    m1 = jnp.max(le, axis=-1, keepdims=True)
    i1 = jnp.min(jnp.where(le == m1, lane_f, 1e9), axis=-1, keepdims=True)
    le2 = jnp.where(lane_f == i1, ninf, le)
    m2 = jnp.max(le2, axis=-1, keepdims=True)
    i2 = jnp.min(jnp.where(le2 == m2, lane_f, 1e9), axis=-1, keepdims=True)
    e2 = jnp.exp(m2 - m1)
    w1 = p_g / (1.0 + e2)
    w2 = p_g * e2 / (1.0 + e2)
    return jnp.where(lane_f == i1, w1, 0.0) + jnp.where(lane_f == i2, w2, 0.0)


def _gate_expand(branch):
    r = _iota((LANES, MIX), 0)
    col = _iota((LANES, MIX), 1)
    head = col // LANES + GROUP * ((col % LANES) // HEAD_DIM)
    return (r == 3 * head + branch).astype(BF16)


def _out_kernel(*refs, nsa):
    if nsa:
        (x_ref, o1_ref, ocmp_ref, oslc_ref, owin_ref, gates_ref, w1_ref, w2_ref,
         ng_ref, wr_ref, br_ref, x1_ref, h_ref, comb_ref) = refs
        g = gates_ref[...]
        o2 = (_dot_hilo(g, _gate_expand(0)) * ocmp_ref[...]
              + _dot_hilo(g, _gate_expand(1)) * oslc_ref[...]
              + _dot_hilo(g, _gate_expand(2)) * owin_ref[...]).astype(BF16)
    else:
        (x_ref, o1_ref, o2_ref, w1_ref, w2_ref,
         ng_ref, wr_ref, br_ref, x1_ref, h_ref, comb_ref) = refs
        o2 = o2_ref[...]
    x1 = x_ref[...] + _dot(o1_ref[...], w1_ref[...]) + _dot(o2, w2_ref[...])
    x1_ref[...] = x1
    ms = jnp.mean(x1 * x1, axis=-1, keepdims=True)
    h = x1 * lax.rsqrt(ms + EPS) * ng_ref[...]
    h_ref[...] = h.astype(BF16)
    h_hi, h_lo = _split(h)
    w_hi, w_lo = _split(wr_ref[...])
    logits = _dot(h_hi, w_hi) + (_dot(h_hi, w_lo) + _dot(h_lo, w_hi)) + br_ref[...]
    comb_ref[...] = _route(logits)


def _out_proj_route(x2, attn, w1, w2, moe_g, w_route, b_route, tm=512):
    t, d = x2.shape
    nsa = len(attn) > 2
    row = lambda width: pl.BlockSpec((tm, width), lambda i: (i, 0))
    full = lambda a: pl.BlockSpec(a.shape, lambda i: (0, 0))
    consts = [w1, w2, moe_g.reshape(1, d), w_route, b_route]
    return pl.pallas_call(
        functools.partial(_out_kernel, nsa=nsa),
        out_shape=[jax.ShapeDtypeStruct((t, d), F32), jax.ShapeDtypeStruct((t, d), BF16),
                   jax.ShapeDtypeStruct((t, LANES), F32)],
        grid=(t // tm,),
        in_specs=[row(d)] + [row(a.shape[1]) for a in attn] + [full(a) for a in consts],
        out_specs=[row(d), row(d), row(LANES)],
        compiler_params=_cparams("parallel"),
        name="out_proj_route",
    )(x2, *attn, *consts)


def _moe_kernel(h_ref, comb_ref, x1_ref, wgu_ref, wd_ref, o_ref):
    e = pl.program_id(1)

    @pl.when(e == 0)
    def _():
        o_ref[...] = x1_ref[...]

    gu = _dot(h_ref[...], wgu_ref[...])
    g = gu[:, :EXPERT_FF]
    u = gu[:, EXPERT_FF:]
    act = (g * jax.nn.sigmoid(g) * u).astype(BF16)
    y = _dot(act, wd_ref[...])
    comb = comb_ref[...]
    c = jnp.sum(jnp.where(_iota(comb.shape, 1) == e, comb, 0.0), axis=-1, keepdims=True)
    o_ref[...] += c * y


def _moe(h, comb, x1, wgu, wd, tm=1024):
    t, d = h.shape
    assert t % tm == 0
    return pl.pallas_call(
        _moe_kernel,
        out_shape=jax.ShapeDtypeStruct((t, d), F32),
        grid=(t // tm, N_EXPERTS),
        in_specs=[
            pl.BlockSpec((tm, d), lambda i, e: (i, 0)),
            pl.BlockSpec((tm, LANES), lambda i, e: (i, 0)),
            pl.BlockSpec((tm, d), lambda i, e: (i, 0)),
            pl.BlockSpec((None, d, 2 * EXPERT_FF), lambda i, e: (e, 0, 0)),
            pl.BlockSpec((None, EXPERT_FF, d), lambda i, e: (e, 0, 0)),
        ],
        out_specs=pl.BlockSpec((tm, d), lambda i, e: (i, 0)),
        compiler_params=_cparams("parallel", "arbitrary"),
        name="moe_experts",
    )(h, comb, x1, wgu, wd)


def _moe_block(x2, attn, w1, w2, moe_g, w_grp, b_grp, w_exp, b_exp, w_gate, w_up, w_down):
    d = x2.shape[1]
    pad = LANES - N_EXPERTS - N_GROUPS
    w_route = jnp.concatenate([w_exp, w_grp, jnp.zeros((d, pad), F32)], axis=1)
    b_route = jnp.concatenate([b_exp, b_grp, jnp.zeros((pad,), F32)]).reshape(1, LANES)
    x1, h, comb = _out_proj_route(x2, attn, w1, w2, moe_g, w_route, b_route)
    wgu = jnp.concatenate([w_gate, w_up], axis=2).astype(BF16)
    return _moe(h, comb, x1, wgu, w_down.astype(BF16))


def _even_attn(x2, b, s, norm_g, w_in, q_norm, k_norm, sink, w_out):
    gq = _head_cols(GQA_PERM)
    w = jnp.concatenate([w_in[:, :MIX][:, gq], w_in[:, MIX:]], axis=1).astype(BF16)
    n = w.shape[1]
    kvw = N_KV * HEAD_DIM
    gain = jnp.concatenate([jnp.tile(q_norm, N_HEADS), jnp.tile(k_norm, N_KV),
                            jnp.ones((n - MIX - kvw,), F32)]).reshape(1, n)
    plan, col = [], 0
    for width, op in ((MIX, "norm"), (kvw, "norm"), (kvw, None), (MIX, None), (MIX, None), (MIX, None)):
        plan.append((col, width, op))
        col += width
    qa, ka, va, qb, kb, vb = _proj(x2, norm_g, w, gain, tuple(plan), [BF16] * 6)
    r3 = lambda a: a.reshape(b, s, a.shape[-1])
    slopes = _alibi(N_HEADS)
    oa = _band_attention(r3(qa), r3(ka), r3(va), window=A_WINDOW,
                         slopes=[slopes[h] for h in GQA_PERM], sink=sink[np.asarray(GQA_PERM)])
    ob = _stick_attention(r3(qb), r3(kb), r3(vb))
    w1 = w_out[:MIX][gq].astype(BF16)
    w2 = w_out[MIX:].astype(BF16)
    t = b * s
    return [oa.reshape(t, MIX), ob.reshape(t, MIX)], w1, w2


def _rank_rows(score, j, n, seg):
    row = _iota(score.shape, 0)
    rank = jnp.zeros(score.shape, F32)
    for jj in range(n):
        other = score[jj:jj + 1, :]
        for sgm in range(1, score.shape[0] // seg):
            other = jnp.where(row // seg == sgm, score[sgm * seg + jj:sgm * seg + jj + 1, :], other)
        beats = (other > score) | ((other == score) & (jj < j))
        rank = rank + jnp.where(beats, 1.0, 0.0)
    return rank


def _ones_beside(v, p):
    lane = _iota((1, LANES), 1)
    return jnp.where(lane // HEAD_DIM == p, v, jnp.ones_like(v))


def _normalize(acc):
    return acc / pltpu.roll(acc, HEAD_DIM, 1)


def _moba_kernel(slope_ref, q_ref, k_ref, v_ref, o_ref, kmean_ref, m_ref, acc_ref, *,
                 blk, nblk, topk, cpb):
    g = pl.program_id(1)
    i = pl.program_id(2)
    s_len = k_ref.shape[0]
    heads = [(cc, p) for cc in range(cpb) for p in range(HEADS_PER_CHUNK)]
    nh = len(heads)
    chunk = lambda cc: slice(cc * LANES, (cc + 1) * LANES)

    @pl.when(i == 0)
    def _():
        member = (_iota((LANES, s_len), 1) // blk == _iota((LANES, s_len), 0)).astype(BF16)
        for cc in range(cpb):
            kmean_ref[cc] = _dot(member, k_ref[:, chunk(cc)]) * (1.0 / blk)

    hms = _half_masks(BF16)
    q0 = pl.multiple_of(i * blk, blk)
    rel = (_iota((blk, blk), 0) - _iota((blk, blk), 1)).astype(F32)
    slopes = [slope_ref[HEADS_PER_CHUNK * (g * cpb + cc) + p] for cc, p in heads]
    kms = [_split(kmean_ref[cc]) for cc in range(cpb)]
    gates = [_dot_nt(kms[cc][0], q_ref[:, chunk(cc)] * hms[p]) + _dot_nt(kms[cc][1], q_ref[:, chunk(cc)] * hms[p])
             for cc, p in heads]
    qs = [q_ref[:, chunk(cc)] * (hms[p] * SCALE) for cc, p in heads]
    raw = [_dot_nt(qs[h], k_ref[pl.ds(q0, blk), chunk(cc)]) for h, (cc, p) in enumerate(heads)]
    nrow = -(-nblk // 8) * 8
    blk_id = _iota((nrow, blk), 0)
    sels = []
    for h in range(nh):
        gate = jnp.where(blk_id < i, gates[h][:nrow], NEG)
        rank = _rank_rows(gate, blk_id, nblk, nrow)
        sel_t = jnp.where((rank < topk) & (blk_id < i), 1.0, 0.0)
        sel_t = jnp.concatenate([sel_t, jnp.zeros((LANES - nrow, blk), F32)], axis=0)
        sels.append(sel_t.T.astype(BF16))
    own_bias = jnp.where(rel >= 0, 0.0, NEG)
    es = []
    for h in range(nh):
        s = raw[h] - slopes[h] * rel + own_bias
        m = jnp.max(s, axis=-1, keepdims=True)
        m_ref[h] = m
        es.append(jnp.exp(s - m).astype(BF16))
    for h, (cc, p) in enumerate(heads):
        acc_ref[h] = _dot(es[h], _ones_beside(v_ref[pl.ds(q0, blk), chunk(cc)], p))

    def body(j, carry):
        k0 = pl.multiple_of(j * blk, blk)
        raw = [_dot_nt(qs[h], k_ref[pl.ds(k0, blk), chunk(cc)]) for h, (cc, p) in enumerate(heads)]
        onehot = (_iota((LANES, blk), 0) == j).astype(BF16)
        picked = [_dot(sels[h], onehot) for h in range(nh)]
        diff = rel + ((i - j) * blk).astype(F32)
        es, alphas = [], []
        for h in range(nh):
            s = raw[h] - slopes[h] * diff + jnp.where(picked[h] > 0.5, 0.0, NEG)
            m_old = m_ref[h]
            m_new = jnp.maximum(m_old, jnp.max(s, axis=-1, keepdims=True))
            m_ref[h] = m_new
            es.append(jnp.exp(s - m_new).astype(BF16))
            alphas.append(jnp.exp(m_old - m_new))
        pvs = [_dot(es[h], _ones_beside(v_ref[pl.ds(k0, blk), chunk(cc)], p)) for h, (cc, p) in enumerate(heads)]
        for h in range(nh):
            acc_ref[h] = alphas[h] * acc_ref[h] + pvs[h]
        return carry

    lax.fori_loop(0, i, body, 0)
    lane = _iota((1, LANES), 1)
    for cc in range(cpb):
        h0 = HEADS_PER_CHUNK * cc
        o = jnp.where(lane < HEAD_DIM, _normalize(acc_ref[h0]), _normalize(acc_ref[h0 + 1]))
        o_ref[:, chunk(cc)] = o.astype(o_ref.dtype)


def _moba_attention(q, k, v, slopes, cpb=2):
    b, s, _ = q.shape
    assert s % C_BLOCK == 0
    nblk = s // C_BLOCK
    assert nblk <= LANES
    wide = cpb * LANES
    nh = cpb * HEADS_PER_CHUNK
    spec_q = pl.BlockSpec((None, C_BLOCK, wide), lambda bi, c, i, sl: (bi, i, c))
    spec_kv = pl.BlockSpec((None, s, wide), lambda bi, c, i, sl: (bi, 0, c))
    return pl.pallas_call(
        functools.partial(_moba_kernel, blk=C_BLOCK, nblk=nblk, topk=min(C_TOPK, nblk), cpb=cpb),
        out_shape=jax.ShapeDtypeStruct((b, s, MIX), BF16),
        grid_spec=pltpu.PrefetchScalarGridSpec(
            num_scalar_prefetch=1,
            grid=(b, N_CHUNKS // cpb, nblk),
            in_specs=[spec_q, spec_kv, spec_kv],
            out_specs=spec_q,
            scratch_shapes=[pltpu.VMEM((cpb, LANES, LANES), F32), pltpu.VMEM((nh, C_BLOCK, 1), F32),
                            pltpu.VMEM((nh, C_BLOCK, LANES), F32)],
        ),
        compiler_params=_cparams("parallel", "parallel", "arbitrary"),
        name="moba_attention",
    )(jnp.asarray(slopes, F32), q, k, v)


def _compress_kernel(xk_ref, xv_ref, wk_lo_ref, wk_hi_ref, wv_lo_ref, wv_hi_ref,
                     pk_lo_ref, pk_hi_ref, pv_lo_ref, pv_hi_ref, gain_ref, kc_ref, vc_ref):
    def compress(x_ref, w_lo_ref, w_hi_ref, p_lo_ref, p_hi_ref):
        x = x_ref[...]
        nrow = x.shape[0]
        first = _dot(x, w_lo_ref[...])
        second = pltpu.roll(_dot(x, w_hi_ref[...]), nrow - 1, 0)
        p_lo = jnp.broadcast_to(p_lo_ref[...], (8, p_lo_ref.shape[1]))
        p_hi = jnp.broadcast_to(p_hi_ref[...], (8, p_hi_ref.shape[1]))
        bias = _dot_hilo(p_lo, w_lo_ref[...]) + _dot_hilo(p_hi, w_hi_ref[...])
        return first + second + bias[0:1]

    kc = compress(xk_ref, wk_lo_ref, wk_hi_ref, pk_lo_ref, pk_hi_ref)
    kc = kc * lax.rsqrt(_head_mean_sq(kc) + EPS) * gain_ref[...]
    kc_ref[...] = kc.astype(kc_ref.dtype)
    vc_ref[...] = compress(xv_ref, wv_lo_ref, wv_hi_ref, pv_lo_ref, pv_hi_ref).astype(vc_ref.dtype)


def _compress_weights(pos, w):
    half = CMP_LEN // 2
    eye = jnp.eye(N_KV, dtype=F32)
    wd = jnp.einsum("gh,lde->lgdhe", eye, w).reshape(CMP_LEN, LANES, LANES)
    w_lo = wd[:half].reshape(half * LANES, LANES).astype(BF16)
    w_hi = wd[half:].reshape(half * LANES, LANES).astype(BF16)
    pt = jnp.tile(pos, (1, N_KV))
    return w_lo, w_hi, pt[:half].reshape(1, half * LANES), pt[half:].reshape(1, half * LANES)


def _compress(kcmp, vcmp, cmp_pos, cmp_w, k_gain):
    b, s, _ = kcmp.shape
    assert CMP_LEN == 2 * CMP_STRIDE and s % CMP_STRIDE == 0
    nrow = s // CMP_STRIDE
    wide = CMP_STRIDE * LANES
    xk = kcmp.reshape(b, nrow, wide)
    xv = vcmp.reshape(b, nrow, wide)
    wk = _compress_weights(cmp_pos[0], cmp_w[0])
    wv = _compress_weights(cmp_pos[1], cmp_w[1])
    consts = [wk[0], wk[1], wv[0], wv[1], wk[2], wk[3], wv[2], wv[3],
              jnp.tile(k_gain, N_KV).reshape(1, LANES)]
    spec_x = pl.BlockSpec((None, nrow, wide), lambda bi: (bi, 0, 0))
    spec_o = pl.BlockSpec((None, nrow, LANES), lambda bi: (bi, 0, 0))
    return pl.pallas_call(
        _compress_kernel,
        out_shape=[jax.ShapeDtypeStruct((b, nrow, LANES), BF16)] * 2,
        grid=(b,),
        in_specs=[spec_x, spec_x] + [pl.BlockSpec(a.shape, lambda bi: (0, 0)) for a in consts],
        out_specs=[spec_o, spec_o],
        compiler_params=_cparams("parallel"),
        name="nsa_compress",
    )(xk, xv, *consts)


def _nsa_cmp_kernel(q_ref, kc_ref, vc_ref, o_ref, sel_ref, *, tq, n_cmp, n_slc, topn, slopes):
    q0 = pl.program_id(1) * tq
    kc = kc_ref[...]
    vc = vc_ref[...]
    ncp = kc.shape[0]
    rows = N_CHUNKS * tq
    seg = LANES // N_KV
    qpos = q0 + _iota((rows, ncp), 0) % tq
    ncol = _iota((rows, ncp), 1)
    diff = qpos - (ncol * CMP_STRIDE + CMP_LEN - 1)
    mask = (diff >= 0) & (ncol < n_cmp)
    diff_f = diff.astype(F32)
    orow = _iota((LANES, ncp), 0)
    cst = _iota((LANES, ncp), 1) * CMP_STRIDE
    sst = (orow % seg) * SLC_BLOCK
    overlap = (cst < sst + SLC_BLOCK) & (cst + CMP_LEN > sst) & (orow % seg < n_slc)
    halves = range(HEADS_PER_CHUNK)
    hms = _half_masks(BF16)
    raw = [_dot_nt(_stack_group(q_ref, hms[p] * SCALE), kc) for p in halves]
    pcs = []
    for p in halves:
        slope = _per_chunk_rows(rows, tq, [slopes[2 * c + p] for c in range(N_CHUNKS)])
        sc = jnp.where(mask, raw[p] - slope * diff_f, NEG)
        mx = jnp.max(sc, axis=-1, keepdims=True)
        e = jnp.where(mask, jnp.exp(sc - mx), 0.0)
        den = jnp.sum(e, axis=-1, keepdims=True)
        pcs.append(e / jnp.where(den > 0, den, 1.0))
    _store_group(o_ref, [_dot(pcs[p].astype(BF16), vc) for p in halves], tq)
    p_slc = jnp.zeros((LANES, tq), F32)
    for p in halves:
        pg = pcs[p][0:tq]
        for c in range(1, N_CHUNKS):
            pg = pg + pcs[p][c * tq:(c + 1) * tq]
        ov = (overlap & (orow // seg == p)).astype(BF16)
        pg_hi, pg_lo = _split(pg)
        p_slc = p_slc + (_dot_nt(ov, pg_hi) + _dot_nt(ov, pg_lo))
    j = _iota((LANES, tq), 0) % seg
    cur = (q0 + _iota((LANES, tq), 1)) // SLC_BLOCK
    forced = (j == 0) | (j == cur) | (j == cur - 1)
    usable = (j <= cur) & (j < n_slc)
    score = jnp.where(usable, p_slc + jnp.where(forced, FORCE_BONUS, 0.0), NEG)
    rank = _rank_rows(score, j, n_slc, seg)
    sel_t = jnp.where((rank < topn) & usable, 1.0, 0.0)
    sel_ref[...] = sel_t.T.astype(sel_ref.dtype)


def _nsa_cmp(q, kc, vc, slopes, tq=128):
    b, s, _ = q.shape
    ncp = kc.shape[1]
    n_slc = s // SLC_BLOCK
    assert n_slc <= LANES // N_KV and ncp % LANES == 0
    kern = functools.partial(_nsa_cmp_kernel, tq=tq, n_cmp=ncp - 1, n_slc=n_slc,
                             topn=min(SLC_TOPN, n_slc), slopes=slopes)
    spec_c = pl.BlockSpec((None, ncp, LANES), lambda bi, n: (bi, 0, 0))
    return pl.pallas_call(
        kern,
        out_shape=[jax.ShapeDtypeStruct((b, s, MIX), BF16), jax.ShapeDtypeStruct((b, s, LANES), BF16)],
        grid=(b, s // tq),
        in_specs=[pl.BlockSpec((None, tq, MIX), lambda bi, n: (bi, n, 0)), spec_c, spec_c],
        out_specs=[pl.BlockSpec((None, tq, MIX), lambda bi, n: (bi, n, 0)),
                   pl.BlockSpec((None, tq, LANES), lambda bi, n: (bi, n, 0))],
        compiler_params=_cparams("parallel", "parallel"),
        name="nsa_compressed",
    )(q, kc, vc)


def _nsa_slc_kernel(q_ref, k_ref, v_ref, sel_ref, o_ref, m_ref, acc_ref, *, tq, tk, slopes):
    n = pl.program_id(1)
    q0 = n * tq
    rows = N_CHUNKS * tq
    seg = LANES // N_KV
    halves = range(HEADS_PER_CHUNK)
    sel = sel_ref[...]
    rel = _iota((tq, tk), 0) - _iota((tq, tk), 1)
    hms = _half_masks(BF16)
    qs = [_stack_group(q_ref, hms[p] * SCALE) for p in halves]
    for p in halves:
        m_ref[p] = jnp.full((rows, 1), 0.1 * NEG, F32)
        acc_ref[p] = jnp.zeros((rows, LANES), F32)

    def body(kt, carry):
        k0 = pl.multiple_of(kt * tk, tk)
        kj = k_ref[pl.ds(k0, tk), :]
        vj = v_ref[pl.ds(k0, tk), :]
        raw = [_dot_nt(qs[p], kj) for p in halves]
        blk_of_key = (k0 + _iota((LANES, tk), 1)) // SLC_BLOCK
        picked = [_dot(sel, (_iota((LANES, tk), 0) == p * seg + blk_of_key).astype(BF16)) for p in halves]
        diff = rel + (q0 - k0)
        diff_f = diff.astype(F32)
        es, alphas = [], []
        for p in halves:
            bias = jnp.where((picked[p] > 0.5) & (diff >= 0), 0.0, NEG)
            s = jnp.concatenate([raw[p][c * tq:(c + 1) * tq] - slopes[2 * c + p] * diff_f + bias
                                 for c in range(N_CHUNKS)], axis=0)
            m_old = m_ref[p]
            m_new = jnp.maximum(m_old, jnp.max(s, axis=-1, keepdims=True))
            m_ref[p] = m_new
            es.append(jnp.exp(s - m_new).astype(BF16))
            alphas.append(jnp.exp(m_old - m_new))
        pvs = [_dot(es[p], _ones_beside(vj, p)) for p in halves]
        for p in halves:
            acc_ref[p] = alphas[p] * acc_ref[p] + pvs[p]
        return carry

    lax.fori_loop(0, q0 // tk + 1, body, 0)
    _store_group(o_ref, [_normalize(acc_ref[p]) for p in halves], tq)


def _nsa_slc(q, k, v, sel, slopes, tq=128, tk=256):
    b, s, _ = q.shape
    assert tk % tq == 0 and tq % SLC_BLOCK == 0 and s % tk == 0
    rows = N_CHUNKS * tq
    spec_q = pl.BlockSpec((None, tq, MIX), lambda bi, n: (bi, n, 0))
    spec_kv = pl.BlockSpec((None, s, LANES), lambda bi, n: (bi, 0, 0))
    return pl.pallas_call(
        functools.partial(_nsa_slc_kernel, tq=tq, tk=tk, slopes=slopes),
        out_shape=jax.ShapeDtypeStruct((b, s, MIX), BF16),
        grid=(b, s // tq),
        in_specs=[spec_q, spec_kv, spec_kv, pl.BlockSpec((None, tq, LANES), lambda bi, n: (bi, n, 0))],
        out_specs=spec_q,
        scratch_shapes=[pltpu.VMEM((N_KV, rows, 1), F32), pltpu.VMEM((N_KV, rows, LANES), F32)],
        compiler_params=_cparams("parallel", "parallel"),
        name="nsa_selected",
    )(q, k, v, sel)


def _odd_attn(x2, b, s, norm_g, w_in, c_q_norm, c_k_norm, d_q_norm, d_k_norm, cmp_pos, cmp_w, w_out):
    gq = _head_cols(GQA_PERM)
    kvw = N_KV * HEAD_DIM
    n_gate = N_HEADS * 3
    qd0 = 3 * MIX
    w = jnp.concatenate([w_in[:, :qd0], w_in[:, qd0:qd0 + MIX][:, gq], w_in[:, qd0 + MIX:],
                         jnp.zeros((w_in.shape[0], LANES - n_gate), F32)], axis=1).astype(BF16)
    n = w.shape[1]
    ones = lambda width: jnp.ones((width,), F32)
    gain = jnp.concatenate([
        jnp.tile(c_q_norm, N_HEADS), jnp.tile(c_k_norm, N_HEADS), ones(MIX), jnp.tile(d_q_norm, N_HEADS),
        ones(2 * kvw), jnp.tile(d_k_norm[1], N_KV), ones(kvw), jnp.tile(d_k_norm[2], N_KV), ones(kvw),
        ones(LANES)]).reshape(1, n)
    plan, col = [], 0
    for width, op in ((MIX, "norm"), (MIX, "norm"), (MIX, None), (MIX, "norm"), (kvw, None), (kvw, None),
                      (kvw, "norm"), (kvw, None), (kvw, "norm"), (kvw, None), (LANES, "sigmoid")):
        plan.append((col, width, op))
        col += width
    outs = _proj(x2, norm_g, w, gain, tuple(plan), [BF16] * 10 + [F32])
    r3 = lambda a: a.reshape(b, s, a.shape[-1])
    qc, kc, vc, qd, kcmp, vcmp, kslc, vslc, kwin, vwin = [r3(a) for a in outs[:10]]
    gates = outs[10]
    slopes = _alibi(N_HEADS)
    gslopes = [slopes[h] for h in GQA_PERM]
    oc = _moba_attention(qc, kc, vc, slopes)
    k_cmp, v_cmp = _compress(kcmp, vcmp, cmp_pos, cmp_w, d_k_norm[0])
    o_cmp, sel = _nsa_cmp(qd, k_cmp, v_cmp, gslopes)
    o_slc = _nsa_slc(qd, kslc, vslc, sel, gslopes)
    o_win = _band_attention(qd, kwin, vwin, window=D_WINDOW, slopes=gslopes)
    t = b * s
    w1 = w_out[:MIX].astype(BF16)
    w2 = w_out[MIX:][gq].astype(BF16)
    flat = lambda a: a.reshape(t, a.shape[-1])
    return [flat(oc), flat(o_cmp), flat(o_slc), flat(o_win), gates], w1, w2


def kernel(x, ev_norm, ev_w_in, ev_q_norm, ev_k_norm, ev_sink, ev_w_out, od_norm, od_w_in, od_c_q_norm,
           od_c_k_norm, od_d_q_norm, od_d_k_norm, od_cmp_pos, od_cmp_w, od_w_out, moe_norm, moe_w_grp,
           moe_b_grp, moe_w_exp, moe_b_exp, moe_w_gate, moe_w_up, moe_w_down):
    b, s, d = x.shape
    x2 = x.reshape(b * s, d)
    depth = moe_norm.shape[0]
    for layer in range(depth):
        i = layer // 2
        if layer % 2 == 0:
            attn, w1, w2 = _even_attn(x2, b, s, ev_norm[i], ev_w_in[i], ev_q_norm[i], ev_k_norm[i],
                                      ev_sink[i], ev_w_out[i])
        else:
            attn, w1, w2 = _odd_attn(x2, b, s, od_norm[i], od_w_in[i], od_c_q_norm[i], od_c_k_norm[i],
                                     od_d_q_norm[i], od_d_k_norm[i], od_cmp_pos[i], od_cmp_w[i], od_w_out[i])
        x2 = _moe_block(x2, attn, w1, w2, moe_norm[layer], moe_w_grp[layer], moe_b_grp[layer],
                        moe_w_exp[layer], moe_b_exp[layer], moe_w_gate[layer], moe_w_up[layer],
                        moe_w_down[layer])
    return x2.reshape(b, s, d)
```

```python
import functools
import math

import numpy as np
import jax
import jax.numpy as jnp
from jax import lax
from jax.experimental import pallas as pl
from jax.experimental.pallas import tpu as pltpu
from jax.experimental.pallas import tpu_sc as plsc

F32 = jnp.float32
BF16 = jnp.bfloat16

LANES = 128
HEAD_DIM = 64
HEADS_PER_CHUNK = LANES // HEAD_DIM
D_MODEL = 1024
N_HEADS = 8
N_KV = 2
GROUP = N_HEADS // N_KV
N_CHUNKS = N_HEADS // HEADS_PER_CHUNK
MIX = N_HEADS * HEAD_DIM
SCALE = 1.0 / math.sqrt(HEAD_DIM)
EPS = 1e-6
NEG = -1e30

A_WINDOW = 128
C_BLOCK = 256
C_TOPK = 3
CMP_LEN = 32
CMP_STRIDE = 16
SLC_BLOCK = 64
SLC_TOPN = 16
D_WINDOW = 512
FORCE_BONUS = 1000.0

N_GROUPS = 4
EXPERTS_PER_GROUP = 8
N_EXPERTS = N_GROUPS * EXPERTS_PER_GROUP
EXPERT_FF = 256

VMEM_LIMIT = 48 * 1024 * 1024

GQA_PERM = tuple(h for c in range(N_CHUNKS) for h in (c, c + GROUP))
MHA_PERM = tuple(range(N_HEADS))


def _alibi(n_heads):
    return [float(2.0 ** (-8.0 * (i + 1) / n_heads)) for i in range(n_heads)]


def _head_cols(perm):
    return np.concatenate([np.arange(h * HEAD_DIM, (h + 1) * HEAD_DIM) for h in perm])


def _cparams(*sem):
    return pltpu.CompilerParams(dimension_semantics=sem, vmem_limit_bytes=VMEM_LIMIT)


def _dot(a, b):
    return jnp.dot(a, b, preferred_element_type=F32)


def _dot_nt(a, b):
    return lax.dot_general(a, b, (((1,), (1,)), ((), ())), preferred_element_type=F32)


def _split(x):
    hi = x.astype(BF16)
    lo = (x - hi.astype(F32)).astype(BF16)
    return hi, lo


def _dot_hilo(a, b):
    hi, lo = _split(a)
    return _dot(hi, b) + _dot(lo, b)


def _pack_pairs(x):
    half = x.shape[1] // 2
    lo = lax.bitcast_convert_type(x[:, :half].astype(BF16).astype(F32), jnp.uint32)
    hi = lax.bitcast_convert_type(x[:, half:].astype(BF16).astype(F32), jnp.uint32)
    return (lo >> 16) | hi


def _unpack_pairs(p):
    lo = lax.bitcast_convert_type(p << 16, F32)
    hi = lax.bitcast_convert_type(p & jnp.uint32(0xFFFF0000), F32)
    return jnp.concatenate([lo, hi], axis=1)


def _iota(shape, dim):
    return lax.broadcasted_iota(jnp.int32, shape, dim)


def _half_masks(dtype):
    lane = _iota((1, LANES), 1)
    return [(lane // HEAD_DIM == p).astype(dtype) for p in range(HEADS_PER_CHUNK)]


def _head_mean_sq(y):
    same = (_iota((LANES, LANES), 0) // HEAD_DIM == _iota((LANES, LANES), 1) // HEAD_DIM)
    return _dot_hilo(y * y, same.astype(BF16)) * (1.0 / HEAD_DIM)


def _proj_kernel(x_ref, g_ref, w_ref, gain_ref, *out_refs, plan):
    x = x_ref[...]
    ms = jnp.mean(x * x, axis=-1, keepdims=True)
    xn = (x * lax.rsqrt(ms + EPS) * g_ref[...]).astype(BF16)
    for o_ref, (col0, width, op) in zip(out_refs, plan):
        for a in range(0, width, 2 * LANES):
            wd = min(2 * LANES, width - a)
            y = _dot(xn, w_ref[:, col0 + a:col0 + a + wd])
            for c in range(0, wd, LANES):
                yc = y[:, c:c + LANES]
                if op == "norm":
                    gain = gain_ref[:, col0 + a + c:col0 + a + c + LANES]
                    yc = yc * lax.rsqrt(_head_mean_sq(yc) + EPS) * gain
                elif op == "sigmoid":
                    yc = jax.nn.sigmoid(yc)
                o_ref[:, a + c:a + c + LANES] = yc.astype(o_ref.dtype)


def _proj(x2, norm_g, w, gain, plan, out_dtypes, tm=512):
    t, d = x2.shape
    n = w.shape[1]
    out_shape = [jax.ShapeDtypeStruct((t, width), dt) for (_, width, _), dt in zip(plan, out_dtypes)]
    return pl.pallas_call(
        functools.partial(_proj_kernel, plan=plan),
        out_shape=out_shape,
        grid=(t // tm,),
        in_specs=[
            pl.BlockSpec((tm, d), lambda i: (i, 0)),
            pl.BlockSpec((1, d), lambda i: (0, 0)),
            pl.BlockSpec((d, n), lambda i: (0, 0)),
            pl.BlockSpec((1, n), lambda i: (0, 0)),
        ],
        out_specs=[pl.BlockSpec((tm, width), lambda i: (i, 0)) for (_, width, _) in plan],
        compiler_params=_cparams("parallel"),
        name="norm_proj",
    )(x2, norm_g.reshape(1, d), w, gain)


def _stack_group(q_ref, halfmask):
    return jnp.concatenate(
        [q_ref[:, c * LANES:(c + 1) * LANES] * halfmask for c in range(N_CHUNKS)], axis=0)


def _per_chunk_rows(rows, tq, values):
    r = _iota((rows, 1), 0) // tq
    out = jnp.full((rows, 1), values[N_CHUNKS - 1], F32)
    for c in range(N_CHUNKS - 2, -1, -1):
        out = jnp.where(r == c, values[c], out)
    return out


def _band_kernel(*refs, tq, window, wpad, slopes, has_sink):
    if has_sink:
        sink_ref, q_ref, k_ref, v_ref, o_ref = refs
    else:
        q_ref, k_ref, v_ref, o_ref = refs
    q0 = pl.program_id(1) * tq
    kw = tq + wpad
    kstart = pl.multiple_of(jnp.maximum(q0 - wpad, 0), LANES)
    ks = k_ref[pl.ds(kstart, kw), :]
    vs = v_ref[pl.ds(kstart, kw), :]
    rows = N_CHUNKS * tq
    qpos = q0 + _iota((rows, kw), 0) % tq
    kpos = kstart + _iota((rows, kw), 1)
    diff = qpos - kpos
    mask = (diff >= 0) & (diff < window)
    diff_f = diff.astype(F32)
    halves = range(HEADS_PER_CHUNK)
    hms = _half_masks(BF16)
    raw = [_dot_nt(_stack_group(q_ref, hms[p] * SCALE), ks) for p in halves]
    es, dens = [], []
    for p in halves:
        slope = _per_chunk_rows(rows, tq, [slopes[2 * c + p] for c in range(N_CHUNKS)])
        s = jnp.where(mask, raw[p] - slope * diff_f, NEG)
        mx = jnp.max(s, axis=-1, keepdims=True)
        if has_sink:
            sk = _per_chunk_rows(rows, tq, [sink_ref[2 * c + p] for c in range(N_CHUNKS)])
            mx = jnp.maximum(mx, sk)
        e = jnp.exp(s - mx)
        den = jnp.sum(e, axis=-1, keepdims=True)
        if has_sink:
            den = den + jnp.exp(sk - mx)
        es.append(e.astype(BF16))
        dens.append(den)
    os_ = [_dot(es[p], vs) / dens[p] for p in halves]
    _store_group(o_ref, os_, tq)


def _store_group(o_ref, os_, tq):
    lane = _iota((1, LANES), 1)
    for c in range(N_CHUNKS):
        o = jnp.where(lane < HEAD_DIM, os_[0][c * tq:(c + 1) * tq], os_[1][c * tq:(c + 1) * tq])
        o_ref[:, c * LANES:(c + 1) * LANES] = o.astype(o_ref.dtype)


def _band_attention(q, k, v, *, window, slopes, sink=None, tq=128):
    b, s, _ = q.shape
    wpad = -(-window // LANES) * LANES
    assert s >= tq + wpad and s % tq == 0
    kern = functools.partial(_band_kernel, tq=tq, window=window, wpad=wpad,
                             slopes=slopes, has_sink=sink is not None)
    in_specs = [
        pl.BlockSpec((None, tq, MIX), lambda bi, n: (bi, n, 0)),
        pl.BlockSpec((None, s, LANES), lambda bi, n: (bi, 0, 0)),
        pl.BlockSpec((None, s, LANES), lambda bi, n: (bi, 0, 0)),
    ]
    args = [q, k, v]
    if sink is not None:
        in_specs = [pl.BlockSpec(memory_space=pltpu.SMEM)] + in_specs
        args = [sink] + args
    return pl.pallas_call(
        kern,
        out_shape=jax.ShapeDtypeStruct((b, s, MIX), BF16),
        grid=(b, s // tq),
        in_specs=in_specs,
        out_specs=pl.BlockSpec((None, tq, MIX), lambda bi, n: (bi, n, 0)),
        compiler_params=_cparams("parallel", "parallel"),
        name="band_attention",
    )(*args)


def _stick_kernel(q_ref, k_ref, v_ref, o_ref, acc_ref, run_ref, *, tq, cpb):
    i = pl.program_id(2)
    q0 = pl.multiple_of(i * tq, tq)
    hms = _half_masks(BF16)
    heads = [(cc, p) for cc in range(cpb) for p in range(HEADS_PER_CHUNK)]
    qs = [q_ref[:, cc * LANES:(cc + 1) * LANES] * (hms[p] * SCALE) for cc, p in heads]
    upper = (_iota((tq, tq), 0) > _iota((tq, tq), 1)).astype(BF16)

    def block(kstart, diag):
        if diag:
            causal = _iota((tq, tq), 1) < _iota((tq, tq), 0)
        kjs = [k_ref[pl.ds(kstart, tq), cc * LANES:(cc + 1) * LANES] for cc in range(cpb)]
        vjs = [v_ref[pl.ds(kstart, tq), cc * LANES:(cc + 1) * LANES] for cc in range(cpb)]
        zs = [_dot_nt(qs[h], kjs[cc]) for h, (cc, p) in enumerate(heads)]
        lss, lks = [], []
        for z in zs:
            ls = jnp.minimum(z, 0.0) - jnp.log(1.0 + jnp.exp(-jnp.abs(z)))
            lk = ls - z
            if diag:
                lk = jnp.where(causal, lk, 0.0)
            lss.append(ls)
            lks.append(lk)
        sufs = [_dot_hilo(lk, upper) for lk in lks]
        ws = []
        for h in range(len(heads)):
            if diag:
                a = jnp.where(causal, jnp.exp(lss[h] + sufs[h]), 0.0)
            else:
                a = jnp.exp(lss[h] + sufs[h] + run_ref[h])
            ws.append(a.astype(BF16))
        pvs = [_dot(ws[h], vjs[cc]) for h, (cc, p) in enumerate(heads)]
        for h in range(len(heads)):
            rowsum = jnp.sum(lks[h], axis=-1, keepdims=True)
            if diag:
                acc_ref[h] = pvs[h]
                run_ref[h] = rowsum
            else:
                acc_ref[h] += pvs[h]
                run_ref[h] += rowsum

    block(q0, True)

    def body(t, carry):
        block(pl.multiple_of((i - 1 - t) * tq, tq), False)
        return carry

    lax.fori_loop(0, i, body, 0)
    lane = _iota((1, LANES), 1)
    for cc in range(cpb):
        o = jnp.where(lane < HEAD_DIM, acc_ref[HEADS_PER_CHUNK * cc], acc_ref[HEADS_PER_CHUNK * cc + 1])
        o_ref[:, cc * LANES:(cc + 1) * LANES] = o.astype(o_ref.dtype)


def _stick_attention(q, k, v, tq=256, cpb=2):
    b, s, _ = q.shape
    wide = cpb * LANES
    n_heads = cpb * HEADS_PER_CHUNK
    spec_q = pl.BlockSpec((None, tq, wide), lambda bi, c, i: (bi, i, c))
    spec_kv = pl.BlockSpec((None, s, wide), lambda bi, c, i: (bi, 0, c))
    return pl.pallas_call(
        functools.partial(_stick_kernel, tq=tq, cpb=cpb),
        out_shape=jax.ShapeDtypeStruct((b, s, MIX), BF16),
        grid=(b, N_CHUNKS // cpb, s // tq),
        in_specs=[spec_q, spec_kv, spec_kv],
        out_specs=spec_q,
        scratch_shapes=[pltpu.VMEM((n_heads, tq, LANES), F32), pltpu.VMEM((n_heads, tq, 1), F32)],
        compiler_params=_cparams("parallel", "parallel", "parallel"),
        name="stick_breaking",
    )(q, k, v)


def _route(logits):
    lane = _iota(logits.shape, 1)
    lane_f = lane.astype(F32)
    ninf = -jnp.inf
    is_g = (lane >= N_EXPERTS) & (lane < N_EXPERTS + N_GROUPS)
    gmax = jnp.max(jnp.where(is_g, logits, ninf), axis=-1, keepdims=True)
    gidx = jnp.min(jnp.where(is_g & (logits == gmax), lane_f - N_EXPERTS, 1e9), axis=-1, keepdims=True)
    p_g = 1.0 / jnp.sum(jnp.where(is_g, jnp.exp(logits - gmax), 0.0), axis=-1, keepdims=True)
    in_grp = (lane < N_EXPERTS) & ((lane // EXPERTS_PER_GROUP).astype(F32) == gidx)
    le = jnp.where(in_grp, logits, ninf)
    m1 = jnp.max(le, axis=-1, keepdims=True)
    i1 = jnp.min(jnp.where(le == m1, lane_f, 1e9), axis=-1, keepdims=True)
    le2 = jnp.where(lane_f == i1, ninf, le)
    m2 = jnp.max(le2, axis=-1, keepdims=True)
    i2 = jnp.min(jnp.where(le2 == m2, lane_f, 1e9), axis=-1, keepdims=True)
    e2 = jnp.exp(m2 - m1)
    w1 = p_g / (1.0 + e2)
    w2 = p_g * e2 / (1.0 + e2)
    return w1, w2, i1, i2


def _gate_expand(branch):
    r = _iota((LANES, MIX), 0)
    col = _iota((LANES, MIX), 1)
    head = col // LANES + GROUP * ((col % LANES) // HEAD_DIM)
    return (r == 3 * head + branch).astype(BF16)


META_W, META_ID, META_RANK = 0, 2, 4


def _out_kernel(*refs, nsa):
    if nsa:
        (x_ref, o1_ref, ocmp_ref, oslc_ref, owin_ref, gates_ref, w1_ref, w2_ref,
         ng_ref, wr_ref, br_ref, x1_ref, h_ref, meta_ref, count_ref, seen_ref) = refs
        g = gates_ref[...]
        o2 = (_dot_hilo(g, _gate_expand(0)) * ocmp_ref[...]
              + _dot_hilo(g, _gate_expand(1)) * oslc_ref[...]
              + _dot_hilo(g, _gate_expand(2)) * owin_ref[...]).astype(BF16)
    else:
        (x_ref, o1_ref, o2_ref, w1_ref, w2_ref,
         ng_ref, wr_ref, br_ref, x1_ref, h_ref, meta_ref, count_ref, seen_ref) = refs
        o2 = o2_ref[...]

    @pl.when(pl.program_id(0) == 0)
    def _():
        seen_ref[...] = jnp.zeros(seen_ref.shape, F32)

    x1 = x_ref[...] + _dot(o1_ref[...], w1_ref[...]) + _dot(o2, w2_ref[...])
    x1_ref[...] = x1
    ms = jnp.mean(x1 * x1, axis=-1, keepdims=True)
    h = x1 * lax.rsqrt(ms + EPS) * ng_ref[...]
    h_ref[...] = _pack_pairs(h)
    h_hi, h_lo = _split(h)
    w_hi, w_lo = _split(wr_ref[...])
    logits = _dot(h_hi, w_hi) + (_dot(h_hi, w_lo) + _dot(h_lo, w_hi)) + br_ref[...]
    w1, w2, i1, i2 = _route(logits)
    tm = logits.shape[0]
    lane_f = _iota(logits.shape, 1).astype(F32)
    pick = jnp.where((lane_f == i1) | (lane_f == i2), 1.0, 0.0).astype(BF16)
    earlier = (_iota((tm, tm), 1) < _iota((tm, tm), 0)).astype(BF16)
    rank = _dot(earlier, pick) + seen_ref[0:1, :]
    r1 = jnp.sum(jnp.where(lane_f == i1, rank, 0.0), axis=-1, keepdims=True)
    r2 = jnp.sum(jnp.where(lane_f == i2, rank, 0.0), axis=-1, keepdims=True)
    meta = jnp.zeros(logits.shape, F32)
    for lane, val in ((META_W, w1), (META_W + 1, w2), (META_ID, i1), (META_ID + 1, i2),
                      (META_RANK, r1), (META_RANK + 1, r2)):
        meta = jnp.where(lane_f == lane, val, meta)
    meta_ref[...] = meta
    seen_ref[...] += _dot(jnp.ones((seen_ref.shape[0], tm), BF16), pick)
    count_ref[...] = seen_ref[...]


def _out_proj_route(x2, attn, w1, w2, moe_g, w_route, b_route, tm=512):
    t, d = x2.shape
    nsa = len(attn) > 2
    row = lambda width: pl.BlockSpec((tm, width), lambda i: (i, 0))
    full = lambda a: pl.BlockSpec(a.shape, lambda i: (0, 0))
    consts = [w1, w2, moe_g.reshape(1, d), w_route, b_route]
    return pl.pallas_call(
        functools.partial(_out_kernel, nsa=nsa),
        out_shape=[jax.ShapeDtypeStruct((t, d), F32), jax.ShapeDtypeStruct((t, d // 2), jnp.uint32),
                   jax.ShapeDtypeStruct((t, LANES), F32), jax.ShapeDtypeStruct((8, LANES), F32)],
        grid=(t // tm,),
        in_specs=[row(d)] + [row(a.shape[1]) for a in attn] + [full(a) for a in consts],
        out_specs=[row(d), row(d // 2), row(LANES), pl.BlockSpec((8, LANES), lambda i: (0, 0))],
        scratch_shapes=[pltpu.VMEM((8, LANES), F32)],
        compiler_params=_cparams("arbitrary"),
        name="out_proj_route",
    )(x2, *attn, *consts)


EXPERT_TILE = 256
SC_WINDOW = 128
SC_SPLIT = 2


def _sc_gather(table, idx):
    n, d = table.shape
    out = _sc_gather_rows(table.reshape(n * SC_SPLIT, d // SC_SPLIT),
                          (idx[:, None] * SC_SPLIT + jnp.arange(SC_SPLIT, dtype=jnp.int32)).reshape(-1))
    return out.reshape(idx.shape[0], d)


def _sc_gather_rows(table, idx):
    m = idx.shape[0]
    d = table.shape[1]
    mesh = plsc.VectorSubcoreMesh(core_axis_name="core", subcore_axis_name="subcore")
    assert m % (SC_WINDOW * mesh.num_cores * mesh.num_subcores) == 0

    @pl.kernel(out_type=jax.ShapeDtypeStruct((m, d), table.dtype), mesh=mesh)
    def gather(table_hbm, idx_hbm, out_hbm):
        def body(idx_vmem, out_vmem):
            pltpu.sync_copy(table_hbm.at[idx_vmem.at[0]], out_vmem)

        pltpu.emit_pipeline(
            body,
            grid=(m // SC_WINDOW,),
            in_specs=[pl.BlockSpec((1, SC_WINDOW), lambda i: (0, i))],
            out_specs=[pl.BlockSpec((SC_WINDOW, d), lambda i: (i, 0))],
            core_axis_name=("core", "subcore"),
            dimension_semantics=(pltpu.PARALLEL,),
        )(idx_hbm, out_hbm)

    return gather(table, idx.reshape(1, m))


def _experts_kernel(tile_expert_ref, n_used_ref, x_ref, wgu_ref, wd_ref, o_ref):
    i = pl.program_id(0)

    @pl.when(i < n_used_ref[0])
    def _():
        gu = _dot(_unpack_pairs(x_ref[...]).astype(BF16), wgu_ref[...])
        g = gu[:, :EXPERT_FF]
        u = gu[:, EXPERT_FF:]
        act = (g * jax.nn.sigmoid(g) * u).astype(BF16)
        o_ref[...] = _pack_pairs(_dot(act, wd_ref[...]))

    @pl.when(i >= n_used_ref[0])
    def _():
        o_ref[...] = jnp.zeros(o_ref.shape, o_ref.dtype)


def _experts(xs, tile_expert, n_used, wgu, wd):
    rows, packed = xs.shape
    d = wgu.shape[1]
    n_tiles = rows // EXPERT_TILE
    return pl.pallas_call(
        _experts_kernel,
        out_shape=jax.ShapeDtypeStruct((rows, packed), jnp.uint32),
        grid_spec=pltpu.PrefetchScalarGridSpec(
            num_scalar_prefetch=2,
            grid=(n_tiles,),
            in_specs=[
                pl.BlockSpec((EXPERT_TILE, packed), lambda i, te, nu: (i, 0)),
                pl.BlockSpec((None, d, 2 * EXPERT_FF), lambda i, te, nu: (te[i], 0, 0)),
                pl.BlockSpec((None, EXPERT_FF, d), lambda i, te, nu: (te[i], 0, 0)),
            ],
            out_specs=pl.BlockSpec((EXPERT_TILE, packed), lambda i, te, nu: (i, 0)),
        ),
        compiler_params=_cparams("arbitrary"),
        name="moe_experts",
    )(tile_expert, n_used, xs, wgu, wd)


def _combine_kernel(x1_ref, y_ref, meta_ref, o_ref):
    packed = y_ref.shape[1] // 2
    w1 = meta_ref[:, META_W:META_W + 1]
    w2 = meta_ref[:, META_W + 1:META_W + 2]
    o_ref[...] = x1_ref[...] + (w1 * _unpack_pairs(y_ref[:, :packed]) + w2 * _unpack_pairs(y_ref[:, packed:]))


def _combine(x1, y_pairs, meta, tm=512):
    t, d = x1.shape
    return pl.pallas_call(
        _combine_kernel,
        out_shape=jax.ShapeDtypeStruct((t, d), F32),
        grid=(t // tm,),
        in_specs=[pl.BlockSpec((tm, d), lambda i: (i, 0)), pl.BlockSpec((tm, d), lambda i: (i, 0)),
                  pl.BlockSpec((tm, LANES), lambda i: (i, 0))],
        out_specs=pl.BlockSpec((tm, d), lambda i: (i, 0)),
        compiler_params=_cparams("parallel"),
        name="moe_combine",
    )(x1, y_pairs, meta)


def _dispatch_plan(meta, counts, t):
    n_tiles = 2 * t // EXPERT_TILE + N_EXPERTS
    cnt = counts[0, :N_EXPERTS].astype(jnp.int32)
    tiles = (cnt + EXPERT_TILE - 1) // EXPERT_TILE
    tile_end = jnp.cumsum(tiles)
    row0 = (tile_end - tiles) * EXPERT_TILE
    ids = meta[:, META_ID:META_ID + 2].astype(jnp.int32)
    dest = (row0[ids] + meta[:, META_RANK:META_RANK + 2].astype(jnp.int32)).reshape(2 * t)
    token = jnp.repeat(jnp.arange(t, dtype=jnp.int32), 2)
    src = jnp.zeros((n_tiles * EXPERT_TILE,), jnp.int32).at[dest].set(token)
    n_used = tile_end[N_EXPERTS - 1:]
    tile_id = jnp.minimum(jnp.arange(n_tiles, dtype=jnp.int32), n_used[0] - 1)
    tile_expert = jnp.searchsorted(tile_end, tile_id, side="right").astype(jnp.int32)
    return dest, src, tile_expert, n_used.astype(jnp.int32)


def _moe_block(x2, attn, w1, w2, moe_g, w_grp, b_grp, w_exp, b_exp, w_gate, w_up, w_down):
    t, d = x2.shape
    pad = LANES - N_EXPERTS - N_GROUPS
    w_route = jnp.concatenate([w_exp, w_grp, jnp.zeros((d, pad), F32)], axis=1)
    b_route = jnp.concatenate([b_exp, b_grp, jnp.zeros((pad,), F32)]).reshape(1, LANES)
    x1, h, meta, counts = _out_proj_route(x2, attn, w1, w2, moe_g, w_route, b_route)
    dest, src, tile_expert, n_used = _dispatch_plan(meta, counts, t)
    wgu = jnp.concatenate([w_gate, w_up], axis=2).astype(BF16)
    ys = _experts(_sc_gather(h, src), tile_expert, n_used, wgu, w_down.astype(BF16))
    return _combine(x1, _sc_gather(ys, dest).reshape(t, d), meta)


def _even_attn(x2, b, s, norm_g, w_in, q_norm, k_norm, sink, w_out):
    gq = _head_cols(GQA_PERM)
    w = jnp.concatenate([w_in[:, :MIX][:, gq], w_in[:, MIX:]], axis=1).astype(BF16)
    n = w.shape[1]
    kvw = N_KV * HEAD_DIM
    gain = jnp.concatenate([jnp.tile(q_norm, N_HEADS), jnp.tile(k_norm, N_KV),
                            jnp.ones((n - MIX - kvw,), F32)]).reshape(1, n)
    plan, col = [], 0
    for width, op in ((MIX, "norm"), (kvw, "norm"), (kvw, None), (MIX, None), (MIX, None), (MIX, None)):
        plan.append((col, width, op))
        col += width
    qa, ka, va, qb, kb, vb = _proj(x2, norm_g, w, gain, tuple(plan), [BF16] * 6)
    r3 = lambda a: a.reshape(b, s, a.shape[-1])
    slopes = _alibi(N_HEADS)
    oa = _band_attention(r3(qa), r3(ka), r3(va), window=A_WINDOW,
                         slopes=[slopes[h] for h in GQA_PERM], sink=sink[np.asarray(GQA_PERM)])
    ob = _stick_attention(r3(qb), r3(kb), r3(vb))
    w1 = w_out[:MIX][gq].astype(BF16)
    w2 = w_out[MIX:].astype(BF16)
    t = b * s
    return [oa.reshape(t, MIX), ob.reshape(t, MIX)], w1, w2


def _rank_rows(score, j, n, seg):
    row = _iota(score.shape, 0)
    rank = jnp.zeros(score.shape, F32)
    for jj in range(n):
        other = score[jj:jj + 1, :]
        for sgm in range(1, score.shape[0] // seg):
            other = jnp.where(row // seg == sgm, score[sgm * seg + jj:sgm * seg + jj + 1, :], other)
        beats = (other > score) | ((other == score) & (jj < j))
        rank = rank + jnp.where(beats, 1.0, 0.0)
    return rank


def _ones_beside(v, p):
    lane = _iota((1, LANES), 1)
    return jnp.where(lane // HEAD_DIM == p, v, jnp.ones_like(v))


def _normalize(acc):
    return acc / pltpu.roll(acc, HEAD_DIM, 1)


def _moba_kernel(slope_ref, q_ref, k_ref, v_ref, o_ref, kmean_ref, m_ref, acc_ref, *,
                 blk, nblk, topk, cpb):
    g = pl.program_id(1)
    i = pl.program_id(2)
    s_len = k_ref.shape[0]
    heads = [(cc, p) for cc in range(cpb) for p in range(HEADS_PER_CHUNK)]
    nh = len(heads)
    chunk = lambda cc: slice(cc * LANES, (cc + 1) * LANES)

    @pl.when(i == 0)
    def _():
        member = (_iota((LANES, s_len), 1) // blk == _iota((LANES, s_len), 0)).astype(BF16)
        for cc in range(cpb):
            kmean_ref[cc] = _dot(member, k_ref[:, chunk(cc)]) * (1.0 / blk)

    hms = _half_masks(BF16)
    q0 = pl.multiple_of(i * blk, blk)
    rel = (_iota((blk, blk), 0) - _iota((blk, blk), 1)).astype(F32)
    slopes = [slope_ref[HEADS_PER_CHUNK * (g * cpb + cc) + p] for cc, p in heads]
    kms = [_split(kmean_ref[cc]) for cc in range(cpb)]
    gates = [_dot_nt(kms[cc][0], q_ref[:, chunk(cc)] * hms[p]) + _dot_nt(kms[cc][1], q_ref[:, chunk(cc)] * hms[p])
             for cc, p in heads]
    qs = [q_ref[:, chunk(cc)] * (hms[p] * SCALE) for cc, p in heads]
    raw = [_dot_nt(qs[h], k_ref[pl.ds(q0, blk), chunk(cc)]) for h, (cc, p) in enumerate(heads)]
    nrow = -(-nblk // 8) * 8
    blk_id = _iota((nrow, blk), 0)
    sels = []
    for h in range(nh):
        gate = jnp.where(blk_id < i, gates[h][:nrow], NEG)
        rank = _rank_rows(gate, blk_id, nblk, nrow)
        sel_t = jnp.where((rank < topk) & (blk_id < i), 1.0, 0.0)
        sel_t = jnp.concatenate([sel_t, jnp.zeros((LANES - nrow, blk), F32)], axis=0)
        sels.append(sel_t.T.astype(BF16))
    own_bias = jnp.where(rel >= 0, 0.0, NEG)
    es = []
    for h in range(nh):
        s = raw[h] - slopes[h] * rel + own_bias
        m = jnp.max(s, axis=-1, keepdims=True)
        m_ref[h] = m
        es.append(jnp.exp(s - m).astype(BF16))
    for h, (cc, p) in enumerate(heads):
        acc_ref[h] = _dot(es[h], _ones_beside(v_ref[pl.ds(q0, blk), chunk(cc)], p))

    def body(j, carry):
        k0 = pl.multiple_of(j * blk, blk)
        raw = [_dot_nt(qs[h], k_ref[pl.ds(k0, blk), chunk(cc)]) for h, (cc, p) in enumerate(heads)]
        onehot = (_iota((LANES, blk), 0) == j).astype(BF16)
        picked = [_dot(sels[h], onehot) for h in range(nh)]
        diff = rel + ((i - j) * blk).astype(F32)
        es, alphas = [], []
        for h in range(nh):
            s = raw[h] - slopes[h] * diff + jnp.where(picked[h] > 0.5, 0.0, NEG)
            m_old = m_ref[h]
            m_new = jnp.maximum(m_old, jnp.max(s, axis=-1, keepdims=True))
            m_ref[h] = m_new
            es.append(jnp.exp(s - m_new).astype(BF16))
            alphas.append(jnp.exp(m_old - m_new))
        pvs = [_dot(es[h], _ones_beside(v_ref[pl.ds(k0, blk), chunk(cc)], p)) for h, (cc, p) in enumerate(heads)]
        for h in range(nh):
            acc_ref[h] = alphas[h] * acc_ref[h] + pvs[h]
        return carry

    lax.fori_loop(0, i, body, 0)
    lane = _iota((1, LANES), 1)
    for cc in range(cpb):
        h0 = HEADS_PER_CHUNK * cc
        o = jnp.where(lane < HEAD_DIM, _normalize(acc_ref[h0]), _normalize(acc_ref[h0 + 1]))
        o_ref[:, chunk(cc)] = o.astype(o_ref.dtype)


def _moba_attention(q, k, v, slopes, cpb=2):
    b, s, _ = q.shape
    assert s % C_BLOCK == 0
    nblk = s // C_BLOCK
    assert nblk <= LANES
    wide = cpb * LANES
    nh = cpb * HEADS_PER_CHUNK
    spec_q = pl.BlockSpec((None, C_BLOCK, wide), lambda bi, c, i, sl: (bi, i, c))
    spec_kv = pl.BlockSpec((None, s, wide), lambda bi, c, i, sl: (bi, 0, c))
    return pl.pallas_call(
        functools.partial(_moba_kernel, blk=C_BLOCK, nblk=nblk, topk=min(C_TOPK, nblk), cpb=cpb),
        out_shape=jax.ShapeDtypeStruct((b, s, MIX), BF16),
        grid_spec=pltpu.PrefetchScalarGridSpec(
            num_scalar_prefetch=1,
            grid=(b, N_CHUNKS // cpb, nblk),
            in_specs=[spec_q, spec_kv, spec_kv],
            out_specs=spec_q,
            scratch_shapes=[pltpu.VMEM((cpb, LANES, LANES), F32), pltpu.VMEM((nh, C_BLOCK, 1), F32),
                            pltpu.VMEM((nh, C_BLOCK, LANES), F32)],
        ),
        compiler_params=_cparams("parallel", "parallel", "arbitrary"),
        name="moba_attention",
    )(jnp.asarray(slopes, F32), q, k, v)


def _compress_kernel(xk_ref, xv_ref, wk_lo_ref, wk_hi_ref, wv_lo_ref, wv_hi_ref,
                     pk_lo_ref, pk_hi_ref, pv_lo_ref, pv_hi_ref, gain_ref, kc_ref, vc_ref):
    def compress(x_ref, w_lo_ref, w_hi_ref, p_lo_ref, p_hi_ref):
        x = x_ref[...]
        nrow = x.shape[0]
        first = _dot(x, w_lo_ref[...])
        second = pltpu.roll(_dot(x, w_hi_ref[...]), nrow - 1, 0)
        p_lo = jnp.broadcast_to(p_lo_ref[...], (8, p_lo_ref.shape[1]))
        p_hi = jnp.broadcast_to(p_hi_ref[...], (8, p_hi_ref.shape[1]))
        bias = _dot_hilo(p_lo, w_lo_ref[...]) + _dot_hilo(p_hi, w_hi_ref[...])
        return first + second + bias[0:1]

    kc = compress(xk_ref, wk_lo_ref, wk_hi_ref, pk_lo_ref, pk_hi_ref)
    kc = kc * lax.rsqrt(_head_mean_sq(kc) + EPS) * gain_ref[...]
    kc_ref[...] = kc.astype(kc_ref.dtype)
    vc_ref[...] = compress(xv_ref, wv_lo_ref, wv_hi_ref, pv_lo_ref, pv_hi_ref).astype(vc_ref.dtype)


def _compress_weights(pos, w):
    half = CMP_LEN // 2
    eye = jnp.eye(N_KV, dtype=F32)
    wd = jnp.einsum("gh,lde->lgdhe", eye, w).reshape(CMP_LEN, LANES, LANES)
    w_lo = wd[:half].reshape(half * LANES, LANES).astype(BF16)
    w_hi = wd[half:].reshape(half * LANES, LANES).astype(BF16)
    pt = jnp.tile(pos, (1, N_KV))
    return w_lo, w_hi, pt[:half].reshape(1, half * LANES), pt[half:].reshape(1, half * LANES)


def _compress(kcmp, vcmp, cmp_pos, cmp_w, k_gain):
    b, s, _ = kcmp.shape
    assert CMP_LEN == 2 * CMP_STRIDE and s % CMP_STRIDE == 0
    nrow = s // CMP_STRIDE
    wide = CMP_STRIDE * LANES
    xk = kcmp.reshape(b, nrow, wide)
    xv = vcmp.reshape(b, nrow, wide)
    wk = _compress_weights(cmp_pos[0], cmp_w[0])
    wv = _compress_weights(cmp_pos[1], cmp_w[1])
    consts = [wk[0], wk[1], wv[0], wv[1], wk[2], wk[3], wv[2], wv[3],
              jnp.tile(k_gain, N_KV).reshape(1, LANES)]
    spec_x = pl.BlockSpec((None, nrow, wide), lambda bi: (bi, 0, 0))
    spec_o = pl.BlockSpec((None, nrow, LANES), lambda bi: (bi, 0, 0))
    return pl.pallas_call(
        _compress_kernel,
        out_shape=[jax.ShapeDtypeStruct((b, nrow, LANES), BF16)] * 2,
        grid=(b,),
        in_specs=[spec_x, spec_x] + [pl.BlockSpec(a.shape, lambda bi: (0, 0)) for a in consts],
        out_specs=[spec_o, spec_o],
        compiler_params=_cparams("parallel"),
        name="nsa_compress",
    )(xk, xv, *consts)


def _nsa_cmp_kernel(q_ref, kc_ref, vc_ref, o_ref, sel_ref, *, tq, n_cmp, n_slc, topn, slopes):
    q0 = pl.program_id(1) * tq
    kc = kc_ref[...]
    vc = vc_ref[...]
    ncp = kc.shape[0]
    rows = N_CHUNKS * tq
    seg = LANES // N_KV
    qpos = q0 + _iota((rows, ncp), 0) % tq
    ncol = _iota((rows, ncp), 1)
    diff = qpos - (ncol * CMP_STRIDE + CMP_LEN - 1)
    mask = (diff >= 0) & (ncol < n_cmp)
    diff_f = diff.astype(F32)
    orow = _iota((LANES, ncp), 0)
    cst = _iota((LANES, ncp), 1) * CMP_STRIDE
    sst = (orow % seg) * SLC_BLOCK
    overlap = (cst < sst + SLC_BLOCK) & (cst + CMP_LEN > sst) & (orow % seg < n_slc)
    halves = range(HEADS_PER_CHUNK)
    hms = _half_masks(BF16)
    raw = [_dot_nt(_stack_group(q_ref, hms[p] * SCALE), kc) for p in halves]
    pcs = []
    for p in halves:
        slope = _per_chunk_rows(rows, tq, [slopes[2 * c + p] for c in range(N_CHUNKS)])
        sc = jnp.where(mask, raw[p] - slope * diff_f, NEG)
        mx = jnp.max(sc, axis=-1, keepdims=True)
        e = jnp.where(mask, jnp.exp(sc - mx), 0.0)
        den = jnp.sum(e, axis=-1, keepdims=True)
        pcs.append(e / jnp.where(den > 0, den, 1.0))
    _store_group(o_ref, [_dot(pcs[p].astype(BF16), vc) for p in halves], tq)
    p_slc = jnp.zeros((LANES, tq), F32)
    for p in halves:
        pg = pcs[p][0:tq]
        for c in range(1, N_CHUNKS):
            pg = pg + pcs[p][c * tq:(c + 1) * tq]
        ov = (overlap & (orow // seg == p)).astype(BF16)
        pg_hi, pg_lo = _split(pg)
        p_slc = p_slc + (_dot_nt(ov, pg_hi) + _dot_nt(ov, pg_lo))
    j = _iota((LANES, tq), 0) % seg
    cur = (q0 + _iota((LANES, tq), 1)) // SLC_BLOCK
    forced = (j == 0) | (j == cur) | (j == cur - 1)
    usable = (j <= cur) & (j < n_slc)
    score = jnp.where(usable, p_slc + jnp.where(forced, FORCE_BONUS, 0.0), NEG)
    rank = _rank_rows(score, j, n_slc, seg)
    sel_t = jnp.where((rank < topn) & usable, 1.0, 0.0)
    sel_ref[...] = sel_t.T.astype(sel_ref.dtype)


def _nsa_cmp(q, kc, vc, slopes, tq=128):
    b, s, _ = q.shape
    ncp = kc.shape[1]
    n_slc = s // SLC_BLOCK
    assert n_slc <= LANES // N_KV and ncp % LANES == 0
    kern = functools.partial(_nsa_cmp_kernel, tq=tq, n_cmp=ncp - 1, n_slc=n_slc,
                             topn=min(SLC_TOPN, n_slc), slopes=slopes)
    spec_c = pl.BlockSpec((None, ncp, LANES), lambda bi, n: (bi, 0, 0))
    return pl.pallas_call(
        kern,
        out_shape=[jax.ShapeDtypeStruct((b, s, MIX), BF16), jax.ShapeDtypeStruct((b, s, LANES), BF16)],
        grid=(b, s // tq),
        in_specs=[pl.BlockSpec((None, tq, MIX), lambda bi, n: (bi, n, 0)), spec_c, spec_c],
        out_specs=[pl.BlockSpec((None, tq, MIX), lambda bi, n: (bi, n, 0)),
                   pl.BlockSpec((None, tq, LANES), lambda bi, n: (bi, n, 0))],
        compiler_params=_cparams("parallel", "parallel"),
        name="nsa_compressed",
    )(q, kc, vc)


def _nsa_slc_kernel(q_ref, k_ref, v_ref, sel_ref, o_ref, m_ref, acc_ref, *, tq, tk, slopes):
    n = pl.program_id(1)
    q0 = n * tq
    rows = N_CHUNKS * tq
    seg = LANES // N_KV
    halves = range(HEADS_PER_CHUNK)
    sel = sel_ref[...]
    rel = _iota((tq, tk), 0) - _iota((tq, tk), 1)
    hms = _half_masks(BF16)
    qs = [_stack_group(q_ref, hms[p] * SCALE) for p in halves]
    for p in halves:
        m_ref[p] = jnp.full((rows, 1), 0.1 * NEG, F32)
        acc_ref[p] = jnp.zeros((rows, LANES), F32)

    def body(kt, carry):
        k0 = pl.multiple_of(kt * tk, tk)
        kj = k_ref[pl.ds(k0, tk), :]
        vj = v_ref[pl.ds(k0, tk), :]
        raw = [_dot_nt(qs[p], kj) for p in halves]
        blk_of_key = (k0 + _iota((LANES, tk), 1)) // SLC_BLOCK
        picked = [_dot(sel, (_iota((LANES, tk), 0) == p * seg + blk_of_key).astype(BF16)) for p in halves]
        diff = rel + (q0 - k0)
        diff_f = diff.astype(F32)
        es, alphas = [], []
        for p in halves:
            bias = jnp.where((picked[p] > 0.5) & (diff >= 0), 0.0, NEG)
            s = jnp.concatenate([raw[p][c * tq:(c + 1) * tq] - slopes[2 * c + p] * diff_f + bias
                                 for c in range(N_CHUNKS)], axis=0)
            m_old = m_ref[p]
            m_new = jnp.maximum(m_old, jnp.max(s, axis=-1, keepdims=True))
            m_ref[p] = m_new
            es.append(jnp.exp(s - m_new).astype(BF16))
            alphas.append(jnp.exp(m_old - m_new))
        pvs = [_dot(es[p], _ones_beside(vj, p)) for p in halves]
        for p in halves:
            acc_ref[p] = alphas[p] * acc_ref[p] + pvs[p]
        return carry

    lax.fori_loop(0, q0 // tk + 1, body, 0)
    _store_group(o_ref, [_normalize(acc_ref[p]) for p in halves], tq)


def _nsa_slc(q, k, v, sel, slopes, tq=128, tk=256):
    b, s, _ = q.shape
    assert tk % tq == 0 and tq % SLC_BLOCK == 0 and s % tk == 0
    rows = N_CHUNKS * tq
    spec_q = pl.BlockSpec((None, tq, MIX), lambda bi, n: (bi, n, 0))
    spec_kv = pl.BlockSpec((None, s, LANES), lambda bi, n: (bi, 0, 0))
    return pl.pallas_call(
        functools.partial(_nsa_slc_kernel, tq=tq, tk=tk, slopes=slopes),
        out_shape=jax.ShapeDtypeStruct((b, s, MIX), BF16),
        grid=(b, s // tq),
        in_specs=[spec_q, spec_kv, spec_kv, pl.BlockSpec((None, tq, LANES), lambda bi, n: (bi, n, 0))],
        out_specs=spec_q,
        scratch_shapes=[pltpu.VMEM((N_KV, rows, 1), F32), pltpu.VMEM((N_KV, rows, LANES), F32)],
        compiler_params=_cparams("parallel", "parallel"),
        name="nsa_selected",
    )(q, k, v, sel)


def _odd_attn(x2, b, s, norm_g, w_in, c_q_norm, c_k_norm, d_q_norm, d_k_norm, cmp_pos, cmp_w, w_out):
    gq = _head_cols(GQA_PERM)
    kvw = N_KV * HEAD_DIM
    n_gate = N_HEADS * 3
    qd0 = 3 * MIX
    w = jnp.concatenate([w_in[:, :qd0], w_in[:, qd0:qd0 + MIX][:, gq], w_in[:, qd0 + MIX:],
                         jnp.zeros((w_in.shape[0], LANES - n_gate), F32)], axis=1).astype(BF16)
    n = w.shape[1]
    ones = lambda width: jnp.ones((width,), F32)
    gain = jnp.concatenate([
        jnp.tile(c_q_norm, N_HEADS), jnp.tile(c_k_norm, N_HEADS), ones(MIX), jnp.tile(d_q_norm, N_HEADS),
        ones(2 * kvw), jnp.tile(d_k_norm[1], N_KV), ones(kvw), jnp.tile(d_k_norm[2], N_KV), ones(kvw),
        ones(LANES)]).reshape(1, n)
    plan, col = [], 0
    for width, op in ((MIX, "norm"), (MIX, "norm"), (MIX, None), (MIX, "norm"), (kvw, None), (kvw, None),
                      (kvw, "norm"), (kvw, None), (kvw, "norm"), (kvw, None), (LANES, "sigmoid")):
        plan.append((col, width, op))
        col += width
    outs = _proj(x2, norm_g, w, gain, tuple(plan), [BF16] * 10 + [F32])
    r3 = lambda a: a.reshape(b, s, a.shape[-1])
    qc, kc, vc, qd, kcmp, vcmp, kslc, vslc, kwin, vwin = [r3(a) for a in outs[:10]]
    gates = outs[10]
    slopes = _alibi(N_HEADS)
    gslopes = [slopes[h] for h in GQA_PERM]
    oc = _moba_attention(qc, kc, vc, slopes)
    k_cmp, v_cmp = _compress(kcmp, vcmp, cmp_pos, cmp_w, d_k_norm[0])
    o_cmp, sel = _nsa_cmp(qd, k_cmp, v_cmp, gslopes)
    o_slc = _nsa_slc(qd, kslc, vslc, sel, gslopes)
    o_win = _band_attention(qd, kwin, vwin, window=D_WINDOW, slopes=gslopes)
    t = b * s
    w1 = w_out[:MIX].astype(BF16)
    w2 = w_out[MIX:][gq].astype(BF16)
    flat = lambda a: a.reshape(t, a.shape[-1])
    return [flat(oc), flat(o_cmp), flat(o_slc), flat(o_win), gates], w1, w2


def kernel(x, ev_norm, ev_w_in, ev_q_norm, ev_k_norm, ev_sink, ev_w_out, od_norm, od_w_in, od_c_q_norm,
           od_c_k_norm, od_d_q_norm, od_d_k_norm, od_cmp_pos, od_cmp_w, od_w_out, moe_norm, moe_w_grp,
           moe_b_grp, moe_w_exp, moe_b_exp, moe_w_gate, moe_w_up, moe_w_down):
    b, s, d = x.shape
    x2 = x.reshape(b * s, d)
    depth = moe_norm.shape[0]
    for layer in range(depth):
        i = layer // 2
        if layer % 2 == 0:
            attn, w1, w2 = _even_attn(x2, b, s, ev_norm[i], ev_w_in[i], ev_q_norm[i], ev_k_norm[i],
                                      ev_sink[i], ev_w_out[i])
        else:
            attn, w1, w2 = _odd_attn(x2, b, s, od_norm[i], od_w_in[i], od_c_q_norm[i], od_c_k_norm[i],
                                     od_d_q_norm[i], od_d_k_norm[i], od_cmp_pos[i], od_cmp_w[i], od_w_out[i])
        x2 = _moe_block(x2, attn, w1, w2, moe_norm[layer], moe_w_grp[layer], moe_b_grp[layer],
                        moe_w_exp[layer], moe_b_exp[layer], moe_w_gate[layer], moe_w_up[layer],
                        moe_w_down[layer])
    return x2.reshape(b, s, d)
```

```python
import functools
import math

import numpy as np
import jax
import jax.numpy as jnp
from jax import lax
from jax.experimental import pallas as pl
from jax.experimental.pallas import tpu as pltpu

F32 = jnp.float32
BF16 = jnp.bfloat16

LANES = 128
HEAD_DIM = 64
HEADS_PER_CHUNK = LANES // HEAD_DIM
D_MODEL = 1024
N_HEADS = 8
N_KV = 2
GROUP = N_HEADS // N_KV
N_CHUNKS = N_HEADS // HEADS_PER_CHUNK
MIX = N_HEADS * HEAD_DIM
SCALE = 1.0 / math.sqrt(HEAD_DIM)
EPS = 1e-6
NEG = -1e30

A_WINDOW = 128
C_BLOCK = 256
C_TOPK = 3
CMP_LEN = 32
CMP_STRIDE = 16
SLC_BLOCK = 64
SLC_TOPN = 16
D_WINDOW = 512
FORCE_BONUS = 1000.0

N_GROUPS = 4
EXPERTS_PER_GROUP = 8
N_EXPERTS = N_GROUPS * EXPERTS_PER_GROUP
EXPERT_FF = 256

VMEM_LIMIT = 48 * 1024 * 1024

GQA_PERM = tuple(h for c in range(N_CHUNKS) for h in (c, c + GROUP))
MHA_PERM = tuple(range(N_HEADS))


def _alibi(n_heads):
    return [float(2.0 ** (-8.0 * (i + 1) / n_heads)) for i in range(n_heads)]


def _is_pow2(x):
    return math.frexp(x)[0] == 0.5


def _head_cols(perm):
    return np.concatenate([np.arange(h * HEAD_DIM, (h + 1) * HEAD_DIM) for h in perm])


def _cparams(*sem):
    return pltpu.CompilerParams(dimension_semantics=sem, vmem_limit_bytes=VMEM_LIMIT)


def _dot(a, b):
    return jnp.dot(a, b, preferred_element_type=F32)


def _dot_nt(a, b):
    return lax.dot_general(a, b, (((1,), (1,)), ((), ())), preferred_element_type=F32)


def _split(x):
    hi = x.astype(BF16)
    lo = (x - hi.astype(F32)).astype(BF16)
    return hi, lo


def _dot_hilo(a, b):
    hi, lo = _split(a)
    return _dot(hi, b) + _dot(lo, b)


def _iota(shape, dim):
    return lax.broadcasted_iota(jnp.int32, shape, dim)


def _half_masks(dtype):
    lane = _iota((1, LANES), 1)
    return [(lane // HEAD_DIM == p).astype(dtype) for p in range(HEADS_PER_CHUNK)]


def _head_mean_sq(y):
    same = (_iota((LANES, LANES), 0) // HEAD_DIM == _iota((LANES, LANES), 1) // HEAD_DIM)
    return _dot_hilo(y * y, same.astype(BF16)) * (1.0 / HEAD_DIM)


def _proj_kernel(x_ref, g_ref, w_ref, gain_ref, *out_refs, plan):
    x = x_ref[...]
    ms = jnp.mean(x * x, axis=-1, keepdims=True)
    xn = (x * lax.rsqrt(ms + EPS) * g_ref[...]).astype(BF16)
    for o_ref, (col0, width, op) in zip(out_refs, plan):
        for a in range(0, width, 2 * LANES):
            wd = min(2 * LANES, width - a)
            y = _dot(xn, w_ref[:, col0 + a:col0 + a + wd])
            for c in range(0, wd, LANES):
                yc = y[:, c:c + LANES]
                if op == "norm":
                    gain = gain_ref[:, col0 + a + c:col0 + a + c + LANES]
                    yc = yc * lax.rsqrt(_head_mean_sq(yc) + EPS) * gain
                elif op == "sigmoid":
                    yc = jax.nn.sigmoid(yc)
                o_ref[:, a + c:a + c + LANES] = yc.astype(o_ref.dtype)


def _proj(x2, norm_g, w, gain, plan, out_dtypes, tm=512):
    t, d = x2.shape
    n = w.shape[1]
    out_shape = [jax.ShapeDtypeStruct((t, width), dt) for (_, width, _), dt in zip(plan, out_dtypes)]
    return pl.pallas_call(
        functools.partial(_proj_kernel, plan=plan),
        out_shape=out_shape,
        grid=(t // tm,),
        in_specs=[
            pl.BlockSpec((tm, d), lambda i: (i, 0)),
            pl.BlockSpec((1, d), lambda i: (0, 0)),
            pl.BlockSpec((d, n), lambda i: (0, 0)),
            pl.BlockSpec((1, n), lambda i: (0, 0)),
        ],
        out_specs=[pl.BlockSpec((tm, width), lambda i: (i, 0)) for (_, width, _) in plan],
        compiler_params=_cparams("parallel"),
        name="norm_proj",
    )(x2, norm_g.reshape(1, d), w, gain)


def _stack_group(q_ref, halfmask):
    return jnp.concatenate(
        [q_ref[:, c * LANES:(c + 1) * LANES] * halfmask for c in range(N_CHUNKS)], axis=0)


def _per_chunk_rows(rows, tq, values):
    r = _iota((rows, 1), 0) // tq
    out = jnp.full((rows, 1), values[N_CHUNKS - 1], F32)
    for c in range(N_CHUNKS - 2, -1, -1):
        out = jnp.where(r == c, values[c], out)
    return out


def _band_kernel(*refs, tq, window, wpad, slopes, has_sink):
    if has_sink:
        sink_ref, q_ref, k_ref, v_ref, o_ref = refs
    else:
        q_ref, k_ref, v_ref, o_ref = refs
    q0 = pl.program_id(1) * tq
    kw = tq + wpad
    kstart = pl.multiple_of(jnp.maximum(q0 - wpad, 0), LANES)
    ks = k_ref[pl.ds(kstart, kw), :]
    vs = v_ref[pl.ds(kstart, kw), :]
    rows = N_CHUNKS * tq
    qpos = q0 + _iota((rows, kw), 0) % tq
    kpos = kstart + _iota((rows, kw), 1)
    diff = qpos - kpos
    mask = (diff >= 0) & (diff < window)
    diff_f = diff.astype(F32)
    halves = range(HEADS_PER_CHUNK)
    hms = _half_masks(BF16)
    raw = [_dot_nt(_stack_group(q_ref, hms[p] * SCALE), ks) for p in halves]
    es, dens = [], []
    for p in halves:
        slope = _per_chunk_rows(rows, tq, [slopes[2 * c + p] for c in range(N_CHUNKS)])
        s = jnp.where(mask, raw[p] - slope * diff_f, NEG)
        mx = jnp.max(s, axis=-1, keepdims=True)
        if has_sink:
            sk = _per_chunk_rows(rows, tq, [sink_ref[2 * c + p] for c in range(N_CHUNKS)])
            mx = jnp.maximum(mx, sk)
        e = jnp.exp(s - mx)
        den = jnp.sum(e, axis=-1, keepdims=True)
        if has_sink:
            den = den + jnp.exp(sk - mx)
        es.append(e.astype(BF16))
        dens.append(den)
    os_ = [_dot(es[p], vs) / dens[p] for p in halves]
    _store_group(o_ref, os_, tq)


def _store_group(o_ref, os_, tq):
    lane = _iota((1, LANES), 1)
    for c in range(N_CHUNKS):
        o = jnp.where(lane < HEAD_DIM, os_[0][c * tq:(c + 1) * tq], os_[1][c * tq:(c + 1) * tq])
        o_ref[:, c * LANES:(c + 1) * LANES] = o.astype(o_ref.dtype)


def _band_attention(q, k, v, *, window, slopes, sink=None, tq=128):
    b, s, _ = q.shape
    wpad = -(-window // LANES) * LANES
    assert s >= tq + wpad and s % tq == 0
    kern = functools.partial(_band_kernel, tq=tq, window=window, wpad=wpad,
                             slopes=slopes, has_sink=sink is not None)
    in_specs = [
        pl.BlockSpec((None, tq, MIX), lambda bi, n: (bi, n, 0)),
        pl.BlockSpec((None, s, LANES), lambda bi, n: (bi, 0, 0)),
        pl.BlockSpec((None, s, LANES), lambda bi, n: (bi, 0, 0)),
    ]
    args = [q, k, v]
    if sink is not None:
        in_specs = [pl.BlockSpec(memory_space=pltpu.SMEM)] + in_specs
        args = [sink] + args
    return pl.pallas_call(
        kern,
        out_shape=jax.ShapeDtypeStruct((b, s, MIX), BF16),
        grid=(b, s // tq),
        in_specs=in_specs,
        out_specs=pl.BlockSpec((None, tq, MIX), lambda bi, n: (bi, n, 0)),
        compiler_params=_cparams("parallel", "parallel"),
        name="band_attention",
    )(*args)


def _stick_kernel(q_ref, k_ref, v_ref, o_ref, acc_ref, run_ref, *, tq, cpb):
    i = pl.program_id(2)
    q0 = pl.multiple_of(i * tq, tq)
    hms = _half_masks(BF16)
    heads = [(cc, p) for cc in range(cpb) for p in range(HEADS_PER_CHUNK)]
    qs = [q_ref[:, cc * LANES:(cc + 1) * LANES] * (hms[p] * SCALE) for cc, p in heads]
    upper = (_iota((tq, tq), 0) > _iota((tq, tq), 1)).astype(BF16)

    def block(kstart, diag):
        if diag:
            causal = _iota((tq, tq), 1) < _iota((tq, tq), 0)
        kjs = [k_ref[pl.ds(kstart, tq), cc * LANES:(cc + 1) * LANES] for cc in range(cpb)]
        vjs = [v_ref[pl.ds(kstart, tq), cc * LANES:(cc + 1) * LANES] for cc in range(cpb)]
        zs = [_dot_nt(qs[h], kjs[cc]) for h, (cc, p) in enumerate(heads)]
        lss, lks = [], []
        for z in zs:
            ls = jnp.minimum(z, 0.0) - jnp.log(1.0 + jnp.exp(-jnp.abs(z)))
            lk = ls - z
            if diag:
                lk = jnp.where(causal, lk, 0.0)
            lss.append(ls)
            lks.append(lk)
        sufs = [_dot_hilo(lk, upper) for lk in lks]
        ws = []
        for h in range(len(heads)):
            if diag:
                a = jnp.where(causal, jnp.exp(lss[h] + sufs[h]), 0.0)
            else:
                a = jnp.exp(lss[h] + sufs[h] + run_ref[h])
            ws.append(a.astype(BF16))
        pvs = [_dot(ws[h], vjs[cc]) for h, (cc, p) in enumerate(heads)]
        for h in range(len(heads)):
            rowsum = jnp.sum(lks[h], axis=-1, keepdims=True)
            if diag:
                acc_ref[h] = pvs[h]
                run_ref[h] = rowsum
            else:
                acc_ref[h] += pvs[h]
                run_ref[h] += rowsum

    block(q0, True)

    def body(t, carry):
        block(pl.multiple_of((i - 1 - t) * tq, tq), False)
        return carry

    lax.fori_loop(0, i, body, 0)
    lane = _iota((1, LANES), 1)
    for cc in range(cpb):
        o = jnp.where(lane < HEAD_DIM, acc_ref[HEADS_PER_CHUNK * cc], acc_ref[HEADS_PER_CHUNK * cc + 1])
        o_ref[:, cc * LANES:(cc + 1) * LANES] = o.astype(o_ref.dtype)


def _stick_attention(q, k, v, tq=256, cpb=2):
    b, s, _ = q.shape
    wide = cpb * LANES
    n_heads = cpb * HEADS_PER_CHUNK
    spec_q = pl.BlockSpec((None, tq, wide), lambda bi, c, i: (bi, i, c))
    spec_kv = pl.BlockSpec((None, s, wide), lambda bi, c, i: (bi, 0, c))
    return pl.pallas_call(
        functools.partial(_stick_kernel, tq=tq, cpb=cpb),
        out_shape=jax.ShapeDtypeStruct((b, s, MIX), BF16),
        grid=(b, N_CHUNKS // cpb, s // tq),
        in_specs=[spec_q, spec_kv, spec_kv],
        out_specs=spec_q,
        scratch_shapes=[pltpu.VMEM((n_heads, tq, LANES), F32), pltpu.VMEM((n_heads, tq, 1), F32)],
        compiler_params=_cparams("parallel", "parallel", "parallel"),
        name="stick_breaking",
    )(q, k, v)


def _route(logits):
    lane = _iota(logits.shape, 1)
    lane_f = lane.astype(F32)
    ninf = -jnp.inf
    is_g = (lane >= N_EXPERTS) & (lane < N_EXPERTS + N_GROUPS)
    gmax = jnp.max(jnp.where(is_g, logits, ninf), axis=-1, keepdims=True)
    gidx = jnp.min(jnp.where(is_g & (logits == gmax), lane_f - N_EXPERTS, 1e9), axis=-1, keepdims=True)
    p_g = 1.0 / jnp.sum(jnp.where(is_g, jnp.exp(logits - gmax), 0.0), axis=-1, keepdims=True)
    in_grp = (lane < N_EXPERTS) & ((lane // EXPERTS_PER_GROUP).astype(F32) == gidx)
    le = jnp.where(in_grp, logits, ninf)
    m1 = jnp.max(le, axis=-1, keepdims=True)
    i1 = jnp.min(jnp.where(le == m1, lane_f, 1e9), axis=-1, keepdims=True)
    le2 = jnp.where(lane_f == i1, ninf, le)
    m2 = jnp.max(le2, axis=-1, keepdims=True)
    i2 = jnp.min(jnp.where(le2 == m2, lane_f, 1e9), axis=-1, keepdims=True)
    e2 = jnp.exp(m2 - m1)
    w1 = p_g / (1.0 + e2)
    w2 = p_g * e2 / (1.0 + e2)
    return jnp.where(lane_f == i1, w1, 0.0) + jnp.where(lane_f == i2, w2, 0.0)


def _gate_expand(branch):
    r = _iota((LANES, MIX), 0)
    col = _iota((LANES, MIX), 1)
    head = col // LANES + GROUP * ((col % LANES) // HEAD_DIM)
    return (r == 3 * head + branch).astype(BF16)


def _out_kernel(*refs, nsa):
    if nsa:
        (x_ref, o1_ref, ocmp_ref, oslc_ref, owin_ref, gates_ref, w1_ref, w2_ref,
         ng_ref, wr_ref, br_ref, x1_ref, h_ref, comb_ref) = refs
        g = gates_ref[...]
        o2 = (_dot_hilo(g, _gate_expand(0)) * ocmp_ref[...]
              + _dot_hilo(g, _gate_expand(1)) * oslc_ref[...]
              + _dot_hilo(g, _gate_expand(2)) * owin_ref[...]).astype(BF16)
    else:
        (x_ref, o1_ref, o2_ref, w1_ref, w2_ref,
         ng_ref, wr_ref, br_ref, x1_ref, h_ref, comb_ref) = refs
        o2 = o2_ref[...]
    x1 = x_ref[...] + _dot(o1_ref[...], w1_ref[...]) + _dot(o2, w2_ref[...])
    x1_ref[...] = x1
    ms = jnp.mean(x1 * x1, axis=-1, keepdims=True)
    h = x1 * lax.rsqrt(ms + EPS) * ng_ref[...]
    h_ref[...] = h.astype(BF16)
    h_hi, h_lo = _split(h)
    w_hi, w_lo = _split(wr_ref[...])
    logits = _dot(h_hi, w_hi) + (_dot(h_hi, w_lo) + _dot(h_lo, w_hi)) + br_ref[...]
    comb_ref[...] = _route(logits)


def _out_proj_route(x2, attn, w1, w2, moe_g, w_route, b_route, tm=512):
    t, d = x2.shape
    nsa = len(attn) > 2
    row = lambda width: pl.BlockSpec((tm, width), lambda i: (i, 0))
    full = lambda a: pl.BlockSpec(a.shape, lambda i: (0, 0))
    consts = [w1, w2, moe_g.reshape(1, d), w_route, b_route]
    return pl.pallas_call(
        functools.partial(_out_kernel, nsa=nsa),
        out_shape=[jax.ShapeDtypeStruct((t, d), F32), jax.ShapeDtypeStruct((t, d), BF16),
                   jax.ShapeDtypeStruct((t, LANES), F32)],
        grid=(t // tm,),
        in_specs=[row(d)] + [row(a.shape[1]) for a in attn] + [full(a) for a in consts],
        out_specs=[row(d), row(d), row(LANES)],
        compiler_params=_cparams("parallel"),
        name="out_proj_route",
    )(x2, *attn, *consts)


def _moe_kernel(h_ref, comb_ref, x1_ref, wg_ref, wu_ref, wd_ref, o_ref, *, per_step):
    step = pl.program_id(1)

    @pl.when(step == 0)
    def _():
        o_ref[...] = x1_ref[...]

    h = h_ref[...]
    comb = comb_ref[...]
    lane = _iota(comb.shape, 1)
    acts = []
    for j in range(per_step):
        g = _dot(h, wg_ref[j].astype(BF16))
        u = _dot(h, wu_ref[j].astype(BF16))
        c = jnp.sum(jnp.where(lane == step * per_step + j, comb, 0.0), axis=-1, keepdims=True)
        acts.append((c * (g * jax.nn.sigmoid(g) * u)).astype(BF16))
    w_down = jnp.concatenate([wd_ref[j].astype(BF16) for j in range(per_step)], axis=0)
    o_ref[...] += _dot(jnp.concatenate(acts, axis=1), w_down)


def _moe(h, comb, x1, w_gate, w_up, w_down, tm=1024, per_step=2):
    t, d = h.shape
    n_exp, _, ff = w_gate.shape
    assert t % tm == 0 and n_exp % per_step == 0
    return pl.pallas_call(
        functools.partial(_moe_kernel, per_step=per_step),
        out_shape=jax.ShapeDtypeStruct((t, d), F32),
        grid=(t // tm, n_exp // per_step),
        in_specs=[
            pl.BlockSpec((tm, d), lambda i, e: (i, 0)),
            pl.BlockSpec((tm, LANES), lambda i, e: (i, 0)),
            pl.BlockSpec((tm, d), lambda i, e: (i, 0)),
            pl.BlockSpec((per_step, d, ff), lambda i, e: (e, 0, 0)),
            pl.BlockSpec((per_step, d, ff), lambda i, e: (e, 0, 0)),
            pl.BlockSpec((per_step, ff, d), lambda i, e: (e, 0, 0)),
        ],
        out_specs=pl.BlockSpec((tm, d), lambda i, e: (i, 0)),
        compiler_params=_cparams("parallel", "arbitrary"),
        name="moe_experts",
    )(h, comb, x1, w_gate, w_up, w_down)


def _moe_block(x2, attn, w1, w2, moe_g, w_grp, b_grp, w_exp, b_exp, w_gate, w_up, w_down):
    d = x2.shape[1]
    pad = LANES - N_EXPERTS - N_GROUPS
    w_route = jnp.concatenate([w_exp, w_grp, jnp.zeros((d, pad), F32)], axis=1)
    b_route = jnp.concatenate([b_exp, b_grp, jnp.zeros((pad,), F32)]).reshape(1, LANES)
    x1, h, comb = _out_proj_route(x2, attn, w1, w2, moe_g, w_route, b_route)
    return _moe(h, comb, x1, w_gate, w_up, w_down)


def _even_attn(x2, b, s, norm_g, w_in, q_norm, k_norm, sink, w_out):
    gq = _head_cols(GQA_PERM)
    w = jnp.concatenate([w_in[:, :MIX][:, gq], w_in[:, MIX:]], axis=1).astype(BF16)
    n = w.shape[1]
    kvw = N_KV * HEAD_DIM
    gain = jnp.concatenate([jnp.tile(q_norm, N_HEADS), jnp.tile(k_norm, N_KV),
                            jnp.ones((n - MIX - kvw,), F32)]).reshape(1, n)
    plan, col = [], 0
    for width, op in ((MIX, "norm"), (kvw, "norm"), (kvw, None), (MIX, None), (MIX, None), (MIX, None)):
        plan.append((col, width, op))
        col += width
    qa, ka, va, qb, kb, vb = _proj(x2, norm_g, w, gain, tuple(plan), [BF16] * 6)
    r3 = lambda a: a.reshape(b, s, a.shape[-1])
    slopes = _alibi(N_HEADS)
    oa = _band_attention(r3(qa), r3(ka), r3(va), window=A_WINDOW,
                         slopes=[slopes[h] for h in GQA_PERM], sink=sink[np.asarray(GQA_PERM)])
    ob = _stick_attention(r3(qb), r3(kb), r3(vb))
    w1 = w_out[:MIX][gq].astype(BF16)
    w2 = w_out[MIX:].astype(BF16)
    t = b * s
    return [oa.reshape(t, MIX), ob.reshape(t, MIX)], w1, w2


def _rank_rows(score, j, n, seg):
    row = _iota(score.shape, 0)
    rank = jnp.zeros(score.shape, F32)
    for jj in range(n):
        other = score[jj:jj + 1, :]
        for sgm in range(1, score.shape[0] // seg):
            other = jnp.where(row // seg == sgm, score[sgm * seg + jj:sgm * seg + jj + 1, :], other)
        beats = (other > score) | ((other == score) & (jj < j))
        rank = rank + jnp.where(beats, 1.0, 0.0)
    return rank


def _ones_beside(v, p):
    lane = _iota((1, LANES), 1)
    return jnp.where(lane // HEAD_DIM == p, v, jnp.ones_like(v))


def _normalize(acc):
    return acc / pltpu.roll(acc, HEAD_DIM, 1)


def _moba_kernel(slope_ref, q_ref, k_ref, v_ref, o_ref, kmean_ref, vt_ref, m_ref, acc_ref, *,
                 blk, nblk, topk, cpb):
    g = pl.program_id(1)
    i = pl.program_id(2)
    s_len = k_ref.shape[0]
    heads = [(cc, p) for cc in range(cpb) for p in range(HEADS_PER_CHUNK)]
    nh = len(heads)
    chunk = lambda cc: slice(cc * LANES, (cc + 1) * LANES)

    @pl.when(i == 0)
    def _():
        member = (_iota((LANES, s_len), 1) // blk == _iota((LANES, s_len), 0)).astype(BF16)
        for cc in range(cpb):
            kmean_ref[cc] = _dot(member, k_ref[:, chunk(cc)]) * (1.0 / blk)
            vt_ref[cc] = v_ref[:, chunk(cc)].astype(F32).T.astype(BF16)

    hms = _half_masks(BF16)
    q0 = pl.multiple_of(i * blk, blk)
    rel = (_iota((blk, blk), 0) - _iota((blk, blk), 1)).astype(F32)
    slopes = [slope_ref[HEADS_PER_CHUNK * (g * cpb + cc) + p] for cc, p in heads]
    kms = [_split(kmean_ref[cc]) for cc in range(cpb)]
    gates = [_dot_nt(kms[cc][0], q_ref[:, chunk(cc)] * hms[p]) + _dot_nt(kms[cc][1], q_ref[:, chunk(cc)] * hms[p])
             for cc, p in heads]
    lane = _iota((blk, LANES), 1)
    off_f = _iota((blk, LANES), 0).astype(F32)
    off_b = off_f.astype(BF16)
    spare = lambda p, n: (1 - p) * HEAD_DIM + n

    def with_key_lanes(kj, p, shift):
        return jnp.where(lane == spare(p, 0), 1.0,
                         jnp.where(lane == spare(p, 1), off_b,
                                   jnp.where(lane == spare(p, 2), shift.astype(BF16), kj)))

    qs = []
    for h, (cc, p) in enumerate(heads):
        extra = jnp.where(lane == spare(p, 0), -slopes[h] * off_f,
                          jnp.where((lane == spare(p, 1)) | (lane == spare(p, 2)), slopes[h], 0.0))
        qs.append(q_ref[:, chunk(cc)] * (hms[p] * SCALE) + extra.astype(BF16))
    zero = jnp.zeros((), F32)
    raw = [_dot_nt(with_key_lanes(k_ref[pl.ds(q0, blk), chunk(cc)], p, zero), qs[h])
           for h, (cc, p) in enumerate(heads)]
    nrow = -(-nblk // 8) * 8
    blk_id = _iota((nrow, blk), 0)
    skips = []
    for h in range(nh):
        gate = jnp.where(blk_id < i, gates[h][:nrow], NEG)
        rank = _rank_rows(gate, blk_id, nblk, nrow)
        skip_t = jnp.where((rank < topk) & (blk_id < i), 0.0, 1.0)
        skip_t = jnp.concatenate([skip_t, jnp.ones((LANES - nrow, blk), F32)], axis=0)
        skips.append(skip_t.T.astype(BF16))
    own_bias = jnp.where(rel <= 0, 0.0, NEG)
    vrow = _iota((LANES, blk), 0)

    def values_t(k0, cc, p):
        vt = vt_ref[cc, :, pl.ds(k0, blk)]
        return jnp.where(vrow // HEAD_DIM == p, vt, jnp.ones_like(vt))

    es = []
    for h in range(nh):
        s = raw[h] + own_bias
        m = jnp.max(s, axis=0, keepdims=True)
        m_ref[h] = m
        es.append(jnp.exp(s - m).astype(BF16))
    for h, (cc, p) in enumerate(heads):
        acc_ref[h] = _dot(values_t(q0, cc, p), es[h])

    def body(j, carry):
        k0 = pl.multiple_of(j * blk, blk)
        shift = ((j - i) * blk).astype(F32)
        to_lane = [((_iota((LANES, LANES), 0) == j) & (_iota((LANES, LANES), 1) == spare(p, 0))).astype(BF16)
                   for p in range(HEADS_PER_CHUNK)]
        skipped = [_dot(skips[h], to_lane[p]) for h, (cc, p) in enumerate(heads)]
        raw = [_dot_nt(with_key_lanes(k_ref[pl.ds(k0, blk), chunk(cc)], p, shift),
                       qs[h] + (skipped[h] * NEG).astype(BF16))
               for h, (cc, p) in enumerate(heads)]
        es, alphas = [], []
        for h in range(nh):
            s = raw[h]
            m_old = m_ref[h]
            m_new = jnp.maximum(m_old, jnp.max(s, axis=0, keepdims=True))
            m_ref[h] = m_new
            es.append(jnp.exp(s - m_new).astype(BF16))
            alphas.append(jnp.exp(m_old - m_new))
        pvs = [_dot(values_t(k0, cc, p), es[h]) for h, (cc, p) in enumerate(heads)]
        for h in range(nh):
            acc_ref[h] = alphas[h] * acc_ref[h] + pvs[h]
        return carry

    lax.fori_loop(0, i, body, 0)
    lane = _iota((1, LANES), 1)
    for cc in range(cpb):
        outs = []
        for p in range(HEADS_PER_CHUNK):
            acc = acc_ref[HEADS_PER_CHUNK * cc + p]
            den = acc[(1 - p) * HEAD_DIM:(1 - p) * HEAD_DIM + 1, :]
            outs.append((acc / den).T)
        o_ref[:, chunk(cc)] = jnp.where(lane < HEAD_DIM, outs[0], outs[1]).astype(o_ref.dtype)


def _moba_attention(q, k, v, slopes, cpb=2):
    b, s, _ = q.shape
    assert s % C_BLOCK == 0 and C_BLOCK <= 256 and all(_is_pow2(x) for x in slopes)
    nblk = s // C_BLOCK
    assert nblk <= LANES
    wide = cpb * LANES
    nh = cpb * HEADS_PER_CHUNK
    spec_q = pl.BlockSpec((None, C_BLOCK, wide), lambda bi, c, i, sl: (bi, i, c))
    spec_kv = pl.BlockSpec((None, s, wide), lambda bi, c, i, sl: (bi, 0, c))
    return pl.pallas_call(
        functools.partial(_moba_kernel, blk=C_BLOCK, nblk=nblk, topk=min(C_TOPK, nblk), cpb=cpb),
        out_shape=jax.ShapeDtypeStruct((b, s, MIX), BF16),
        grid_spec=pltpu.PrefetchScalarGridSpec(
            num_scalar_prefetch=1,
            grid=(b, N_CHUNKS // cpb, nblk),
            in_specs=[spec_q, spec_kv, spec_kv],
            out_specs=spec_q,
            scratch_shapes=[pltpu.VMEM((cpb, LANES, LANES), F32), pltpu.VMEM((cpb, LANES, s), BF16),
                            pltpu.VMEM((nh, 1, C_BLOCK), F32), pltpu.VMEM((nh, LANES, C_BLOCK), F32)],
        ),
        compiler_params=_cparams("parallel", "parallel", "arbitrary"),
        name="moba_attention",
    )(jnp.asarray(slopes, F32), q, k, v)


def _compress_kernel(xk_ref, xv_ref, wk_lo_ref, wk_hi_ref, wv_lo_ref, wv_hi_ref,
                     pk_lo_ref, pk_hi_ref, pv_lo_ref, pv_hi_ref, gain_ref, kc_ref, vc_ref):
    def compress(x_ref, w_lo_ref, w_hi_ref, p_lo_ref, p_hi_ref):
        x = x_ref[...]
        nrow = x.shape[0]
        first = _dot(x, w_lo_ref[...])
        second = pltpu.roll(_dot(x, w_hi_ref[...]), nrow - 1, 0)
        p_lo = jnp.broadcast_to(p_lo_ref[...], (8, p_lo_ref.shape[1]))
        p_hi = jnp.broadcast_to(p_hi_ref[...], (8, p_hi_ref.shape[1]))
        bias = _dot_hilo(p_lo, w_lo_ref[...]) + _dot_hilo(p_hi, w_hi_ref[...])
        return first + second + bias[0:1]

    kc = compress(xk_ref, wk_lo_ref, wk_hi_ref, pk_lo_ref, pk_hi_ref)
    kc = kc * lax.rsqrt(_head_mean_sq(kc) + EPS) * gain_ref[...]
    kc_ref[...] = kc.astype(kc_ref.dtype)
    vc_ref[...] = compress(xv_ref, wv_lo_ref, wv_hi_ref, pv_lo_ref, pv_hi_ref).astype(vc_ref.dtype)


def _compress_weights(pos, w):
    half = CMP_LEN // 2
    eye = jnp.eye(N_KV, dtype=F32)
    wd = jnp.einsum("gh,lde->lgdhe", eye, w).reshape(CMP_LEN, LANES, LANES)
    w_lo = wd[:half].reshape(half * LANES, LANES).astype(BF16)
    w_hi = wd[half:].reshape(half * LANES, LANES).astype(BF16)
    pt = jnp.tile(pos, (1, N_KV))
    return w_lo, w_hi, pt[:half].reshape(1, half * LANES), pt[half:].reshape(1, half * LANES)


def _compress(kcmp, vcmp, cmp_pos, cmp_w, k_gain):
    b, s, _ = kcmp.shape
    assert CMP_LEN == 2 * CMP_STRIDE and s % CMP_STRIDE == 0
    nrow = s // CMP_STRIDE
    wide = CMP_STRIDE * LANES
    xk = kcmp.reshape(b, nrow, wide)
    xv = vcmp.reshape(b, nrow, wide)
    wk = _compress_weights(cmp_pos[0], cmp_w[0])
    wv = _compress_weights(cmp_pos[1], cmp_w[1])
    consts = [wk[0], wk[1], wv[0], wv[1], wk[2], wk[3], wv[2], wv[3],
              jnp.tile(k_gain, N_KV).reshape(1, LANES)]
    spec_x = pl.BlockSpec((None, nrow, wide), lambda bi: (bi, 0, 0))
    spec_o = pl.BlockSpec((None, nrow, LANES), lambda bi: (bi, 0, 0))
    return pl.pallas_call(
        _compress_kernel,
        out_shape=[jax.ShapeDtypeStruct((b, nrow, LANES), BF16)] * 2,
        grid=(b,),
        in_specs=[spec_x, spec_x] + [pl.BlockSpec(a.shape, lambda bi: (0, 0)) for a in consts],
        out_specs=[spec_o, spec_o],
        compiler_params=_cparams("parallel"),
        name="nsa_compress",
    )(xk, xv, *consts)


def _nsa_cmp_kernel(q_ref, kc_ref, vc_ref, o_ref, sel_ref, *, tq, n_cmp, n_slc, topn, slopes):
    q0 = pl.program_id(1) * tq
    kc = kc_ref[...]
    vc = vc_ref[...]
    ncp = kc.shape[0]
    rows = N_CHUNKS * tq
    seg = LANES // N_KV
    qpos = q0 + _iota((rows, ncp), 0) % tq
    ncol = _iota((rows, ncp), 1)
    diff = qpos - (ncol * CMP_STRIDE + CMP_LEN - 1)
    mask = (diff >= 0) & (ncol < n_cmp)
    diff_f = diff.astype(F32)
    orow = _iota((LANES, ncp), 0)
    cst = _iota((LANES, ncp), 1) * CMP_STRIDE
    sst = (orow % seg) * SLC_BLOCK
    overlap = (cst < sst + SLC_BLOCK) & (cst + CMP_LEN > sst) & (orow % seg < n_slc)
    halves = range(HEADS_PER_CHUNK)
    hms = _half_masks(BF16)
    raw = [_dot_nt(_stack_group(q_ref, hms[p] * SCALE), kc) for p in halves]
    pcs = []
    for p in halves:
        slope = _per_chunk_rows(rows, tq, [slopes[2 * c + p] for c in range(N_CHUNKS)])
        sc = jnp.where(mask, raw[p] - slope * diff_f, NEG)
        mx = jnp.max(sc, axis=-1, keepdims=True)
        e = jnp.where(mask, jnp.exp(sc - mx), 0.0)
        den = jnp.sum(e, axis=-1, keepdims=True)
        pcs.append(e / jnp.where(den > 0, den, 1.0))
    _store_group(o_ref, [_dot(pcs[p].astype(BF16), vc) for p in halves], tq)
    p_slc = jnp.zeros((LANES, tq), F32)
    for p in halves:
        pg = pcs[p][0:tq]
        for c in range(1, N_CHUNKS):
            pg = pg + pcs[p][c * tq:(c + 1) * tq]
        ov = (overlap & (orow // seg == p)).astype(BF16)
        pg_hi, pg_lo = _split(pg)
        p_slc = p_slc + (_dot_nt(ov, pg_hi) + _dot_nt(ov, pg_lo))
    j = _iota((LANES, tq), 0) % seg
    cur = (q0 + _iota((LANES, tq), 1)) // SLC_BLOCK
    forced = (j == 0) | (j == cur) | (j == cur - 1)
    usable = (j <= cur) & (j < n_slc)
    score = jnp.where(usable, p_slc + jnp.where(forced, FORCE_BONUS, 0.0), NEG)
    rank = _rank_rows(score, j, n_slc, seg)
    sel_t = jnp.where((rank < topn) & usable, 1.0, 0.0)
    sel_ref[...] = sel_t.T.astype(sel_ref.dtype)


def _nsa_cmp(q, kc, vc, slopes, tq=128):
    b, s, _ = q.shape
    ncp = kc.shape[1]
    n_slc = s // SLC_BLOCK
    assert n_slc <= LANES // N_KV and ncp % LANES == 0
    kern = functools.partial(_nsa_cmp_kernel, tq=tq, n_cmp=ncp - 1, n_slc=n_slc,
                             topn=min(SLC_TOPN, n_slc), slopes=slopes)
    spec_c = pl.BlockSpec((None, ncp, LANES), lambda bi, n: (bi, 0, 0))
    return pl.pallas_call(
        kern,
        out_shape=[jax.ShapeDtypeStruct((b, s, MIX), BF16), jax.ShapeDtypeStruct((b, s, LANES), BF16)],
        grid=(b, s // tq),
        in_specs=[pl.BlockSpec((None, tq, MIX), lambda bi, n: (bi, n, 0)), spec_c, spec_c],
        out_specs=[pl.BlockSpec((None, tq, MIX), lambda bi, n: (bi, n, 0)),
                   pl.BlockSpec((None, tq, LANES), lambda bi, n: (bi, n, 0))],
        compiler_params=_cparams("parallel", "parallel"),
        name="nsa_compressed",
    )(q, kc, vc)


def _nsa_slc_kernel(q_ref, k_ref, v_ref, sel_ref, o_ref, vt_ref, m_ref, acc_ref, *, tq, tk, slopes):
    n = pl.program_id(1)
    q0 = n * tq
    rows = N_CHUNKS * tq
    seg = LANES // N_KV
    halves = range(HEADS_PER_CHUNK)
    skip = 1.0 - sel_ref[...]
    nsub = tk // SLC_BLOCK
    lane_q = _iota((tq, LANES), 1)
    row_q = _iota((tq, LANES), 0).astype(F32)
    lane_k = _iota((tk, LANES), 1)
    off_k = _iota((tk, LANES), 0)
    spare = lambda p, n: (1 - p) * HEAD_DIM + n
    hms = _half_masks(BF16)
    qs = []
    for p in halves:
        extra = jnp.concatenate(
            [jnp.where(lane_q == spare(p, 0), -slopes[2 * c + p] * row_q,
                       jnp.where((lane_q == spare(p, 1)) | (lane_q == spare(p, 2)), slopes[2 * c + p], 0.0))
             for c in range(N_CHUNKS)], axis=0)
        qs.append(_stack_group(q_ref, hms[p] * SCALE) + extra.astype(BF16))
    @pl.when(n == 0)
    def _():
        vt_ref[...] = v_ref[...].astype(F32).T.astype(BF16)

    for p in halves:
        m_ref[p] = jnp.full((1, rows), 0.1 * NEG, F32)
        acc_ref[p] = jnp.zeros((LANES, rows), F32)

    def tile(kt, diagonal):
        k0 = pl.multiple_of(kt * tk, tk)
        kj = k_ref[pl.ds(k0, tk), :]
        vt = vt_ref[:, pl.ds(k0, tk)]
        shift = (k0 - q0).astype(F32).astype(BF16)
        raw = []
        for p in halves:
            in_sub = (lane_k >= spare(p, 3)) & (lane_k < spare(p, 3) + nsub)
            kp = jnp.where(lane_k == spare(p, 0), 1.0,
                           jnp.where(lane_k == spare(p, 1), off_k.astype(F32).astype(BF16),
                                     jnp.where(lane_k == spare(p, 2), shift,
                                               jnp.where(in_sub, jnp.where(off_k // SLC_BLOCK == lane_k - spare(p, 3),
                                                                           1.0, 0.0).astype(BF16), kj))))
            src = _iota((LANES, LANES), 0) - (p * seg + k0 // SLC_BLOCK)
            dst = _iota((LANES, LANES), 1) - spare(p, 3)
            to_lane = ((src == dst) & (dst >= 0) & (dst < nsub)).astype(BF16)
            bias = (_dot(skip, to_lane) * NEG).astype(BF16)
            raw.append(_dot_nt(kp, qs[p] + jnp.concatenate([bias] * N_CHUNKS, axis=0)))
        if diagonal:
            future = jnp.where(_iota((tk, tq), 1) + (q0 - k0) >= _iota((tk, tq), 0), 0.0, NEG)
            future = jnp.concatenate([future] * N_CHUNKS, axis=1)
        es, alphas = [], []
        for p in halves:
            s = raw[p] + future if diagonal else raw[p]
            m_old = m_ref[p]
            m_new = jnp.maximum(m_old, jnp.max(s, axis=0, keepdims=True))
            m_ref[p] = m_new
            es.append(jnp.exp(s - m_new).astype(BF16))
            alphas.append(jnp.exp(m_old - m_new))
        row = _iota((LANES, tk), 0)
        pvs = [_dot(jnp.where(row // HEAD_DIM == p, vt, jnp.ones_like(vt)), es[p]) for p in halves]
        for p in halves:
            acc_ref[p] = alphas[p] * acc_ref[p] + pvs[p]

    def body(kt, carry):
        tile(kt, False)
        return carry

    n_past = q0 // tk
    lax.fori_loop(0, n_past, body, 0)
    tile(n_past, True)
    outs = []
    for p in halves:
        acc = acc_ref[p]
        den = acc[(1 - p) * HEAD_DIM:(1 - p) * HEAD_DIM + 1, :]
        outs.append((acc / den).T)
    _store_group(o_ref, outs, tq)


def _nsa_slc(q, k, v, sel, slopes, tq=256, tk=256):
    b, s, _ = q.shape
    assert tk % tq == 0 and tq % SLC_BLOCK == 0 and s % tk == 0 and tk <= 256
    assert all(_is_pow2(x) for x in slopes)
    rows = N_CHUNKS * tq
    spec_q = pl.BlockSpec((None, tq, MIX), lambda bi, n: (bi, n, 0))
    spec_kv = pl.BlockSpec((None, s, LANES), lambda bi, n: (bi, 0, 0))
    return pl.pallas_call(
        functools.partial(_nsa_slc_kernel, tq=tq, tk=tk, slopes=slopes),
        out_shape=jax.ShapeDtypeStruct((b, s, MIX), BF16),
        grid=(b, s // tq),
        in_specs=[spec_q, spec_kv, spec_kv, pl.BlockSpec((None, tq, LANES), lambda bi, n: (bi, n, 0))],
        out_specs=spec_q,
        scratch_shapes=[pltpu.VMEM((LANES, s), BF16), pltpu.VMEM((N_KV, 1, rows), F32),
                        pltpu.VMEM((N_KV, LANES, rows), F32)],
        compiler_params=_cparams("parallel", "arbitrary"),
        name="nsa_selected",
    )(q, k, v, sel)


def _odd_attn(x2, b, s, norm_g, w_in, c_q_norm, c_k_norm, d_q_norm, d_k_norm, cmp_pos, cmp_w, w_out):
    gq = _head_cols(GQA_PERM)
    kvw = N_KV * HEAD_DIM
    n_gate = N_HEADS * 3
    qd0 = 3 * MIX
    w = jnp.concatenate([w_in[:, :qd0], w_in[:, qd0:qd0 + MIX][:, gq], w_in[:, qd0 + MIX:],
                         jnp.zeros((w_in.shape[0], LANES - n_gate), F32)], axis=1).astype(BF16)
    n = w.shape[1]
    ones = lambda width: jnp.ones((width,), F32)
    gain = jnp.concatenate([
        jnp.tile(c_q_norm, N_HEADS), jnp.tile(c_k_norm, N_HEADS), ones(MIX), jnp.tile(d_q_norm, N_HEADS),
        ones(2 * kvw), jnp.tile(d_k_norm[1], N_KV), ones(kvw), jnp.tile(d_k_norm[2], N_KV), ones(kvw),
        ones(LANES)]).reshape(1, n)
    plan, col = [], 0
    for width, op in ((MIX, "norm"), (MIX, "norm"), (MIX, None), (MIX, "norm"), (kvw, None), (kvw, None),
                      (kvw, "norm"), (kvw, None), (kvw, "norm"), (kvw, None), (LANES, "sigmoid")):
        plan.append((col, width, op))
        col += width
    outs = _proj(x2, norm_g, w, gain, tuple(plan), [BF16] * 10 + [F32])
    r3 = lambda a: a.reshape(b, s, a.shape[-1])
    qc, kc, vc, qd, kcmp, vcmp, kslc, vslc, kwin, vwin = [r3(a) for a in outs[:10]]
    gates = outs[10]
    slopes = _alibi(N_HEADS)
    gslopes = [slopes[h] for h in GQA_PERM]
    oc = _moba_attention(qc, kc, vc, slopes)
    k_cmp, v_cmp = _compress(kcmp, vcmp, cmp_pos, cmp_w, d_k_norm[0])
    o_cmp, sel = _nsa_cmp(qd, k_cmp, v_cmp, gslopes)
    o_slc = _nsa_slc(qd, kslc, vslc, sel, gslopes)
    o_win = _band_attention(qd, kwin, vwin, window=D_WINDOW, slopes=gslopes)
    t = b * s
    w1 = w_out[:MIX].astype(BF16)
    w2 = w_out[MIX:][gq].astype(BF16)
    flat = lambda a: a.reshape(t, a.shape[-1])
    return [flat(oc), flat(o_cmp), flat(o_slc), flat(o_win), gates], w1, w2


def kernel(x, ev_norm, ev_w_in, ev_q_norm, ev_k_norm, ev_sink, ev_w_out, od_norm, od_w_in, od_c_q_norm,
           od_c_k_norm, od_d_q_norm, od_d_k_norm, od_cmp_pos, od_cmp_w, od_w_out, moe_norm, moe_w_grp,
           moe_b_grp, moe_w_exp, moe_b_exp, moe_w_gate, moe_w_up, moe_w_down):
    b, s, d = x.shape
    x2 = x.reshape(b * s, d)
    depth = moe_norm.shape[0]
    for layer in range(depth):
        i = layer // 2
        if layer % 2 == 0:
            attn, w1, w2 = _even_attn(x2, b, s, ev_norm[i], ev_w_in[i], ev_q_norm[i], ev_k_norm[i],
                                      ev_sink[i], ev_w_out[i])
        else:
            attn, w1, w2 = _odd_attn(x2, b, s, od_norm[i], od_w_in[i], od_c_q_norm[i], od_c_k_norm[i],
                                     od_d_q_norm[i], od_d_k_norm[i], od_cmp_pos[i], od_cmp_w[i], od_w_out[i])
        x2 = _moe_block(x2, attn, w1, w2, moe_norm[layer], moe_w_grp[layer], moe_b_grp[layer],
                        moe_w_exp[layer], moe_b_exp[layer], moe_w_gate[layer], moe_w_up[layer],
                        moe_w_down[layer])
    return x2.reshape(b, s, d)
```

```python
import functools
import math

import numpy as np
import jax
import jax.numpy as jnp
from jax import lax
from jax.experimental import pallas as pl
from jax.experimental.pallas import tpu as pltpu

F32 = jnp.float32
BF16 = jnp.bfloat16

LANES = 128
HEAD_DIM = 64
HEADS_PER_CHUNK = LANES // HEAD_DIM
D_MODEL = 1024
N_HEADS = 8
N_KV = 2
GROUP = N_HEADS // N_KV
N_CHUNKS = N_HEADS // HEADS_PER_CHUNK
MIX = N_HEADS * HEAD_DIM
SCALE = 1.0 / math.sqrt(HEAD_DIM)
EPS = 1e-6
NEG = -1e30
EXP_UNDERFLOW = -104.0

A_WINDOW = 128
C_BLOCK = 256
C_TOPK = 3
CMP_LEN = 32
CMP_STRIDE = 16
SLC_BLOCK = 64
SLC_TOPN = 16
D_WINDOW = 512
FORCE_BONUS = 1000.0

N_GROUPS = 4
EXPERTS_PER_GROUP = 8
N_EXPERTS = N_GROUPS * EXPERTS_PER_GROUP
EXPERT_FF = 256

VMEM_LIMIT = 48 * 1024 * 1024

GQA_PERM = tuple(h for c in range(N_CHUNKS) for h in (c, c + GROUP))
MHA_PERM = tuple(range(N_HEADS))


def _alibi(n_heads):
    return [float(2.0 ** (-8.0 * (i + 1) / n_heads)) for i in range(n_heads)]


def _is_pow2(x):
    return math.frexp(x)[0] == 0.5


def _gqa_cols(w):
    lead = w.shape[:-1]
    w = w.reshape(*lead, N_KV, GROUP, HEAD_DIM)
    return jnp.swapaxes(w, -3, -2).reshape(*lead, MIX)


def _gqa_rows(w):
    tail = w.shape[1:]
    return jnp.swapaxes(w.reshape(N_KV, GROUP, HEAD_DIM, *tail), 0, 1).reshape(MIX, *tail)


def _cparams(*sem):
    return pltpu.CompilerParams(dimension_semantics=sem, vmem_limit_bytes=VMEM_LIMIT)


def _dot(a, b):
    return jnp.dot(a, b, preferred_element_type=F32)


def _dot_nt(a, b):
    return lax.dot_general(a, b, (((1,), (1,)), ((), ())), preferred_element_type=F32)


def _split(x):
    hi = x.astype(BF16)
    lo = (x - hi.astype(F32)).astype(BF16)
    return hi, lo


def _dot_hilo(a, b):
    hi, lo = _split(a)
    return _dot(hi, b) + _dot(lo, b)


def _iota(shape, dim):
    return lax.broadcasted_iota(jnp.int32, shape, dim)


def _half_masks(dtype):
    lane = _iota((1, LANES), 1)
    return [(lane // HEAD_DIM == p).astype(dtype) for p in range(HEADS_PER_CHUNK)]


def _head_mean_sq(y):
    same = (_iota((LANES, LANES), 0) // HEAD_DIM == _iota((LANES, LANES), 1) // HEAD_DIM)
    return _dot((y * y).astype(BF16), same.astype(BF16)) * (1.0 / HEAD_DIM)


def _proj_kernel(x_ref, g_ref, w_ref, gain_ref, *out_refs, plan):
    x = x_ref[...]
    ms = jnp.mean(x * x, axis=-1, keepdims=True)
    xn = (x * lax.rsqrt(ms + EPS) * g_ref[...]).astype(BF16)
    for o_ref, (col0, width, op) in zip(out_refs, plan):
        for a in range(0, width, 2 * LANES):
            wd = min(2 * LANES, width - a)
            y = _dot(xn, w_ref[:, col0 + a:col0 + a + wd])
            for c in range(0, wd, LANES):
                yc = y[:, c:c + LANES]
                if op == "norm":
                    gain = gain_ref[:, col0 + a + c:col0 + a + c + LANES]
                    yc = yc * lax.rsqrt(_head_mean_sq(yc) + EPS) * gain
                elif op == "sigmoid":
                    yc = jax.nn.sigmoid(yc)
                o_ref[:, a + c:a + c + LANES] = yc.astype(o_ref.dtype)


def _proj(x2, norm_g, w, gain, plan, out_dtypes, tm=512):
    t, d = x2.shape
    n = w.shape[1]
    out_shape = [jax.ShapeDtypeStruct((t, width), dt) for (_, width, _), dt in zip(plan, out_dtypes)]
    return pl.pallas_call(
        functools.partial(_proj_kernel, plan=plan),
        out_shape=out_shape,
        grid=(t // tm,),
        in_specs=[
            pl.BlockSpec((tm, d), lambda i: (i, 0)),
            pl.BlockSpec((1, d), lambda i: (0, 0)),
            pl.BlockSpec((d, n), lambda i: (0, 0)),
            pl.BlockSpec((1, n), lambda i: (0, 0)),
        ],
        out_specs=[pl.BlockSpec((tm, width), lambda i: (i, 0)) for (_, width, _) in plan],
        compiler_params=_cparams("parallel"),
        name="norm_proj",
    )(x2, norm_g.reshape(1, d), w, gain)


def _stack_group(q_ref, halfmask):
    return jnp.concatenate(
        [q_ref[:, c * LANES:(c + 1) * LANES] * halfmask for c in range(N_CHUNKS)], axis=0)


def _per_chunk_rows(rows, tq, values):
    r = _iota((rows, 1), 0) // tq
    out = jnp.full((rows, 1), values[N_CHUNKS - 1], F32)
    for c in range(N_CHUNKS - 2, -1, -1):
        out = jnp.where(r == c, values[c], out)
    return out


def _band_kernel(*refs, tq, window, wpad, slopes, has_sink):
    if has_sink:
        sink_ref, q_ref, k_ref, v_ref, o_ref = refs
    else:
        q_ref, k_ref, v_ref, o_ref = refs
    q0 = pl.program_id(1) * tq
    kw = tq + wpad
    kstart = pl.multiple_of(jnp.maximum(q0 - wpad, 0), LANES)
    ks = k_ref[pl.ds(kstart, kw), :]
    vs = v_ref[pl.ds(kstart, kw), :]
    rows = N_CHUNKS * tq
    qpos = q0 + _iota((rows, kw), 0) % tq
    kpos = kstart + _iota((rows, kw), 1)
    diff = qpos - kpos
    mask = (diff >= 0) & (diff < window)
    diff_f = diff.astype(F32)
    halves = range(HEADS_PER_CHUNK)
    hms = _half_masks(BF16)
    raw = [_dot_nt(_stack_group(q_ref, hms[p] * SCALE), ks) for p in halves]
    es, dens = [], []
    for p in halves:
        slope = _per_chunk_rows(rows, tq, [slopes[2 * c + p] for c in range(N_CHUNKS)])
        s = jnp.where(mask, raw[p] - slope * diff_f, NEG)
        mx = jnp.max(s, axis=-1, keepdims=True)
        if has_sink:
            sk = _per_chunk_rows(rows, tq, [sink_ref[2 * c + p] for c in range(N_CHUNKS)])
            mx = jnp.maximum(mx, sk)
        e = jnp.exp(s - mx)
        den = jnp.sum(e, axis=-1, keepdims=True)
        if has_sink:
            den = den + jnp.exp(sk - mx)
        es.append(e.astype(BF16))
        dens.append(den)
    os_ = [_dot(es[p], vs) / dens[p] for p in halves]
    _store_group(o_ref, os_, tq)


def _store_group(o_ref, os_, tq):
    lane = _iota((1, LANES), 1)
    for c in range(N_CHUNKS):
        o = jnp.where(lane < HEAD_DIM, os_[0][c * tq:(c + 1) * tq], os_[1][c * tq:(c + 1) * tq])
        o_ref[:, c * LANES:(c + 1) * LANES] = o.astype(o_ref.dtype)


def _band_attention(q, k, v, *, window, slopes, sink=None, tq=128):
    b, s, _ = q.shape
    wpad = -(-window // LANES) * LANES
    assert s >= tq + wpad and s % tq == 0
    kern = functools.partial(_band_kernel, tq=tq, window=window, wpad=wpad,
                             slopes=slopes, has_sink=sink is not None)
    in_specs = [
        pl.BlockSpec((None, tq, MIX), lambda bi, n: (bi, n, 0)),
        pl.BlockSpec((None, s, LANES), lambda bi, n: (bi, 0, 0)),
        pl.BlockSpec((None, s, LANES), lambda bi, n: (bi, 0, 0)),
    ]
    args = [q, k, v]
    if sink is not None:
        in_specs = [pl.BlockSpec(memory_space=pltpu.SMEM)] + in_specs
        args = [sink] + args
    return pl.pallas_call(
        kern,
        out_shape=jax.ShapeDtypeStruct((b, s, MIX), BF16),
        grid=(b, s // tq),
        in_specs=in_specs,
        out_specs=pl.BlockSpec((None, tq, MIX), lambda bi, n: (bi, n, 0)),
        compiler_params=_cparams("parallel", "parallel"),
        name="band_attention",
    )(*args)


def _stick_kernel(q_ref, k_ref, v_ref, o_ref, acc_ref, run_ref, *, tq, cpb):
    i = pl.program_id(2)
    q0 = pl.multiple_of(i * tq, tq)
    hms = _half_masks(BF16)
    heads = [(cc, p) for cc in range(cpb) for p in range(HEADS_PER_CHUNK)]
    qs = [q_ref[:, cc * LANES:(cc + 1) * LANES] * (hms[p] * SCALE) for cc, p in heads]
    upper = (_iota((2 * tq, tq), 0) % tq > _iota((2 * tq, tq), 1)).astype(BF16)

    def suffix_sum(x):
        hi, lo = _split(x)
        return _dot(jnp.concatenate([hi, lo], axis=1), upper)

    def block(kstart, diag):
        if diag:
            causal = _iota((tq, tq), 1) < _iota((tq, tq), 0)
        kjs = [k_ref[pl.ds(kstart, tq), cc * LANES:(cc + 1) * LANES] for cc in range(cpb)]
        vjs = [v_ref[pl.ds(kstart, tq), cc * LANES:(cc + 1) * LANES] for cc in range(cpb)]
        zs = [_dot_nt(qs[h], kjs[cc]) for h, (cc, p) in enumerate(heads)]
        lss, lks = [], []
        for z in zs:
            ls = jnp.minimum(z, 0.0) - jnp.log(1.0 + jnp.exp(-jnp.abs(z)))
            lk = ls - z
            if diag:
                lk = jnp.where(causal, lk, 0.0)
            lss.append(ls)
            lks.append(lk)
        sufs = [suffix_sum(lk) for lk in lks]
        ws = []
        for h in range(len(heads)):
            if diag:
                a = jnp.where(causal, jnp.exp(lss[h] + sufs[h]), 0.0)
            else:
                a = jnp.exp(lss[h] + sufs[h] + run_ref[h])
            ws.append(a.astype(BF16))
        pvs = [_dot(ws[h], vjs[cc]) for h, (cc, p) in enumerate(heads)]
        for h in range(len(heads)):
            rowsum = jnp.sum(lks[h], axis=-1, keepdims=True)
            if diag:
                acc_ref[h] = pvs[h]
                run_ref[h] = rowsum
            else:
                acc_ref[h] += pvs[h]
                run_ref[h] += rowsum

    block(q0, True)

    def weights_alive():
        run = run_ref[0]
        for h in range(1, len(heads)):
            run = jnp.maximum(run, run_ref[h])
        return jnp.max(run) > EXP_UNDERFLOW

    def body(carry):
        t, _ = carry
        block(pl.multiple_of((i - 1 - t) * tq, tq), False)
        return t + 1, weights_alive()

    lax.while_loop(lambda c: (c[0] < i) & c[1], body, (jnp.int32(0), weights_alive()))
    lane = _iota((1, LANES), 1)
    for cc in range(cpb):
        o = jnp.where(lane < HEAD_DIM, acc_ref[HEADS_PER_CHUNK * cc], acc_ref[HEADS_PER_CHUNK * cc + 1])
        o_ref[:, cc * LANES:(cc + 1) * LANES] = o.astype(o_ref.dtype)


def _stick_attention(q, k, v, tq=256, cpb=2):
    b, s, _ = q.shape
    wide = cpb * LANES
    n_heads = cpb * HEADS_PER_CHUNK
    spec_q = pl.BlockSpec((None, tq, wide), lambda bi, c, i: (bi, i, c))
    spec_kv = pl.BlockSpec((None, s, wide), lambda bi, c, i: (bi, 0, c))
    return pl.pallas_call(
        functools.partial(_stick_kernel, tq=tq, cpb=cpb),
        out_shape=jax.ShapeDtypeStruct((b, s, MIX), BF16),
        grid=(b, N_CHUNKS // cpb, s // tq),
        in_specs=[spec_q, spec_kv, spec_kv],
        out_specs=spec_q,
        scratch_shapes=[pltpu.VMEM((n_heads, tq, LANES), F32), pltpu.VMEM((n_heads, tq, 1), F32)],
        compiler_params=_cparams("parallel", "parallel", "parallel"),
        name="stick_breaking",
    )(q, k, v)


def _route(logits):
    lane = _iota(logits.shape, 1)
    lane_f = lane.astype(F32)
    ninf = -jnp.inf
    is_g = (lane >= N_EXPERTS) & (lane < N_EXPERTS + N_GROUPS)
    gmax = jnp.max(jnp.where(is_g, logits, ninf), axis=-1, keepdims=True)
    gidx = jnp.min(jnp.where(is_g & (logits == gmax), lane_f - N_EXPERTS, 1e9), axis=-1, keepdims=True)
    p_g = 1.0 / jnp.sum(jnp.where(is_g, jnp.exp(logits - gmax), 0.0), axis=-1, keepdims=True)
    in_grp = (lane < N_EXPERTS) & ((lane // EXPERTS_PER_GROUP).astype(F32) == gidx)
    le = jnp.where(in_grp, logits, ninf)
    m1 = jnp.max(le, axis=-1, keepdims=True)
    i1 = jnp.min(jnp.where(le == m1, lane_f, 1e9), axis=-1, keepdims=True)
    le2 = jnp.where(lane_f == i1, ninf, le)
    m2 = jnp.max(le2, axis=-1, keepdims=True)
    i2 = jnp.min(jnp.where(le2 == m2, lane_f, 1e9), axis=-1, keepdims=True)
    e2 = jnp.exp(m2 - m1)
    w1 = p_g / (1.0 + e2)
    w2 = p_g * e2 / (1.0 + e2)
    return jnp.where(lane_f == i1, w1, 0.0) + jnp.where(lane_f == i2, w2, 0.0)


def _gate_expand(branch):
    r = _iota((LANES, MIX), 0)
    col = _iota((LANES, MIX), 1)
    head = col // LANES + GROUP * ((col % LANES) // HEAD_DIM)
    return (r == 3 * head + branch).astype(BF16)


def _out_kernel(*refs, nsa):
    if nsa:
        (x_ref, o1_ref, ocmp_ref, oslc_ref, owin_ref, gates_ref, w1_ref, w2_ref,
         ng_ref, wr_ref, br_ref, x1_ref, h_ref, comb_ref) = refs
        g = gates_ref[...]
        o2 = (_dot_hilo(g, _gate_expand(0)) * ocmp_ref[...]
              + _dot_hilo(g, _gate_expand(1)) * oslc_ref[...]
              + _dot_hilo(g, _gate_expand(2)) * owin_ref[...]).astype(BF16)
    else:
        (x_ref, o1_ref, o2_ref, w1_ref, w2_ref,
         ng_ref, wr_ref, br_ref, x1_ref, h_ref, comb_ref) = refs
        o2 = o2_ref[...]
    x1 = x_ref[...] + _dot(o1_ref[...], w1_ref[...]) + _dot(o2, w2_ref[...])
    x1_ref[...] = x1
    ms = jnp.mean(x1 * x1, axis=-1, keepdims=True)
    h = x1 * lax.rsqrt(ms + EPS) * ng_ref[...]
    h_ref[...] = h.astype(BF16)
    h_hi, h_lo = _split(h)
    w_hi, w_lo = _split(wr_ref[...])
    logits = _dot(h_hi, w_hi) + (_dot(h_hi, w_lo) + _dot(h_lo, w_hi)) + br_ref[...]
    comb_ref[...] = _route(logits)


def _out_proj_route(x2, attn, w1, w2, moe_g, w_route, b_route, tm=512):
    t, d = x2.shape
    nsa = len(attn) > 2
    row = lambda width: pl.BlockSpec((tm, width), lambda i: (i, 0))
    full = lambda a: pl.BlockSpec(a.shape, lambda i: (0, 0))
    consts = [w1, w2, moe_g.reshape(1, d), w_route, b_route]
    return pl.pallas_call(
        functools.partial(_out_kernel, nsa=nsa),
        out_shape=[jax.ShapeDtypeStruct((t, d), F32), jax.ShapeDtypeStruct((t, d), BF16),
                   jax.ShapeDtypeStruct((t, LANES), F32)],
        grid=(t // tm,),
        in_specs=[row(d)] + [row(a.shape[1]) for a in attn] + [full(a) for a in consts],
        out_specs=[row(d), row(d), row(LANES)],
        compiler_params=_cparams("parallel"),
        name="out_proj_route",
    )(x2, *attn, *consts)


def _moe_kernel(h_ref, comb_ref, x1_ref, wg_ref, wu_ref, wd_ref, o_ref, *, per_step):
    step = pl.program_id(1)

    @pl.when(step == 0)
    def _():
        o_ref[...] = x1_ref[...]

    h = h_ref[...]
    comb = comb_ref[...]
    lane = _iota(comb.shape, 1)
    acts = []
    for j in range(per_step):
        g = _dot(h, wg_ref[j].astype(BF16))
        u = _dot(h, wu_ref[j].astype(BF16))
        c = jnp.sum(jnp.where(lane == step * per_step + j, comb, 0.0), axis=-1, keepdims=True)
        acts.append((c * (g * jax.nn.sigmoid(g) * u)).astype(BF16))
    w_down = jnp.concatenate([wd_ref[j].astype(BF16) for j in range(per_step)], axis=0)
    o_ref[...] += _dot(jnp.concatenate(acts, axis=1), w_down)


def _moe(h, comb, x1, w_gate, w_up, w_down, tm=1024, per_step=2):
    t, d = h.shape
    n_exp, _, ff = w_gate.shape
    assert t % tm == 0 and n_exp % per_step == 0
    return pl.pallas_call(
        functools.partial(_moe_kernel, per_step=per_step),
        out_shape=jax.ShapeDtypeStruct((t, d), F32),
        grid=(t // tm, n_exp // per_step),
        in_specs=[
            pl.BlockSpec((tm, d), lambda i, e: (i, 0)),
            pl.BlockSpec((tm, LANES), lambda i, e: (i, 0)),
            pl.BlockSpec((tm, d), lambda i, e: (i, 0)),
            pl.BlockSpec((per_step, d, ff), lambda i, e: (e, 0, 0)),
            pl.BlockSpec((per_step, d, ff), lambda i, e: (e, 0, 0)),
            pl.BlockSpec((per_step, ff, d), lambda i, e: (e, 0, 0)),
        ],
        out_specs=pl.BlockSpec((tm, d), lambda i, e: (i, 0)),
        compiler_params=_cparams("parallel", "arbitrary"),
        name="moe_experts",
    )(h, comb, x1, w_gate, w_up, w_down)


def _moe_block(x2, attn, w1, w2, moe_g, w_grp, b_grp, w_exp, b_exp, w_gate, w_up, w_down):
    d = x2.shape[1]
    pad = LANES - N_EXPERTS - N_GROUPS
    w_route = jnp.concatenate([w_exp, w_grp, jnp.zeros((d, pad), F32)], axis=1)
    b_route = jnp.concatenate([b_exp, b_grp, jnp.zeros((pad,), F32)]).reshape(1, LANES)
    x1, h, comb = _out_proj_route(x2, attn, w1, w2, moe_g, w_route, b_route)
    return _moe(h, comb, x1, w_gate, w_up, w_down)


def _even_attn(x2, b, s, norm_g, w_in, q_norm, k_norm, sink, w_out):
    w = jnp.concatenate([_gqa_cols(w_in[:, :MIX]), w_in[:, MIX:]], axis=1).astype(BF16)
    n = w.shape[1]
    kvw = N_KV * HEAD_DIM
    gain = jnp.concatenate([jnp.tile(q_norm, N_HEADS), jnp.tile(k_norm, N_KV),
                            jnp.ones((n - MIX - kvw,), F32)]).reshape(1, n)
    plan, col = [], 0
    for width, op in ((MIX, "norm"), (kvw, "norm"), (kvw, None), (MIX, None), (MIX, None), (MIX, None)):
        plan.append((col, width, op))
        col += width
    qa, ka, va, qb, kb, vb = _proj(x2, norm_g, w, gain, tuple(plan), [BF16] * 6)
    r3 = lambda a: a.reshape(b, s, a.shape[-1])
    slopes = _alibi(N_HEADS)
    oa = _band_attention(r3(qa), r3(ka), r3(va), window=A_WINDOW,
                         slopes=[slopes[h] for h in GQA_PERM], sink=sink[np.asarray(GQA_PERM)])
    ob = _stick_attention(r3(qb), r3(kb), r3(vb))
    w1 = _gqa_rows(w_out[:MIX]).astype(BF16)
    w2 = w_out[MIX:].astype(BF16)
    t = b * s
    return [oa.reshape(t, MIX), ob.reshape(t, MIX)], w1, w2


def _rank_rows(score, j, n, seg):
    row = _iota(score.shape, 0)
    rank = jnp.zeros(score.shape, F32)
    for jj in range(n):
        other = score[jj:jj + 1, :]
        for sgm in range(1, score.shape[0] // seg):
            other = jnp.where(row // seg == sgm, score[sgm * seg + jj:sgm * seg + jj + 1, :], other)
        beats = (other > score) | ((other == score) & (jj < j))
        rank = rank + jnp.where(beats, 1.0, 0.0)
    return rank


def _ones_beside(v, p):
    lane = _iota((1, LANES), 1)
    return jnp.where(lane // HEAD_DIM == p, v, jnp.ones_like(v))


def _normalize(acc):
    return acc / pltpu.roll(acc, HEAD_DIM, 1)


def _moba_kernel(slope_ref, q_ref, k_ref, v_ref, o_ref, kmean_ref, vt_ref, m_ref, acc_ref, *,
                 blk, nblk, topk, cpb):
    g = pl.program_id(1)
    i = pl.program_id(2)
    s_len = k_ref.shape[0]
    heads = [(cc, p) for cc in range(cpb) for p in range(HEADS_PER_CHUNK)]
    nh = len(heads)
    chunk = lambda cc: slice(cc * LANES, (cc + 1) * LANES)

    @pl.when(i == 0)
    def _():
        member = (_iota((LANES, s_len), 1) // blk == _iota((LANES, s_len), 0)).astype(BF16)
        for cc in range(cpb):
            kmean_ref[cc] = _dot(member, k_ref[:, chunk(cc)]) * (1.0 / blk)
            vt_ref[cc] = v_ref[:, chunk(cc)].astype(F32).T.astype(BF16)

    hms = _half_masks(BF16)
    q0 = pl.multiple_of(i * blk, blk)
    rel = (_iota((blk, blk), 0) - _iota((blk, blk), 1)).astype(F32)
    slopes = [slope_ref[HEADS_PER_CHUNK * (g * cpb + cc) + p] for cc, p in heads]
    kms = [_split(kmean_ref[cc]) for cc in range(cpb)]
    gates = [_dot_nt(kms[cc][0], q_ref[:, chunk(cc)] * hms[p]) + _dot_nt(kms[cc][1], q_ref[:, chunk(cc)] * hms[p])
             for cc, p in heads]
    lane = _iota((blk, LANES), 1)
    off_f = _iota((blk, LANES), 0).astype(F32)
    off_b = off_f.astype(BF16)
    spare = lambda p, n: (1 - p) * HEAD_DIM + n

    def with_key_lanes(kj, p, shift):
        return jnp.where(lane == spare(p, 0), 1.0,
                         jnp.where(lane == spare(p, 1), off_b,
                                   jnp.where(lane == spare(p, 2), shift.astype(BF16), kj)))

    qs = []
    for h, (cc, p) in enumerate(heads):
        extra = jnp.where(lane == spare(p, 0), -slopes[h] * off_f,
                          jnp.where((lane == spare(p, 1)) | (lane == spare(p, 2)), slopes[h], 0.0))
        qs.append(q_ref[:, chunk(cc)] * (hms[p] * SCALE) + extra.astype(BF16))
    zero = jnp.zeros((), F32)
    raw = [_dot_nt(with_key_lanes(k_ref[pl.ds(q0, blk), chunk(cc)], p, zero), qs[h])
           for h, (cc, p) in enumerate(heads)]
    nrow = -(-nblk // 8) * 8
    blk_id = _iota((nrow, blk), 0)
    skips = []
    for h in range(nh):
        gate = jnp.where(blk_id < i, gates[h][:nrow], NEG)
        rank = _rank_rows(gate, blk_id, nblk, nrow)
        skip_t = jnp.where((rank < topk) & (blk_id < i), 0.0, 1.0)
        skip_t = jnp.concatenate([skip_t, jnp.ones((LANES - nrow, blk), F32)], axis=0)
        skips.append(skip_t.T.astype(BF16))
    own_bias = jnp.where(rel <= 0, 0.0, NEG)
    vrow = _iota((LANES, blk), 0)

    def values_t(k0, cc, p):
        vt = vt_ref[cc, :, pl.ds(k0, blk)]
        return jnp.where(vrow // HEAD_DIM == p, vt, jnp.ones_like(vt))

    es = []
    for h in range(nh):
        s = raw[h] + own_bias
        m = jnp.max(s, axis=0, keepdims=True)
        m_ref[h] = m
        es.append(jnp.exp(s - m).astype(BF16))
    for h, (cc, p) in enumerate(heads):
        acc_ref[h] = _dot(values_t(q0, cc, p), es[h])

    def body(j, carry):
        k0 = pl.multiple_of(j * blk, blk)
        shift = ((j - i) * blk).astype(F32)
        to_lane = [((_iota((LANES, LANES), 0) == j) & (_iota((LANES, LANES), 1) == spare(p, 0))).astype(BF16)
                   for p in range(HEADS_PER_CHUNK)]
        skipped = [_dot(skips[h], to_lane[p]) for h, (cc, p) in enumerate(heads)]
        raw = [_dot_nt(with_key_lanes(k_ref[pl.ds(k0, blk), chunk(cc)], p, shift),
                       qs[h] + (skipped[h] * NEG).astype(BF16))
               for h, (cc, p) in enumerate(heads)]
        es, alphas = [], []
        for h in range(nh):
            s = raw[h]
            m_old = m_ref[h]
            m_new = jnp.maximum(m_old, jnp.max(s, axis=0, keepdims=True))
            m_ref[h] = m_new
            es.append(jnp.exp(s - m_new).astype(BF16))
            alphas.append(jnp.exp(m_old - m_new))
        pvs = [_dot(values_t(k0, cc, p), es[h]) for h, (cc, p) in enumerate(heads)]
        for h in range(nh):
            acc_ref[h] = alphas[h] * acc_ref[h] + pvs[h]
        return carry

    lax.fori_loop(0, i, body, 0)
    lane = _iota((1, LANES), 1)
    for cc in range(cpb):
        outs = []
        for p in range(HEADS_PER_CHUNK):
            acc = acc_ref[HEADS_PER_CHUNK * cc + p]
            den = acc[(1 - p) * HEAD_DIM:(1 - p) * HEAD_DIM + 1, :]
            outs.append((acc / den).T)
        o_ref[:, chunk(cc)] = jnp.where(lane < HEAD_DIM, outs[0], outs[1]).astype(o_ref.dtype)


def _moba_attention(q, k, v, slopes, cpb=2):
    b, s, _ = q.shape
    assert s % C_BLOCK == 0 and C_BLOCK <= 256 and all(_is_pow2(x) for x in slopes)
    nblk = s // C_BLOCK
    assert nblk <= LANES
    wide = cpb * LANES
    nh = cpb * HEADS_PER_CHUNK
    spec_q = pl.BlockSpec((None, C_BLOCK, wide), lambda bi, c, i, sl: (bi, i, c))
    spec_kv = pl.BlockSpec((None, s, wide), lambda bi, c, i, sl: (bi, 0, c))
    return pl.pallas_call(
        functools.partial(_moba_kernel, blk=C_BLOCK, nblk=nblk, topk=min(C_TOPK, nblk), cpb=cpb),
        out_shape=jax.ShapeDtypeStruct((b, s, MIX), BF16),
        grid_spec=pltpu.PrefetchScalarGridSpec(
            num_scalar_prefetch=1,
            grid=(b, N_CHUNKS // cpb, nblk),
            in_specs=[spec_q, spec_kv, spec_kv],
            out_specs=spec_q,
            scratch_shapes=[pltpu.VMEM((cpb, LANES, LANES), F32), pltpu.VMEM((cpb, LANES, s), BF16),
                            pltpu.VMEM((nh, 1, C_BLOCK), F32), pltpu.VMEM((nh, LANES, C_BLOCK), F32)],
        ),
        compiler_params=_cparams("parallel", "parallel", "arbitrary"),
        name="moba_attention",
    )(jnp.asarray(slopes, F32), q, k, v)


def _compress_kernel(xk_ref, xv_ref, wk_lo_ref, wk_hi_ref, wv_lo_ref, wv_hi_ref,
                     pk_lo_ref, pk_hi_ref, pv_lo_ref, pv_hi_ref, gain_ref, kc_ref, vc_ref):
    def compress(x_ref, w_lo_ref, w_hi_ref, p_lo_ref, p_hi_ref):
        x = x_ref[...]
        nrow = x.shape[0]
        first = _dot(x, w_lo_ref[...])
        second = pltpu.roll(_dot(x, w_hi_ref[...]), nrow - 1, 0)
        p_lo = jnp.broadcast_to(p_lo_ref[...], (8, p_lo_ref.shape[1]))
        p_hi = jnp.broadcast_to(p_hi_ref[...], (8, p_hi_ref.shape[1]))
        bias = _dot_hilo(p_lo, w_lo_ref[...]) + _dot_hilo(p_hi, w_hi_ref[...])
        return first + second + bias[0:1]

    kc = compress(xk_ref, wk_lo_ref, wk_hi_ref, pk_lo_ref, pk_hi_ref)
    kc = kc * lax.rsqrt(_head_mean_sq(kc) + EPS) * gain_ref[...]
    kc_ref[...] = kc.astype(kc_ref.dtype)
    vc_ref[...] = compress(xv_ref, wv_lo_ref, wv_hi_ref, pv_lo_ref, pv_hi_ref).astype(vc_ref.dtype)


def _compress_weights(pos, w):
    half = CMP_LEN // 2
    eye = jnp.eye(N_KV, dtype=F32)
    wd = jnp.einsum("gh,lde->lgdhe", eye, w).reshape(CMP_LEN, LANES, LANES)
    w_lo = wd[:half].reshape(half * LANES, LANES).astype(BF16)
    w_hi = wd[half:].reshape(half * LANES, LANES).astype(BF16)
    pt = jnp.tile(pos, (1, N_KV))
    return w_lo, w_hi, pt[:half].reshape(1, half * LANES), pt[half:].reshape(1, half * LANES)


def _compress(kcmp, vcmp, cmp_pos, cmp_w, k_gain):
    b, s, _ = kcmp.shape
    assert CMP_LEN == 2 * CMP_STRIDE and s % CMP_STRIDE == 0
    nrow = s // CMP_STRIDE
    wide = CMP_STRIDE * LANES
    xk = kcmp.reshape(b, nrow, wide)
    xv = vcmp.reshape(b, nrow, wide)
    wk = _compress_weights(cmp_pos[0], cmp_w[0])
    wv = _compress_weights(cmp_pos[1], cmp_w[1])
    consts = [wk[0], wk[1], wv[0], wv[1], wk[2], wk[3], wv[2], wv[3],
              jnp.tile(k_gain, N_KV).reshape(1, LANES)]
    spec_x = pl.BlockSpec((None, nrow, wide), lambda bi: (bi, 0, 0))
    spec_o = pl.BlockSpec((None, nrow, LANES), lambda bi: (bi, 0, 0))
    return pl.pallas_call(
        _compress_kernel,
        out_shape=[jax.ShapeDtypeStruct((b, nrow, LANES), BF16)] * 2,
        grid=(b,),
        in_specs=[spec_x, spec_x] + [pl.BlockSpec(a.shape, lambda bi: (0, 0)) for a in consts],
        out_specs=[spec_o, spec_o],
        compiler_params=_cparams("parallel"),
        name="nsa_compress",
    )(xk, xv, *consts)


def _nsa_cmp_kernel(q_ref, kc_ref, vc_ref, o_ref, sel_ref, *, tq, n_cmp, n_slc, topn, slopes):
    q0 = pl.program_id(1) * tq
    kc = kc_ref[...]
    vc = vc_ref[...]
    ncp = kc.shape[0]
    rows = N_CHUNKS * tq
    seg = LANES // N_KV
    qpos = q0 + _iota((rows, ncp), 0) % tq
    ncol = _iota((rows, ncp), 1)
    diff = qpos - (ncol * CMP_STRIDE + CMP_LEN - 1)
    mask = (diff >= 0) & (ncol < n_cmp)
    diff_f = diff.astype(F32)
    orow = _iota((LANES, ncp), 0)
    cst = _iota((LANES, ncp), 1) * CMP_STRIDE
    sst = (orow % seg) * SLC_BLOCK
    overlap = (cst < sst + SLC_BLOCK) & (cst + CMP_LEN > sst) & (orow % seg < n_slc)
    halves = range(HEADS_PER_CHUNK)
    hms = _half_masks(BF16)
    raw = [_dot_nt(_stack_group(q_ref, hms[p] * SCALE), kc) for p in halves]
    pcs = []
    for p in halves:
        slope = _per_chunk_rows(rows, tq, [slopes[2 * c + p] for c in range(N_CHUNKS)])
        sc = jnp.where(mask, raw[p] - slope * diff_f, NEG)
        mx = jnp.max(sc, axis=-1, keepdims=True)
        e = jnp.where(mask, jnp.exp(sc - mx), 0.0)
        den = jnp.sum(e, axis=-1, keepdims=True)
        pcs.append(e / jnp.where(den > 0, den, 1.0))
    _store_group(o_ref, [_dot(pcs[p].astype(BF16), vc) for p in halves], tq)
    p_slc = jnp.zeros((LANES, tq), F32)
    for p in halves:
        pg = pcs[p][0:tq]
        for c in range(1, N_CHUNKS):
            pg = pg + pcs[p][c * tq:(c + 1) * tq]
        ov = (overlap & (orow // seg == p)).astype(BF16)
        pg_hi, pg_lo = _split(pg)
        p_slc = p_slc + (_dot_nt(ov, pg_hi) + _dot_nt(ov, pg_lo))
    j = _iota((LANES, tq), 0) % seg
    cur = (q0 + _iota((LANES, tq), 1)) // SLC_BLOCK
    forced = (j == 0) | (j == cur) | (j == cur - 1)
    usable = (j <= cur) & (j < n_slc)
    score = jnp.where(usable, p_slc + jnp.where(forced, FORCE_BONUS, 0.0), NEG)
    rank = _rank_rows(score, j, n_slc, seg)
    sel_t = jnp.where((rank < topn) & usable, 1.0, 0.0)
    sel_ref[...] = sel_t.T.astype(sel_ref.dtype)


def _nsa_cmp(q, kc, vc, slopes, tq=128):
    b, s, _ = q.shape
    ncp = kc.shape[1]
    n_slc = s // SLC_BLOCK
    assert n_slc <= LANES // N_KV and ncp % LANES == 0
    kern = functools.partial(_nsa_cmp_kernel, tq=tq, n_cmp=ncp - 1, n_slc=n_slc,
                             topn=min(SLC_TOPN, n_slc), slopes=slopes)
    spec_c = pl.BlockSpec((None, ncp, LANES), lambda bi, n: (bi, 0, 0))
    return pl.pallas_call(
        kern,
        out_shape=[jax.ShapeDtypeStruct((b, s, MIX), BF16), jax.ShapeDtypeStruct((b, s, LANES), BF16)],
        grid=(b, s // tq),
        in_specs=[pl.BlockSpec((None, tq, MIX), lambda bi, n: (bi, n, 0)), spec_c, spec_c],
        out_specs=[pl.BlockSpec((None, tq, MIX), lambda bi, n: (bi, n, 0)),
                   pl.BlockSpec((None, tq, LANES), lambda bi, n: (bi, n, 0))],
        compiler_params=_cparams("parallel", "parallel"),
        name="nsa_compressed",
    )(q, kc, vc)


def _nsa_slc_kernel(q_ref, k_ref, v_ref, sel_ref, o_ref, vt_ref, m_ref, acc_ref, *, tq, tk, slopes):
    n = pl.program_id(1)
    q0 = n * tq
    rows = N_CHUNKS * tq
    seg = LANES // N_KV
    halves = range(HEADS_PER_CHUNK)
    skip = 1.0 - sel_ref[...]
    nsub = tk // SLC_BLOCK
    lane_q = _iota((tq, LANES), 1)
    row_q = _iota((tq, LANES), 0).astype(F32)
    lane_k = _iota((tk, LANES), 1)
    off_k = _iota((tk, LANES), 0)
    spare = lambda p, n: (1 - p) * HEAD_DIM + n
    hms = _half_masks(BF16)
    qs = []
    for p in halves:
        extra = jnp.concatenate(
            [jnp.where(lane_q == spare(p, 0), -slopes[2 * c + p] * row_q,
                       jnp.where((lane_q == spare(p, 1)) | (lane_q == spare(p, 2)), slopes[2 * c + p], 0.0))
             for c in range(N_CHUNKS)], axis=0)
        qs.append(_stack_group(q_ref, hms[p] * SCALE) + extra.astype(BF16))
    @pl.when(n == 0)
    def _():
        vt_ref[...] = v_ref[...].astype(F32).T.astype(BF16)

    for p in halves:
        m_ref[p] = jnp.full((1, rows), 0.1 * NEG, F32)
        acc_ref[p] = jnp.zeros((LANES, rows), F32)

    def tile(kt, diagonal):
        k0 = pl.multiple_of(kt * tk, tk)
        kj = k_ref[pl.ds(k0, tk), :]
        vt = vt_ref[:, pl.ds(k0, tk)]
        shift = (k0 - q0).astype(F32).astype(BF16)
        raw = []
        for p in halves:
            in_sub = (lane_k >= spare(p, 3)) & (lane_k < spare(p, 3) + nsub)
            kp = jnp.where(lane_k == spare(p, 0), 1.0,
                           jnp.where(lane_k == spare(p, 1), off_k.astype(F32).astype(BF16),
                                     jnp.where(lane_k == spare(p, 2), shift,
                                               jnp.where(in_sub, jnp.where(off_k // SLC_BLOCK == lane_k - spare(p, 3),
                                                                           1.0, 0.0).astype(BF16), kj))))
            src = _iota((LANES, LANES), 0) - (p * seg + k0 // SLC_BLOCK)
            dst = _iota((LANES, LANES), 1) - spare(p, 3)
            to_lane = ((src == dst) & (dst >= 0) & (dst < nsub)).astype(BF16)
            bias = (_dot(skip, to_lane) * NEG).astype(BF16)
            raw.append(_dot_nt(kp, qs[p] + jnp.concatenate([bias] * N_CHUNKS, axis=0)))
        if diagonal:
            future = jnp.where(_iota((tk, tq), 1) + (q0 - k0) >= _iota((tk, tq), 0), 0.0, NEG)
            future = jnp.concatenate([future] * N_CHUNKS, axis=1)
        es, alphas = [], []
        for p in halves:
            s = raw[p] + future if diagonal else raw[p]
            m_old = m_ref[p]
            m_new = jnp.maximum(m_old, jnp.max(s, axis=0, keepdims=True))
            m_ref[p] = m_new
            es.append(jnp.exp(s - m_new).astype(BF16))
            alphas.append(jnp.exp(m_old - m_new))
        row = _iota((LANES, tk), 0)
        pvs = [_dot(jnp.where(row // HEAD_DIM == p, vt, jnp.ones_like(vt)), es[p]) for p in halves]
        for p in halves:
            acc_ref[p] = alphas[p] * acc_ref[p] + pvs[p]

    def body(kt, carry):
        tile(kt, False)
        return carry

    n_past = q0 // tk
    lax.fori_loop(0, n_past, body, 0)
    tile(n_past, True)
    outs = []
    for p in halves:
        acc = acc_ref[p]
        den = acc[(1 - p) * HEAD_DIM:(1 - p) * HEAD_DIM + 1, :]
        outs.append((acc / den).T)
    _store_group(o_ref, outs, tq)


def _nsa_slc(q, k, v, sel, slopes, tq=256, tk=256):
    b, s, _ = q.shape
    assert tk % tq == 0 and tq % SLC_BLOCK == 0 and s % tk == 0 and tk <= 256
    assert all(_is_pow2(x) for x in slopes)
    rows = N_CHUNKS * tq
    spec_q = pl.BlockSpec((None, tq, MIX), lambda bi, n: (bi, n, 0))
    spec_kv = pl.BlockSpec((None, s, LANES), lambda bi, n: (bi, 0, 0))
    return pl.pallas_call(
        functools.partial(_nsa_slc_kernel, tq=tq, tk=tk, slopes=slopes),
        out_shape=jax.ShapeDtypeStruct((b, s, MIX), BF16),
        grid=(b, s // tq),
        in_specs=[spec_q, spec_kv, spec_kv, pl.BlockSpec((None, tq, LANES), lambda bi, n: (bi, n, 0))],
        out_specs=spec_q,
        scratch_shapes=[pltpu.VMEM((LANES, s), BF16), pltpu.VMEM((N_KV, 1, rows), F32),
                        pltpu.VMEM((N_KV, LANES, rows), F32)],
        compiler_params=_cparams("parallel", "arbitrary"),
        name="nsa_selected",
    )(q, k, v, sel)


def _odd_attn(x2, b, s, norm_g, w_in, c_q_norm, c_k_norm, d_q_norm, d_k_norm, cmp_pos, cmp_w, w_out):
    kvw = N_KV * HEAD_DIM
    n_gate = N_HEADS * 3
    qd0 = 3 * MIX
    w = jnp.concatenate([w_in[:, :qd0], _gqa_cols(w_in[:, qd0:qd0 + MIX]), w_in[:, qd0 + MIX:],
                         jnp.zeros((w_in.shape[0], LANES - n_gate), F32)], axis=1).astype(BF16)
    n = w.shape[1]
    ones = lambda width: jnp.ones((width,), F32)
    gain = jnp.concatenate([
        jnp.tile(c_q_norm, N_HEADS), jnp.tile(c_k_norm, N_HEADS), ones(MIX), jnp.tile(d_q_norm, N_HEADS),
        ones(2 * kvw), jnp.tile(d_k_norm[1], N_KV), ones(kvw), jnp.tile(d_k_norm[2], N_KV), ones(kvw),
        ones(LANES)]).reshape(1, n)
    plan, col = [], 0
    for width, op in ((MIX, "norm"), (MIX, "norm"), (MIX, None), (MIX, "norm"), (kvw, None), (kvw, None),
                      (kvw, "norm"), (kvw, None), (kvw, "norm"), (kvw, None), (LANES, "sigmoid")):
        plan.append((col, width, op))
        col += width
    outs = _proj(x2, norm_g, w, gain, tuple(plan), [BF16] * 10 + [F32])
    r3 = lambda a: a.reshape(b, s, a.shape[-1])
    qc, kc, vc, qd, kcmp, vcmp, kslc, vslc, kwin, vwin = [r3(a) for a in outs[:10]]
    gates = outs[10]
    slopes = _alibi(N_HEADS)
    gslopes = [slopes[h] for h in GQA_PERM]
    oc = _moba_attention(qc, kc, vc, slopes)
    k_cmp, v_cmp = _compress(kcmp, vcmp, cmp_pos, cmp_w, d_k_norm[0])
    o_cmp, sel = _nsa_cmp(qd, k_cmp, v_cmp, gslopes)
    o_slc = _nsa_slc(qd, kslc, vslc, sel, gslopes)
    o_win = _band_attention(qd, kwin, vwin, window=D_WINDOW, slopes=gslopes)
    t = b * s
    w1 = w_out[:MIX].astype(BF16)
    w2 = _gqa_rows(w_out[MIX:]).astype(BF16)
    flat = lambda a: a.reshape(t, a.shape[-1])
    return [flat(oc), flat(o_cmp), flat(o_slc), flat(o_win), gates], w1, w2


def kernel(x, ev_norm, ev_w_in, ev_q_norm, ev_k_norm, ev_sink, ev_w_out, od_norm, od_w_in, od_c_q_norm,
           od_c_k_norm, od_d_q_norm, od_d_k_norm, od_cmp_pos, od_cmp_w, od_w_out, moe_norm, moe_w_grp,
           moe_b_grp, moe_w_exp, moe_b_exp, moe_w_gate, moe_w_up, moe_w_down):
    b, s, d = x.shape
    x2 = x.reshape(b * s, d)
    depth = moe_norm.shape[0]
    for layer in range(depth):
        i = layer // 2
        if layer % 2 == 0:
            attn, w1, w2 = _even_attn(x2, b, s, ev_norm[i], ev_w_in[i], ev_q_norm[i], ev_k_norm[i],
                                      ev_sink[i], ev_w_out[i])
        else:
            attn, w1, w2 = _odd_attn(x2, b, s, od_norm[i], od_w_in[i], od_c_q_norm[i], od_c_k_norm[i],
                                     od_d_q_norm[i], od_d_k_norm[i], od_cmp_pos[i], od_cmp_w[i], od_w_out[i])
        x2 = _moe_block(x2, attn, w1, w2, moe_norm[layer], moe_w_grp[layer], moe_b_grp[layer],
                        moe_w_exp[layer], moe_b_exp[layer], moe_w_gate[layer], moe_w_up[layer],
                        moe_w_down[layer])
    return x2.reshape(b, s, d)
```

```python
import functools
import math

import numpy as np
import jax
import jax.numpy as jnp
from jax import lax
from jax.experimental import pallas as pl
from jax.experimental.pallas import tpu as pltpu

F32 = jnp.float32
BF16 = jnp.bfloat16

LANES = 128
HEAD_DIM = 64
HEADS_PER_CHUNK = LANES // HEAD_DIM
D_MODEL = 1024
N_HEADS = 8
N_KV = 2
GROUP = N_HEADS // N_KV
N_CHUNKS = N_HEADS // HEADS_PER_CHUNK
MIX = N_HEADS * HEAD_DIM
SCALE = 1.0 / math.sqrt(HEAD_DIM)
EPS = 1e-6
NEG = -1e30
EXP_UNDERFLOW = -104.0

A_WINDOW = 128
C_BLOCK = 256
C_TOPK = 3
CMP_LEN = 32
CMP_STRIDE = 16
SLC_BLOCK = 64
SLC_TOPN = 16
D_WINDOW = 512
FORCE_BONUS = 1000.0

N_GROUPS = 4
EXPERTS_PER_GROUP = 8
N_EXPERTS = N_GROUPS * EXPERTS_PER_GROUP
EXPERT_FF = 256

VMEM_LIMIT = 48 * 1024 * 1024

GQA_PERM = tuple(h for c in range(N_CHUNKS) for h in (c, c + GROUP))
MHA_PERM = tuple(range(N_HEADS))


def _alibi(n_heads):
    return [float(2.0 ** (-8.0 * (i + 1) / n_heads)) for i in range(n_heads)]


def _is_pow2(x):
    return math.frexp(x)[0] == 0.5


def _gqa_cols(w):
    lead = w.shape[:-1]
    w = w.reshape(*lead, N_KV, GROUP, HEAD_DIM)
    return jnp.swapaxes(w, -3, -2).reshape(*lead, MIX)


def _gqa_rows(w):
    tail = w.shape[1:]
    return jnp.swapaxes(w.reshape(N_KV, GROUP, HEAD_DIM, *tail), 0, 1).reshape(MIX, *tail)


def _cparams(*sem):
    return pltpu.CompilerParams(dimension_semantics=sem, vmem_limit_bytes=VMEM_LIMIT)


def _dot(a, b):
    return jnp.dot(a, b, preferred_element_type=F32)


def _dot_nt(a, b):
    return lax.dot_general(a, b, (((1,), (1,)), ((), ())), preferred_element_type=F32)


def _split(x):
    hi = x.astype(BF16)
    lo = (x - hi.astype(F32)).astype(BF16)
    return hi, lo


def _dot_hilo(a, b):
    hi, lo = _split(a)
    return _dot(hi, b) + _dot(lo, b)


def _iota(shape, dim):
    return lax.broadcasted_iota(jnp.int32, shape, dim)


def _half_masks(dtype):
    lane = _iota((1, LANES), 1)
    return [(lane // HEAD_DIM == p).astype(dtype) for p in range(HEADS_PER_CHUNK)]


def _head_mean_sq(y):
    same = (_iota((LANES, LANES), 0) // HEAD_DIM == _iota((LANES, LANES), 1) // HEAD_DIM)
    return _dot((y * y).astype(BF16), same.astype(BF16)) * (1.0 / HEAD_DIM)


def _proj_kernel(x_ref, g_ref, w_ref, gain_ref, *out_refs, plan):
    x = x_ref[...]
    ms = jnp.mean(x * x, axis=-1, keepdims=True)
    xn = (x * lax.rsqrt(ms + EPS) * g_ref[...]).astype(BF16)
    for o_ref, (col0, width, op) in zip(out_refs, plan):
        for a in range(0, width, 2 * LANES):
            wd = min(2 * LANES, width - a)
            y = _dot(xn, w_ref[:, col0 + a:col0 + a + wd])
            for c in range(0, wd, LANES):
                yc = y[:, c:c + LANES]
                if op == "norm":
                    gain = gain_ref[:, col0 + a + c:col0 + a + c + LANES]
                    yc = yc * lax.rsqrt(_head_mean_sq(yc) + EPS) * gain
                elif op == "sigmoid":
                    yc = jax.nn.sigmoid(yc)
                o_ref[:, a + c:a + c + LANES] = yc.astype(o_ref.dtype)


def _proj(x2, norm_g, w, gain, plan, out_dtypes, tm=512):
    t, d = x2.shape
    n = w.shape[1]
    out_shape = [jax.ShapeDtypeStruct((t, width), dt) for (_, width, _), dt in zip(plan, out_dtypes)]
    return pl.pallas_call(
        functools.partial(_proj_kernel, plan=plan),
        out_shape=out_shape,
        grid=(t // tm,),
        in_specs=[
            pl.BlockSpec((tm, d), lambda i: (i, 0)),
            pl.BlockSpec((1, d), lambda i: (0, 0)),
            pl.BlockSpec((d, n), lambda i: (0, 0)),
            pl.BlockSpec((1, n), lambda i: (0, 0)),
        ],
        out_specs=[pl.BlockSpec((tm, width), lambda i: (i, 0)) for (_, width, _) in plan],
        compiler_params=_cparams("parallel"),
        name="norm_proj",
    )(x2, norm_g.reshape(1, d), w, gain)


def _stack_group(q_ref, halfmask):
    return jnp.concatenate(
        [q_ref[:, c * LANES:(c + 1) * LANES] * halfmask for c in range(N_CHUNKS)], axis=0)


def _per_chunk_rows(rows, tq, values):
    r = _iota((rows, 1), 0) // tq
    out = jnp.full((rows, 1), values[N_CHUNKS - 1], F32)
    for c in range(N_CHUNKS - 2, -1, -1):
        out = jnp.where(r == c, values[c], out)
    return out


def _per_chunk_lanes(rows, tq, values):
    r = _iota((1, rows), 1) // tq
    out = jnp.full((1, rows), values[N_CHUNKS - 1], F32)
    for c in range(N_CHUNKS - 2, -1, -1):
        out = jnp.where(r == c, values[c], out)
    return out


def _band_kernel(*refs, tq, window, wpad, slopes, has_sink):
    if has_sink:
        sink_ref, q_ref, k_ref, v_ref, o_ref, vt_ref = refs
    else:
        q_ref, k_ref, v_ref, o_ref, vt_ref = refs
    n = pl.program_id(1)

    @pl.when(n == 0)
    def _():
        vt_ref[...] = v_ref[...].astype(F32).T.astype(BF16)

    q0 = n * tq
    kw = tq + wpad
    kstart = pl.multiple_of(jnp.maximum(q0 - wpad, 0), LANES)
    ks = k_ref[pl.ds(kstart, kw), :]
    vt = vt_ref[:, pl.ds(kstart, kw)]
    rows = N_CHUNKS * tq
    halves = range(HEADS_PER_CHUNK)
    hms = _half_masks(BF16)
    spare = lambda p, j: (1 - p) * HEAD_DIM + j
    lane_q = _iota((tq, LANES), 1)
    row_q = _iota((tq, LANES), 0).astype(F32)
    lane_k = _iota((kw, LANES), 1)
    off_k = _iota((kw, LANES), 0)
    fine = (off_k % LANES).astype(F32).astype(BF16)
    coarse = ((off_k // LANES) * LANES + (kstart - q0)).astype(F32).astype(BF16)
    diff = _iota((kw, tq), 1) + (q0 - kstart) - _iota((kw, tq), 0)
    band = jnp.where((diff >= 0) & (diff < window), 0.0, NEG)
    band = jnp.concatenate([band] * N_CHUNKS, axis=1)
    raw = []
    for p in halves:
        extra = jnp.concatenate(
            [jnp.where(lane_q == spare(p, 0), -slopes[2 * c + p] * row_q,
                       jnp.where((lane_q == spare(p, 1)) | (lane_q == spare(p, 2)), slopes[2 * c + p], 0.0))
             for c in range(N_CHUNKS)], axis=0)
        qp = _stack_group(q_ref, hms[p] * SCALE) + extra.astype(BF16)
        kp = jnp.where(lane_k == spare(p, 0), 1.0,
                       jnp.where(lane_k == spare(p, 1), fine, jnp.where(lane_k == spare(p, 2), coarse, ks)))
        raw.append(_dot_nt(kp, qp))
    es, sinks = [], []
    for p in halves:
        s = raw[p] + band
        mx = jnp.max(s, axis=0, keepdims=True)
        if has_sink:
            sk = _per_chunk_lanes(rows, tq, [sink_ref[2 * c + p] for c in range(N_CHUNKS)])
            mx = jnp.maximum(mx, sk)
            sinks.append(jnp.exp(sk - mx))
        es.append(jnp.exp(s - mx).astype(BF16))
    vrow = _iota((LANES, kw), 0)
    outs = []
    for p in halves:
        acc = _dot(jnp.where(vrow // HEAD_DIM == p, vt, jnp.ones_like(vt)), es[p])
        den = acc[(1 - p) * HEAD_DIM:(1 - p) * HEAD_DIM + 1, :]
        if has_sink:
            den = den + sinks[p]
        outs.append((acc / den).T)
    _store_group(o_ref, outs, tq)


def _store_group(o_ref, os_, tq):
    lane = _iota((1, LANES), 1)
    for c in range(N_CHUNKS):
        o = jnp.where(lane < HEAD_DIM, os_[0][c * tq:(c + 1) * tq], os_[1][c * tq:(c + 1) * tq])
        o_ref[:, c * LANES:(c + 1) * LANES] = o.astype(o_ref.dtype)


def _band_attention(q, k, v, *, window, slopes, sink=None, tq=128):
    b, s, _ = q.shape
    wpad = -(-window // LANES) * LANES
    assert s >= tq + wpad and s % tq == 0 and tq <= LANES and all(_is_pow2(x) for x in slopes)
    kern = functools.partial(_band_kernel, tq=tq, window=window, wpad=wpad,
                             slopes=slopes, has_sink=sink is not None)
    in_specs = [
        pl.BlockSpec((None, tq, MIX), lambda bi, n: (bi, n, 0)),
        pl.BlockSpec((None, s, LANES), lambda bi, n: (bi, 0, 0)),
        pl.BlockSpec((None, s, LANES), lambda bi, n: (bi, 0, 0)),
    ]
    args = [q, k, v]
    if sink is not None:
        in_specs = [pl.BlockSpec(memory_space=pltpu.SMEM)] + in_specs
        args = [sink] + args
    return pl.pallas_call(
        kern,
        out_shape=jax.ShapeDtypeStruct((b, s, MIX), BF16),
        grid=(b, s // tq),
        in_specs=in_specs,
        out_specs=pl.BlockSpec((None, tq, MIX), lambda bi, n: (bi, n, 0)),
        scratch_shapes=[pltpu.VMEM((LANES, s), BF16)],
        compiler_params=_cparams("parallel", "arbitrary"),
        name="band_attention",
    )(*args)


def _stick_kernel(q_ref, k_ref, v_ref, o_ref, acc_ref, run_ref, *, tq, cpb):
    i = pl.program_id(2)
    q0 = pl.multiple_of(i * tq, tq)
    hms = _half_masks(BF16)
    heads = [(cc, p) for cc in range(cpb) for p in range(HEADS_PER_CHUNK)]
    qs = [q_ref[:, cc * LANES:(cc + 1) * LANES] * (hms[p] * SCALE) for cc, p in heads]
    upper = (_iota((2 * tq, tq), 0) % tq > _iota((2 * tq, tq), 1)).astype(BF16)

    def suffix_sum(x):
        hi, lo = _split(x)
        return _dot(jnp.concatenate([hi, lo], axis=1), upper)

    def block(kstart, diag):
        if diag:
            causal = _iota((tq, tq), 1) < _iota((tq, tq), 0)
        kjs = [k_ref[pl.ds(kstart, tq), cc * LANES:(cc + 1) * LANES] for cc in range(cpb)]
        vjs = [v_ref[pl.ds(kstart, tq), cc * LANES:(cc + 1) * LANES] for cc in range(cpb)]
        zs = [_dot_nt(qs[h], kjs[cc]) for h, (cc, p) in enumerate(heads)]
        lss, lks = [], []
        for z in zs:
            ls = jnp.minimum(z, 0.0) - jnp.log(1.0 + jnp.exp(-jnp.abs(z)))
            lk = ls - z
            if diag:
                lk = jnp.where(causal, lk, 0.0)
            lss.append(ls)
            lks.append(lk)
        sufs = [suffix_sum(lk) for lk in lks]
        ws = []
        for h in range(len(heads)):
            if diag:
                a = jnp.where(causal, jnp.exp(lss[h] + sufs[h]), 0.0)
            else:
                a = jnp.exp(lss[h] + sufs[h] + run_ref[h])
            ws.append(a.astype(BF16))
        pvs = [_dot(ws[h], vjs[cc]) for h, (cc, p) in enumerate(heads)]
        for h in range(len(heads)):
            rowsum = jnp.sum(lks[h], axis=-1, keepdims=True)
            if diag:
                acc_ref[h] = pvs[h]
                run_ref[h] = rowsum
            else:
                acc_ref[h] += pvs[h]
                run_ref[h] += rowsum

    block(q0, True)

    def weights_alive():
        run = run_ref[0]
        for h in range(1, len(heads)):
            run = jnp.maximum(run, run_ref[h])
        return jnp.max(run) > EXP_UNDERFLOW

    def body(carry):
        t, _ = carry
        block(pl.multiple_of((i - 1 - t) * tq, tq), False)
        return t + 1, weights_alive()

    lax.while_loop(lambda c: (c[0] < i) & c[1], body, (jnp.int32(0), weights_alive()))
    lane = _iota((1, LANES), 1)
    for cc in range(cpb):
        o = jnp.where(lane < HEAD_DIM, acc_ref[HEADS_PER_CHUNK * cc], acc_ref[HEADS_PER_CHUNK * cc + 1])
        o_ref[:, cc * LANES:(cc + 1) * LANES] = o.astype(o_ref.dtype)


def _stick_attention(q, k, v, tq=256, cpb=4):
    b, s, _ = q.shape
    wide = cpb * LANES
    n_heads = cpb * HEADS_PER_CHUNK
    spec_q = pl.BlockSpec((None, tq, wide), lambda bi, c, i: (bi, i, c))
    spec_kv = pl.BlockSpec((None, s, wide), lambda bi, c, i: (bi, 0, c))
    return pl.pallas_call(
        functools.partial(_stick_kernel, tq=tq, cpb=cpb),
        out_shape=jax.ShapeDtypeStruct((b, s, MIX), BF16),
        grid=(b, N_CHUNKS // cpb, s // tq),
        in_specs=[spec_q, spec_kv, spec_kv],
        out_specs=spec_q,
        scratch_shapes=[pltpu.VMEM((n_heads, tq, LANES), F32), pltpu.VMEM((n_heads, tq, 1), F32)],
        compiler_params=_cparams("parallel", "parallel", "parallel"),
        name="stick_breaking",
    )(q, k, v)


def _route(logits):
    lane = _iota(logits.shape, 1)
    lane_f = lane.astype(F32)
    ninf = -jnp.inf
    is_g = (lane >= N_EXPERTS) & (lane < N_EXPERTS + N_GROUPS)
    gmax = jnp.max(jnp.where(is_g, logits, ninf), axis=-1, keepdims=True)
    gidx = jnp.min(jnp.where(is_g & (logits == gmax), lane_f - N_EXPERTS, 1e9), axis=-1, keepdims=True)
    p_g = 1.0 / jnp.sum(jnp.where(is_g, jnp.exp(logits - gmax), 0.0), axis=-1, keepdims=True)
    in_grp = (lane < N_EXPERTS) & ((lane // EXPERTS_PER_GROUP).astype(F32) == gidx)
    le = jnp.where(in_grp, logits, ninf)
    m1 = jnp.max(le, axis=-1, keepdims=True)
    i1 = jnp.min(jnp.where(le == m1, lane_f, 1e9), axis=-1, keepdims=True)
    le2 = jnp.where(lane_f == i1, ninf, le)
    m2 = jnp.max(le2, axis=-1, keepdims=True)
    i2 = jnp.min(jnp.where(le2 == m2, lane_f, 1e9), axis=-1, keepdims=True)
    e2 = jnp.exp(m2 - m1)
    w1 = p_g / (1.0 + e2)
    w2 = p_g * e2 / (1.0 + e2)
    return jnp.where(lane_f == i1, w1, 0.0) + jnp.where(lane_f == i2, w2, 0.0)


def _gate_expand(branch):
    r = _iota((LANES, MIX), 0)
    col = _iota((LANES, MIX), 1)
    head = col // LANES + GROUP * ((col % LANES) // HEAD_DIM)
    return (r == 3 * head + branch).astype(BF16)


def _out_kernel(*refs, nsa):
    if nsa:
        (x_ref, o1_ref, ocmp_ref, oslc_ref, owin_ref, gates_ref, w1_ref, w2_ref,
         ng_ref, wr_ref, br_ref, x1_ref, h_ref, comb_ref) = refs
        g = gates_ref[...]
        o2 = (_dot_hilo(g, _gate_expand(0)) * ocmp_ref[...]
              + _dot_hilo(g, _gate_expand(1)) * oslc_ref[...]
              + _dot_hilo(g, _gate_expand(2)) * owin_ref[...]).astype(BF16)
    else:
        (x_ref, o1_ref, o2_ref, w1_ref, w2_ref,
         ng_ref, wr_ref, br_ref, x1_ref, h_ref, comb_ref) = refs
        o2 = o2_ref[...]
    x1 = x_ref[...] + _dot(o1_ref[...], w1_ref[...]) + _dot(o2, w2_ref[...])
    x1_ref[...] = x1
    ms = jnp.mean(x1 * x1, axis=-1, keepdims=True)
    h = x1 * lax.rsqrt(ms + EPS) * ng_ref[...]
    h_ref[...] = h.astype(BF16)
    h_hi, h_lo = _split(h)
    w_hi, w_lo = _split(wr_ref[...])
    logits = _dot(h_hi, w_hi) + (_dot(h_hi, w_lo) + _dot(h_lo, w_hi)) + br_ref[...]
    comb_ref[...] = _route(logits)


def _out_proj_route(x2, attn, w1, w2, moe_g, w_route, b_route, tm=512):
    t, d = x2.shape
    nsa = len(attn) > 2
    row = lambda width: pl.BlockSpec((tm, width), lambda i: (i, 0))
    full = lambda a: pl.BlockSpec(a.shape, lambda i: (0, 0))
    consts = [w1, w2, moe_g.reshape(1, d), w_route, b_route]
    return pl.pallas_call(
        functools.partial(_out_kernel, nsa=nsa),
        out_shape=[jax.ShapeDtypeStruct((t, d), F32), jax.ShapeDtypeStruct((t, d), BF16),
                   jax.ShapeDtypeStruct((t, LANES), F32)],
        grid=(t // tm,),
        in_specs=[row(d)] + [row(a.shape[1]) for a in attn] + [full(a) for a in consts],
        out_specs=[row(d), row(d), row(LANES)],
        compiler_params=_cparams("parallel"),
        name="out_proj_route",
    )(x2, *attn, *consts)


def _moe_kernel(h_ref, comb_ref, x1_ref, wg_ref, wu_ref, wd_ref, o_ref, *, per_step):
    step = pl.program_id(1)

    @pl.when(step == 0)
    def _():
        o_ref[...] = x1_ref[...]

    h = h_ref[...]
    comb = comb_ref[...]
    lane = _iota(comb.shape, 1)
    acts = []
    for j in range(per_step):
        g = _dot(h, wg_ref[j].astype(BF16))
        u = _dot(h, wu_ref[j].astype(BF16))
        c = jnp.sum(jnp.where(lane == step * per_step + j, comb, 0.0), axis=-1, keepdims=True)
        acts.append((c * (g * jax.nn.sigmoid(g) * u)).astype(BF16))
    w_down = jnp.concatenate([wd_ref[j].astype(BF16) for j in range(per_step)], axis=0)
    o_ref[...] += _dot(jnp.concatenate(acts, axis=1), w_down)


def _moe(h, comb, x1, w_gate, w_up, w_down, layer, tm=1024, per_step=2):
    t, d = h.shape
    _, n_exp, _, ff = w_gate.shape
    assert t % tm == 0 and n_exp % per_step == 0
    return pl.pallas_call(
        functools.partial(_moe_kernel, per_step=per_step),
        out_shape=jax.ShapeDtypeStruct((t, d), F32),
        grid=(t // tm, n_exp // per_step),
        in_specs=[
            pl.BlockSpec((tm, d), lambda i, e: (i, 0)),
            pl.BlockSpec((tm, LANES), lambda i, e: (i, 0)),
            pl.BlockSpec((tm, d), lambda i, e: (i, 0)),
            pl.BlockSpec((None, per_step, d, ff), lambda i, e: (layer, e, 0, 0)),
            pl.BlockSpec((None, per_step, d, ff), lambda i, e: (layer, e, 0, 0)),
            pl.BlockSpec((None, per_step, ff, d), lambda i, e: (layer, e, 0, 0)),
        ],
        out_specs=pl.BlockSpec((tm, d), lambda i, e: (i, 0)),
        compiler_params=_cparams("parallel", "arbitrary"),
        name="moe_experts",
    )(h, comb, x1, w_gate, w_up, w_down)


def _moe_block(x2, attn, w1, w2, moe_g, w_grp, b_grp, w_exp, b_exp, w_gate, w_up, w_down, layer):
    d = x2.shape[1]
    pad = LANES - N_EXPERTS - N_GROUPS
    w_route = jnp.concatenate([w_exp, w_grp, jnp.zeros((d, pad), F32)], axis=1)
    b_route = jnp.concatenate([b_exp, b_grp, jnp.zeros((pad,), F32)]).reshape(1, LANES)
    x1, h, comb = _out_proj_route(x2, attn, w1, w2, moe_g, w_route, b_route)
    return _moe(h, comb, x1, w_gate, w_up, w_down, layer)


def _even_attn(x2, b, s, norm_g, w_in, q_norm, k_norm, sink, w_out):
    w = jnp.concatenate([_gqa_cols(w_in[:, :MIX]), w_in[:, MIX:]], axis=1).astype(BF16)
    n = w.shape[1]
    kvw = N_KV * HEAD_DIM
    gain = jnp.concatenate([jnp.tile(q_norm, N_HEADS), jnp.tile(k_norm, N_KV),
                            jnp.ones((n - MIX - kvw,), F32)]).reshape(1, n)
    plan, col = [], 0
    for width, op in ((MIX, "norm"), (kvw, "norm"), (kvw, None), (MIX, None), (MIX, None), (MIX, None)):
        plan.append((col, width, op))
        col += width
    qa, ka, va, qb, kb, vb = _proj(x2, norm_g, w, gain, tuple(plan), [BF16] * 6)
    r3 = lambda a: a.reshape(b, s, a.shape[-1])
    slopes = _alibi(N_HEADS)
    oa = _band_attention(r3(qa), r3(ka), r3(va), window=A_WINDOW,
                         slopes=[slopes[h] for h in GQA_PERM], sink=sink[np.asarray(GQA_PERM)])
    ob = _stick_attention(r3(qb), r3(kb), r3(vb))
    w1 = _gqa_rows(w_out[:MIX]).astype(BF16)
    w2 = w_out[MIX:].astype(BF16)
    t = b * s
    return [oa.reshape(t, MIX), ob.reshape(t, MIX)], w1, w2


def _rank_rows(score, j, n, seg):
    row = _iota(score.shape, 0)
    rank = jnp.zeros(score.shape, F32)
    for jj in range(n):
        other = score[jj:jj + 1, :]
        for sgm in range(1, score.shape[0] // seg):
            other = jnp.where(row // seg == sgm, score[sgm * seg + jj:sgm * seg + jj + 1, :], other)
        beats = (other > score) | ((other == score) & (jj < j))
        rank = rank + jnp.where(beats, 1.0, 0.0)
    return rank


def _ones_beside(v, p):
    lane = _iota((1, LANES), 1)
    return jnp.where(lane // HEAD_DIM == p, v, jnp.ones_like(v))


def _normalize(acc):
    return acc / pltpu.roll(acc, HEAD_DIM, 1)


def _moba_kernel(slope_ref, q_ref, k_ref, v_ref, o_ref, kmean_ref, vt_ref, m_ref, acc_ref, *,
                 blk, nblk, topk, cpb):
    g = pl.program_id(1)
    i = pl.program_id(2)
    s_len = k_ref.shape[0]
    heads = [(cc, p) for cc in range(cpb) for p in range(HEADS_PER_CHUNK)]
    nh = len(heads)
    chunk = lambda cc: slice(cc * LANES, (cc + 1) * LANES)

    @pl.when(i == 0)
    def _():
        member = (_iota((LANES, s_len), 1) // blk == _iota((LANES, s_len), 0)).astype(BF16)
        for cc in range(cpb):
            kmean_ref[cc] = _dot(member, k_ref[:, chunk(cc)]) * (1.0 / blk)
            vt_ref[cc] = v_ref[:, chunk(cc)].astype(F32).T.astype(BF16)

    hms = _half_masks(BF16)
    q0 = pl.multiple_of(i * blk, blk)
    rel = (_iota((blk, blk), 0) - _iota((blk, blk), 1)).astype(F32)
    slopes = [slope_ref[HEADS_PER_CHUNK * (g * cpb + cc) + p] for cc, p in heads]
    kms = [_split(kmean_ref[cc]) for cc in range(cpb)]
    gates = [_dot_nt(kms[cc][0], q_ref[:, chunk(cc)] * hms[p]) + _dot_nt(kms[cc][1], q_ref[:, chunk(cc)] * hms[p])
             for cc, p in heads]
    lane = _iota((blk, LANES), 1)
    off_f = _iota((blk, LANES), 0).astype(F32)
    off_b = off_f.astype(BF16)
    spare = lambda p, n: (1 - p) * HEAD_DIM + n

    def with_key_lanes(kj, p, shift):
        return jnp.where(lane == spare(p, 0), 1.0,
                         jnp.where(lane == spare(p, 1), off_b,
                                   jnp.where(lane == spare(p, 2), shift.astype(BF16), kj)))

    qs = []
    for h, (cc, p) in enumerate(heads):
        extra = jnp.where(lane == spare(p, 0), -slopes[h] * off_f,
                          jnp.where((lane == spare(p, 1)) | (lane == spare(p, 2)), slopes[h], 0.0))
        qs.append(q_ref[:, chunk(cc)] * (hms[p] * SCALE) + extra.astype(BF16))
    zero = jnp.zeros((), F32)
    raw = [_dot_nt(with_key_lanes(k_ref[pl.ds(q0, blk), chunk(cc)], p, zero), qs[h])
           for h, (cc, p) in enumerate(heads)]
    nrow = -(-nblk // 8) * 8
    blk_id = _iota((nrow, blk), 0)
    skips = []
    for h in range(nh):
        gate = jnp.where(blk_id < i, gates[h][:nrow], NEG)
        rank = _rank_rows(gate, blk_id, nblk, nrow)
        skip_t = jnp.where((rank < topk) & (blk_id < i), 0.0, 1.0)
        skip_t = jnp.concatenate([skip_t, jnp.ones((LANES - nrow, blk), F32)], axis=0)
        skips.append(skip_t.T.astype(BF16))
    own_bias = jnp.where(rel <= 0, 0.0, NEG)
    vrow = _iota((LANES, blk), 0)

    def values_t(k0, cc, p):
        vt = vt_ref[cc, :, pl.ds(k0, blk)]
        return jnp.where(vrow // HEAD_DIM == p, vt, jnp.ones_like(vt))

    es = []
    for h in range(nh):
        s = raw[h] + own_bias
        m = jnp.max(s, axis=0, keepdims=True)
        m_ref[h] = m
        es.append(jnp.exp(s - m).astype(BF16))
    for h, (cc, p) in enumerate(heads):
        acc_ref[h] = _dot(values_t(q0, cc, p), es[h])

    def body(j, carry):
        k0 = pl.multiple_of(j * blk, blk)
        shift = ((j - i) * blk).astype(F32)
        to_lane = [((_iota((LANES, LANES), 0) == j) & (_iota((LANES, LANES), 1) == spare(p, 0))).astype(BF16)
                   for p in range(HEADS_PER_CHUNK)]
        skipped = [_dot(skips[h], to_lane[p]) for h, (cc, p) in enumerate(heads)]
        raw = [_dot_nt(with_key_lanes(k_ref[pl.ds(k0, blk), chunk(cc)], p, shift),
                       qs[h] + (skipped[h] * NEG).astype(BF16))
               for h, (cc, p) in enumerate(heads)]
        es, alphas = [], []
        for h in range(nh):
            s = raw[h]
            m_old = m_ref[h]
            m_new = jnp.maximum(m_old, jnp.max(s, axis=0, keepdims=True))
            m_ref[h] = m_new
            es.append(jnp.exp(s - m_new).astype(BF16))
            alphas.append(jnp.exp(m_old - m_new))
        pvs = [_dot(values_t(k0, cc, p), es[h]) for h, (cc, p) in enumerate(heads)]
        for h in range(nh):
            acc_ref[h] = alphas[h] * acc_ref[h] + pvs[h]
        return carry

    lax.fori_loop(0, i, body, 0)
    lane = _iota((1, LANES), 1)
    for cc in range(cpb):
        outs = []
        for p in range(HEADS_PER_CHUNK):
            acc = acc_ref[HEADS_PER_CHUNK * cc + p]
            den = acc[(1 - p) * HEAD_DIM:(1 - p) * HEAD_DIM + 1, :]
            outs.append((acc / den).T)
        o_ref[:, chunk(cc)] = jnp.where(lane < HEAD_DIM, outs[0], outs[1]).astype(o_ref.dtype)


def _moba_attention(q, k, v, slopes, cpb=4):
    b, s, _ = q.shape
    assert s % C_BLOCK == 0 and C_BLOCK <= 256 and all(_is_pow2(x) for x in slopes)
    nblk = s // C_BLOCK
    assert nblk <= LANES
    wide = cpb * LANES
    nh = cpb * HEADS_PER_CHUNK
    spec_q = pl.BlockSpec((None, C_BLOCK, wide), lambda bi, c, i, sl: (bi, i, c))
    spec_kv = pl.BlockSpec((None, s, wide), lambda bi, c, i, sl: (bi, 0, c))
    return pl.pallas_call(
        functools.partial(_moba_kernel, blk=C_BLOCK, nblk=nblk, topk=min(C_TOPK, nblk), cpb=cpb),
        out_shape=jax.ShapeDtypeStruct((b, s, MIX), BF16),
        grid_spec=pltpu.PrefetchScalarGridSpec(
            num_scalar_prefetch=1,
            grid=(b, N_CHUNKS // cpb, nblk),
            in_specs=[spec_q, spec_kv, spec_kv],
            out_specs=spec_q,
            scratch_shapes=[pltpu.VMEM((cpb, LANES, LANES), F32), pltpu.VMEM((cpb, LANES, s), BF16),
                            pltpu.VMEM((nh, 1, C_BLOCK), F32), pltpu.VMEM((nh, LANES, C_BLOCK), F32)],
        ),
        compiler_params=_cparams("parallel", "parallel", "arbitrary"),
        name="moba_attention",
    )(jnp.asarray(slopes, F32), q, k, v)


def _compress_kernel(xk_ref, xv_ref, wk_lo_ref, wk_hi_ref, wv_lo_ref, wv_hi_ref,
                     pk_lo_ref, pk_hi_ref, pv_lo_ref, pv_hi_ref, gain_ref, kc_ref, vc_ref):
    def compress(x_ref, w_lo_ref, w_hi_ref, p_lo_ref, p_hi_ref):
        x = x_ref[...]
        nrow = x.shape[0]
        first = _dot(x, w_lo_ref[...])
        second = pltpu.roll(_dot(x, w_hi_ref[...]), nrow - 1, 0)
        p_lo = jnp.broadcast_to(p_lo_ref[...], (8, p_lo_ref.shape[1]))
        p_hi = jnp.broadcast_to(p_hi_ref[...], (8, p_hi_ref.shape[1]))
        bias = _dot_hilo(p_lo, w_lo_ref[...]) + _dot_hilo(p_hi, w_hi_ref[...])
        return first + second + bias[0:1]

    kc = compress(xk_ref, wk_lo_ref, wk_hi_ref, pk_lo_ref, pk_hi_ref)
    kc = kc * lax.rsqrt(_head_mean_sq(kc) + EPS) * gain_ref[...]
    kc_ref[...] = kc.astype(kc_ref.dtype)
    vc_ref[...] = compress(xv_ref, wv_lo_ref, wv_hi_ref, pv_lo_ref, pv_hi_ref).astype(vc_ref.dtype)


def _compress_weights(pos, w):
    half = CMP_LEN // 2
    eye = jnp.eye(N_KV, dtype=F32)
    wd = jnp.einsum("gh,lde->lgdhe", eye, w).reshape(CMP_LEN, LANES, LANES)
    w_lo = wd[:half].reshape(half * LANES, LANES).astype(BF16)
    w_hi = wd[half:].reshape(half * LANES, LANES).astype(BF16)
    pt = jnp.tile(pos, (1, N_KV))
    return w_lo, w_hi, pt[:half].reshape(1, half * LANES), pt[half:].reshape(1, half * LANES)


def _compress(kcmp, vcmp, cmp_pos, cmp_w, k_gain):
    b, s, _ = kcmp.shape
    assert CMP_LEN == 2 * CMP_STRIDE and s % CMP_STRIDE == 0
    nrow = s // CMP_STRIDE
    wide = CMP_STRIDE * LANES
    xk = kcmp.reshape(b, nrow, wide)
    xv = vcmp.reshape(b, nrow, wide)
    wk = _compress_weights(cmp_pos[0], cmp_w[0])
    wv = _compress_weights(cmp_pos[1], cmp_w[1])
    consts = [wk[0], wk[1], wv[0], wv[1], wk[2], wk[3], wv[2], wv[3],
              jnp.tile(k_gain, N_KV).reshape(1, LANES)]
    spec_x = pl.BlockSpec((None, nrow, wide), lambda bi: (bi, 0, 0))
    spec_o = pl.BlockSpec((None, nrow, LANES), lambda bi: (bi, 0, 0))
    return pl.pallas_call(
        _compress_kernel,
        out_shape=[jax.ShapeDtypeStruct((b, nrow, LANES), BF16)] * 2,
        grid=(b,),
        in_specs=[spec_x, spec_x] + [pl.BlockSpec(a.shape, lambda bi: (0, 0)) for a in consts],
        out_specs=[spec_o, spec_o],
        compiler_params=_cparams("parallel"),
        name="nsa_compress",
    )(xk, xv, *consts)


def _nsa_cmp_kernel(q_ref, kc_ref, vc_ref, o_ref, sel_ref, *, tq, n_cmp, n_slc, topn, slopes):
    q0 = pl.program_id(1) * tq
    kc = kc_ref[...]
    vc = vc_ref[...]
    ncp = kc.shape[0]
    rows = N_CHUNKS * tq
    seg = LANES // N_KV
    qpos = q0 + _iota((rows, ncp), 0) % tq
    ncol = _iota((rows, ncp), 1)
    diff = qpos - (ncol * CMP_STRIDE + CMP_LEN - 1)
    mask = (diff >= 0) & (ncol < n_cmp)
    diff_f = diff.astype(F32)
    orow = _iota((LANES, ncp), 0)
    cst = _iota((LANES, ncp), 1) * CMP_STRIDE
    sst = (orow % seg) * SLC_BLOCK
    overlap = (cst < sst + SLC_BLOCK) & (cst + CMP_LEN > sst) & (orow % seg < n_slc)
    halves = range(HEADS_PER_CHUNK)
    hms = _half_masks(BF16)
    raw = [_dot_nt(_stack_group(q_ref, hms[p] * SCALE), kc) for p in halves]
    pcs = []
    for p in halves:
        slope = _per_chunk_rows(rows, tq, [slopes[2 * c + p] for c in range(N_CHUNKS)])
        sc = jnp.where(mask, raw[p] - slope * diff_f, NEG)
        mx = jnp.max(sc, axis=-1, keepdims=True)
        e = jnp.where(mask, jnp.exp(sc - mx), 0.0)
        den = jnp.sum(e, axis=-1, keepdims=True)
        pcs.append(e / jnp.where(den > 0, den, 1.0))
    _store_group(o_ref, [_dot(pcs[p].astype(BF16), vc) for p in halves], tq)
    p_slc = jnp.zeros((LANES, tq), F32)
    for p in halves:
        pg = pcs[p][0:tq]
        for c in range(1, N_CHUNKS):
            pg = pg + pcs[p][c * tq:(c + 1) * tq]
        ov = (overlap & (orow // seg == p)).astype(BF16)
        pg_hi, pg_lo = _split(pg)
        p_slc = p_slc + (_dot_nt(ov, pg_hi) + _dot_nt(ov, pg_lo))
    j = _iota((LANES, tq), 0) % seg
    cur = (q0 + _iota((LANES, tq), 1)) // SLC_BLOCK
    forced = (j == 0) | (j == cur) | (j == cur - 1)
    usable = (j <= cur) & (j < n_slc)
    score = jnp.where(usable, p_slc + jnp.where(forced, FORCE_BONUS, 0.0), NEG)
    rank = _rank_rows(score, j, n_slc, seg)
    sel_t = jnp.where((rank < topn) & usable, 1.0, 0.0)
    sel_ref[...] = sel_t.T.astype(sel_ref.dtype)


def _nsa_cmp(q, kc, vc, slopes, tq=128):
    b, s, _ = q.shape
    ncp = kc.shape[1]
    n_slc = s // SLC_BLOCK
    assert n_slc <= LANES // N_KV and ncp % LANES == 0
    kern = functools.partial(_nsa_cmp_kernel, tq=tq, n_cmp=ncp - 1, n_slc=n_slc,
                             topn=min(SLC_TOPN, n_slc), slopes=slopes)
    spec_c = pl.BlockSpec((None, ncp, LANES), lambda bi, n: (bi, 0, 0))
    return pl.pallas_call(
        kern,
        out_shape=[jax.ShapeDtypeStruct((b, s, MIX), BF16), jax.ShapeDtypeStruct((b, s, LANES), BF16)],
        grid=(b, s // tq),
        in_specs=[pl.BlockSpec((None, tq, MIX), lambda bi, n: (bi, n, 0)), spec_c, spec_c],
        out_specs=[pl.BlockSpec((None, tq, MIX), lambda bi, n: (bi, n, 0)),
                   pl.BlockSpec((None, tq, LANES), lambda bi, n: (bi, n, 0))],
        compiler_params=_cparams("parallel", "parallel"),
        name="nsa_compressed",
    )(q, kc, vc)


def _nsa_slc_kernel(q_ref, k_ref, v_ref, sel_ref, o_ref, vt_ref, m_ref, acc_ref, *, tq, tk, slopes):
    n = pl.program_id(1)
    q0 = n * tq
    rows = N_CHUNKS * tq
    seg = LANES // N_KV
    halves = range(HEADS_PER_CHUNK)
    skip = 1.0 - sel_ref[...]
    nsub = tk // SLC_BLOCK
    lane_q = _iota((tq, LANES), 1)
    row_q = _iota((tq, LANES), 0).astype(F32)
    lane_k = _iota((tk, LANES), 1)
    off_k = _iota((tk, LANES), 0)
    spare = lambda p, n: (1 - p) * HEAD_DIM + n
    hms = _half_masks(BF16)
    qs = []
    for p in halves:
        extra = jnp.concatenate(
            [jnp.where(lane_q == spare(p, 0), -slopes[2 * c + p] * row_q,
                       jnp.where((lane_q == spare(p, 1)) | (lane_q == spare(p, 2)), slopes[2 * c + p], 0.0))
             for c in range(N_CHUNKS)], axis=0)
        qs.append(_stack_group(q_ref, hms[p] * SCALE) + extra.astype(BF16))
    @pl.when(n == 0)
    def _():
        vt_ref[...] = v_ref[...].astype(F32).T.astype(BF16)

    for p in halves:
        m_ref[p] = jnp.full((1, rows), 0.1 * NEG, F32)
        acc_ref[p] = jnp.zeros((LANES, rows), F32)

    n_grp = 2
    grp = rows // n_grp
    lanes = lambda g: slice(g * grp, (g + 1) * grp)
    chains = [(p, g) for p in halves for g in range(n_grp)]

    def tile(kt, diagonal):
        k0 = pl.multiple_of(kt * tk, tk)
        kj = k_ref[pl.ds(k0, tk), :]
        vt = vt_ref[:, pl.ds(k0, tk)]
        shift = (k0 - q0).astype(F32).astype(BF16)
        raw = []
        for p in halves:
            in_sub = (lane_k >= spare(p, 3)) & (lane_k < spare(p, 3) + nsub)
            kp = jnp.where(lane_k == spare(p, 0), 1.0,
                           jnp.where(lane_k == spare(p, 1), off_k.astype(F32).astype(BF16),
                                     jnp.where(lane_k == spare(p, 2), shift,
                                               jnp.where(in_sub, jnp.where(off_k // SLC_BLOCK == lane_k - spare(p, 3),
                                                                           1.0, 0.0).astype(BF16), kj))))
            src = _iota((LANES, LANES), 0) - (p * seg + k0 // SLC_BLOCK)
            dst = _iota((LANES, LANES), 1) - spare(p, 3)
            to_lane = ((src == dst) & (dst >= 0) & (dst < nsub)).astype(BF16)
            bias = (_dot(skip, to_lane) * NEG).astype(BF16)
            qp = qs[p] + jnp.concatenate([bias] * N_CHUNKS, axis=0)
            raw += [_dot_nt(kp, qp[lanes(g)]) for g in range(n_grp)]
        if diagonal:
            future = jnp.where(_iota((tk, tq), 1) + (q0 - k0) >= _iota((tk, tq), 0), 0.0, NEG)
            future = jnp.concatenate([future] * (grp // tq), axis=1)
        es, alphas = [], []
        for p, g in chains:
            s = raw[p * n_grp + g] + future if diagonal else raw[p * n_grp + g]
            m_old = m_ref[p, :, lanes(g)]
            m_new = jnp.maximum(m_old, jnp.max(s, axis=0, keepdims=True))
            m_ref[p, :, lanes(g)] = m_new
            es.append(jnp.exp(s - m_new).astype(BF16))
            alphas.append(jnp.exp(m_old - m_new))
        row = _iota((LANES, tk), 0)
        vts = [jnp.where(row // HEAD_DIM == p, vt, jnp.ones_like(vt)) for p in halves]
        pvs = [_dot(vts[p], es[p * n_grp + g]) for p, g in chains]
        for (p, g), alpha, pv in zip(chains, alphas, pvs):
            acc_ref[p, :, lanes(g)] = alpha * acc_ref[p, :, lanes(g)] + pv

    def body(kt, carry):
        tile(kt, False)
        return carry

    n_past = q0 // tk
    lax.fori_loop(0, n_past, body, 0)
    tile(n_past, True)
    outs = []
    for p in halves:
        acc = acc_ref[p]
        den = acc[(1 - p) * HEAD_DIM:(1 - p) * HEAD_DIM + 1, :]
        outs.append((acc / den).T)
    _store_group(o_ref, outs, tq)


def _nsa_slc(q, k, v, sel, slopes, tq=256, tk=256):
    b, s, _ = q.shape
    assert tk % tq == 0 and tq % SLC_BLOCK == 0 and s % tk == 0 and tk <= 256
    assert all(_is_pow2(x) for x in slopes)
    rows = N_CHUNKS * tq
    spec_q = pl.BlockSpec((None, tq, MIX), lambda bi, n: (bi, n, 0))
    spec_kv = pl.BlockSpec((None, s, LANES), lambda bi, n: (bi, 0, 0))
    return pl.pallas_call(
        functools.partial(_nsa_slc_kernel, tq=tq, tk=tk, slopes=slopes),
        out_shape=jax.ShapeDtypeStruct((b, s, MIX), BF16),
        grid=(b, s // tq),
        in_specs=[spec_q, spec_kv, spec_kv, pl.BlockSpec((None, tq, LANES), lambda bi, n: (bi, n, 0))],
        out_specs=spec_q,
        scratch_shapes=[pltpu.VMEM((LANES, s), BF16), pltpu.VMEM((N_KV, 1, rows), F32),
                        pltpu.VMEM((N_KV, LANES, rows), F32)],
        compiler_params=_cparams("parallel", "arbitrary"),
        name="nsa_selected",
    )(q, k, v, sel)


def _odd_attn(x2, b, s, norm_g, w_in, c_q_norm, c_k_norm, d_q_norm, d_k_norm, cmp_pos, cmp_w, w_out):
    kvw = N_KV * HEAD_DIM
    n_gate = N_HEADS * 3
    qd0 = 3 * MIX
    w = jnp.concatenate([w_in[:, :qd0], _gqa_cols(w_in[:, qd0:qd0 + MIX]), w_in[:, qd0 + MIX:],
                         jnp.zeros((w_in.shape[0], LANES - n_gate), F32)], axis=1).astype(BF16)
    n = w.shape[1]
    ones = lambda width: jnp.ones((width,), F32)
    gain = jnp.concatenate([
        jnp.tile(c_q_norm, N_HEADS), jnp.tile(c_k_norm, N_HEADS), ones(MIX), jnp.tile(d_q_norm, N_HEADS),
        ones(2 * kvw), jnp.tile(d_k_norm[1], N_KV), ones(kvw), jnp.tile(d_k_norm[2], N_KV), ones(kvw),
        ones(LANES)]).reshape(1, n)
    plan, col = [], 0
    for width, op in ((MIX, "norm"), (MIX, "norm"), (MIX, None), (MIX, "norm"), (kvw, None), (kvw, None),
                      (kvw, "norm"), (kvw, None), (kvw, "norm"), (kvw, None), (LANES, "sigmoid")):
        plan.append((col, width, op))
        col += width
    outs = _proj(x2, norm_g, w, gain, tuple(plan), [BF16] * 10 + [F32])
    r3 = lambda a: a.reshape(b, s, a.shape[-1])
    qc, kc, vc, qd, kcmp, vcmp, kslc, vslc, kwin, vwin = [r3(a) for a in outs[:10]]
    gates = outs[10]
    slopes = _alibi(N_HEADS)
    gslopes = [slopes[h] for h in GQA_PERM]
    oc = _moba_attention(qc, kc, vc, slopes)
    k_cmp, v_cmp = _compress(kcmp, vcmp, cmp_pos, cmp_w, d_k_norm[0])
    o_cmp, sel = _nsa_cmp(qd, k_cmp, v_cmp, gslopes)
    o_slc = _nsa_slc(qd, kslc, vslc, sel, gslopes)
    o_win = _band_attention(qd, kwin, vwin, window=D_WINDOW, slopes=gslopes)
    t = b * s
    w1 = w_out[:MIX].astype(BF16)
    w2 = _gqa_rows(w_out[MIX:]).astype(BF16)
    flat = lambda a: a.reshape(t, a.shape[-1])
    return [flat(oc), flat(o_cmp), flat(o_slc), flat(o_win), gates], w1, w2


def kernel(x, ev_norm, ev_w_in, ev_q_norm, ev_k_norm, ev_sink, ev_w_out, od_norm, od_w_in, od_c_q_norm,
           od_c_k_norm, od_d_q_norm, od_d_k_norm, od_cmp_pos, od_cmp_w, od_w_out, moe_norm, moe_w_grp,
           moe_b_grp, moe_w_exp, moe_b_exp, moe_w_gate, moe_w_up, moe_w_down):
    b, s, d = x.shape
    x2 = x.reshape(b * s, d)
    depth = moe_norm.shape[0]
    for layer in range(depth):
        i = layer // 2
        if layer % 2 == 0:
            attn, w1, w2 = _even_attn(x2, b, s, ev_norm[i], ev_w_in[i], ev_q_norm[i], ev_k_norm[i],
                                      ev_sink[i], ev_w_out[i])
        else:
            attn, w1, w2 = _odd_attn(x2, b, s, od_norm[i], od_w_in[i], od_c_q_norm[i], od_c_k_norm[i],
                                     od_d_q_norm[i], od_d_k_norm[i], od_cmp_pos[i], od_cmp_w[i], od_w_out[i])
        x2 = _moe_block(x2, attn, w1, w2, moe_norm[layer], moe_w_grp[layer], moe_b_grp[layer],
                        moe_w_exp[layer], moe_b_exp[layer], moe_w_gate, moe_w_up, moe_w_down, layer)
    return x2.reshape(b, s, d)
```

```python
import functools
import math

import numpy as np
import jax
import jax.numpy as jnp
from jax import lax
from jax.experimental import pallas as pl
from jax.experimental.pallas import tpu as pltpu

F32 = jnp.float32
BF16 = jnp.bfloat16

LANES = 128
HEAD_DIM = 64
HEADS_PER_CHUNK = LANES // HEAD_DIM
D_MODEL = 1024
N_HEADS = 8
N_KV = 2
GROUP = N_HEADS // N_KV
N_CHUNKS = N_HEADS // HEADS_PER_CHUNK
MIX = N_HEADS * HEAD_DIM
SCALE = 1.0 / math.sqrt(HEAD_DIM)
EPS = 1e-6
NEG = -1e30
EXP_UNDERFLOW = -104.0

A_WINDOW = 128
C_BLOCK = 256
C_TOPK = 3
CMP_LEN = 32
CMP_STRIDE = 16
SLC_BLOCK = 64
SLC_TOPN = 16
D_WINDOW = 512
FORCE_BONUS = 1000.0

N_GROUPS = 4
EXPERTS_PER_GROUP = 8
N_EXPERTS = N_GROUPS * EXPERTS_PER_GROUP
EXPERT_FF = 256

VMEM_LIMIT = 48 * 1024 * 1024
MOE_VMEM_LIMIT = 56 * 1024 * 1024

GQA_PERM = tuple(h for c in range(N_CHUNKS) for h in (c, c + GROUP))
MHA_PERM = tuple(range(N_HEADS))


def _alibi(n_heads):
    return [float(2.0 ** (-8.0 * (i + 1) / n_heads)) for i in range(n_heads)]


def _is_pow2(x):
    return math.frexp(x)[0] == 0.5


def _gqa_cols(w):
    lead = w.shape[:-1]
    w = w.reshape(*lead, N_KV, GROUP, HEAD_DIM)
    return jnp.swapaxes(w, -3, -2).reshape(*lead, MIX)


def _gqa_rows(w):
    tail = w.shape[1:]
    return jnp.swapaxes(w.reshape(N_KV, GROUP, HEAD_DIM, *tail), 0, 1).reshape(MIX, *tail)


def _cparams(*sem, vmem=VMEM_LIMIT):
    return pltpu.CompilerParams(dimension_semantics=sem, vmem_limit_bytes=vmem)


def _dot(a, b):
    return jnp.dot(a, b, preferred_element_type=F32)


def _dot_nt(a, b):
    return lax.dot_general(a, b, (((1,), (1,)), ((), ())), preferred_element_type=F32)


def _split(x):
    hi = x.astype(BF16)
    lo = (x - hi.astype(F32)).astype(BF16)
    return hi, lo


def _dot_hilo(a, b):
    hi, lo = _split(a)
    return _dot(hi, b) + _dot(lo, b)


def _iota(shape, dim):
    return lax.broadcasted_iota(jnp.int32, shape, dim)


def _half_masks(dtype):
    lane = _iota((1, LANES), 1)
    return [(lane // HEAD_DIM == p).astype(dtype) for p in range(HEADS_PER_CHUNK)]


def _head_mean_sq(y):
    same = (_iota((LANES, LANES), 0) // HEAD_DIM == _iota((LANES, LANES), 1) // HEAD_DIM)
    return _dot((y * y).astype(BF16), same.astype(BF16)) * (1.0 / HEAD_DIM)


def _proj_kernel(x_ref, g_ref, w_ref, gain_ref, *out_refs, plan):
    x = x_ref[...]
    ms = jnp.mean(x * x, axis=-1, keepdims=True)
    xn = (x * lax.rsqrt(ms + EPS) * g_ref[...]).astype(BF16)
    for o_ref, (col0, width, op) in zip(out_refs, plan):
        for a in range(0, width, 2 * LANES):
            wd = min(2 * LANES, width - a)
            y = _dot(xn, w_ref[:, col0 + a:col0 + a + wd])
            for c in range(0, wd, LANES):
                yc = y[:, c:c + LANES]
                if op == "norm":
                    gain = gain_ref[:, col0 + a + c:col0 + a + c + LANES]
                    yc = yc * lax.rsqrt(_head_mean_sq(yc) + EPS) * gain
                elif op == "sigmoid":
                    yc = jax.nn.sigmoid(yc)
                o_ref[:, a + c:a + c + LANES] = yc.astype(o_ref.dtype)


def _proj(x2, norm_g, w, gain, plan, out_dtypes, tm=512):
    t, d = x2.shape
    n = w.shape[1]
    out_shape = [jax.ShapeDtypeStruct((t, width), dt) for (_, width, _), dt in zip(plan, out_dtypes)]
    return pl.pallas_call(
        functools.partial(_proj_kernel, plan=plan),
        out_shape=out_shape,
        grid=(t // tm,),
        in_specs=[
            pl.BlockSpec((tm, d), lambda i: (i, 0)),
            pl.BlockSpec((1, d), lambda i: (0, 0)),
            pl.BlockSpec((d, n), lambda i: (0, 0)),
            pl.BlockSpec((1, n), lambda i: (0, 0)),
        ],
        out_specs=[pl.BlockSpec((tm, width), lambda i: (i, 0)) for (_, width, _) in plan],
        compiler_params=_cparams("parallel"),
        name="norm_proj",
    )(x2, norm_g.reshape(1, d), w, gain)


def _stack_group(q_ref, halfmask):
    return jnp.concatenate(
        [q_ref[:, c * LANES:(c + 1) * LANES] * halfmask for c in range(N_CHUNKS)], axis=0)


def _per_chunk_rows(rows, tq, values):
    r = _iota((rows, 1), 0) // tq
    out = jnp.full((rows, 1), values[N_CHUNKS - 1], F32)
    for c in range(N_CHUNKS - 2, -1, -1):
        out = jnp.where(r == c, values[c], out)
    return out


def _per_chunk_lanes(rows, tq, values):
    r = _iota((1, rows), 1) // tq
    out = jnp.full((1, rows), values[N_CHUNKS - 1], F32)
    for c in range(N_CHUNKS - 2, -1, -1):
        out = jnp.where(r == c, values[c], out)
    return out


def _band_kernel(*refs, tq, window, wpad, slopes, has_sink):
    if has_sink:
        sink_ref, q_ref, k_ref, v_ref, o_ref, vt_ref = refs
    else:
        q_ref, k_ref, v_ref, o_ref, vt_ref = refs
    n = pl.program_id(1)

    @pl.when(n == 0)
    def _():
        vt_ref[...] = v_ref[...].astype(F32).T.astype(BF16)

    q0 = n * tq
    kw = tq + wpad
    kstart = pl.multiple_of(jnp.maximum(q0 - wpad, 0), LANES)
    ks = k_ref[pl.ds(kstart, kw), :]
    vt = vt_ref[:, pl.ds(kstart, kw)]
    rows = N_CHUNKS * tq
    halves = range(HEADS_PER_CHUNK)
    hms = _half_masks(BF16)
    spare = lambda p, j: (1 - p) * HEAD_DIM + j
    lane_q = _iota((tq, LANES), 1)
    row_q = _iota((tq, LANES), 0).astype(F32)
    lane_k = _iota((kw, LANES), 1)
    off_k = _iota((kw, LANES), 0)
    fine = (off_k % LANES).astype(F32).astype(BF16)
    coarse = ((off_k // LANES) * LANES + (kstart - q0)).astype(F32).astype(BF16)
    diff = _iota((kw, tq), 1) + (q0 - kstart) - _iota((kw, tq), 0)
    band = jnp.where((diff >= 0) & (diff < window), 0.0, NEG)
    band = jnp.concatenate([band] * N_CHUNKS, axis=1)
    raw = []
    for p in halves:
        extra = jnp.concatenate(
            [jnp.where(lane_q == spare(p, 0), -slopes[2 * c + p] * row_q,
                       jnp.where((lane_q == spare(p, 1)) | (lane_q == spare(p, 2)), slopes[2 * c + p], 0.0))
             for c in range(N_CHUNKS)], axis=0)
        qp = _stack_group(q_ref, hms[p] * SCALE) + extra.astype(BF16)
        kp = jnp.where(lane_k == spare(p, 0), 1.0,
                       jnp.where(lane_k == spare(p, 1), fine, jnp.where(lane_k == spare(p, 2), coarse, ks)))
        raw.append(_dot_nt(kp, qp))
    es, sinks = [], []
    for p in halves:
        s = raw[p] + band
        mx = jnp.max(s, axis=0, keepdims=True)
        if has_sink:
            sk = _per_chunk_lanes(rows, tq, [sink_ref[2 * c + p] for c in range(N_CHUNKS)])
            mx = jnp.maximum(mx, sk)
            sinks.append(jnp.exp(sk - mx))
        es.append(jnp.exp(s - mx).astype(BF16))
    vrow = _iota((LANES, kw), 0)
    outs = []
    for p in halves:
        acc = _dot(jnp.where(vrow // HEAD_DIM == p, vt, jnp.ones_like(vt)), es[p])
        den = acc[(1 - p) * HEAD_DIM:(1 - p) * HEAD_DIM + 1, :]
        if has_sink:
            den = den + sinks[p]
        outs.append((acc / den).T)
    _store_group(o_ref, outs, tq)


def _store_group(o_ref, os_, tq):
    lane = _iota((1, LANES), 1)
    for c in range(N_CHUNKS):
        o = jnp.where(lane < HEAD_DIM, os_[0][c * tq:(c + 1) * tq], os_[1][c * tq:(c + 1) * tq])
        o_ref[:, c * LANES:(c + 1) * LANES] = o.astype(o_ref.dtype)


def _band_attention(q, k, v, *, window, slopes, sink=None, tq=128):
    b, s, _ = q.shape
    wpad = -(-window // LANES) * LANES
    assert s >= tq + wpad and s % tq == 0 and tq <= LANES and all(_is_pow2(x) for x in slopes)
    kern = functools.partial(_band_kernel, tq=tq, window=window, wpad=wpad,
                             slopes=slopes, has_sink=sink is not None)
    in_specs = [
        pl.BlockSpec((None, tq, MIX), lambda bi, n: (bi, n, 0)),
        pl.BlockSpec((None, s, LANES), lambda bi, n: (bi, 0, 0)),
        pl.BlockSpec((None, s, LANES), lambda bi, n: (bi, 0, 0)),
    ]
    args = [q, k, v]
    if sink is not None:
        in_specs = [pl.BlockSpec(memory_space=pltpu.SMEM)] + in_specs
        args = [sink] + args
    return pl.pallas_call(
        kern,
        out_shape=jax.ShapeDtypeStruct((b, s, MIX), BF16),
        grid=(b, s // tq),
        in_specs=in_specs,
        out_specs=pl.BlockSpec((None, tq, MIX), lambda bi, n: (bi, n, 0)),
        scratch_shapes=[pltpu.VMEM((LANES, s), BF16)],
        compiler_params=_cparams("parallel", "arbitrary"),
        name="band_attention",
    )(*args)


def _stick_kernel(q_ref, k_ref, v_ref, o_ref, acc_ref, run_ref, *, tq, cpb):
    i = pl.program_id(2)
    q0 = pl.multiple_of(i * tq, tq)
    hms = _half_masks(BF16)
    heads = [(cc, p) for cc in range(cpb) for p in range(HEADS_PER_CHUNK)]
    qs = [q_ref[:, cc * LANES:(cc + 1) * LANES] * (hms[p] * SCALE) for cc, p in heads]
    upper = (_iota((2 * tq, tq), 0) % tq > _iota((2 * tq, tq), 1)).astype(BF16)

    def suffix_sum(x):
        hi, lo = _split(x)
        return _dot(jnp.concatenate([hi, lo], axis=1), upper)

    def block(kstart, diag):
        if diag:
            causal = _iota((tq, tq), 1) < _iota((tq, tq), 0)
        kjs = [k_ref[pl.ds(kstart, tq), cc * LANES:(cc + 1) * LANES] for cc in range(cpb)]
        vjs = [v_ref[pl.ds(kstart, tq), cc * LANES:(cc + 1) * LANES] for cc in range(cpb)]
        zs = [_dot_nt(qs[h], kjs[cc]) for h, (cc, p) in enumerate(heads)]
        lss, lks = [], []
        for z in zs:
            ls = jnp.minimum(z, 0.0) - jnp.log(1.0 + jnp.exp(-jnp.abs(z)))
            lk = ls - z
            if diag:
                lk = jnp.where(causal, lk, 0.0)
            lss.append(ls)
            lks.append(lk)
        sufs = [suffix_sum(lk) for lk in lks]
        ws = []
        for h in range(len(heads)):
            if diag:
                a = jnp.where(causal, jnp.exp(lss[h] + sufs[h]), 0.0)
            else:
                a = jnp.exp(lss[h] + sufs[h] + run_ref[h])
            ws.append(a.astype(BF16))
        pvs = [_dot(ws[h], vjs[cc]) for h, (cc, p) in enumerate(heads)]
        for h in range(len(heads)):
            rowsum = jnp.sum(lks[h], axis=-1, keepdims=True)
            if diag:
                acc_ref[h] = pvs[h]
                run_ref[h] = rowsum
            else:
                acc_ref[h] += pvs[h]
                run_ref[h] += rowsum

    block(q0, True)

    def weights_alive():
        run = run_ref[0]
        for h in range(1, len(heads)):
            run = jnp.maximum(run, run_ref[h])
        return jnp.max(run) > EXP_UNDERFLOW

    def body(carry):
        t, _ = carry
        block(pl.multiple_of((i - 1 - t) * tq, tq), False)
        return t + 1, weights_alive()

    lax.while_loop(lambda c: (c[0] < i) & c[1], body, (jnp.int32(0), weights_alive()))
    lane = _iota((1, LANES), 1)
    for cc in range(cpb):
        o = jnp.where(lane < HEAD_DIM, acc_ref[HEADS_PER_CHUNK * cc], acc_ref[HEADS_PER_CHUNK * cc + 1])
        o_ref[:, cc * LANES:(cc + 1) * LANES] = o.astype(o_ref.dtype)


def _stick_attention(q, k, v, tq=256, cpb=4):
    b, s, _ = q.shape
    wide = cpb * LANES
    n_heads = cpb * HEADS_PER_CHUNK
    spec_q = pl.BlockSpec((None, tq, wide), lambda bi, c, i: (bi, i, c))
    spec_kv = pl.BlockSpec((None, s, wide), lambda bi, c, i: (bi, 0, c))
    return pl.pallas_call(
        functools.partial(_stick_kernel, tq=tq, cpb=cpb),
        out_shape=jax.ShapeDtypeStruct((b, s, MIX), BF16),
        grid=(b, N_CHUNKS // cpb, s // tq),
        in_specs=[spec_q, spec_kv, spec_kv],
        out_specs=spec_q,
        scratch_shapes=[pltpu.VMEM((n_heads, tq, LANES), F32), pltpu.VMEM((n_heads, tq, 1), F32)],
        compiler_params=_cparams("parallel", "parallel", "parallel"),
        name="stick_breaking",
    )(q, k, v)


def _route(logits):
    lane = _iota(logits.shape, 1)
    lane_f = lane.astype(F32)
    ninf = -jnp.inf
    is_g = (lane >= N_EXPERTS) & (lane < N_EXPERTS + N_GROUPS)
    gmax = jnp.max(jnp.where(is_g, logits, ninf), axis=-1, keepdims=True)
    gidx = jnp.min(jnp.where(is_g & (logits == gmax), lane_f - N_EXPERTS, 1e9), axis=-1, keepdims=True)
    p_g = 1.0 / jnp.sum(jnp.where(is_g, jnp.exp(logits - gmax), 0.0), axis=-1, keepdims=True)
    in_grp = (lane < N_EXPERTS) & ((lane // EXPERTS_PER_GROUP).astype(F32) == gidx)
    le = jnp.where(in_grp, logits, ninf)
    m1 = jnp.max(le, axis=-1, keepdims=True)
    i1 = jnp.min(jnp.where(le == m1, lane_f, 1e9), axis=-1, keepdims=True)
    le2 = jnp.where(lane_f == i1, ninf, le)
    m2 = jnp.max(le2, axis=-1, keepdims=True)
    i2 = jnp.min(jnp.where(le2 == m2, lane_f, 1e9), axis=-1, keepdims=True)
    e2 = jnp.exp(m2 - m1)
    w1 = p_g / (1.0 + e2)
    w2 = p_g * e2 / (1.0 + e2)
    return jnp.where(lane_f == i1, w1, 0.0) + jnp.where(lane_f == i2, w2, 0.0)


def _gate_expand(branch):
    r = _iota((LANES, MIX), 0)
    col = _iota((LANES, MIX), 1)
    head = col // LANES + GROUP * ((col % LANES) // HEAD_DIM)
    return (r == 3 * head + branch).astype(BF16)


def _out_kernel(*refs, nsa):
    if nsa:
        (x_ref, o1_ref, ocmp_ref, oslc_ref, owin_ref, gates_ref, w1_ref, w2_ref,
         ng_ref, wr_ref, br_ref, x1_ref, h_ref, comb_ref) = refs
        g = gates_ref[...]
        o2 = (_dot_hilo(g, _gate_expand(0)) * ocmp_ref[...]
              + _dot_hilo(g, _gate_expand(1)) * oslc_ref[...]
              + _dot_hilo(g, _gate_expand(2)) * owin_ref[...]).astype(BF16)
    else:
        (x_ref, o1_ref, o2_ref, w1_ref, w2_ref,
         ng_ref, wr_ref, br_ref, x1_ref, h_ref, comb_ref) = refs
        o2 = o2_ref[...]
    x1 = x_ref[...] + _dot(o1_ref[...], w1_ref[...]) + _dot(o2, w2_ref[...])
    x1_ref[...] = x1
    ms = jnp.mean(x1 * x1, axis=-1, keepdims=True)
    h = x1 * lax.rsqrt(ms + EPS) * ng_ref[...]
    h_ref[...] = h.astype(BF16)
    h_hi, h_lo = _split(h)
    w_hi, w_lo = _split(wr_ref[...])
    logits = _dot(h_hi, w_hi) + (_dot(h_hi, w_lo) + _dot(h_lo, w_hi)) + br_ref[...]
    comb_ref[...] = _route(logits)


def _out_proj_route(x2, attn, w1, w2, moe_g, w_route, b_route, tm=512):
    t, d = x2.shape
    nsa = len(attn) > 2
    row = lambda width: pl.BlockSpec((tm, width), lambda i: (i, 0))
    full = lambda a: pl.BlockSpec(a.shape, lambda i: (0, 0))
    consts = [w1, w2, moe_g.reshape(1, d), w_route, b_route]
    return pl.pallas_call(
        functools.partial(_out_kernel, nsa=nsa),
        out_shape=[jax.ShapeDtypeStruct((t, d), F32), jax.ShapeDtypeStruct((t, d), BF16),
                   jax.ShapeDtypeStruct((t, LANES), F32)],
        grid=(t // tm,),
        in_specs=[row(d)] + [row(a.shape[1]) for a in attn] + [full(a) for a in consts],
        out_specs=[row(d), row(d), row(LANES)],
        compiler_params=_cparams("parallel"),
        name="out_proj_route",
    )(x2, *attn, *consts)


MOE_CHUNK = 256
SEG_ALIGN = 16


def _moe_sort(h_ref, comb_ref, hs_ref, cs_ref, pos_ref, acc_ref, seg_ref):
    tm = h_ref.shape[0]
    rows_s = hs_ref.shape[0]
    comb = comb_ref[...]
    used = jnp.where(comb > 0, 1.0, 0.0).astype(BF16)
    of_group = ((_iota((LANES, LANES), 0) // EXPERTS_PER_GROUP == _iota((LANES, LANES), 1))
                & (_iota((LANES, LANES), 0) < N_EXPERTS)).astype(BF16)
    member = jnp.where(_dot(used, of_group) > 0.5, 1.0, 0.0)
    member_t = member.T
    earlier = (_iota((tm, tm), 0) < _iota((tm, tm), 1)).astype(BF16)
    rank_t = _dot(member_t.astype(BF16), earlier)
    count = jnp.sum(member_t, axis=1, keepdims=True)
    counts = [count[g:g + 1, :] for g in range(N_GROUPS)]
    starts = [jnp.zeros((1, 1), F32)]
    for g in range(1, N_GROUPS):
        padded = jnp.floor((counts[g - 1] + (SEG_ALIGN - 1)) * (1.0 / SEG_ALIGN)) * SEG_ALIGN
        starts.append(starts[g - 1] + padded)
    grow = _iota((LANES, 1), 0)
    start_col = jnp.zeros((LANES, 1), F32)
    for g in range(1, N_GROUPS):
        start_col = jnp.where(grow == g, starts[g], start_col)
    pos_t = jnp.sum(member_t * (start_col + rank_t), axis=0, keepdims=True)
    place = jnp.where(_iota((rows_s, tm), 0).astype(F32) == pos_t, 1.0, 0.0).astype(BF16)
    hs_ref[...] = _dot(place, h_ref[...]).astype(BF16)
    c_hi = comb.astype(BF16)
    rest = comb - c_hi.astype(F32)
    c_mid = rest.astype(BF16)
    c_lo = (rest - c_mid.astype(F32)).astype(BF16)
    cs_ref[...] = _dot(place, c_hi) + (_dot(place, c_mid) + _dot(place, c_lo))
    pos_ref[...] = jnp.broadcast_to(pos_t, (LANES, tm)).T
    acc_ref[...] = jnp.zeros(acc_ref.shape, F32)
    for g in range(N_GROUPS):
        seg_ref[g] = starts[g][0, 0].astype(jnp.int32)
        seg_ref[N_GROUPS + g] = counts[g][0, 0].astype(jnp.int32)


def _moe_kernel(h_ref, comb_ref, x1_ref, wg_ref, wu_ref, wd_ref, o_ref,
                hs_ref, cs_ref, pos_ref, acc_ref, seg_ref, *, per_step):
    step = pl.program_id(1)
    tm = h_ref.shape[0]
    rows_s = hs_ref.shape[0]

    @pl.when(step == 0)
    def _():
        _moe_sort(h_ref, comb_ref, hs_ref, cs_ref, pos_ref, acc_ref, seg_ref)

    first = step * per_step
    group = first // EXPERTS_PER_GROUP
    start = seg_ref[group]
    count = seg_ref[N_GROUPS + group]
    w_gate = [wg_ref[j].astype(BF16) for j in range(per_step)]
    w_up = [wu_ref[j].astype(BF16) for j in range(per_step)]
    w_down = jnp.concatenate([wd_ref[j].astype(BF16) for j in range(per_step)], axis=0)
    sizes = [MOE_CHUNK, MOE_CHUNK // 2, MOE_CHUNK // 2] + [MOE_CHUNK] * (tm // MOE_CHUNK - 2)
    begin = 0
    for size in sizes:
        @pl.when(begin < count)
        def _(begin=begin, size=size):
            off = pl.multiple_of(start + begin, SEG_ALIGN)
            hs = hs_ref[pl.ds(off, size), :]
            cs = cs_ref[pl.ds(off, size), :]
            lane = _iota((size, LANES), 1)
            acts = []
            for j in range(per_step):
                g = _dot(hs, w_gate[j])
                u = _dot(hs, w_up[j])
                c = jnp.sum(jnp.where(lane == first + j, cs, 0.0), axis=-1, keepdims=True)
                acts.append((c * (g * jax.nn.sigmoid(g) * u)).astype(BF16))
            acc_ref[pl.ds(off, size), :] += _dot(jnp.concatenate(acts, axis=1), w_down)
        begin += size

    @pl.when(step == pl.num_programs(1) - 1)
    def _():
        pos = pos_ref[...]
        back = jnp.concatenate(
            [jnp.where(pos == (_iota((tm, LANES), 1) + blk * LANES).astype(F32), 1.0, 0.0).astype(BF16)
             for blk in range(rows_s // LANES)], axis=1)
        o_ref[...] = x1_ref[...] + _dot(back, acc_ref[...].astype(BF16))


def _moe(h, comb, x1, w_gate, w_up, w_down, layer, tm=1024, per_step=2):
    t, d = h.shape
    _, n_exp, _, ff = w_gate.shape
    assert t % tm == 0 and EXPERTS_PER_GROUP % per_step == 0 and tm % MOE_CHUNK == 0
    rows_s = -(-(tm + N_GROUPS * SEG_ALIGN + MOE_CHUNK) // LANES) * LANES
    return pl.pallas_call(
        functools.partial(_moe_kernel, per_step=per_step),
        out_shape=jax.ShapeDtypeStruct((t, d), F32),
        grid=(t // tm, n_exp // per_step),
        scratch_shapes=[pltpu.VMEM((rows_s, d), BF16), pltpu.VMEM((rows_s, LANES), F32),
                        pltpu.VMEM((tm, LANES), F32), pltpu.VMEM((rows_s, d), F32),
                        pltpu.SMEM((2 * N_GROUPS,), jnp.int32)],
        in_specs=[
            pl.BlockSpec((tm, d), lambda i, e: (i, 0)),
            pl.BlockSpec((tm, LANES), lambda i, e: (i, 0)),
            pl.BlockSpec((tm, d), lambda i, e: (i, 0)),
            pl.BlockSpec((None, per_step, d, ff), lambda i, e: (layer, e, 0, 0)),
            pl.BlockSpec((None, per_step, d, ff), lambda i, e: (layer, e, 0, 0)),
            pl.BlockSpec((None, per_step, ff, d), lambda i, e: (layer, e, 0, 0)),
        ],
        out_specs=pl.BlockSpec((tm, d), lambda i, e: (i, 0)),
        compiler_params=_cparams("parallel", "arbitrary", vmem=MOE_VMEM_LIMIT),
        name="moe_experts",
    )(h, comb, x1, w_gate, w_up, w_down)


def _moe_block(x2, attn, w1, w2, moe_g, w_grp, b_grp, w_exp, b_exp, w_gate, w_up, w_down, layer):
    d = x2.shape[1]
    pad = LANES - N_EXPERTS - N_GROUPS
    w_route = jnp.concatenate([w_exp, w_grp, jnp.zeros((d, pad), F32)], axis=1)
    b_route = jnp.concatenate([b_exp, b_grp, jnp.zeros((pad,), F32)]).reshape(1, LANES)
    x1, h, comb = _out_proj_route(x2, attn, w1, w2, moe_g, w_route, b_route)
    return _moe(h, comb, x1, w_gate, w_up, w_down, layer)


def _even_attn(x2, b, s, norm_g, w_in, q_norm, k_norm, sink, w_out):
    w = jnp.concatenate([_gqa_cols(w_in[:, :MIX]), w_in[:, MIX:]], axis=1).astype(BF16)
    n = w.shape[1]
    kvw = N_KV * HEAD_DIM
    gain = jnp.concatenate([jnp.tile(q_norm, N_HEADS), jnp.tile(k_norm, N_KV),
                            jnp.ones((n - MIX - kvw,), F32)]).reshape(1, n)
    plan, col = [], 0
    for width, op in ((MIX, "norm"), (kvw, "norm"), (kvw, None), (MIX, None), (MIX, None), (MIX, None)):
        plan.append((col, width, op))
        col += width
    qa, ka, va, qb, kb, vb = _proj(x2, norm_g, w, gain, tuple(plan), [BF16] * 6)
    r3 = lambda a: a.reshape(b, s, a.shape[-1])
    slopes = _alibi(N_HEADS)
    oa = _band_attention(r3(qa), r3(ka), r3(va), window=A_WINDOW,
                         slopes=[slopes[h] for h in GQA_PERM], sink=sink[np.asarray(GQA_PERM)])
    ob = _stick_attention(r3(qb), r3(kb), r3(vb))
    w1 = _gqa_rows(w_out[:MIX]).astype(BF16)
    w2 = w_out[MIX:].astype(BF16)
    t = b * s
    return [oa.reshape(t, MIX), ob.reshape(t, MIX)], w1, w2


def _rank_rows(score, j, n, seg):
    row = _iota(score.shape, 0)
    rank = jnp.zeros(score.shape, F32)
    for jj in range(n):
        other = score[jj:jj + 1, :]
        for sgm in range(1, score.shape[0] // seg):
            other = jnp.where(row // seg == sgm, score[sgm * seg + jj:sgm * seg + jj + 1, :], other)
        beats = (other > score) | ((other == score) & (jj < j))
        rank = rank + jnp.where(beats, 1.0, 0.0)
    return rank


def _ones_beside(v, p):
    lane = _iota((1, LANES), 1)
    return jnp.where(lane // HEAD_DIM == p, v, jnp.ones_like(v))


def _normalize(acc):
    return acc / pltpu.roll(acc, HEAD_DIM, 1)


def _moba_kernel(slope_ref, q_ref, k_ref, v_ref, o_ref, kmean_ref, vt_ref, m_ref, acc_ref, *,
                 blk, nblk, topk, cpb):
    g = pl.program_id(1)
    i = pl.program_id(2)
    s_len = k_ref.shape[0]
    heads = [(cc, p) for cc in range(cpb) for p in range(HEADS_PER_CHUNK)]
    nh = len(heads)
    chunk = lambda cc: slice(cc * LANES, (cc + 1) * LANES)

    @pl.when(i == 0)
    def _():
        member = (_iota((LANES, s_len), 1) // blk == _iota((LANES, s_len), 0)).astype(BF16)
        for cc in range(cpb):
            kmean_ref[cc] = _dot(member, k_ref[:, chunk(cc)]) * (1.0 / blk)
            vt_ref[cc] = v_ref[:, chunk(cc)].astype(F32).T.astype(BF16)

    hms = _half_masks(BF16)
    q0 = pl.multiple_of(i * blk, blk)
    rel = (_iota((blk, blk), 0) - _iota((blk, blk), 1)).astype(F32)
    slopes = [slope_ref[HEADS_PER_CHUNK * (g * cpb + cc) + p] for cc, p in heads]
    kms = [_split(kmean_ref[cc]) for cc in range(cpb)]
    gates = [_dot_nt(kms[cc][0], q_ref[:, chunk(cc)] * hms[p]) + _dot_nt(kms[cc][1], q_ref[:, chunk(cc)] * hms[p])
             for cc, p in heads]
    lane = _iota((blk, LANES), 1)
    off_f = _iota((blk, LANES), 0).astype(F32)
    off_b = off_f.astype(BF16)
    spare = lambda p, n: (1 - p) * HEAD_DIM + n

    def with_key_lanes(kj, p, shift):
        return jnp.where(lane == spare(p, 0), 1.0,
                         jnp.where(lane == spare(p, 1), off_b,
                                   jnp.where(lane == spare(p, 2), shift.astype(BF16), kj)))

    qs = []
    for h, (cc, p) in enumerate(heads):
        extra = jnp.where(lane == spare(p, 0), -slopes[h] * off_f,
                          jnp.where((lane == spare(p, 1)) | (lane == spare(p, 2)), slopes[h], 0.0))
        qs.append(q_ref[:, chunk(cc)] * (hms[p] * SCALE) + extra.astype(BF16))
    zero = jnp.zeros((), F32)
    raw = [_dot_nt(with_key_lanes(k_ref[pl.ds(q0, blk), chunk(cc)], p, zero), qs[h])
           for h, (cc, p) in enumerate(heads)]
    nrow = -(-nblk // 8) * 8
    blk_id = _iota((nrow, blk), 0)
    skips = []
    for h in range(nh):
        gate = jnp.where(blk_id < i, gates[h][:nrow], NEG)
        rank = _rank_rows(gate, blk_id, nblk, nrow)
        skip_t = jnp.where((rank < topk) & (blk_id < i), 0.0, 1.0)
        skip_t = jnp.concatenate([skip_t, jnp.ones((LANES - nrow, blk), F32)], axis=0)
        skips.append(skip_t.T.astype(BF16))
    own_bias = jnp.where(rel <= 0, 0.0, NEG)
    vrow = _iota((LANES, blk), 0)

    def values_t(k0, cc, p):
        vt = vt_ref[cc, :, pl.ds(k0, blk)]
        return jnp.where(vrow // HEAD_DIM == p, vt, jnp.ones_like(vt))

    es = []
    for h in range(nh):
        s = raw[h] + own_bias
        m = jnp.max(s, axis=0, keepdims=True)
        m_ref[h] = m
        es.append(jnp.exp(s - m).astype(BF16))
    for h, (cc, p) in enumerate(heads):
        acc_ref[h] = _dot(values_t(q0, cc, p), es[h])

    def body(j, carry):
        k0 = pl.multiple_of(j * blk, blk)
        shift = ((j - i) * blk).astype(F32)
        to_lane = [((_iota((LANES, LANES), 0) == j) & (_iota((LANES, LANES), 1) == spare(p, 0))).astype(BF16)
                   for p in range(HEADS_PER_CHUNK)]
        skipped = [_dot(skips[h], to_lane[p]) for h, (cc, p) in enumerate(heads)]
        raw = [_dot_nt(with_key_lanes(k_ref[pl.ds(k0, blk), chunk(cc)], p, shift),
                       qs[h] + (skipped[h] * NEG).astype(BF16))
               for h, (cc, p) in enumerate(heads)]
        es, alphas = [], []
        for h in range(nh):
            s = raw[h]
            m_old = m_ref[h]
            m_new = jnp.maximum(m_old, jnp.max(s, axis=0, keepdims=True))
            m_ref[h] = m_new
            es.append(jnp.exp(s - m_new).astype(BF16))
            alphas.append(jnp.exp(m_old - m_new))
        pvs = [_dot(values_t(k0, cc, p), es[h]) for h, (cc, p) in enumerate(heads)]
        for h in range(nh):
            acc_ref[h] = alphas[h] * acc_ref[h] + pvs[h]
        return carry

    lax.fori_loop(0, i, body, 0)
    lane = _iota((1, LANES), 1)
    for cc in range(cpb):
        outs = []
        for p in range(HEADS_PER_CHUNK):
            acc = acc_ref[HEADS_PER_CHUNK * cc + p]
            den = acc[(1 - p) * HEAD_DIM:(1 - p) * HEAD_DIM + 1, :]
            outs.append((acc / den).T)
        o_ref[:, chunk(cc)] = jnp.where(lane < HEAD_DIM, outs[0], outs[1]).astype(o_ref.dtype)


def _moba_attention(q, k, v, slopes, cpb=4):
    b, s, _ = q.shape
    assert s % C_BLOCK == 0 and C_BLOCK <= 256 and all(_is_pow2(x) for x in slopes)
    nblk = s // C_BLOCK
    assert nblk <= LANES
    wide = cpb * LANES
    nh = cpb * HEADS_PER_CHUNK
    spec_q = pl.BlockSpec((None, C_BLOCK, wide), lambda bi, c, i, sl: (bi, i, c))
    spec_kv = pl.BlockSpec((None, s, wide), lambda bi, c, i, sl: (bi, 0, c))
    return pl.pallas_call(
        functools.partial(_moba_kernel, blk=C_BLOCK, nblk=nblk, topk=min(C_TOPK, nblk), cpb=cpb),
        out_shape=jax.ShapeDtypeStruct((b, s, MIX), BF16),
        grid_spec=pltpu.PrefetchScalarGridSpec(
            num_scalar_prefetch=1,
            grid=(b, N_CHUNKS // cpb, nblk),
            in_specs=[spec_q, spec_kv, spec_kv],
            out_specs=spec_q,
            scratch_shapes=[pltpu.VMEM((cpb, LANES, LANES), F32), pltpu.VMEM((cpb, LANES, s), BF16),
                            pltpu.VMEM((nh, 1, C_BLOCK), F32), pltpu.VMEM((nh, LANES, C_BLOCK), F32)],
        ),
        compiler_params=_cparams("parallel", "parallel", "arbitrary"),
        name="moba_attention",
    )(jnp.asarray(slopes, F32), q, k, v)


def _compress_kernel(xk_ref, xv_ref, wk_lo_ref, wk_hi_ref, wv_lo_ref, wv_hi_ref,
                     pk_lo_ref, pk_hi_ref, pv_lo_ref, pv_hi_ref, gain_ref, kc_ref, vc_ref):
    def compress(x_ref, w_lo_ref, w_hi_ref, p_lo_ref, p_hi_ref):
        x = x_ref[...]
        nrow = x.shape[0]
        first = _dot(x, w_lo_ref[...])
        second = pltpu.roll(_dot(x, w_hi_ref[...]), nrow - 1, 0)
        p_lo = jnp.broadcast_to(p_lo_ref[...], (8, p_lo_ref.shape[1]))
        p_hi = jnp.broadcast_to(p_hi_ref[...], (8, p_hi_ref.shape[1]))
        bias = _dot_hilo(p_lo, w_lo_ref[...]) + _dot_hilo(p_hi, w_hi_ref[...])
        return first + second + bias[0:1]

    kc = compress(xk_ref, wk_lo_ref, wk_hi_ref, pk_lo_ref, pk_hi_ref)
    kc = kc * lax.rsqrt(_head_mean_sq(kc) + EPS) * gain_ref[...]
    kc_ref[...] = kc.astype(kc_ref.dtype)
    vc_ref[...] = compress(xv_ref, wv_lo_ref, wv_hi_ref, pv_lo_ref, pv_hi_ref).astype(vc_ref.dtype)


def _compress_weights(pos, w):
    half = CMP_LEN // 2
    eye = jnp.eye(N_KV, dtype=F32)
    wd = jnp.einsum("gh,lde->lgdhe", eye, w).reshape(CMP_LEN, LANES, LANES)
    w_lo = wd[:half].reshape(half * LANES, LANES).astype(BF16)
    w_hi = wd[half:].reshape(half * LANES, LANES).astype(BF16)
    pt = jnp.tile(pos, (1, N_KV))
    return w_lo, w_hi, pt[:half].reshape(1, half * LANES), pt[half:].reshape(1, half * LANES)


def _compress(kcmp, vcmp, cmp_pos, cmp_w, k_gain):
    b, s, _ = kcmp.shape
    assert CMP_LEN == 2 * CMP_STRIDE and s % CMP_STRIDE == 0
    nrow = s // CMP_STRIDE
    wide = CMP_STRIDE * LANES
    xk = kcmp.reshape(b, nrow, wide)
    xv = vcmp.reshape(b, nrow, wide)
    wk = _compress_weights(cmp_pos[0], cmp_w[0])
    wv = _compress_weights(cmp_pos[1], cmp_w[1])
    consts = [wk[0], wk[1], wv[0], wv[1], wk[2], wk[3], wv[2], wv[3],
              jnp.tile(k_gain, N_KV).reshape(1, LANES)]
    spec_x = pl.BlockSpec((None, nrow, wide), lambda bi: (bi, 0, 0))
    spec_o = pl.BlockSpec((None, nrow, LANES), lambda bi: (bi, 0, 0))
    return pl.pallas_call(
        _compress_kernel,
        out_shape=[jax.ShapeDtypeStruct((b, nrow, LANES), BF16)] * 2,
        grid=(b,),
        in_specs=[spec_x, spec_x] + [pl.BlockSpec(a.shape, lambda bi: (0, 0)) for a in consts],
        out_specs=[spec_o, spec_o],
        compiler_params=_cparams("parallel"),
        name="nsa_compress",
    )(xk, xv, *consts)


def _nsa_cmp_kernel(q_ref, kc_ref, vc_ref, o_ref, sel_ref, *, tq, n_cmp, n_slc, topn, slopes):
    q0 = pl.program_id(1) * tq
    kc = kc_ref[...]
    vc = vc_ref[...]
    ncp = kc.shape[0]
    rows = N_CHUNKS * tq
    seg = LANES // N_KV
    qpos = q0 + _iota((rows, ncp), 0) % tq
    ncol = _iota((rows, ncp), 1)
    diff = qpos - (ncol * CMP_STRIDE + CMP_LEN - 1)
    mask = (diff >= 0) & (ncol < n_cmp)
    diff_f = diff.astype(F32)
    orow = _iota((LANES, ncp), 0)
    cst = _iota((LANES, ncp), 1) * CMP_STRIDE
    sst = (orow % seg) * SLC_BLOCK
    overlap = (cst < sst + SLC_BLOCK) & (cst + CMP_LEN > sst) & (orow % seg < n_slc)
    halves = range(HEADS_PER_CHUNK)
    hms = _half_masks(BF16)
    raw = [_dot_nt(_stack_group(q_ref, hms[p] * SCALE), kc) for p in halves]
    pcs = []
    for p in halves:
        slope = _per_chunk_rows(rows, tq, [slopes[2 * c + p] for c in range(N_CHUNKS)])
        sc = jnp.where(mask, raw[p] - slope * diff_f, NEG)
        mx = jnp.max(sc, axis=-1, keepdims=True)
        e = jnp.where(mask, jnp.exp(sc - mx), 0.0)
        den = jnp.sum(e, axis=-1, keepdims=True)
        pcs.append(e / jnp.where(den > 0, den, 1.0))
    _store_group(o_ref, [_dot(pcs[p].astype(BF16), vc) for p in halves], tq)
    p_slc = jnp.zeros((LANES, tq), F32)
    for p in halves:
        pg = pcs[p][0:tq]
        for c in range(1, N_CHUNKS):
            pg = pg + pcs[p][c * tq:(c + 1) * tq]
        ov = (overlap & (orow // seg == p)).astype(BF16)
        pg_hi, pg_lo = _split(pg)
        p_slc = p_slc + (_dot_nt(ov, pg_hi) + _dot_nt(ov, pg_lo))
    j = _iota((LANES, tq), 0) % seg
    cur = (q0 + _iota((LANES, tq), 1)) // SLC_BLOCK
    forced = (j == 0) | (j == cur) | (j == cur - 1)
    usable = (j <= cur) & (j < n_slc)
    score = jnp.where(usable, p_slc + jnp.where(forced, FORCE_BONUS, 0.0), NEG)
    rank = _rank_rows(score, j, n_slc, seg)
    sel_t = jnp.where((rank < topn) & usable, 1.0, 0.0)
    sel_ref[...] = sel_t.T.astype(sel_ref.dtype)


def _nsa_cmp(q, kc, vc, slopes, tq=128):
    b, s, _ = q.shape
    ncp = kc.shape[1]
    n_slc = s // SLC_BLOCK
    assert n_slc <= LANES // N_KV and ncp % LANES == 0
    kern = functools.partial(_nsa_cmp_kernel, tq=tq, n_cmp=ncp - 1, n_slc=n_slc,
                             topn=min(SLC_TOPN, n_slc), slopes=slopes)
    spec_c = pl.BlockSpec((None, ncp, LANES), lambda bi, n: (bi, 0, 0))
    return pl.pallas_call(
        kern,
        out_shape=[jax.ShapeDtypeStruct((b, s, MIX), BF16), jax.ShapeDtypeStruct((b, s, LANES), BF16)],
        grid=(b, s // tq),
        in_specs=[pl.BlockSpec((None, tq, MIX), lambda bi, n: (bi, n, 0)), spec_c, spec_c],
        out_specs=[pl.BlockSpec((None, tq, MIX), lambda bi, n: (bi, n, 0)),
                   pl.BlockSpec((None, tq, LANES), lambda bi, n: (bi, n, 0))],
        compiler_params=_cparams("parallel", "parallel"),
        name="nsa_compressed",
    )(q, kc, vc)


def _nsa_slc_kernel(q_ref, k_ref, v_ref, sel_ref, o_ref, vt_ref, m_ref, acc_ref, *, tq, tk, slopes):
    n = pl.program_id(1)
    q0 = n * tq
    rows = N_CHUNKS * tq
    seg = LANES // N_KV
    halves = range(HEADS_PER_CHUNK)
    skip = 1.0 - sel_ref[...]
    nsub = tk // SLC_BLOCK
    lane_q = _iota((tq, LANES), 1)
    row_q = _iota((tq, LANES), 0).astype(F32)
    lane_k = _iota((tk, LANES), 1)
    off_k = _iota((tk, LANES), 0)
    spare = lambda p, n: (1 - p) * HEAD_DIM + n
    hms = _half_masks(BF16)
    qs = []
    for p in halves:
        extra = jnp.concatenate(
            [jnp.where(lane_q == spare(p, 0), -slopes[2 * c + p] * row_q,
                       jnp.where((lane_q == spare(p, 1)) | (lane_q == spare(p, 2)), slopes[2 * c + p], 0.0))
             for c in range(N_CHUNKS)], axis=0)
        qs.append(_stack_group(q_ref, hms[p] * SCALE) + extra.astype(BF16))
    @pl.when(n == 0)
    def _():
        vt_ref[...] = v_ref[...].astype(F32).T.astype(BF16)

    for p in halves:
        m_ref[p] = jnp.full((1, rows), 0.1 * NEG, F32)
        acc_ref[p] = jnp.zeros((LANES, rows), F32)

    n_grp = 2
    grp = rows // n_grp
    lanes = lambda g: slice(g * grp, (g + 1) * grp)
    chains = [(p, g) for p in halves for g in range(n_grp)]

    def tile(kt, diagonal):
        k0 = pl.multiple_of(kt * tk, tk)
        kj = k_ref[pl.ds(k0, tk), :]
        vt = vt_ref[:, pl.ds(k0, tk)]
        shift = (k0 - q0).astype(F32).astype(BF16)
        raw = []
        for p in halves:
            in_sub = (lane_k >= spare(p, 3)) & (lane_k < spare(p, 3) + nsub)
            kp = jnp.where(lane_k == spare(p, 0), 1.0,
                           jnp.where(lane_k == spare(p, 1), off_k.astype(F32).astype(BF16),
                                     jnp.where(lane_k == spare(p, 2), shift,
                                               jnp.where(in_sub, jnp.where(off_k // SLC_BLOCK == lane_k - spare(p, 3),
                                                                           1.0, 0.0).astype(BF16), kj))))
            src = _iota((LANES, LANES), 0) - (p * seg + k0 // SLC_BLOCK)
            dst = _iota((LANES, LANES), 1) - spare(p, 3)
            to_lane = ((src == dst) & (dst >= 0) & (dst < nsub)).astype(BF16)
            bias = (_dot(skip, to_lane) * NEG).astype(BF16)
            qp = qs[p] + jnp.concatenate([bias] * N_CHUNKS, axis=0)
            raw += [_dot_nt(kp, qp[lanes(g)]) for g in range(n_grp)]
        if diagonal:
            future = jnp.where(_iota((tk, tq), 1) + (q0 - k0) >= _iota((tk, tq), 0), 0.0, NEG)
            future = jnp.concatenate([future] * (grp // tq), axis=1)
        es, alphas = [], []
        for p, g in chains:
            s = raw[p * n_grp + g] + future if diagonal else raw[p * n_grp + g]
            m_old = m_ref[p, :, lanes(g)]
            m_new = jnp.maximum(m_old, jnp.max(s, axis=0, keepdims=True))
            m_ref[p, :, lanes(g)] = m_new
            es.append(jnp.exp(s - m_new).astype(BF16))
            alphas.append(jnp.exp(m_old - m_new))
        row = _iota((LANES, tk), 0)
        vts = [jnp.where(row // HEAD_DIM == p, vt, jnp.ones_like(vt)) for p in halves]
        pvs = [_dot(vts[p], es[p * n_grp + g]) for p, g in chains]
        for (p, g), alpha, pv in zip(chains, alphas, pvs):
            acc_ref[p, :, lanes(g)] = alpha * acc_ref[p, :, lanes(g)] + pv

    def body(kt, carry):
        tile(kt, False)
        return carry

    n_past = q0 // tk
    lax.fori_loop(0, n_past, body, 0)
    tile(n_past, True)
    outs = []
    for p in halves:
        acc = acc_ref[p]
        den = acc[(1 - p) * HEAD_DIM:(1 - p) * HEAD_DIM + 1, :]
        outs.append((acc / den).T)
    _store_group(o_ref, outs, tq)


def _nsa_slc(q, k, v, sel, slopes, tq=256, tk=256):
    b, s, _ = q.shape
    assert tk % tq == 0 and tq % SLC_BLOCK == 0 and s % tk == 0 and tk <= 256
    assert all(_is_pow2(x) for x in slopes)
    rows = N_CHUNKS * tq
    spec_q = pl.BlockSpec((None, tq, MIX), lambda bi, n: (bi, n, 0))
    spec_kv = pl.BlockSpec((None, s, LANES), lambda bi, n: (bi, 0, 0))
    return pl.pallas_call(
        functools.partial(_nsa_slc_kernel, tq=tq, tk=tk, slopes=slopes),
        out_shape=jax.ShapeDtypeStruct((b, s, MIX), BF16),
        grid=(b, s // tq),
        in_specs=[spec_q, spec_kv, spec_kv, pl.BlockSpec((None, tq, LANES), lambda bi, n: (bi, n, 0))],
        out_specs=spec_q,
        scratch_shapes=[pltpu.VMEM((LANES, s), BF16), pltpu.VMEM((N_KV, 1, rows), F32),
                        pltpu.VMEM((N_KV, LANES, rows), F32)],
        compiler_params=_cparams("parallel", "arbitrary"),
        name="nsa_selected",
    )(q, k, v, sel)


def _odd_attn(x2, b, s, norm_g, w_in, c_q_norm, c_k_norm, d_q_norm, d_k_norm, cmp_pos, cmp_w, w_out):
    kvw = N_KV * HEAD_DIM
    n_gate = N_HEADS * 3
    qd0 = 3 * MIX
    w = jnp.concatenate([w_in[:, :qd0], _gqa_cols(w_in[:, qd0:qd0 + MIX]), w_in[:, qd0 + MIX:],
                         jnp.zeros((w_in.shape[0], LANES - n_gate), F32)], axis=1).astype(BF16)
    n = w.shape[1]
    ones = lambda width: jnp.ones((width,), F32)
    gain = jnp.concatenate([
        jnp.tile(c_q_norm, N_HEADS), jnp.tile(c_k_norm, N_HEADS), ones(MIX), jnp.tile(d_q_norm, N_HEADS),
        ones(2 * kvw), jnp.tile(d_k_norm[1], N_KV), ones(kvw), jnp.tile(d_k_norm[2], N_KV), ones(kvw),
        ones(LANES)]).reshape(1, n)
    plan, col = [], 0
    for width, op in ((MIX, "norm"), (MIX, "norm"), (MIX, None), (MIX, "norm"), (kvw, None), (kvw, None),
                      (kvw, "norm"), (kvw, None), (kvw, "norm"), (kvw, None), (LANES, "sigmoid")):
        plan.append((col, width, op))
        col += width
    outs = _proj(x2, norm_g, w, gain, tuple(plan), [BF16] * 10 + [F32])
    r3 = lambda a: a.reshape(b, s, a.shape[-1])
    qc, kc, vc, qd, kcmp, vcmp, kslc, vslc, kwin, vwin = [r3(a) for a in outs[:10]]
    gates = outs[10]
    slopes = _alibi(N_HEADS)
    gslopes = [slopes[h] for h in GQA_PERM]
    oc = _moba_attention(qc, kc, vc, slopes)
    k_cmp, v_cmp = _compress(kcmp, vcmp, cmp_pos, cmp_w, d_k_norm[0])
    o_cmp, sel = _nsa_cmp(qd, k_cmp, v_cmp, gslopes)
    o_slc = _nsa_slc(qd, kslc, vslc, sel, gslopes)
    o_win = _band_attention(qd, kwin, vwin, window=D_WINDOW, slopes=gslopes)
    t = b * s
    w1 = w_out[:MIX].astype(BF16)
    w2 = _gqa_rows(w_out[MIX:]).astype(BF16)
    flat = lambda a: a.reshape(t, a.shape[-1])
    return [flat(oc), flat(o_cmp), flat(o_slc), flat(o_win), gates], w1, w2


def kernel(x, ev_norm, ev_w_in, ev_q_norm, ev_k_norm, ev_sink, ev_w_out, od_norm, od_w_in, od_c_q_norm,
           od_c_k_norm, od_d_q_norm, od_d_k_norm, od_cmp_pos, od_cmp_w, od_w_out, moe_norm, moe_w_grp,
           moe_b_grp, moe_w_exp, moe_b_exp, moe_w_gate, moe_w_up, moe_w_down):
    b, s, d = x.shape
    x2 = x.reshape(b * s, d)
    depth = moe_norm.shape[0]
    for layer in range(depth):
        i = layer // 2
        if layer % 2 == 0:
            attn, w1, w2 = _even_attn(x2, b, s, ev_norm[i], ev_w_in[i], ev_q_norm[i], ev_k_norm[i],
                                      ev_sink[i], ev_w_out[i])
        else:
            attn, w1, w2 = _odd_attn(x2, b, s, od_norm[i], od_w_in[i], od_c_q_norm[i], od_c_k_norm[i],
                                     od_d_q_norm[i], od_d_k_norm[i], od_cmp_pos[i], od_cmp_w[i], od_w_out[i])
        x2 = _moe_block(x2, attn, w1, w2, moe_norm[layer], moe_w_grp[layer], moe_b_grp[layer],
                        moe_w_exp[layer], moe_b_exp[layer], moe_w_gate, moe_w_up, moe_w_down, layer)
    return x2.reshape(b, s, d)
```

```python
import functools
import math

import numpy as np
import jax
import jax.numpy as jnp
from jax import lax
from jax.experimental import pallas as pl
from jax.experimental.pallas import tpu as pltpu

F32 = jnp.float32
BF16 = jnp.bfloat16

LANES = 128
HEAD_DIM = 64
HEADS_PER_CHUNK = LANES // HEAD_DIM
D_MODEL = 1024
N_HEADS = 8
N_KV = 2
GROUP = N_HEADS // N_KV
N_CHUNKS = N_HEADS // HEADS_PER_CHUNK
MIX = N_HEADS * HEAD_DIM
SCALE = 1.0 / math.sqrt(HEAD_DIM)
EPS = 1e-6
NEG = -1e30
EXP_UNDERFLOW = -104.0

A_WINDOW = 128
C_BLOCK = 256
C_TOPK = 3
CMP_LEN = 32
CMP_STRIDE = 16
SLC_BLOCK = 64
SLC_TOPN = 16
D_WINDOW = 512
FORCE_BONUS = 1000.0

N_GROUPS = 4
EXPERTS_PER_GROUP = 8
N_EXPERTS = N_GROUPS * EXPERTS_PER_GROUP
EXPERT_FF = 256

VMEM_LIMIT = 48 * 1024 * 1024
MOE_VMEM_LIMIT = 56 * 1024 * 1024

GQA_PERM = tuple(h for c in range(N_CHUNKS) for h in (c, c + GROUP))
MHA_PERM = tuple(range(N_HEADS))


def _alibi(n_heads):
    return [float(2.0 ** (-8.0 * (i + 1) / n_heads)) for i in range(n_heads)]


def _is_pow2(x):
    return math.frexp(x)[0] == 0.5


def _gqa_cols(w):
    lead = w.shape[:-1]
    w = w.reshape(*lead, N_KV, GROUP, HEAD_DIM)
    return jnp.swapaxes(w, -3, -2).reshape(*lead, MIX)


def _gqa_rows(w):
    tail = w.shape[1:]
    return jnp.swapaxes(w.reshape(N_KV, GROUP, HEAD_DIM, *tail), 0, 1).reshape(MIX, *tail)


def _cparams(*sem, vmem=VMEM_LIMIT):
    return pltpu.CompilerParams(dimension_semantics=sem, vmem_limit_bytes=vmem)


def _dot(a, b):
    return jnp.dot(a, b, preferred_element_type=F32)


def _dot_nt(a, b):
    return lax.dot_general(a, b, (((1,), (1,)), ((), ())), preferred_element_type=F32)


def _split(x):
    hi = x.astype(BF16)
    lo = (x - hi.astype(F32)).astype(BF16)
    return hi, lo


def _dot_hilo(a, b):
    hi, lo = _split(a)
    return _dot(hi, b) + _dot(lo, b)


def _iota(shape, dim):
    return lax.broadcasted_iota(jnp.int32, shape, dim)


def _half_masks(dtype):
    lane = _iota((1, LANES), 1)
    return [(lane // HEAD_DIM == p).astype(dtype) for p in range(HEADS_PER_CHUNK)]


def _head_mean_sq(y):
    same = (_iota((LANES, LANES), 0) // HEAD_DIM == _iota((LANES, LANES), 1) // HEAD_DIM)
    return _dot((y * y).astype(BF16), same.astype(BF16)) * (1.0 / HEAD_DIM)


def _proj_kernel(x_ref, g_ref, w_ref, gain_ref, *out_refs, plan):
    x = x_ref[...]
    ms = jnp.mean(x * x, axis=-1, keepdims=True)
    xn = (x * lax.rsqrt(ms + EPS) * g_ref[...]).astype(BF16)
    for o_ref, (col0, width, op) in zip(out_refs, plan):
        for a in range(0, width, 2 * LANES):
            wd = min(2 * LANES, width - a)
            y = _dot(xn, w_ref[:, col0 + a:col0 + a + wd])
            for c in range(0, wd, LANES):
                yc = y[:, c:c + LANES]
                if op == "norm":
                    gain = gain_ref[:, col0 + a + c:col0 + a + c + LANES]
                    yc = yc * lax.rsqrt(_head_mean_sq(yc) + EPS) * gain
                elif op == "sigmoid":
                    yc = jax.nn.sigmoid(yc)
                o_ref[:, a + c:a + c + LANES] = yc.astype(o_ref.dtype)


def _proj(x2, norm_g, w, gain, plan, out_dtypes, tm=1024):
    t, d = x2.shape
    n = w.shape[1]
    out_shape = [jax.ShapeDtypeStruct((t, width), dt) for (_, width, _), dt in zip(plan, out_dtypes)]
    return pl.pallas_call(
        functools.partial(_proj_kernel, plan=plan),
        out_shape=out_shape,
        grid=(t // tm,),
        in_specs=[
            pl.BlockSpec((tm, d), lambda i: (i, 0)),
            pl.BlockSpec((1, d), lambda i: (0, 0)),
            pl.BlockSpec((d, n), lambda i: (0, 0)),
            pl.BlockSpec((1, n), lambda i: (0, 0)),
        ],
        out_specs=[pl.BlockSpec((tm, width), lambda i: (i, 0)) for (_, width, _) in plan],
        compiler_params=_cparams("parallel"),
        name="norm_proj",
    )(x2, norm_g.reshape(1, d), w, gain)


def _stack_group(q_ref, halfmask):
    return jnp.concatenate(
        [q_ref[:, c * LANES:(c + 1) * LANES] * halfmask for c in range(N_CHUNKS)], axis=0)


def _per_chunk_rows(rows, tq, values):
    r = _iota((rows, 1), 0) // tq
    out = jnp.full((rows, 1), values[N_CHUNKS - 1], F32)
    for c in range(N_CHUNKS - 2, -1, -1):
        out = jnp.where(r == c, values[c], out)
    return out


def _per_chunk_lanes(rows, tq, values):
    r = _iota((1, rows), 1) // tq
    out = jnp.full((1, rows), values[N_CHUNKS - 1], F32)
    for c in range(N_CHUNKS - 2, -1, -1):
        out = jnp.where(r == c, values[c], out)
    return out


def _band_kernel(*refs, tq, window, wpad, slopes, has_sink):
    if has_sink:
        sink_ref, q_ref, k_ref, v_ref, o_ref, vt_ref = refs
    else:
        q_ref, k_ref, v_ref, o_ref, vt_ref = refs
    n = pl.program_id(1)

    @pl.when(n == 0)
    def _():
        vt_ref[...] = v_ref[...].astype(F32).T.astype(BF16)

    q0 = n * tq
    kw = tq + wpad
    kstart = pl.multiple_of(jnp.maximum(q0 - wpad, 0), LANES)
    ks = k_ref[pl.ds(kstart, kw), :]
    vt = vt_ref[:, pl.ds(kstart, kw)]
    rows = N_CHUNKS * tq
    halves = range(HEADS_PER_CHUNK)
    hms = _half_masks(BF16)
    spare = lambda p, j: (1 - p) * HEAD_DIM + j
    lane_q = _iota((tq, LANES), 1)
    row_q = _iota((tq, LANES), 0).astype(F32)
    lane_k = _iota((kw, LANES), 1)
    off_k = _iota((kw, LANES), 0)
    fine = (off_k % LANES).astype(F32).astype(BF16)
    coarse = ((off_k // LANES) * LANES + (kstart - q0)).astype(F32).astype(BF16)
    diff = _iota((kw, tq), 1) + (q0 - kstart) - _iota((kw, tq), 0)
    band = jnp.where((diff >= 0) & (diff < window), 0.0, NEG)
    band = jnp.concatenate([band] * N_CHUNKS, axis=1)
    raw = []
    for p in halves:
        extra = jnp.concatenate(
            [jnp.where(lane_q == spare(p, 0), -slopes[2 * c + p] * row_q,
                       jnp.where((lane_q == spare(p, 1)) | (lane_q == spare(p, 2)), slopes[2 * c + p], 0.0))
             for c in range(N_CHUNKS)], axis=0)
        qp = _stack_group(q_ref, hms[p] * SCALE) + extra.astype(BF16)
        kp = jnp.where(lane_k == spare(p, 0), 1.0,
                       jnp.where(lane_k == spare(p, 1), fine, jnp.where(lane_k == spare(p, 2), coarse, ks)))
        raw.append(_dot_nt(kp, qp))
    es, sinks = [], []
    for p in halves:
        s = raw[p] + band
        mx = jnp.max(s, axis=0, keepdims=True)
        if has_sink:
            sk = _per_chunk_lanes(rows, tq, [sink_ref[2 * c + p] for c in range(N_CHUNKS)])
            mx = jnp.maximum(mx, sk)
            sinks.append(jnp.exp(sk - mx))
        es.append(jnp.exp(s - mx).astype(BF16))
    vrow = _iota((LANES, kw), 0)
    outs = []
    for p in halves:
        acc = _dot(jnp.where(vrow // HEAD_DIM == p, vt, jnp.ones_like(vt)), es[p])
        den = acc[(1 - p) * HEAD_DIM:(1 - p) * HEAD_DIM + 1, :]
        if has_sink:
            den = den + sinks[p]
        outs.append((acc / den).T)
    _store_group(o_ref, outs, tq)


def _store_group(o_ref, os_, tq):
    lane = _iota((1, LANES), 1)
    for c in range(N_CHUNKS):
        o = jnp.where(lane < HEAD_DIM, os_[0][c * tq:(c + 1) * tq], os_[1][c * tq:(c + 1) * tq])
        o_ref[:, c * LANES:(c + 1) * LANES] = o.astype(o_ref.dtype)


def _band_attention(q, k, v, *, window, slopes, sink=None, tq=128):
    b, s, _ = q.shape
    wpad = -(-window // LANES) * LANES
    assert s >= tq + wpad and s % tq == 0 and tq <= LANES and all(_is_pow2(x) for x in slopes)
    kern = functools.partial(_band_kernel, tq=tq, window=window, wpad=wpad,
                             slopes=slopes, has_sink=sink is not None)
    in_specs = [
        pl.BlockSpec((None, tq, MIX), lambda bi, n: (bi, n, 0)),
        pl.BlockSpec((None, s, LANES), lambda bi, n: (bi, 0, 0)),
        pl.BlockSpec((None, s, LANES), lambda bi, n: (bi, 0, 0)),
    ]
    args = [q, k, v]
    if sink is not None:
        in_specs = [pl.BlockSpec(memory_space=pltpu.SMEM)] + in_specs
        args = [sink] + args
    return pl.pallas_call(
        kern,
        out_shape=jax.ShapeDtypeStruct((b, s, MIX), BF16),
        grid=(b, s // tq),
        in_specs=in_specs,
        out_specs=pl.BlockSpec((None, tq, MIX), lambda bi, n: (bi, n, 0)),
        scratch_shapes=[pltpu.VMEM((LANES, s), BF16)],
        compiler_params=_cparams("parallel", "arbitrary"),
        name="band_attention",
    )(*args)


def _stick_kernel(q_ref, k_ref, v_ref, o_ref, acc_ref, run_ref, *, tq, cpb):
    i = pl.program_id(2)
    q0 = pl.multiple_of(i * tq, tq)
    hms = _half_masks(BF16)
    heads = [(cc, p) for cc in range(cpb) for p in range(HEADS_PER_CHUNK)]
    qs = [q_ref[:, cc * LANES:(cc + 1) * LANES] * (hms[p] * SCALE) for cc, p in heads]
    upper = (_iota((2 * tq, tq), 0) % tq > _iota((2 * tq, tq), 1)).astype(BF16)

    def suffix_sum(x):
        hi, lo = _split(x)
        return _dot(jnp.concatenate([hi, lo], axis=1), upper)

    def block(kstart, diag):
        if diag:
            causal = _iota((tq, tq), 1) < _iota((tq, tq), 0)
        kjs = [k_ref[pl.ds(kstart, tq), cc * LANES:(cc + 1) * LANES] for cc in range(cpb)]
        vjs = [v_ref[pl.ds(kstart, tq), cc * LANES:(cc + 1) * LANES] for cc in range(cpb)]
        zs = [_dot_nt(qs[h], kjs[cc]) for h, (cc, p) in enumerate(heads)]
        lss, lks = [], []
        for z in zs:
            ls = jnp.minimum(z, 0.0) - jnp.log(1.0 + jnp.exp(-jnp.abs(z)))
            lk = ls - z
            if diag:
                lk = jnp.where(causal, lk, 0.0)
            lss.append(ls)
            lks.append(lk)
        sufs = [suffix_sum(lk) for lk in lks]
        ws = []
        for h in range(len(heads)):
            if diag:
                a = jnp.where(causal, jnp.exp(lss[h] + sufs[h]), 0.0)
            else:
                a = jnp.exp(lss[h] + sufs[h] + run_ref[h])
            ws.append(a.astype(BF16))
        pvs = [_dot(ws[h], vjs[cc]) for h, (cc, p) in enumerate(heads)]
        for h in range(len(heads)):
            rowsum = jnp.sum(lks[h], axis=-1, keepdims=True)
            if diag:
                acc_ref[h] = pvs[h]
                run_ref[h] = rowsum
            else:
                acc_ref[h] += pvs[h]
                run_ref[h] += rowsum

    block(q0, True)

    def weights_alive():
        run = run_ref[0]
        for h in range(1, len(heads)):
            run = jnp.maximum(run, run_ref[h])
        return jnp.max(run) > EXP_UNDERFLOW

    def body(carry):
        t, _ = carry
        block(pl.multiple_of((i - 1 - t) * tq, tq), False)
        return t + 1, weights_alive()

    lax.while_loop(lambda c: (c[0] < i) & c[1], body, (jnp.int32(0), weights_alive()))
    lane = _iota((1, LANES), 1)
    for cc in range(cpb):
        o = jnp.where(lane < HEAD_DIM, acc_ref[HEADS_PER_CHUNK * cc], acc_ref[HEADS_PER_CHUNK * cc + 1])
        o_ref[:, cc * LANES:(cc + 1) * LANES] = o.astype(o_ref.dtype)


def _stick_attention(q, k, v, tq=256, cpb=4):
    b, s, _ = q.shape
    wide = cpb * LANES
    n_heads = cpb * HEADS_PER_CHUNK
    spec_q = pl.BlockSpec((None, tq, wide), lambda bi, c, i: (bi, i, c))
    spec_kv = pl.BlockSpec((None, s, wide), lambda bi, c, i: (bi, 0, c))
    return pl.pallas_call(
        functools.partial(_stick_kernel, tq=tq, cpb=cpb),
        out_shape=jax.ShapeDtypeStruct((b, s, MIX), BF16),
        grid=(b, N_CHUNKS // cpb, s // tq),
        in_specs=[spec_q, spec_kv, spec_kv],
        out_specs=spec_q,
        scratch_shapes=[pltpu.VMEM((n_heads, tq, LANES), F32), pltpu.VMEM((n_heads, tq, 1), F32)],
        compiler_params=_cparams("parallel", "parallel", "parallel"),
        name="stick_breaking",
    )(q, k, v)


def _route(logits):
    lane = _iota(logits.shape, 1)
    lane_f = lane.astype(F32)
    ninf = -jnp.inf
    is_g = (lane >= N_EXPERTS) & (lane < N_EXPERTS + N_GROUPS)
    gmax = jnp.max(jnp.where(is_g, logits, ninf), axis=-1, keepdims=True)
    gidx = jnp.min(jnp.where(is_g & (logits == gmax), lane_f - N_EXPERTS, 1e9), axis=-1, keepdims=True)
    p_g = 1.0 / jnp.sum(jnp.where(is_g, jnp.exp(logits - gmax), 0.0), axis=-1, keepdims=True)
    in_grp = (lane < N_EXPERTS) & ((lane // EXPERTS_PER_GROUP).astype(F32) == gidx)
    le = jnp.where(in_grp, logits, ninf)
    m1 = jnp.max(le, axis=-1, keepdims=True)
    i1 = jnp.min(jnp.where(le == m1, lane_f, 1e9), axis=-1, keepdims=True)
    le2 = jnp.where(lane_f == i1, ninf, le)
    m2 = jnp.max(le2, axis=-1, keepdims=True)
    i2 = jnp.min(jnp.where(le2 == m2, lane_f, 1e9), axis=-1, keepdims=True)
    e2 = jnp.exp(m2 - m1)
    w1 = p_g / (1.0 + e2)
    w2 = p_g * e2 / (1.0 + e2)
    return jnp.where(lane_f == i1, w1, 0.0) + jnp.where(lane_f == i2, w2, 0.0)


def _gate_expand(branch):
    r = _iota((LANES, MIX), 0)
    col = _iota((LANES, MIX), 1)
    head = col // LANES + GROUP * ((col % LANES) // HEAD_DIM)
    return (r == 3 * head + branch).astype(BF16)


def _out_kernel(*refs, nsa):
    if nsa:
        (x_ref, o1_ref, ocmp_ref, oslc_ref, owin_ref, gates_ref, w1_ref, w2_ref,
         ng_ref, wr_ref, br_ref, x1_ref, h_ref, comb_ref) = refs
        g = gates_ref[...]
        o2 = (_dot_hilo(g, _gate_expand(0)) * ocmp_ref[...]
              + _dot_hilo(g, _gate_expand(1)) * oslc_ref[...]
              + _dot_hilo(g, _gate_expand(2)) * owin_ref[...]).astype(BF16)
    else:
        (x_ref, o1_ref, o2_ref, w1_ref, w2_ref,
         ng_ref, wr_ref, br_ref, x1_ref, h_ref, comb_ref) = refs
        o2 = o2_ref[...]
    x1 = x_ref[...] + _dot(o1_ref[...], w1_ref[...]) + _dot(o2, w2_ref[...])
    x1_ref[...] = x1
    ms = jnp.mean(x1 * x1, axis=-1, keepdims=True)
    h = x1 * lax.rsqrt(ms + EPS) * ng_ref[...]
    h_ref[...] = h.astype(BF16)
    h_hi, h_lo = _split(h)
    w_hi, w_lo = _split(wr_ref[...])
    logits = _dot(h_hi, w_hi) + (_dot(h_hi, w_lo) + _dot(h_lo, w_hi)) + br_ref[...]
    comb_ref[...] = _route(logits)


def _out_proj_route(x2, attn, w1, w2, moe_g, w_route, b_route, tm=512):
    t, d = x2.shape
    nsa = len(attn) > 2
    row = lambda width: pl.BlockSpec((tm, width), lambda i: (i, 0))
    full = lambda a: pl.BlockSpec(a.shape, lambda i: (0, 0))
    consts = [w1, w2, moe_g.reshape(1, d), w_route, b_route]
    return pl.pallas_call(
        functools.partial(_out_kernel, nsa=nsa),
        out_shape=[jax.ShapeDtypeStruct((t, d), F32), jax.ShapeDtypeStruct((t, d), BF16),
                   jax.ShapeDtypeStruct((t, LANES), F32)],
        grid=(t // tm,),
        in_specs=[row(d)] + [row(a.shape[1]) for a in attn] + [full(a) for a in consts],
        out_specs=[row(d), row(d), row(LANES)],
        compiler_params=_cparams("parallel"),
        name="out_proj_route",
    )(x2, *attn, *consts)


MOE_CHUNK = 256
SEG_ALIGN = 16


def _moe_sort(h_ref, comb_ref, hs_ref, cs_ref, pos_ref, acc_ref, seg_ref):
    tm = h_ref.shape[0]
    rows_s = hs_ref.shape[0]
    comb = comb_ref[...]
    used = jnp.where(comb > 0, 1.0, 0.0).astype(BF16)
    of_group = ((_iota((LANES, LANES), 0) // EXPERTS_PER_GROUP == _iota((LANES, LANES), 1))
                & (_iota((LANES, LANES), 0) < N_EXPERTS)).astype(BF16)
    member = jnp.where(_dot(used, of_group) > 0.5, 1.0, 0.0)
    member_t = member.T
    earlier = (_iota((tm, tm), 0) < _iota((tm, tm), 1)).astype(BF16)
    rank_t = _dot(member_t.astype(BF16), earlier)
    count = jnp.sum(member_t, axis=1, keepdims=True)
    counts = [count[g:g + 1, :] for g in range(N_GROUPS)]
    starts = [jnp.zeros((1, 1), F32)]
    for g in range(1, N_GROUPS):
        padded = jnp.floor((counts[g - 1] + (SEG_ALIGN - 1)) * (1.0 / SEG_ALIGN)) * SEG_ALIGN
        starts.append(starts[g - 1] + padded)
    grow = _iota((LANES, 1), 0)
    start_col = jnp.zeros((LANES, 1), F32)
    for g in range(1, N_GROUPS):
        start_col = jnp.where(grow == g, starts[g], start_col)
    pos_t = jnp.sum(member_t * (start_col + rank_t), axis=0, keepdims=True)
    place = jnp.where(_iota((rows_s, tm), 0).astype(F32) == pos_t, 1.0, 0.0).astype(BF16)
    hs_ref[...] = _dot(place, h_ref[...]).astype(BF16)
    c_hi = comb.astype(BF16)
    rest = comb - c_hi.astype(F32)
    c_mid = rest.astype(BF16)
    c_lo = (rest - c_mid.astype(F32)).astype(BF16)
    cs_ref[...] = _dot(place, c_hi) + (_dot(place, c_mid) + _dot(place, c_lo))
    pos_ref[...] = jnp.broadcast_to(pos_t, (LANES, tm)).T
    acc_ref[...] = jnp.zeros(acc_ref.shape, F32)
    for g in range(N_GROUPS):
        seg_ref[g] = starts[g][0, 0].astype(jnp.int32)
        seg_ref[N_GROUPS + g] = counts[g][0, 0].astype(jnp.int32)


def _moe_kernel(h_ref, comb_ref, x1_ref, wg_ref, wu_ref, wd_ref, o_ref,
                hs_ref, cs_ref, pos_ref, acc_ref, seg_ref, *, per_step):
    step = pl.program_id(1)
    tm = h_ref.shape[0]
    rows_s = hs_ref.shape[0]

    @pl.when(step == 0)
    def _():
        _moe_sort(h_ref, comb_ref, hs_ref, cs_ref, pos_ref, acc_ref, seg_ref)

    first = step * per_step
    group = first // EXPERTS_PER_GROUP
    start = seg_ref[group]
    count = seg_ref[N_GROUPS + group]
    w_gate = [wg_ref[j] for j in range(per_step)]
    w_up = [wu_ref[j] for j in range(per_step)]
    w_down = jnp.concatenate([wd_ref[j] for j in range(per_step)], axis=0)
    sizes = [MOE_CHUNK, MOE_CHUNK // 2, MOE_CHUNK // 2] + [MOE_CHUNK] * (tm // MOE_CHUNK - 2)
    begin = 0
    for size in sizes:
        @pl.when(begin < count)
        def _(begin=begin, size=size):
            off = pl.multiple_of(start + begin, SEG_ALIGN)
            hs = hs_ref[pl.ds(off, size), :]
            cs = cs_ref[pl.ds(off, size), :]
            lane = _iota((size, LANES), 1)
            acts = []
            for j in range(per_step):
                g = _dot(hs, w_gate[j])
                u = _dot(hs, w_up[j])
                c = jnp.sum(jnp.where(lane == first + j, cs, 0.0), axis=-1, keepdims=True)
                acts.append((c * (g * jax.nn.sigmoid(g) * u)).astype(BF16))
            acc_ref[pl.ds(off, size), :] += _dot(jnp.concatenate(acts, axis=1), w_down)
        begin += size

    @pl.when(step == pl.num_programs(1) - 1)
    def _():
        pos = pos_ref[...]
        back = jnp.concatenate(
            [jnp.where(pos == (_iota((tm, LANES), 1) + blk * LANES).astype(F32), 1.0, 0.0).astype(BF16)
             for blk in range(rows_s // LANES)], axis=1)
        o_ref[...] = x1_ref[...] + _dot(back, acc_ref[...].astype(BF16))


def _moe(h, comb, x1, w_gate, w_up, w_down, layer, tm=1024, per_step=4):
    t, d = h.shape
    _, n_exp, _, ff = w_gate.shape
    assert t % tm == 0 and EXPERTS_PER_GROUP % per_step == 0 and tm % MOE_CHUNK == 0
    rows_s = -(-(tm + N_GROUPS * SEG_ALIGN + MOE_CHUNK) // LANES) * LANES
    return pl.pallas_call(
        functools.partial(_moe_kernel, per_step=per_step),
        out_shape=jax.ShapeDtypeStruct((t, d), F32),
        grid=(t // tm, n_exp // per_step),
        scratch_shapes=[pltpu.VMEM((rows_s, d), BF16), pltpu.VMEM((rows_s, LANES), F32),
                        pltpu.VMEM((tm, LANES), F32), pltpu.VMEM((rows_s, d), F32),
                        pltpu.SMEM((2 * N_GROUPS,), jnp.int32)],
        in_specs=[
            pl.BlockSpec((tm, d), lambda i, e: (i, 0)),
            pl.BlockSpec((tm, LANES), lambda i, e: (i, 0)),
            pl.BlockSpec((tm, d), lambda i, e: (i, 0)),
            pl.BlockSpec((None, per_step, d, ff), lambda i, e: (layer, e, 0, 0)),
            pl.BlockSpec((None, per_step, d, ff), lambda i, e: (layer, e, 0, 0)),
            pl.BlockSpec((None, per_step, ff, d), lambda i, e: (layer, e, 0, 0)),
        ],
        out_specs=pl.BlockSpec((tm, d), lambda i, e: (i, 0)),
        compiler_params=_cparams("parallel", "arbitrary", vmem=MOE_VMEM_LIMIT),
        name="moe_experts",
    )(h, comb, x1, w_gate, w_up, w_down)


def _moe_block(x2, attn, w1, w2, moe_g, w_grp, b_grp, w_exp, b_exp, w_gate, w_up, w_down, layer):
    d = x2.shape[1]
    pad = LANES - N_EXPERTS - N_GROUPS
    w_route = jnp.concatenate([w_exp, w_grp, jnp.zeros((d, pad), F32)], axis=1)
    b_route = jnp.concatenate([b_exp, b_grp, jnp.zeros((pad,), F32)]).reshape(1, LANES)
    x1, h, comb = _out_proj_route(x2, attn, w1, w2, moe_g, w_route, b_route)
    return _moe(h, comb, x1, w_gate, w_up, w_down, layer)


def _even_attn(x2, b, s, norm_g, w_in, q_norm, k_norm, sink, w_out):
    w = jnp.concatenate([_gqa_cols(w_in[:, :MIX]), w_in[:, MIX:]], axis=1).astype(BF16)
    n = w.shape[1]
    kvw = N_KV * HEAD_DIM
    gain = jnp.concatenate([jnp.tile(q_norm, N_HEADS), jnp.tile(k_norm, N_KV),
                            jnp.ones((n - MIX - kvw,), F32)]).reshape(1, n)
    plan, col = [], 0
    for width, op in ((MIX, "norm"), (kvw, "norm"), (kvw, None), (MIX, None), (MIX, None), (MIX, None)):
        plan.append((col, width, op))
        col += width
    qa, ka, va, qb, kb, vb = _proj(x2, norm_g, w, gain, tuple(plan), [BF16] * 6)
    r3 = lambda a: a.reshape(b, s, a.shape[-1])
    slopes = _alibi(N_HEADS)
    oa = _band_attention(r3(qa), r3(ka), r3(va), window=A_WINDOW,
                         slopes=[slopes[h] for h in GQA_PERM], sink=sink[np.asarray(GQA_PERM)])
    ob = _stick_attention(r3(qb), r3(kb), r3(vb))
    w1 = _gqa_rows(w_out[:MIX]).astype(BF16)
    w2 = w_out[MIX:].astype(BF16)
    t = b * s
    return [oa.reshape(t, MIX), ob.reshape(t, MIX)], w1, w2


def _rank_rows(score, j, n, seg):
    row = _iota(score.shape, 0)
    rank = jnp.zeros(score.shape, F32)
    for jj in range(n):
        other = score[jj:jj + 1, :]
        for sgm in range(1, score.shape[0] // seg):
            other = jnp.where(row // seg == sgm, score[sgm * seg + jj:sgm * seg + jj + 1, :], other)
        beats = (other > score) | ((other == score) & (jj < j))
        rank = rank + jnp.where(beats, 1.0, 0.0)
    return rank


def _ones_beside(v, p):
    lane = _iota((1, LANES), 1)
    return jnp.where(lane // HEAD_DIM == p, v, jnp.ones_like(v))


def _normalize(acc):
    return acc / pltpu.roll(acc, HEAD_DIM, 1)


def _moba_kernel(slope_ref, q_ref, k_ref, v_ref, o_ref, kmean_ref, vt_ref, m_ref, acc_ref, *,
                 blk, nblk, topk, cpb):
    g = pl.program_id(1)
    i = pl.program_id(2)
    s_len = k_ref.shape[0]
    heads = [(cc, p) for cc in range(cpb) for p in range(HEADS_PER_CHUNK)]
    nh = len(heads)
    chunk = lambda cc: slice(cc * LANES, (cc + 1) * LANES)

    @pl.when(i == 0)
    def _():
        member = (_iota((LANES, s_len), 1) // blk == _iota((LANES, s_len), 0)).astype(BF16)
        for cc in range(cpb):
            kmean_ref[cc] = _dot(member, k_ref[:, chunk(cc)]) * (1.0 / blk)
            vt_ref[cc] = v_ref[:, chunk(cc)].astype(F32).T.astype(BF16)

    hms = _half_masks(BF16)
    q0 = pl.multiple_of(i * blk, blk)
    rel = (_iota((blk, blk), 0) - _iota((blk, blk), 1)).astype(F32)
    slopes = [slope_ref[HEADS_PER_CHUNK * (g * cpb + cc) + p] for cc, p in heads]
    kms = [_split(kmean_ref[cc]) for cc in range(cpb)]
    gates = [_dot_nt(kms[cc][0], q_ref[:, chunk(cc)] * hms[p]) + _dot_nt(kms[cc][1], q_ref[:, chunk(cc)] * hms[p])
             for cc, p in heads]
    lane = _iota((blk, LANES), 1)
    off_f = _iota((blk, LANES), 0).astype(F32)
    off_b = off_f.astype(BF16)
    spare = lambda p, n: (1 - p) * HEAD_DIM + n

    def with_key_lanes(kj, p, shift):
        return jnp.where(lane == spare(p, 0), 1.0,
                         jnp.where(lane == spare(p, 1), off_b,
                                   jnp.where(lane == spare(p, 2), shift.astype(BF16), kj)))

    qs = []
    for h, (cc, p) in enumerate(heads):
        extra = jnp.where(lane == spare(p, 0), -slopes[h] * off_f,
                          jnp.where((lane == spare(p, 1)) | (lane == spare(p, 2)), slopes[h], 0.0))
        qs.append(q_ref[:, chunk(cc)] * (hms[p] * SCALE) + extra.astype(BF16))
    zero = jnp.zeros((), F32)
    raw = [_dot_nt(with_key_lanes(k_ref[pl.ds(q0, blk), chunk(cc)], p, zero), qs[h])
           for h, (cc, p) in enumerate(heads)]
    nrow = -(-nblk // 8) * 8
    blk_id = _iota((nrow, blk), 0)
    skips = []
    for h in range(nh):
        gate = jnp.where(blk_id < i, gates[h][:nrow], NEG)
        rank = _rank_rows(gate, blk_id, nblk, nrow)
        skip_t = jnp.where((rank < topk) & (blk_id < i), 0.0, 1.0)
        skip_t = jnp.concatenate([skip_t, jnp.ones((LANES - nrow, blk), F32)], axis=0)
        skips.append(skip_t.T.astype(BF16))
    own_bias = jnp.where(rel <= 0, 0.0, NEG)
    vrow = _iota((LANES, blk), 0)

    def values_t(k0, cc, p):
        vt = vt_ref[cc, :, pl.ds(k0, blk)]
        return jnp.where(vrow // HEAD_DIM == p, vt, jnp.ones_like(vt))

    es = []
    for h in range(nh):
        s = raw[h] + own_bias
        m = jnp.max(s, axis=0, keepdims=True)
        m_ref[h] = m
        es.append(jnp.exp(s - m).astype(BF16))
    for h, (cc, p) in enumerate(heads):
        acc_ref[h] = _dot(values_t(q0, cc, p), es[h])

    def body(j, carry):
        k0 = pl.multiple_of(j * blk, blk)
        shift = ((j - i) * blk).astype(F32)
        to_lane = [((_iota((LANES, LANES), 0) == j) & (_iota((LANES, LANES), 1) == spare(p, 0))).astype(BF16)
                   for p in range(HEADS_PER_CHUNK)]
        skipped = [_dot(skips[h], to_lane[p]) for h, (cc, p) in enumerate(heads)]
        raw = [_dot_nt(with_key_lanes(k_ref[pl.ds(k0, blk), chunk(cc)], p, shift),
                       qs[h] + (skipped[h] * NEG).astype(BF16))
               for h, (cc, p) in enumerate(heads)]
        es, alphas = [], []
        for h in range(nh):
            s = raw[h]
            m_old = m_ref[h]
            m_new = jnp.maximum(m_old, jnp.max(s, axis=0, keepdims=True))
            m_ref[h] = m_new
            es.append(jnp.exp(s - m_new).astype(BF16))
            alphas.append(jnp.exp(m_old - m_new))
        pvs = [_dot(values_t(k0, cc, p), es[h]) for h, (cc, p) in enumerate(heads)]
        for h in range(nh):
            acc_ref[h] = alphas[h] * acc_ref[h] + pvs[h]
        return carry

    lax.fori_loop(0, i, body, 0)
    lane = _iota((1, LANES), 1)
    for cc in range(cpb):
        outs = []
        for p in range(HEADS_PER_CHUNK):
            acc = acc_ref[HEADS_PER_CHUNK * cc + p]
            den = acc[(1 - p) * HEAD_DIM:(1 - p) * HEAD_DIM + 1, :]
            outs.append((acc / den).T)
        o_ref[:, chunk(cc)] = jnp.where(lane < HEAD_DIM, outs[0], outs[1]).astype(o_ref.dtype)


def _moba_attention(q, k, v, slopes, cpb=4):
    b, s, _ = q.shape
    assert s % C_BLOCK == 0 and C_BLOCK <= 256 and all(_is_pow2(x) for x in slopes)
    nblk = s // C_BLOCK
    assert nblk <= LANES
    wide = cpb * LANES
    nh = cpb * HEADS_PER_CHUNK
    spec_q = pl.BlockSpec((None, C_BLOCK, wide), lambda bi, c, i, sl: (bi, i, c))
    spec_kv = pl.BlockSpec((None, s, wide), lambda bi, c, i, sl: (bi, 0, c))
    return pl.pallas_call(
        functools.partial(_moba_kernel, blk=C_BLOCK, nblk=nblk, topk=min(C_TOPK, nblk), cpb=cpb),
        out_shape=jax.ShapeDtypeStruct((b, s, MIX), BF16),
        grid_spec=pltpu.PrefetchScalarGridSpec(
            num_scalar_prefetch=1,
            grid=(b, N_CHUNKS // cpb, nblk),
            in_specs=[spec_q, spec_kv, spec_kv],
            out_specs=spec_q,
            scratch_shapes=[pltpu.VMEM((cpb, LANES, LANES), F32), pltpu.VMEM((cpb, LANES, s), BF16),
                            pltpu.VMEM((nh, 1, C_BLOCK), F32), pltpu.VMEM((nh, LANES, C_BLOCK), F32)],
        ),
        compiler_params=_cparams("parallel", "parallel", "arbitrary"),
        name="moba_attention",
    )(jnp.asarray(slopes, F32), q, k, v)


def _compress_kernel(xk_ref, xv_ref, wk_lo_ref, wk_hi_ref, wv_lo_ref, wv_hi_ref,
                     pk_lo_ref, pk_hi_ref, pv_lo_ref, pv_hi_ref, gain_ref, kc_ref, vc_ref):
    def compress(x_ref, w_lo_ref, w_hi_ref, p_lo_ref, p_hi_ref):
        x = x_ref[...]
        nrow = x.shape[0]
        first = _dot(x, w_lo_ref[...])
        second = pltpu.roll(_dot(x, w_hi_ref[...]), nrow - 1, 0)
        p_lo = jnp.broadcast_to(p_lo_ref[...], (8, p_lo_ref.shape[1]))
        p_hi = jnp.broadcast_to(p_hi_ref[...], (8, p_hi_ref.shape[1]))
        bias = _dot_hilo(p_lo, w_lo_ref[...]) + _dot_hilo(p_hi, w_hi_ref[...])
        return first + second + bias[0:1]

    kc = compress(xk_ref, wk_lo_ref, wk_hi_ref, pk_lo_ref, pk_hi_ref)
    kc = kc * lax.rsqrt(_head_mean_sq(kc) + EPS) * gain_ref[...]
    kc_ref[...] = kc.astype(kc_ref.dtype)
    vc_ref[...] = compress(xv_ref, wv_lo_ref, wv_hi_ref, pv_lo_ref, pv_hi_ref).astype(vc_ref.dtype)


def _compress_weights(pos, w):
    half = CMP_LEN // 2
    eye = jnp.eye(N_KV, dtype=F32)
    wd = jnp.einsum("gh,lde->lgdhe", eye, w).reshape(CMP_LEN, LANES, LANES)
    w_lo = wd[:half].reshape(half * LANES, LANES).astype(BF16)
    w_hi = wd[half:].reshape(half * LANES, LANES).astype(BF16)
    pt = jnp.tile(pos, (1, N_KV))
    return w_lo, w_hi, pt[:half].reshape(1, half * LANES), pt[half:].reshape(1, half * LANES)


def _compress(kcmp, vcmp, cmp_pos, cmp_w, k_gain):
    b, s, _ = kcmp.shape
    assert CMP_LEN == 2 * CMP_STRIDE and s % CMP_STRIDE == 0
    nrow = s // CMP_STRIDE
    wide = CMP_STRIDE * LANES
    xk = kcmp.reshape(b, nrow, wide)
    xv = vcmp.reshape(b, nrow, wide)
    wk = _compress_weights(cmp_pos[0], cmp_w[0])
    wv = _compress_weights(cmp_pos[1], cmp_w[1])
    consts = [wk[0], wk[1], wv[0], wv[1], wk[2], wk[3], wv[2], wv[3],
              jnp.tile(k_gain, N_KV).reshape(1, LANES)]
    spec_x = pl.BlockSpec((None, nrow, wide), lambda bi: (bi, 0, 0))
    spec_o = pl.BlockSpec((None, nrow, LANES), lambda bi: (bi, 0, 0))
    return pl.pallas_call(
        _compress_kernel,
        out_shape=[jax.ShapeDtypeStruct((b, nrow, LANES), BF16)] * 2,
        grid=(b,),
        in_specs=[spec_x, spec_x] + [pl.BlockSpec(a.shape, lambda bi: (0, 0)) for a in consts],
        out_specs=[spec_o, spec_o],
        compiler_params=_cparams("parallel"),
        name="nsa_compress",
    )(xk, xv, *consts)


def _nsa_cmp_kernel(q_ref, kc_ref, vc_ref, o_ref, sel_ref, *, tq, n_cmp, n_slc, topn, slopes):
    q0 = pl.program_id(1) * tq
    kc = kc_ref[...]
    vc = vc_ref[...]
    ncp = kc.shape[0]
    rows = N_CHUNKS * tq
    seg = LANES // N_KV
    qpos = q0 + _iota((rows, ncp), 0) % tq
    ncol = _iota((rows, ncp), 1)
    diff = qpos - (ncol * CMP_STRIDE + CMP_LEN - 1)
    mask = (diff >= 0) & (ncol < n_cmp)
    diff_f = diff.astype(F32)
    orow = _iota((LANES, ncp), 0)
    cst = _iota((LANES, ncp), 1) * CMP_STRIDE
    sst = (orow % seg) * SLC_BLOCK
    overlap = (cst < sst + SLC_BLOCK) & (cst + CMP_LEN > sst) & (orow % seg < n_slc)
    halves = range(HEADS_PER_CHUNK)
    hms = _half_masks(BF16)
    raw = [_dot_nt(_stack_group(q_ref, hms[p] * SCALE), kc) for p in halves]
    pcs = []
    for p in halves:
        slope = _per_chunk_rows(rows, tq, [slopes[2 * c + p] for c in range(N_CHUNKS)])
        sc = jnp.where(mask, raw[p] - slope * diff_f, NEG)
        mx = jnp.max(sc, axis=-1, keepdims=True)
        e = jnp.where(mask, jnp.exp(sc - mx), 0.0)
        den = jnp.sum(e, axis=-1, keepdims=True)
        pcs.append(e / jnp.where(den > 0, den, 1.0))
    _store_group(o_ref, [_dot(pcs[p].astype(BF16), vc) for p in halves], tq)
    p_slc = jnp.zeros((LANES, tq), F32)
    for p in halves:
        pg = pcs[p][0:tq]
        for c in range(1, N_CHUNKS):
            pg = pg + pcs[p][c * tq:(c + 1) * tq]
        ov = (overlap & (orow // seg == p)).astype(BF16)
        pg_hi, pg_lo = _split(pg)
        p_slc = p_slc + (_dot_nt(ov, pg_hi) + _dot_nt(ov, pg_lo))
    j = _iota((LANES, tq), 0) % seg
    cur = (q0 + _iota((LANES, tq), 1)) // SLC_BLOCK
    forced = (j == 0) | (j == cur) | (j == cur - 1)
    usable = (j <= cur) & (j < n_slc)
    score = jnp.where(usable, p_slc + jnp.where(forced, FORCE_BONUS, 0.0), NEG)
    rank = _rank_rows(score, j, n_slc, seg)
    sel_t = jnp.where((rank < topn) & usable, 1.0, 0.0)
    sel_ref[...] = sel_t.T.astype(sel_ref.dtype)


def _nsa_cmp(q, kc, vc, slopes, tq=128):
    b, s, _ = q.shape
    ncp = kc.shape[1]
    n_slc = s // SLC_BLOCK
    assert n_slc <= LANES // N_KV and ncp % LANES == 0
    kern = functools.partial(_nsa_cmp_kernel, tq=tq, n_cmp=ncp - 1, n_slc=n_slc,
                             topn=min(SLC_TOPN, n_slc), slopes=slopes)
    spec_c = pl.BlockSpec((None, ncp, LANES), lambda bi, n: (bi, 0, 0))
    return pl.pallas_call(
        kern,
        out_shape=[jax.ShapeDtypeStruct((b, s, MIX), BF16), jax.ShapeDtypeStruct((b, s, LANES), BF16)],
        grid=(b, s // tq),
        in_specs=[pl.BlockSpec((None, tq, MIX), lambda bi, n: (bi, n, 0)), spec_c, spec_c],
        out_specs=[pl.BlockSpec((None, tq, MIX), lambda bi, n: (bi, n, 0)),
                   pl.BlockSpec((None, tq, LANES), lambda bi, n: (bi, n, 0))],
        compiler_params=_cparams("parallel", "parallel"),
        name="nsa_compressed",
    )(q, kc, vc)


def _nsa_slc_kernel(q_ref, k_ref, v_ref, sel_ref, o_ref, vt_ref, m_ref, acc_ref, *, tq, tk, slopes):
    n = pl.program_id(1)
    q0 = n * tq
    rows = N_CHUNKS * tq
    seg = LANES // N_KV
    halves = range(HEADS_PER_CHUNK)
    skip = 1.0 - sel_ref[...]
    nsub = tk // SLC_BLOCK
    lane_q = _iota((tq, LANES), 1)
    row_q = _iota((tq, LANES), 0).astype(F32)
    lane_k = _iota((tk, LANES), 1)
    off_k = _iota((tk, LANES), 0)
    spare = lambda p, n: (1 - p) * HEAD_DIM + n
    hms = _half_masks(BF16)
    qs = []
    for p in halves:
        extra = jnp.concatenate(
            [jnp.where(lane_q == spare(p, 0), -slopes[2 * c + p] * row_q,
                       jnp.where((lane_q == spare(p, 1)) | (lane_q == spare(p, 2)), slopes[2 * c + p], 0.0))
             for c in range(N_CHUNKS)], axis=0)
        qs.append(_stack_group(q_ref, hms[p] * SCALE) + extra.astype(BF16))
    @pl.when(n == 0)
    def _():
        vt_ref[...] = v_ref[...].astype(F32).T.astype(BF16)

    for p in halves:
        m_ref[p] = jnp.full((1, rows), 0.1 * NEG, F32)
        acc_ref[p] = jnp.zeros((LANES, rows), F32)

    n_grp = 2
    grp = rows // n_grp
    lanes = lambda g: slice(g * grp, (g + 1) * grp)
    chains = [(p, g) for p in halves for g in range(n_grp)]

    def tile(kt, diagonal):
        k0 = pl.multiple_of(kt * tk, tk)
        kj = k_ref[pl.ds(k0, tk), :]
        vt = vt_ref[:, pl.ds(k0, tk)]
        shift = (k0 - q0).astype(F32).astype(BF16)
        raw = []
        for p in halves:
            in_sub = (lane_k >= spare(p, 3)) & (lane_k < spare(p, 3) + nsub)
            kp = jnp.where(lane_k == spare(p, 0), 1.0,
                           jnp.where(lane_k == spare(p, 1), off_k.astype(F32).astype(BF16),
                                     jnp.where(lane_k == spare(p, 2), shift,
                                               jnp.where(in_sub, jnp.where(off_k // SLC_BLOCK == lane_k - spare(p, 3),
                                                                           1.0, 0.0).astype(BF16), kj))))
            src = _iota((LANES, LANES), 0) - (p * seg + k0 // SLC_BLOCK)
            dst = _iota((LANES, LANES), 1) - spare(p, 3)
            to_lane = ((src == dst) & (dst >= 0) & (dst < nsub)).astype(BF16)
            bias = (_dot(skip, to_lane) * NEG).astype(BF16)
            qp = qs[p] + jnp.concatenate([bias] * N_CHUNKS, axis=0)
            raw += [_dot_nt(kp, qp[lanes(g)]) for g in range(n_grp)]
        if diagonal:
            future = jnp.where(_iota((tk, tq), 1) + (q0 - k0) >= _iota((tk, tq), 0), 0.0, NEG)
            future = jnp.concatenate([future] * (grp // tq), axis=1)
        es, alphas = [], []
        for p, g in chains:
            s = raw[p * n_grp + g] + future if diagonal else raw[p * n_grp + g]
            m_old = m_ref[p, :, lanes(g)]
            m_new = jnp.maximum(m_old, jnp.max(s, axis=0, keepdims=True))
            m_ref[p, :, lanes(g)] = m_new
            es.append(jnp.exp(s - m_new).astype(BF16))
            alphas.append(jnp.exp(m_old - m_new))
        row = _iota((LANES, tk), 0)
        vts = [jnp.where(row // HEAD_DIM == p, vt, jnp.ones_like(vt)) for p in halves]
        pvs = [_dot(vts[p], es[p * n_grp + g]) for p, g in chains]
        for (p, g), alpha, pv in zip(chains, alphas, pvs):
            acc_ref[p, :, lanes(g)] = alpha * acc_ref[p, :, lanes(g)] + pv

    def body(kt, carry):
        tile(kt, False)
        return carry

    n_past = q0 // tk
    lax.fori_loop(0, n_past, body, 0)
    tile(n_past, True)
    outs = []
    for p in halves:
        acc = acc_ref[p]
        den = acc[(1 - p) * HEAD_DIM:(1 - p) * HEAD_DIM + 1, :]
        outs.append((acc / den).T)
    _store_group(o_ref, outs, tq)


def _nsa_slc(q, k, v, sel, slopes, tq=256, tk=256):
    b, s, _ = q.shape
    assert tk % tq == 0 and tq % SLC_BLOCK == 0 and s % tk == 0 and tk <= 256
    assert all(_is_pow2(x) for x in slopes)
    rows = N_CHUNKS * tq
    spec_q = pl.BlockSpec((None, tq, MIX), lambda bi, n: (bi, n, 0))
    spec_kv = pl.BlockSpec((None, s, LANES), lambda bi, n: (bi, 0, 0))
    return pl.pallas_call(
        functools.partial(_nsa_slc_kernel, tq=tq, tk=tk, slopes=slopes),
        out_shape=jax.ShapeDtypeStruct((b, s, MIX), BF16),
        grid=(b, s // tq),
        in_specs=[spec_q, spec_kv, spec_kv, pl.BlockSpec((None, tq, LANES), lambda bi, n: (bi, n, 0))],
        out_specs=spec_q,
        scratch_shapes=[pltpu.VMEM((LANES, s), BF16), pltpu.VMEM((N_KV, 1, rows), F32),
                        pltpu.VMEM((N_KV, LANES, rows), F32)],
        compiler_params=_cparams("parallel", "arbitrary"),
        name="nsa_selected",
    )(q, k, v, sel)


def _odd_attn(x2, b, s, norm_g, w_in, c_q_norm, c_k_norm, d_q_norm, d_k_norm, cmp_pos, cmp_w, w_out):
    kvw = N_KV * HEAD_DIM
    n_gate = N_HEADS * 3
    qd0 = 3 * MIX
    w = jnp.concatenate([w_in[:, :qd0], _gqa_cols(w_in[:, qd0:qd0 + MIX]), w_in[:, qd0 + MIX:],
                         jnp.zeros((w_in.shape[0], LANES - n_gate), F32)], axis=1).astype(BF16)
    n = w.shape[1]
    ones = lambda width: jnp.ones((width,), F32)
    gain = jnp.concatenate([
        jnp.tile(c_q_norm, N_HEADS), jnp.tile(c_k_norm, N_HEADS), ones(MIX), jnp.tile(d_q_norm, N_HEADS),
        ones(2 * kvw), jnp.tile(d_k_norm[1], N_KV), ones(kvw), jnp.tile(d_k_norm[2], N_KV), ones(kvw),
        ones(LANES)]).reshape(1, n)
    plan, col = [], 0
    for width, op in ((MIX, "norm"), (MIX, "norm"), (MIX, None), (MIX, "norm"), (kvw, None), (kvw, None),
                      (kvw, "norm"), (kvw, None), (kvw, "norm"), (kvw, None), (LANES, "sigmoid")):
        plan.append((col, width, op))
        col += width
    outs = _proj(x2, norm_g, w, gain, tuple(plan), [BF16] * 10 + [F32])
    r3 = lambda a: a.reshape(b, s, a.shape[-1])
    qc, kc, vc, qd, kcmp, vcmp, kslc, vslc, kwin, vwin = [r3(a) for a in outs[:10]]
    gates = outs[10]
    slopes = _alibi(N_HEADS)
    gslopes = [slopes[h] for h in GQA_PERM]
    oc = _moba_attention(qc, kc, vc, slopes)
    k_cmp, v_cmp = _compress(kcmp, vcmp, cmp_pos, cmp_w, d_k_norm[0])
    o_cmp, sel = _nsa_cmp(qd, k_cmp, v_cmp, gslopes)
    o_slc = _nsa_slc(qd, kslc, vslc, sel, gslopes)
    o_win = _band_attention(qd, kwin, vwin, window=D_WINDOW, slopes=gslopes)
    t = b * s
    w1 = w_out[:MIX].astype(BF16)
    w2 = _gqa_rows(w_out[MIX:]).astype(BF16)
    flat = lambda a: a.reshape(t, a.shape[-1])
    return [flat(oc), flat(o_cmp), flat(o_slc), flat(o_win), gates], w1, w2


def kernel(x, ev_norm, ev_w_in, ev_q_norm, ev_k_norm, ev_sink, ev_w_out, od_norm, od_w_in, od_c_q_norm,
           od_c_k_norm, od_d_q_norm, od_d_k_norm, od_cmp_pos, od_cmp_w, od_w_out, moe_norm, moe_w_grp,
           moe_b_grp, moe_w_exp, moe_b_exp, moe_w_gate, moe_w_up, moe_w_down):
    b, s, d = x.shape
    x2 = x.reshape(b * s, d)
    depth = moe_norm.shape[0]
    moe_w_gate, moe_w_up, moe_w_down = (w.astype(BF16) for w in (moe_w_gate, moe_w_up, moe_w_down))
    for layer in range(depth):
        i = layer // 2
        if layer % 2 == 0:
            attn, w1, w2 = _even_attn(x2, b, s, ev_norm[i], ev_w_in[i], ev_q_norm[i], ev_k_norm[i],
                                      ev_sink[i], ev_w_out[i])
        else:
            attn, w1, w2 = _odd_attn(x2, b, s, od_norm[i], od_w_in[i], od_c_q_norm[i], od_c_k_norm[i],
                                     od_d_q_norm[i], od_d_k_norm[i], od_cmp_pos[i], od_cmp_w[i], od_w_out[i])
        x2 = _moe_block(x2, attn, w1, w2, moe_norm[layer], moe_w_grp[layer], moe_b_grp[layer],
                        moe_w_exp[layer], moe_b_exp[layer], moe_w_gate, moe_w_up, moe_w_down, layer)
    return x2.reshape(b, s, d)
```

```python
import functools
import math

import numpy as np
import jax
import jax.numpy as jnp
from jax import lax
from jax.experimental import pallas as pl
from jax.experimental.pallas import tpu as pltpu

F32 = jnp.float32
BF16 = jnp.bfloat16

LANES = 128
HEAD_DIM = 64
HEADS_PER_CHUNK = LANES // HEAD_DIM
D_MODEL = 1024
N_HEADS = 8
N_KV = 2
GROUP = N_HEADS // N_KV
N_CHUNKS = N_HEADS // HEADS_PER_CHUNK
MIX = N_HEADS * HEAD_DIM
SCALE = 1.0 / math.sqrt(HEAD_DIM)
EPS = 1e-6
NEG = -1e30
EXP_UNDERFLOW = -104.0

A_WINDOW = 128
C_BLOCK = 256
C_TOPK = 3
CMP_LEN = 32
CMP_STRIDE = 16
SLC_BLOCK = 64
SLC_TOPN = 16
D_WINDOW = 512
FORCE_BONUS = 1000.0

N_GROUPS = 4
EXPERTS_PER_GROUP = 8
N_EXPERTS = N_GROUPS * EXPERTS_PER_GROUP
EXPERT_FF = 256

VMEM_LIMIT = 48 * 1024 * 1024
MOE_VMEM_LIMIT = 56 * 1024 * 1024

GQA_PERM = tuple(h for c in range(N_CHUNKS) for h in (c, c + GROUP))
MHA_PERM = tuple(range(N_HEADS))


def _alibi(n_heads):
    return [float(2.0 ** (-8.0 * (i + 1) / n_heads)) for i in range(n_heads)]


def _is_pow2(x):
    return math.frexp(x)[0] == 0.5


def _gqa_cols(w):
    lead = w.shape[:-1]
    w = w.reshape(*lead, N_KV, GROUP, HEAD_DIM)
    return jnp.swapaxes(w, -3, -2).reshape(*lead, MIX)


def _gqa_rows(w):
    tail = w.shape[1:]
    return jnp.swapaxes(w.reshape(N_KV, GROUP, HEAD_DIM, *tail), 0, 1).reshape(MIX, *tail)


def _cparams(*sem, vmem=VMEM_LIMIT):
    return pltpu.CompilerParams(dimension_semantics=sem, vmem_limit_bytes=vmem)


def _dot(a, b):
    return jnp.dot(a, b, preferred_element_type=F32)


def _dot_nt(a, b):
    return lax.dot_general(a, b, (((1,), (1,)), ((), ())), preferred_element_type=F32)


def _split(x):
    hi = x.astype(BF16)
    lo = (x - hi.astype(F32)).astype(BF16)
    return hi, lo


def _dot_hilo(a, b):
    hi, lo = _split(a)
    return _dot(hi, b) + _dot(lo, b)


def _iota(shape, dim):
    return lax.broadcasted_iota(jnp.int32, shape, dim)


def _half_masks(dtype):
    lane = _iota((1, LANES), 1)
    return [(lane // HEAD_DIM == p).astype(dtype) for p in range(HEADS_PER_CHUNK)]


def _head_mean_sq(y):
    same = (_iota((LANES, LANES), 0) // HEAD_DIM == _iota((LANES, LANES), 1) // HEAD_DIM)
    return _dot((y * y).astype(BF16), same.astype(BF16)) * (1.0 / HEAD_DIM)


def _proj_kernel(x_ref, g_ref, w_ref, gain_ref, *out_refs, plan):
    x = x_ref[...]
    ms = jnp.mean(x * x, axis=-1, keepdims=True)
    xn = (x * lax.rsqrt(ms + EPS) * g_ref[...]).astype(BF16)
    for o_ref, (col0, width, op) in zip(out_refs, plan):
        for a in range(0, width, 2 * LANES):
            wd = min(2 * LANES, width - a)
            y = _dot(xn, w_ref[:, col0 + a:col0 + a + wd])
            for c in range(0, wd, LANES):
                yc = y[:, c:c + LANES]
                if op == "norm":
                    gain = gain_ref[:, col0 + a + c:col0 + a + c + LANES]
                    yc = yc * lax.rsqrt(_head_mean_sq(yc) + EPS) * gain
                elif op == "sigmoid":
                    yc = jax.nn.sigmoid(yc)
                o_ref[:, a + c:a + c + LANES] = yc.astype(o_ref.dtype)


def _proj(x2, norm_g, w, gain, plan, out_dtypes, tm=1024):
    t, d = x2.shape
    n = w.shape[1]
    out_shape = [jax.ShapeDtypeStruct((t, width), dt) for (_, width, _), dt in zip(plan, out_dtypes)]
    return pl.pallas_call(
        functools.partial(_proj_kernel, plan=plan),
        out_shape=out_shape,
        grid=(t // tm,),
        in_specs=[
            pl.BlockSpec((tm, d), lambda i: (i, 0)),
            pl.BlockSpec((1, d), lambda i: (0, 0)),
            pl.BlockSpec((d, n), lambda i: (0, 0)),
            pl.BlockSpec((1, n), lambda i: (0, 0)),
        ],
        out_specs=[pl.BlockSpec((tm, width), lambda i: (i, 0)) for (_, width, _) in plan],
        compiler_params=_cparams("parallel"),
        name="norm_proj",
    )(x2, norm_g.reshape(1, d), w, gain)


def _stack_group(q_ref, halfmask):
    return jnp.concatenate(
        [q_ref[:, c * LANES:(c + 1) * LANES] * halfmask for c in range(N_CHUNKS)], axis=0)


def _per_chunk_rows(rows, tq, values):
    r = _iota((rows, 1), 0) // tq
    out = jnp.full((rows, 1), values[N_CHUNKS - 1], F32)
    for c in range(N_CHUNKS - 2, -1, -1):
        out = jnp.where(r == c, values[c], out)
    return out


def _per_chunk_lanes(rows, tq, values):
    r = _iota((1, rows), 1) // tq
    out = jnp.full((1, rows), values[N_CHUNKS - 1], F32)
    for c in range(N_CHUNKS - 2, -1, -1):
        out = jnp.where(r == c, values[c], out)
    return out


def _band_kernel(*refs, tq, window, wpad, slopes, has_sink):
    if has_sink:
        sink_ref, q_ref, k_ref, v_ref, o_ref, vt_ref = refs
    else:
        q_ref, k_ref, v_ref, o_ref, vt_ref = refs
    n = pl.program_id(1)

    @pl.when(n == 0)
    def _():
        vt_ref[...] = v_ref[...].astype(F32).T.astype(BF16)

    q0 = n * tq
    kw = tq + wpad
    kstart = pl.multiple_of(jnp.maximum(q0 - wpad, 0), LANES)
    ks = k_ref[pl.ds(kstart, kw), :]
    vt = vt_ref[:, pl.ds(kstart, kw)]
    rows = N_CHUNKS * tq
    halves = range(HEADS_PER_CHUNK)
    hms = _half_masks(BF16)
    spare = lambda p, j: (1 - p) * HEAD_DIM + j
    lane_q = _iota((tq, LANES), 1)
    row_q = _iota((tq, LANES), 0).astype(F32)
    lane_k = _iota((kw, LANES), 1)
    off_k = _iota((kw, LANES), 0)
    fine = (off_k % LANES).astype(F32).astype(BF16)
    coarse = ((off_k // LANES) * LANES + (kstart - q0)).astype(F32).astype(BF16)
    diff = _iota((kw, tq), 1) + (q0 - kstart) - _iota((kw, tq), 0)
    band = jnp.where((diff >= 0) & (diff < window), 0.0, NEG)
    band = jnp.concatenate([band] * N_CHUNKS, axis=1)
    raw = []
    for p in halves:
        extra = jnp.concatenate(
            [jnp.where(lane_q == spare(p, 0), -slopes[2 * c + p] * row_q,
                       jnp.where((lane_q == spare(p, 1)) | (lane_q == spare(p, 2)), slopes[2 * c + p], 0.0))
             for c in range(N_CHUNKS)], axis=0)
        qp = _stack_group(q_ref, hms[p] * SCALE) + extra.astype(BF16)
        kp = jnp.where(lane_k == spare(p, 0), 1.0,
                       jnp.where(lane_k == spare(p, 1), fine, jnp.where(lane_k == spare(p, 2), coarse, ks)))
        raw.append(_dot_nt(kp, qp))
    es, sinks = [], []
    for p in halves:
        s = raw[p] + band
        mx = jnp.max(s, axis=0, keepdims=True)
        if has_sink:
            sk = _per_chunk_lanes(rows, tq, [sink_ref[2 * c + p] for c in range(N_CHUNKS)])
            mx = jnp.maximum(mx, sk)
            sinks.append(jnp.exp(sk - mx))
        es.append(jnp.exp(s - mx).astype(BF16))
    vrow = _iota((LANES, kw), 0)
    outs = []
    for p in halves:
        acc = _dot(jnp.where(vrow // HEAD_DIM == p, vt, jnp.ones_like(vt)), es[p])
        den = acc[(1 - p) * HEAD_DIM:(1 - p) * HEAD_DIM + 1, :]
        if has_sink:
            den = den + sinks[p]
        outs.append((acc / den).T)
    _store_group(o_ref, outs, tq)


def _store_group(o_ref, os_, tq):
    lane = _iota((1, LANES), 1)
    for c in range(N_CHUNKS):
        o = jnp.where(lane < HEAD_DIM, os_[0][c * tq:(c + 1) * tq], os_[1][c * tq:(c + 1) * tq])
        o_ref[:, c * LANES:(c + 1) * LANES] = o.astype(o_ref.dtype)


def _band_attention(q, k, v, *, window, slopes, sink=None, tq=256):
    b, s, _ = q.shape
    wpad = -(-window // LANES) * LANES
    assert s >= tq + wpad and s % tq == 0 and tq <= 256 and all(_is_pow2(x) for x in slopes)
    kern = functools.partial(_band_kernel, tq=tq, window=window, wpad=wpad,
                             slopes=slopes, has_sink=sink is not None)
    in_specs = [
        pl.BlockSpec((None, tq, MIX), lambda bi, n: (bi, n, 0)),
        pl.BlockSpec((None, s, LANES), lambda bi, n: (bi, 0, 0)),
        pl.BlockSpec((None, s, LANES), lambda bi, n: (bi, 0, 0)),
    ]
    args = [q, k, v]
    if sink is not None:
        in_specs = [pl.BlockSpec(memory_space=pltpu.SMEM)] + in_specs
        args = [sink] + args
    return pl.pallas_call(
        kern,
        out_shape=jax.ShapeDtypeStruct((b, s, MIX), BF16),
        grid=(b, s // tq),
        in_specs=in_specs,
        out_specs=pl.BlockSpec((None, tq, MIX), lambda bi, n: (bi, n, 0)),
        scratch_shapes=[pltpu.VMEM((LANES, s), BF16)],
        compiler_params=_cparams("parallel", "arbitrary"),
        name="band_attention",
    )(*args)


def _stick_kernel(q_ref, k_ref, v_ref, o_ref, acc_ref, run_ref, *, tq, cpb):
    i = pl.program_id(2)
    q0 = pl.multiple_of(i * tq, tq)
    hms = _half_masks(BF16)
    heads = [(cc, p) for cc in range(cpb) for p in range(HEADS_PER_CHUNK)]
    qs = [q_ref[:, cc * LANES:(cc + 1) * LANES] * (hms[p] * SCALE) for cc, p in heads]
    upper = (_iota((2 * tq, tq), 0) % tq > _iota((2 * tq, tq), 1)).astype(BF16)

    def suffix_sum(x):
        hi, lo = _split(x)
        return _dot(jnp.concatenate([hi, lo], axis=1), upper)

    def block(kstart, diag):
        if diag:
            causal = _iota((tq, tq), 1) < _iota((tq, tq), 0)
        kjs = [k_ref[pl.ds(kstart, tq), cc * LANES:(cc + 1) * LANES] for cc in range(cpb)]
        vjs = [v_ref[pl.ds(kstart, tq), cc * LANES:(cc + 1) * LANES] for cc in range(cpb)]
        zs = [_dot_nt(qs[h], kjs[cc]) for h, (cc, p) in enumerate(heads)]
        lss, lks = [], []
        for z in zs:
            ls = jnp.minimum(z, 0.0) - jnp.log(1.0 + jnp.exp(-jnp.abs(z)))
            lk = ls - z
            if diag:
                lk = jnp.where(causal, lk, 0.0)
            lss.append(ls)
            lks.append(lk)
        sufs = [suffix_sum(lk) for lk in lks]
        ws = []
        for h in range(len(heads)):
            if diag:
                a = jnp.where(causal, jnp.exp(lss[h] + sufs[h]), 0.0)
            else:
                a = jnp.exp(lss[h] + sufs[h] + run_ref[h])
            ws.append(a.astype(BF16))
        pvs = [_dot(ws[h], vjs[cc]) for h, (cc, p) in enumerate(heads)]
        for h in range(len(heads)):
            rowsum = jnp.sum(lks[h], axis=-1, keepdims=True)
            if diag:
                acc_ref[h] = pvs[h]
                run_ref[h] = rowsum
            else:
                acc_ref[h] += pvs[h]
                run_ref[h] += rowsum

    block(q0, True)

    def weights_alive():
        run = run_ref[0]
        for h in range(1, len(heads)):
            run = jnp.maximum(run, run_ref[h])
        return jnp.max(run) > EXP_UNDERFLOW

    def body(carry):
        t, _ = carry
        block(pl.multiple_of((i - 1 - t) * tq, tq), False)
        return t + 1, weights_alive()

    lax.while_loop(lambda c: (c[0] < i) & c[1], body, (jnp.int32(0), weights_alive()))
    lane = _iota((1, LANES), 1)
    for cc in range(cpb):
        o = jnp.where(lane < HEAD_DIM, acc_ref[HEADS_PER_CHUNK * cc], acc_ref[HEADS_PER_CHUNK * cc + 1])
        o_ref[:, cc * LANES:(cc + 1) * LANES] = o.astype(o_ref.dtype)


def _stick_attention(q, k, v, tq=256, cpb=4):
    b, s, _ = q.shape
    wide = cpb * LANES
    n_heads = cpb * HEADS_PER_CHUNK
    spec_q = pl.BlockSpec((None, tq, wide), lambda bi, c, i: (bi, i, c))
    spec_kv = pl.BlockSpec((None, s, wide), lambda bi, c, i: (bi, 0, c))
    return pl.pallas_call(
        functools.partial(_stick_kernel, tq=tq, cpb=cpb),
        out_shape=jax.ShapeDtypeStruct((b, s, MIX), BF16),
        grid=(b, N_CHUNKS // cpb, s // tq),
        in_specs=[spec_q, spec_kv, spec_kv],
        out_specs=spec_q,
        scratch_shapes=[pltpu.VMEM((n_heads, tq, LANES), F32), pltpu.VMEM((n_heads, tq, 1), F32)],
        compiler_params=_cparams("parallel", "parallel", "parallel"),
        name="stick_breaking",
    )(q, k, v)


def _route(logits):
    lane = _iota(logits.shape, 1)
    lane_f = lane.astype(F32)
    ninf = -jnp.inf
    is_g = (lane >= N_EXPERTS) & (lane < N_EXPERTS + N_GROUPS)
    gmax = jnp.max(jnp.where(is_g, logits, ninf), axis=-1, keepdims=True)
    gidx = jnp.min(jnp.where(is_g & (logits == gmax), lane_f - N_EXPERTS, 1e9), axis=-1, keepdims=True)
    p_g = 1.0 / jnp.sum(jnp.where(is_g, jnp.exp(logits - gmax), 0.0), axis=-1, keepdims=True)
    in_grp = (lane < N_EXPERTS) & ((lane // EXPERTS_PER_GROUP).astype(F32) == gidx)
    le = jnp.where(in_grp, logits, ninf)
    m1 = jnp.max(le, axis=-1, keepdims=True)
    i1 = jnp.min(jnp.where(le == m1, lane_f, 1e9), axis=-1, keepdims=True)
    le2 = jnp.where(lane_f == i1, ninf, le)
    m2 = jnp.max(le2, axis=-1, keepdims=True)
    i2 = jnp.min(jnp.where(le2 == m2, lane_f, 1e9), axis=-1, keepdims=True)
    e2 = jnp.exp(m2 - m1)
    w1 = p_g / (1.0 + e2)
    w2 = p_g * e2 / (1.0 + e2)
    return jnp.where(lane_f == i1, w1, 0.0) + jnp.where(lane_f == i2, w2, 0.0)


def _gate_expand(branch):
    r = _iota((LANES, MIX), 0)
    col = _iota((LANES, MIX), 1)
    head = col // LANES + GROUP * ((col % LANES) // HEAD_DIM)
    return (r == 3 * head + branch).astype(BF16)


def _out_kernel(*refs, nsa):
    if nsa:
        (x_ref, o1_ref, ocmp_ref, oslc_ref, owin_ref, gates_ref, w1_ref, w2_ref,
         ng_ref, wr_ref, br_ref, x1_ref, h_ref, comb_ref) = refs
        g = gates_ref[...]
        o2 = (_dot_hilo(g, _gate_expand(0)) * ocmp_ref[...]
              + _dot_hilo(g, _gate_expand(1)) * oslc_ref[...]
              + _dot_hilo(g, _gate_expand(2)) * owin_ref[...]).astype(BF16)
    else:
        (x_ref, o1_ref, o2_ref, w1_ref, w2_ref,
         ng_ref, wr_ref, br_ref, x1_ref, h_ref, comb_ref) = refs
        o2 = o2_ref[...]
    x1 = x_ref[...] + _dot(o1_ref[...], w1_ref[...]) + _dot(o2, w2_ref[...])
    x1_ref[...] = x1
    ms = jnp.mean(x1 * x1, axis=-1, keepdims=True)
    h = x1 * lax.rsqrt(ms + EPS) * ng_ref[...]
    h_ref[...] = h.astype(BF16)
    h_hi, h_lo = _split(h)
    w_hi, w_lo = _split(wr_ref[...])
    logits = _dot(h_hi, w_hi) + (_dot(h_hi, w_lo) + _dot(h_lo, w_hi)) + br_ref[...]
    comb_ref[...] = _route(logits)


def _out_proj_route(x2, attn, w1, w2, moe_g, w_route, b_route, tm=1024):
    t, d = x2.shape
    nsa = len(attn) > 2
    row = lambda width: pl.BlockSpec((tm, width), lambda i: (i, 0))
    full = lambda a: pl.BlockSpec(a.shape, lambda i: (0, 0))
    consts = [w1, w2, moe_g.reshape(1, d), w_route, b_route]
    return pl.pallas_call(
        functools.partial(_out_kernel, nsa=nsa),
        out_shape=[jax.ShapeDtypeStruct((t, d), F32), jax.ShapeDtypeStruct((t, d), BF16),
                   jax.ShapeDtypeStruct((t, LANES), F32)],
        grid=(t // tm,),
        in_specs=[row(d)] + [row(a.shape[1]) for a in attn] + [full(a) for a in consts],
        out_specs=[row(d), row(d), row(LANES)],
        compiler_params=_cparams("parallel"),
        name="out_proj_route",
    )(x2, *attn, *consts)


MOE_CHUNK = 256
MOE_FIRST_CHUNK = MOE_CHUNK + 32
SEG_ALIGN = 16


def _sorted_rows(tm):
    return -(-(tm + N_GROUPS * SEG_ALIGN) // LANES) * LANES


def _moe_sort(h_ref, comb_ref, hs_ref, cs_ref, pos_ref, acc_ref, seg_ref):
    tm = h_ref.shape[0]
    rows_s = hs_ref.shape[0]
    comb = comb_ref[...]
    used = jnp.where(comb > 0, 1.0, 0.0).astype(BF16)
    of_group = ((_iota((LANES, LANES), 0) // EXPERTS_PER_GROUP == _iota((LANES, LANES), 1))
                & (_iota((LANES, LANES), 0) < N_EXPERTS)).astype(BF16)
    member = jnp.where(_dot(used, of_group) > 0.5, 1.0, 0.0)
    member_t = member.T
    earlier = (_iota((tm, tm), 0) < _iota((tm, tm), 1)).astype(BF16)
    rank_t = _dot(member_t.astype(BF16), earlier)
    count = jnp.sum(member_t, axis=1, keepdims=True)
    counts = [count[g:g + 1, :] for g in range(N_GROUPS)]
    starts = [jnp.zeros((1, 1), F32)]
    for g in range(1, N_GROUPS):
        padded = jnp.floor((counts[g - 1] + (SEG_ALIGN - 1)) * (1.0 / SEG_ALIGN)) * SEG_ALIGN
        starts.append(starts[g - 1] + padded)
    grow = _iota((LANES, 1), 0)
    start_col = jnp.zeros((LANES, 1), F32)
    for g in range(1, N_GROUPS):
        start_col = jnp.where(grow == g, starts[g], start_col)
    pos_t = jnp.sum(member_t * (start_col + rank_t), axis=0, keepdims=True)
    used_rows = _sorted_rows(tm)
    place = jnp.where(_iota((used_rows, tm), 0).astype(F32) == pos_t, 1.0, 0.0).astype(BF16)
    hs_ref[:used_rows] = _dot(place, h_ref[...]).astype(BF16)
    hs_ref[used_rows:] = jnp.zeros((rows_s - used_rows, hs_ref.shape[1]), BF16)
    c_hi = comb.astype(BF16)
    rest = comb - c_hi.astype(F32)
    c_mid = rest.astype(BF16)
    c_lo = (rest - c_mid.astype(F32)).astype(BF16)
    cs_ref[:used_rows] = _dot(place, c_hi) + (_dot(place, c_mid) + _dot(place, c_lo))
    cs_ref[used_rows:] = jnp.zeros((rows_s - used_rows, LANES), F32)
    pos_ref[...] = jnp.broadcast_to(pos_t, (LANES, tm)).T
    acc_ref[...] = jnp.zeros(acc_ref.shape, F32)
    for g in range(N_GROUPS):
        seg_ref[g] = starts[g][0, 0].astype(jnp.int32)
        seg_ref[N_GROUPS + g] = counts[g][0, 0].astype(jnp.int32)


def _moe_kernel(h_ref, comb_ref, x1_ref, wg_ref, wu_ref, wd_ref, o_ref,
                hs_ref, cs_ref, pos_ref, acc_ref, seg_ref, *, per_step):
    step = pl.program_id(1)
    tm = h_ref.shape[0]
    rows_s = hs_ref.shape[0]

    @pl.when(step == 0)
    def _():
        _moe_sort(h_ref, comb_ref, hs_ref, cs_ref, pos_ref, acc_ref, seg_ref)

    first = step * per_step
    group = first // EXPERTS_PER_GROUP
    start = seg_ref[group]
    count = seg_ref[N_GROUPS + group]
    w_gate = [wg_ref[j] for j in range(per_step)]
    w_up = [wu_ref[j] for j in range(per_step)]
    w_down = jnp.concatenate([wd_ref[j] for j in range(per_step)], axis=0)
    sizes = [MOE_FIRST_CHUNK, MOE_CHUNK // 2, MOE_CHUNK // 2] + [MOE_CHUNK] * (tm // MOE_CHUNK - 2)
    assert sum(sizes) >= tm and max(sizes) <= MOE_FIRST_CHUNK
    begin = 0
    for size in sizes:
        @pl.when(begin < count)
        def _(begin=begin, size=size):
            off = pl.multiple_of(start + begin, SEG_ALIGN)
            hs = hs_ref[pl.ds(off, size), :]
            cs = cs_ref[pl.ds(off, size), :]
            lane = _iota((size, LANES), 1)
            acts = []
            for j in range(per_step):
                g = _dot(hs, w_gate[j])
                u = _dot(hs, w_up[j])
                c = jnp.sum(jnp.where(lane == first + j, cs, 0.0), axis=-1, keepdims=True)
                acts.append((c * (g * jax.nn.sigmoid(g) * u)).astype(BF16))
            acc_ref[pl.ds(off, size), :] += _dot(jnp.concatenate(acts, axis=1), w_down)
        begin += size

    @pl.when(step == pl.num_programs(1) - 1)
    def _():
        pos = pos_ref[...]
        used_rows = _sorted_rows(tm)
        back = jnp.concatenate(
            [jnp.where(pos == (_iota((tm, LANES), 1) + blk * LANES).astype(F32), 1.0, 0.0).astype(BF16)
             for blk in range(used_rows // LANES)], axis=1)
        o_ref[...] = x1_ref[...] + _dot(back, acc_ref[:used_rows].astype(BF16))


def _moe(h, comb, x1, w_gate, w_up, w_down, layer, tm=1024, per_step=4):
    t, d = h.shape
    _, n_exp, _, ff = w_gate.shape
    assert t % tm == 0 and EXPERTS_PER_GROUP % per_step == 0 and tm % MOE_CHUNK == 0
    rows_s = _sorted_rows(tm) + -(-MOE_FIRST_CHUNK // LANES) * LANES
    return pl.pallas_call(
        functools.partial(_moe_kernel, per_step=per_step),
        out_shape=jax.ShapeDtypeStruct((t, d), F32),
        grid=(t // tm, n_exp // per_step),
        scratch_shapes=[pltpu.VMEM((rows_s, d), BF16), pltpu.VMEM((rows_s, LANES), F32),
                        pltpu.VMEM((tm, LANES), F32), pltpu.VMEM((rows_s, d), F32),
                        pltpu.SMEM((2 * N_GROUPS,), jnp.int32)],
        in_specs=[
            pl.BlockSpec((tm, d), lambda i, e: (i, 0)),
            pl.BlockSpec((tm, LANES), lambda i, e: (i, 0)),
            pl.BlockSpec((tm, d), lambda i, e: (i, 0)),
            pl.BlockSpec((None, per_step, d, ff), lambda i, e: (layer, e, 0, 0)),
            pl.BlockSpec((None, per_step, d, ff), lambda i, e: (layer, e, 0, 0)),
            pl.BlockSpec((None, per_step, ff, d), lambda i, e: (layer, e, 0, 0)),
        ],
        out_specs=pl.BlockSpec((tm, d), lambda i, e: (i, 0)),
        compiler_params=_cparams("parallel", "arbitrary", vmem=MOE_VMEM_LIMIT),
        name="moe_experts",
    )(h, comb, x1, w_gate, w_up, w_down)


def _moe_block(x2, attn, w1, w2, moe_g, w_grp, b_grp, w_exp, b_exp, w_gate, w_up, w_down, layer):
    d = x2.shape[1]
    pad = LANES - N_EXPERTS - N_GROUPS
    w_route = jnp.concatenate([w_exp, w_grp, jnp.zeros((d, pad), F32)], axis=1)
    b_route = jnp.concatenate([b_exp, b_grp, jnp.zeros((pad,), F32)]).reshape(1, LANES)
    x1, h, comb = _out_proj_route(x2, attn, w1, w2, moe_g, w_route, b_route)
    return _moe(h, comb, x1, w_gate, w_up, w_down, layer)


def _even_attn(x2, b, s, norm_g, w_in, q_norm, k_norm, sink, w_out):
    w = jnp.concatenate([_gqa_cols(w_in[:, :MIX]), w_in[:, MIX:]], axis=1).astype(BF16)
    n = w.shape[1]
    kvw = N_KV * HEAD_DIM
    gain = jnp.concatenate([jnp.tile(q_norm, N_HEADS), jnp.tile(k_norm, N_KV),
                            jnp.ones((n - MIX - kvw,), F32)]).reshape(1, n)
    plan, col = [], 0
    for width, op in ((MIX, "norm"), (kvw, "norm"), (kvw, None), (MIX, None), (MIX, None), (MIX, None)):
        plan.append((col, width, op))
        col += width
    qa, ka, va, qb, kb, vb = _proj(x2, norm_g, w, gain, tuple(plan), [BF16] * 6)
    r3 = lambda a: a.reshape(b, s, a.shape[-1])
    slopes = _alibi(N_HEADS)
    oa = _band_attention(r3(qa), r3(ka), r3(va), window=A_WINDOW,
                         slopes=[slopes[h] for h in GQA_PERM], sink=sink[np.asarray(GQA_PERM)])
    ob = _stick_attention(r3(qb), r3(kb), r3(vb))
    w1 = _gqa_rows(w_out[:MIX]).astype(BF16)
    w2 = w_out[MIX:].astype(BF16)
    t = b * s
    return [oa.reshape(t, MIX), ob.reshape(t, MIX)], w1, w2


def _rank_rows(score, j, n, seg):
    row = _iota(score.shape, 0)
    rank = jnp.zeros(score.shape, F32)
    for jj in range(n):
        other = score[jj:jj + 1, :]
        for sgm in range(1, score.shape[0] // seg):
            other = jnp.where(row // seg == sgm, score[sgm * seg + jj:sgm * seg + jj + 1, :], other)
        beats = (other > score) | ((other == score) & (jj < j))
        rank = rank + jnp.where(beats, 1.0, 0.0)
    return rank


def _ones_beside(v, p):
    lane = _iota((1, LANES), 1)
    return jnp.where(lane // HEAD_DIM == p, v, jnp.ones_like(v))


def _normalize(acc):
    return acc / pltpu.roll(acc, HEAD_DIM, 1)


def _moba_kernel(slope_ref, q_ref, k_ref, v_ref, o_ref, kmean_ref, vt_ref, m_ref, acc_ref, *,
                 blk, nblk, topk, cpb):
    g = pl.program_id(1)
    i = pl.program_id(2)
    s_len = k_ref.shape[0]
    heads = [(cc, p) for cc in range(cpb) for p in range(HEADS_PER_CHUNK)]
    nh = len(heads)
    chunk = lambda cc: slice(cc * LANES, (cc + 1) * LANES)

    @pl.when(i == 0)
    def _():
        member = (_iota((LANES, s_len), 1) // blk == _iota((LANES, s_len), 0)).astype(BF16)
        for cc in range(cpb):
            kmean_ref[cc] = _dot(member, k_ref[:, chunk(cc)]) * (1.0 / blk)
            vt_ref[cc] = v_ref[:, chunk(cc)].astype(F32).T.astype(BF16)

    hms = _half_masks(BF16)
    q0 = pl.multiple_of(i * blk, blk)
    rel = (_iota((blk, blk), 0) - _iota((blk, blk), 1)).astype(F32)
    slopes = [slope_ref[HEADS_PER_CHUNK * (g * cpb + cc) + p] for cc, p in heads]
    kms = [_split(kmean_ref[cc]) for cc in range(cpb)]
    gates = [_dot_nt(kms[cc][0], q_ref[:, chunk(cc)] * hms[p]) + _dot_nt(kms[cc][1], q_ref[:, chunk(cc)] * hms[p])
             for cc, p in heads]
    lane = _iota((blk, LANES), 1)
    off_f = _iota((blk, LANES), 0).astype(F32)
    off_b = off_f.astype(BF16)
    spare = lambda p, n: (1 - p) * HEAD_DIM + n

    def with_key_lanes(kj, p, shift):
        return jnp.where(lane == spare(p, 0), 1.0,
                         jnp.where(lane == spare(p, 1), off_b,
                                   jnp.where(lane == spare(p, 2), shift.astype(BF16), kj)))

    qs = []
    for h, (cc, p) in enumerate(heads):
        extra = jnp.where(lane == spare(p, 0), -slopes[h] * off_f,
                          jnp.where((lane == spare(p, 1)) | (lane == spare(p, 2)), slopes[h], 0.0))
        qs.append(q_ref[:, chunk(cc)] * (hms[p] * SCALE) + extra.astype(BF16))
    zero = jnp.zeros((), F32)
    raw = [_dot_nt(with_key_lanes(k_ref[pl.ds(q0, blk), chunk(cc)], p, zero), qs[h])
           for h, (cc, p) in enumerate(heads)]
    nrow = -(-nblk // 8) * 8
    blk_id = _iota((nrow, blk), 0)
    skips = []
    for h in range(nh):
        gate = jnp.where(blk_id < i, gates[h][:nrow], NEG)
        rank = _rank_rows(gate, blk_id, nblk, nrow)
        skip_t = jnp.where((rank < topk) & (blk_id < i), 0.0, 1.0)
        skip_t = jnp.concatenate([skip_t, jnp.ones((LANES - nrow, blk), F32)], axis=0)
        skips.append(skip_t.T.astype(BF16))
    own_bias = jnp.where(rel <= 0, 0.0, NEG)
    vrow = _iota((LANES, blk), 0)

    def values_t(k0, cc, p):
        vt = vt_ref[cc, :, pl.ds(k0, blk)]
        return jnp.where(vrow // HEAD_DIM == p, vt, jnp.ones_like(vt))

    es = []
    for h in range(nh):
        s = raw[h] + own_bias
        m = jnp.max(s, axis=0, keepdims=True)
        m_ref[h] = m
        es.append(jnp.exp(s - m).astype(BF16))
    for h, (cc, p) in enumerate(heads):
        acc_ref[h] = _dot(values_t(q0, cc, p), es[h])

    def body(j, carry):
        k0 = pl.multiple_of(j * blk, blk)
        shift = ((j - i) * blk).astype(F32)
        to_lane = [((_iota((LANES, LANES), 0) == j) & (_iota((LANES, LANES), 1) == spare(p, 0))).astype(BF16)
                   for p in range(HEADS_PER_CHUNK)]
        skipped = [_dot(skips[h], to_lane[p]) for h, (cc, p) in enumerate(heads)]
        raw = [_dot_nt(with_key_lanes(k_ref[pl.ds(k0, blk), chunk(cc)], p, shift),
                       qs[h] + (skipped[h] * NEG).astype(BF16))
               for h, (cc, p) in enumerate(heads)]
        es, alphas = [], []
        for h in range(nh):
            s = raw[h]
            m_old = m_ref[h]
            m_new = jnp.maximum(m_old, jnp.max(s, axis=0, keepdims=True))
            m_ref[h] = m_new
            es.append(jnp.exp(s - m_new).astype(BF16))
            alphas.append(jnp.exp(m_old - m_new))
        pvs = [_dot(values_t(k0, cc, p), es[h]) for h, (cc, p) in enumerate(heads)]
        for h in range(nh):
            acc_ref[h] = alphas[h] * acc_ref[h] + pvs[h]
        return carry

    lax.fori_loop(0, i, body, 0)
    lane = _iota((1, LANES), 1)
    for cc in range(cpb):
        outs = []
        for p in range(HEADS_PER_CHUNK):
            acc = acc_ref[HEADS_PER_CHUNK * cc + p]
            den = acc[(1 - p) * HEAD_DIM:(1 - p) * HEAD_DIM + 1, :]
            outs.append((acc / den).T)
        o_ref[:, chunk(cc)] = jnp.where(lane < HEAD_DIM, outs[0], outs[1]).astype(o_ref.dtype)


def _moba_attention(q, k, v, slopes, cpb=4):
    b, s, _ = q.shape
    assert s % C_BLOCK == 0 and C_BLOCK <= 256 and all(_is_pow2(x) for x in slopes)
    nblk = s // C_BLOCK
    assert nblk <= LANES
    wide = cpb * LANES
    nh = cpb * HEADS_PER_CHUNK
    spec_q = pl.BlockSpec((None, C_BLOCK, wide), lambda bi, c, i, sl: (bi, i, c))
    spec_kv = pl.BlockSpec((None, s, wide), lambda bi, c, i, sl: (bi, 0, c))
    return pl.pallas_call(
        functools.partial(_moba_kernel, blk=C_BLOCK, nblk=nblk, topk=min(C_TOPK, nblk), cpb=cpb),
        out_shape=jax.ShapeDtypeStruct((b, s, MIX), BF16),
        grid_spec=pltpu.PrefetchScalarGridSpec(
            num_scalar_prefetch=1,
            grid=(b, N_CHUNKS // cpb, nblk),
            in_specs=[spec_q, spec_kv, spec_kv],
            out_specs=spec_q,
            scratch_shapes=[pltpu.VMEM((cpb, LANES, LANES), F32), pltpu.VMEM((cpb, LANES, s), BF16),
                            pltpu.VMEM((nh, 1, C_BLOCK), F32), pltpu.VMEM((nh, LANES, C_BLOCK), F32)],
        ),
        compiler_params=_cparams("parallel", "parallel", "arbitrary"),
        name="moba_attention",
    )(jnp.asarray(slopes, F32), q, k, v)


def _compress_kernel(xk_ref, xv_ref, wk_lo_ref, wk_hi_ref, wv_lo_ref, wv_hi_ref,
                     pk_lo_ref, pk_hi_ref, pv_lo_ref, pv_hi_ref, gain_ref, kc_ref, vc_ref):
    def compress(x_ref, w_lo_ref, w_hi_ref, p_lo_ref, p_hi_ref):
        x = x_ref[...]
        nrow = x.shape[0]
        first = _dot(x, w_lo_ref[...])
        second = pltpu.roll(_dot(x, w_hi_ref[...]), nrow - 1, 0)
        p_lo = jnp.broadcast_to(p_lo_ref[...], (8, p_lo_ref.shape[1]))
        p_hi = jnp.broadcast_to(p_hi_ref[...], (8, p_hi_ref.shape[1]))
        bias = _dot_hilo(p_lo, w_lo_ref[...]) + _dot_hilo(p_hi, w_hi_ref[...])
        return first + second + bias[0:1]

    kc = compress(xk_ref, wk_lo_ref, wk_hi_ref, pk_lo_ref, pk_hi_ref)
    kc = kc * lax.rsqrt(_head_mean_sq(kc) + EPS) * gain_ref[...]
    kc_ref[...] = kc.astype(kc_ref.dtype)
    vc_ref[...] = compress(xv_ref, wv_lo_ref, wv_hi_ref, pv_lo_ref, pv_hi_ref).astype(vc_ref.dtype)


def _compress_weights(pos, w):
    half = CMP_LEN // 2
    eye = jnp.eye(N_KV, dtype=F32)
    wd = jnp.einsum("gh,lde->lgdhe", eye, w).reshape(CMP_LEN, LANES, LANES)
    w_lo = wd[:half].reshape(half * LANES, LANES).astype(BF16)
    w_hi = wd[half:].reshape(half * LANES, LANES).astype(BF16)
    pt = jnp.tile(pos, (1, N_KV))
    return w_lo, w_hi, pt[:half].reshape(1, half * LANES), pt[half:].reshape(1, half * LANES)


def _compress(kcmp, vcmp, cmp_pos, cmp_w, k_gain):
    b, s, _ = kcmp.shape
    assert CMP_LEN == 2 * CMP_STRIDE and s % CMP_STRIDE == 0
    nrow = s // CMP_STRIDE
    wide = CMP_STRIDE * LANES
    xk = kcmp.reshape(b, nrow, wide)
    xv = vcmp.reshape(b, nrow, wide)
    wk = _compress_weights(cmp_pos[0], cmp_w[0])
    wv = _compress_weights(cmp_pos[1], cmp_w[1])
    consts = [wk[0], wk[1], wv[0], wv[1], wk[2], wk[3], wv[2], wv[3],
              jnp.tile(k_gain, N_KV).reshape(1, LANES)]
    spec_x = pl.BlockSpec((None, nrow, wide), lambda bi: (bi, 0, 0))
    spec_o = pl.BlockSpec((None, nrow, LANES), lambda bi: (bi, 0, 0))
    return pl.pallas_call(
        _compress_kernel,
        out_shape=[jax.ShapeDtypeStruct((b, nrow, LANES), BF16)] * 2,
        grid=(b,),
        in_specs=[spec_x, spec_x] + [pl.BlockSpec(a.shape, lambda bi: (0, 0)) for a in consts],
        out_specs=[spec_o, spec_o],
        compiler_params=_cparams("parallel"),
        name="nsa_compress",
    )(xk, xv, *consts)


def _nsa_cmp_kernel(q_ref, kc_ref, vc_ref, o_ref, sel_ref, *, tq, n_cmp, n_slc, topn, slopes):
    q0 = pl.program_id(1) * tq
    kc = kc_ref[...]
    vc = vc_ref[...]
    ncp = kc.shape[0]
    rows = N_CHUNKS * tq
    seg = LANES // N_KV
    qpos = q0 + _iota((rows, ncp), 0) % tq
    ncol = _iota((rows, ncp), 1)
    diff = qpos - (ncol * CMP_STRIDE + CMP_LEN - 1)
    mask = (diff >= 0) & (ncol < n_cmp)
    diff_f = diff.astype(F32)
    orow = _iota((LANES, ncp), 0)
    cst = _iota((LANES, ncp), 1) * CMP_STRIDE
    sst = (orow % seg) * SLC_BLOCK
    overlap = (cst < sst + SLC_BLOCK) & (cst + CMP_LEN > sst) & (orow % seg < n_slc)
    halves = range(HEADS_PER_CHUNK)
    hms = _half_masks(BF16)
    raw = [_dot_nt(_stack_group(q_ref, hms[p] * SCALE), kc) for p in halves]
    pcs = []
    for p in halves:
        slope = _per_chunk_rows(rows, tq, [slopes[2 * c + p] for c in range(N_CHUNKS)])
        sc = jnp.where(mask, raw[p] - slope * diff_f, NEG)
        mx = jnp.max(sc, axis=-1, keepdims=True)
        e = jnp.where(mask, jnp.exp(sc - mx), 0.0)
        den = jnp.sum(e, axis=-1, keepdims=True)
        pcs.append(e / jnp.where(den > 0, den, 1.0))
    _store_group(o_ref, [_dot(pcs[p].astype(BF16), vc) for p in halves], tq)
    p_slc = jnp.zeros((LANES, tq), F32)
    for p in halves:
        pg = pcs[p][0:tq]
        for c in range(1, N_CHUNKS):
            pg = pg + pcs[p][c * tq:(c + 1) * tq]
        ov = (overlap & (orow // seg == p)).astype(BF16)
        pg_hi, pg_lo = _split(pg)
        p_slc = p_slc + (_dot_nt(ov, pg_hi) + _dot_nt(ov, pg_lo))
    j = _iota((LANES, tq), 0) % seg
    cur = (q0 + _iota((LANES, tq), 1)) // SLC_BLOCK
    forced = (j == 0) | (j == cur) | (j == cur - 1)
    usable = (j <= cur) & (j < n_slc)
    score = jnp.where(usable, p_slc + jnp.where(forced, FORCE_BONUS, 0.0), NEG)
    rank = _rank_rows(score, j, n_slc, seg)
    sel_t = jnp.where((rank < topn) & usable, 1.0, 0.0)
    sel_ref[...] = sel_t.T.astype(sel_ref.dtype)


def _nsa_cmp(q, kc, vc, slopes, tq=128):
    b, s, _ = q.shape
    ncp = kc.shape[1]
    n_slc = s // SLC_BLOCK
    assert n_slc <= LANES // N_KV and ncp % LANES == 0
    kern = functools.partial(_nsa_cmp_kernel, tq=tq, n_cmp=ncp - 1, n_slc=n_slc,
                             topn=min(SLC_TOPN, n_slc), slopes=slopes)
    spec_c = pl.BlockSpec((None, ncp, LANES), lambda bi, n: (bi, 0, 0))
    return pl.pallas_call(
        kern,
        out_shape=[jax.ShapeDtypeStruct((b, s, MIX), BF16), jax.ShapeDtypeStruct((b, s, LANES), BF16)],
        grid=(b, s // tq),
        in_specs=[pl.BlockSpec((None, tq, MIX), lambda bi, n: (bi, n, 0)), spec_c, spec_c],
        out_specs=[pl.BlockSpec((None, tq, MIX), lambda bi, n: (bi, n, 0)),
                   pl.BlockSpec((None, tq, LANES), lambda bi, n: (bi, n, 0))],
        compiler_params=_cparams("parallel", "parallel"),
        name="nsa_compressed",
    )(q, kc, vc)


def _nsa_slc_kernel(q_ref, k_ref, v_ref, sel_ref, o_ref, vt_ref, m_ref, acc_ref, *, tq, tk, slopes):
    n = pl.program_id(1)
    q0 = n * tq
    rows = N_CHUNKS * tq
    seg = LANES // N_KV
    halves = range(HEADS_PER_CHUNK)
    skip = 1.0 - sel_ref[...]
    nsub = tk // SLC_BLOCK
    lane_q = _iota((tq, LANES), 1)
    row_q = _iota((tq, LANES), 0).astype(F32)
    lane_k = _iota((tk, LANES), 1)
    off_k = _iota((tk, LANES), 0)
    spare = lambda p, n: (1 - p) * HEAD_DIM + n
    hms = _half_masks(BF16)
    qs = []
    for p in halves:
        extra = jnp.concatenate(
            [jnp.where(lane_q == spare(p, 0), -slopes[2 * c + p] * row_q,
                       jnp.where((lane_q == spare(p, 1)) | (lane_q == spare(p, 2)), slopes[2 * c + p], 0.0))
             for c in range(N_CHUNKS)], axis=0)
        qs.append(_stack_group(q_ref, hms[p] * SCALE) + extra.astype(BF16))
    @pl.when(n == 0)
    def _():
        vt_ref[...] = v_ref[...].astype(F32).T.astype(BF16)

    for p in halves:
        m_ref[p] = jnp.full((1, rows), 0.1 * NEG, F32)
        acc_ref[p] = jnp.zeros((LANES, rows), F32)

    n_grp = 2
    grp = rows // n_grp
    lanes = lambda g: slice(g * grp, (g + 1) * grp)
    chains = [(p, g) for p in halves for g in range(n_grp)]

    def tile(kt, diagonal):
        k0 = pl.multiple_of(kt * tk, tk)
        kj = k_ref[pl.ds(k0, tk), :]
        vt = vt_ref[:, pl.ds(k0, tk)]
        shift = (k0 - q0).astype(F32).astype(BF16)
        raw = []
        for p in halves:
            in_sub = (lane_k >= spare(p, 3)) & (lane_k < spare(p, 3) + nsub)
            kp = jnp.where(lane_k == spare(p, 0), 1.0,
                           jnp.where(lane_k == spare(p, 1), off_k.astype(F32).astype(BF16),
                                     jnp.where(lane_k == spare(p, 2), shift,
                                               jnp.where(in_sub, jnp.where(off_k // SLC_BLOCK == lane_k - spare(p, 3),
                                                                           1.0, 0.0).astype(BF16), kj))))
            src = _iota((LANES, LANES), 0) - (p * seg + k0 // SLC_BLOCK)
            dst = _iota((LANES, LANES), 1) - spare(p, 3)
            to_lane = ((src == dst) & (dst >= 0) & (dst < nsub)).astype(BF16)
            bias = (_dot(skip, to_lane) * NEG).astype(BF16)
            qp = qs[p] + jnp.concatenate([bias] * N_CHUNKS, axis=0)
            raw += [_dot_nt(kp, qp[lanes(g)]) for g in range(n_grp)]
        if diagonal:
            future = jnp.where(_iota((tk, tq), 1) + (q0 - k0) >= _iota((tk, tq), 0), 0.0, NEG)
            future = jnp.concatenate([future] * (grp // tq), axis=1)
        es, alphas = [], []
        for p, g in chains:
            s = raw[p * n_grp + g] + future if diagonal else raw[p * n_grp + g]
            m_old = m_ref[p, :, lanes(g)]
            m_new = jnp.maximum(m_old, jnp.max(s, axis=0, keepdims=True))
            m_ref[p, :, lanes(g)] = m_new
            es.append(jnp.exp(s - m_new).astype(BF16))
            alphas.append(jnp.exp(m_old - m_new))
        row = _iota((LANES, tk), 0)
        vts = [jnp.where(row // HEAD_DIM == p, vt, jnp.ones_like(vt)) for p in halves]
        pvs = [_dot(vts[p], es[p * n_grp + g]) for p, g in chains]
        for (p, g), alpha, pv in zip(chains, alphas, pvs):
            acc_ref[p, :, lanes(g)] = alpha * acc_ref[p, :, lanes(g)] + pv

    def body(kt, carry):
        tile(kt, False)
        return carry

    n_past = q0 // tk
    lax.fori_loop(0, n_past, body, 0)
    tile(n_past, True)
    outs = []
    for p in halves:
        acc = acc_ref[p]
        den = acc[(1 - p) * HEAD_DIM:(1 - p) * HEAD_DIM + 1, :]
        outs.append((acc / den).T)
    _store_group(o_ref, outs, tq)


def _nsa_slc(q, k, v, sel, slopes, tq=256, tk=256):
    b, s, _ = q.shape
    assert tk % tq == 0 and tq % SLC_BLOCK == 0 and s % tk == 0 and tk <= 256
    assert all(_is_pow2(x) for x in slopes)
    rows = N_CHUNKS * tq
    spec_q = pl.BlockSpec((None, tq, MIX), lambda bi, n: (bi, n, 0))
    spec_kv = pl.BlockSpec((None, s, LANES), lambda bi, n: (bi, 0, 0))
    return pl.pallas_call(
        functools.partial(_nsa_slc_kernel, tq=tq, tk=tk, slopes=slopes),
        out_shape=jax.ShapeDtypeStruct((b, s, MIX), BF16),
        grid=(b, s // tq),
        in_specs=[spec_q, spec_kv, spec_kv, pl.BlockSpec((None, tq, LANES), lambda bi, n: (bi, n, 0))],
        out_specs=spec_q,
        scratch_shapes=[pltpu.VMEM((LANES, s), BF16), pltpu.VMEM((N_KV, 1, rows), F32),
                        pltpu.VMEM((N_KV, LANES, rows), F32)],
        compiler_params=_cparams("parallel", "arbitrary"),
        name="nsa_selected",
    )(q, k, v, sel)


def _odd_attn(x2, b, s, norm_g, w_in, c_q_norm, c_k_norm, d_q_norm, d_k_norm, cmp_pos, cmp_w, w_out):
    kvw = N_KV * HEAD_DIM
    n_gate = N_HEADS * 3
    qd0 = 3 * MIX
    w = jnp.concatenate([w_in[:, :qd0], _gqa_cols(w_in[:, qd0:qd0 + MIX]), w_in[:, qd0 + MIX:],
                         jnp.zeros((w_in.shape[0], LANES - n_gate), F32)], axis=1).astype(BF16)
    n = w.shape[1]
    ones = lambda width: jnp.ones((width,), F32)
    gain = jnp.concatenate([
        jnp.tile(c_q_norm, N_HEADS), jnp.tile(c_k_norm, N_HEADS), ones(MIX), jnp.tile(d_q_norm, N_HEADS),
        ones(2 * kvw), jnp.tile(d_k_norm[1], N_KV), ones(kvw), jnp.tile(d_k_norm[2], N_KV), ones(kvw),
        ones(LANES)]).reshape(1, n)
    plan, col = [], 0
    for width, op in ((MIX, "norm"), (MIX, "norm"), (MIX, None), (MIX, "norm"), (kvw, None), (kvw, None),
                      (kvw, "norm"), (kvw, None), (kvw, "norm"), (kvw, None), (LANES, "sigmoid")):
        plan.append((col, width, op))
        col += width
    outs = _proj(x2, norm_g, w, gain, tuple(plan), [BF16] * 10 + [F32])
    r3 = lambda a: a.reshape(b, s, a.shape[-1])
    qc, kc, vc, qd, kcmp, vcmp, kslc, vslc, kwin, vwin = [r3(a) for a in outs[:10]]
    gates = outs[10]
    slopes = _alibi(N_HEADS)
    gslopes = [slopes[h] for h in GQA_PERM]
    oc = _moba_attention(qc, kc, vc, slopes)
    k_cmp, v_cmp = _compress(kcmp, vcmp, cmp_pos, cmp_w, d_k_norm[0])
    o_cmp, sel = _nsa_cmp(qd, k_cmp, v_cmp, gslopes)
    o_slc = _nsa_slc(qd, kslc, vslc, sel, gslopes)
    o_win = _band_attention(qd, kwin, vwin, window=D_WINDOW, slopes=gslopes)
    t = b * s
    w1 = w_out[:MIX].astype(BF16)
    w2 = _gqa_rows(w_out[MIX:]).astype(BF16)
    flat = lambda a: a.reshape(t, a.shape[-1])
    return [flat(oc), flat(o_cmp), flat(o_slc), flat(o_win), gates], w1, w2


def kernel(x, ev_norm, ev_w_in, ev_q_norm, ev_k_norm, ev_sink, ev_w_out, od_norm, od_w_in, od_c_q_norm,
           od_c_k_norm, od_d_q_norm, od_d_k_norm, od_cmp_pos, od_cmp_w, od_w_out, moe_norm, moe_w_grp,
           moe_b_grp, moe_w_exp, moe_b_exp, moe_w_gate, moe_w_up, moe_w_down):
    b, s, d = x.shape
    x2 = x.reshape(b * s, d)
    depth = moe_norm.shape[0]
    moe_w_gate, moe_w_up, moe_w_down = (w.astype(BF16) for w in (moe_w_gate, moe_w_up, moe_w_down))
    for layer in range(depth):
        i = layer // 2
        if layer % 2 == 0:
            attn, w1, w2 = _even_attn(x2, b, s, ev_norm[i], ev_w_in[i], ev_q_norm[i], ev_k_norm[i],
                                      ev_sink[i], ev_w_out[i])
        else:
            attn, w1, w2 = _odd_attn(x2, b, s, od_norm[i], od_w_in[i], od_c_q_norm[i], od_c_k_norm[i],
                                     od_d_q_norm[i], od_d_k_norm[i], od_cmp_pos[i], od_cmp_w[i], od_w_out[i])
        x2 = _moe_block(x2, attn, w1, w2, moe_norm[layer], moe_w_grp[layer], moe_b_grp[layer],
                        moe_w_exp[layer], moe_b_exp[layer], moe_w_gate, moe_w_up, moe_w_down, layer)
    return x2.reshape(b, s, d)
```

```python
import functools
import math

import numpy as np
import jax
import jax.numpy as jnp
from jax import lax
from jax.experimental import pallas as pl
from jax.experimental.pallas import tpu as pltpu

F32 = jnp.float32
BF16 = jnp.bfloat16

LANES = 128
HEAD_DIM = 64
HEADS_PER_CHUNK = LANES // HEAD_DIM
D_MODEL = 1024
N_HEADS = 8
N_KV = 2
GROUP = N_HEADS // N_KV
N_CHUNKS = N_HEADS // HEADS_PER_CHUNK
MIX = N_HEADS * HEAD_DIM
SCALE = 1.0 / math.sqrt(HEAD_DIM)
EPS = 1e-6
NEG = -1e30
EXP_UNDERFLOW = -104.0

A_WINDOW = 128
C_BLOCK = 256
C_TOPK = 3
CMP_LEN = 32
CMP_STRIDE = 16
SLC_BLOCK = 64
SLC_TOPN = 16
D_WINDOW = 512
FORCE_BONUS = 1000.0

N_GROUPS = 4
EXPERTS_PER_GROUP = 8
N_EXPERTS = N_GROUPS * EXPERTS_PER_GROUP
EXPERT_FF = 256

VMEM_LIMIT = 48 * 1024 * 1024
MOE_VMEM_LIMIT = 56 * 1024 * 1024

GQA_PERM = tuple(h for c in range(N_CHUNKS) for h in (c, c + GROUP))
MHA_PERM = tuple(range(N_HEADS))


def _alibi(n_heads):
    return [float(2.0 ** (-8.0 * (i + 1) / n_heads)) for i in range(n_heads)]


def _is_pow2(x):
    return math.frexp(x)[0] == 0.5


def _gqa_cols(w):
    lead = w.shape[:-1]
    w = w.reshape(*lead, N_KV, GROUP, HEAD_DIM)
    return jnp.swapaxes(w, -3, -2).reshape(*lead, MIX)


def _gqa_rows(w):
    tail = w.shape[1:]
    return jnp.swapaxes(w.reshape(N_KV, GROUP, HEAD_DIM, *tail), 0, 1).reshape(MIX, *tail)


def _cparams(*sem, vmem=VMEM_LIMIT):
    return pltpu.CompilerParams(dimension_semantics=sem, vmem_limit_bytes=vmem)


def _dot(a, b):
    return jnp.dot(a, b, preferred_element_type=F32)


def _dot_nt(a, b):
    return lax.dot_general(a, b, (((1,), (1,)), ((), ())), preferred_element_type=F32)


def _split(x):
    hi = x.astype(BF16)
    lo = (x - hi.astype(F32)).astype(BF16)
    return hi, lo


def _dot_hilo(a, b):
    hi, lo = _split(a)
    return _dot(jnp.concatenate([hi, lo], axis=1), jnp.concatenate([b, b], axis=0))


def _iota(shape, dim):
    return lax.broadcasted_iota(jnp.int32, shape, dim)


def _half_masks(dtype):
    lane = _iota((1, LANES), 1)
    return [(lane // HEAD_DIM == p).astype(dtype) for p in range(HEADS_PER_CHUNK)]


def _head_mean_sq(y):
    w = y.shape[1]
    same = (_iota((w, w), 0) // HEAD_DIM == _iota((w, w), 1) // HEAD_DIM)
    return _dot((y * y).astype(BF16), same.astype(BF16)) * (1.0 / HEAD_DIM)


def _proj_kernel(x_ref, g_ref, w_ref, gain_ref, *out_refs, plan):
    x = x_ref[...]
    ms = jnp.mean(x * x, axis=-1, keepdims=True)
    xn = (x * lax.rsqrt(ms + EPS) * g_ref[...]).astype(BF16)
    for o_ref, (col0, width, op) in zip(out_refs, plan):
        for a in range(0, width, 2 * LANES):
            wd = min(2 * LANES, width - a)
            y = _dot(xn, w_ref[:, col0 + a:col0 + a + wd])
            if op == "norm":
                y = y * lax.rsqrt(_head_mean_sq(y) + EPS) * gain_ref[:, col0 + a:col0 + a + wd]
            elif op == "sigmoid":
                y = jax.nn.sigmoid(y)
            o_ref[:, a:a + wd] = y.astype(o_ref.dtype)


def _proj(x2, norm_g, w, gain, plan, out_dtypes, tm=1024):
    t, d = x2.shape
    n = w.shape[1]
    out_shape = [jax.ShapeDtypeStruct((t, width), dt) for (_, width, _), dt in zip(plan, out_dtypes)]
    return pl.pallas_call(
        functools.partial(_proj_kernel, plan=plan),
        out_shape=out_shape,
        grid=(t // tm,),
        in_specs=[
            pl.BlockSpec((tm, d), lambda i: (i, 0)),
            pl.BlockSpec((1, d), lambda i: (0, 0)),
            pl.BlockSpec((d, n), lambda i: (0, 0)),
            pl.BlockSpec((1, n), lambda i: (0, 0)),
        ],
        out_specs=[pl.BlockSpec((tm, width), lambda i: (i, 0)) for (_, width, _) in plan],
        compiler_params=_cparams("parallel"),
        name="norm_proj",
    )(x2, norm_g.reshape(1, d), w, gain)


def _stack_group(q_ref, halfmask):
    return jnp.concatenate(
        [q_ref[:, c * LANES:(c + 1) * LANES] * halfmask for c in range(N_CHUNKS)], axis=0)


def _per_chunk_rows(rows, tq, values):
    r = _iota((rows, 1), 0) // tq
    out = jnp.full((rows, 1), values[N_CHUNKS - 1], F32)
    for c in range(N_CHUNKS - 2, -1, -1):
        out = jnp.where(r == c, values[c], out)
    return out


def _per_chunk_lanes(rows, tq, values):
    r = _iota((1, rows), 1) // tq
    out = jnp.full((1, rows), values[N_CHUNKS - 1], F32)
    for c in range(N_CHUNKS - 2, -1, -1):
        out = jnp.where(r == c, values[c], out)
    return out


def _band_kernel(*refs, tq, window, wpad, slopes, has_sink):
    if has_sink:
        sink_ref, q_ref, k_ref, v_ref, o_ref, vt_ref = refs
    else:
        q_ref, k_ref, v_ref, o_ref, vt_ref = refs
    n = pl.program_id(1)

    @pl.when(n == 0)
    def _():
        vt_ref[...] = v_ref[...].astype(F32).T.astype(BF16)

    q0 = n * tq
    kw = tq + wpad
    kstart = pl.multiple_of(jnp.maximum(q0 - wpad, 0), LANES)
    ks = k_ref[pl.ds(kstart, kw), :]
    vt = vt_ref[:, pl.ds(kstart, kw)]
    rows = N_CHUNKS * tq
    halves = range(HEADS_PER_CHUNK)
    hms = _half_masks(BF16)
    spare = lambda p, j: (1 - p) * HEAD_DIM + j
    lane_q = _iota((tq, LANES), 1)
    row_q = _iota((tq, LANES), 0).astype(F32)
    lane_k = _iota((kw, LANES), 1)
    off_k = _iota((kw, LANES), 0)
    fine = (off_k % LANES).astype(F32).astype(BF16)
    coarse = ((off_k // LANES) * LANES + (kstart - q0)).astype(F32).astype(BF16)
    diff = _iota((kw, tq), 1) + (q0 - kstart) - _iota((kw, tq), 0)
    band = jnp.where((diff >= 0) & (diff < window), 0.0, NEG)
    band = jnp.concatenate([band] * N_CHUNKS, axis=1)
    raw = []
    for p in halves:
        extra = jnp.concatenate(
            [jnp.where(lane_q == spare(p, 0), -slopes[2 * c + p] * row_q,
                       jnp.where((lane_q == spare(p, 1)) | (lane_q == spare(p, 2)), slopes[2 * c + p], 0.0))
             for c in range(N_CHUNKS)], axis=0)
        qp = _stack_group(q_ref, hms[p] * SCALE) + extra.astype(BF16)
        kp = jnp.where(lane_k == spare(p, 0), 1.0,
                       jnp.where(lane_k == spare(p, 1), fine, jnp.where(lane_k == spare(p, 2), coarse, ks)))
        raw.append(_dot_nt(kp, qp))
    es, sinks = [], []
    for p in halves:
        s = raw[p] + band
        mx = jnp.max(s, axis=0, keepdims=True)
        if has_sink:
            sk = _per_chunk_lanes(rows, tq, [sink_ref[2 * c + p] for c in range(N_CHUNKS)])
            mx = jnp.maximum(mx, sk)
            sinks.append(jnp.exp(sk - mx))
        es.append(jnp.exp(s - mx).astype(BF16))
    vrow = _iota((LANES, kw), 0)
    outs = []
    for p in halves:
        acc = _dot(jnp.where(vrow // HEAD_DIM == p, vt, jnp.ones_like(vt)), es[p])
        den = acc[(1 - p) * HEAD_DIM:(1 - p) * HEAD_DIM + 1, :]
        if has_sink:
            den = den + sinks[p]
        outs.append((acc / den).T)
    _store_group(o_ref, outs, tq)


def _store_group(o_ref, os_, tq):
    lane = _iota((1, LANES), 1)
    for c in range(N_CHUNKS):
        o = jnp.where(lane < HEAD_DIM, os_[0][c * tq:(c + 1) * tq], os_[1][c * tq:(c + 1) * tq])
        o_ref[:, c * LANES:(c + 1) * LANES] = o.astype(o_ref.dtype)


def _band_attention(q, k, v, *, window, slopes, sink=None, tq=256):
    b, s, _ = q.shape
    wpad = -(-window // LANES) * LANES
    assert s >= tq + wpad and s % tq == 0 and tq <= 256 and all(_is_pow2(x) for x in slopes)
    kern = functools.partial(_band_kernel, tq=tq, window=window, wpad=wpad,
                             slopes=slopes, has_sink=sink is not None)
    in_specs = [
        pl.BlockSpec((None, tq, MIX), lambda bi, n: (bi, n, 0)),
        pl.BlockSpec((None, s, LANES), lambda bi, n: (bi, 0, 0)),
        pl.BlockSpec((None, s, LANES), lambda bi, n: (bi, 0, 0)),
    ]
    args = [q, k, v]
    if sink is not None:
        in_specs = [pl.BlockSpec(memory_space=pltpu.SMEM)] + in_specs
        args = [sink] + args
    return pl.pallas_call(
        kern,
        out_shape=jax.ShapeDtypeStruct((b, s, MIX), BF16),
        grid=(b, s // tq),
        in_specs=in_specs,
        out_specs=pl.BlockSpec((None, tq, MIX), lambda bi, n: (bi, n, 0)),
        scratch_shapes=[pltpu.VMEM((LANES, s), BF16)],
        compiler_params=_cparams("parallel", "arbitrary"),
        name="band_attention",
    )(*args)


def _stick_kernel(q_ref, k_ref, v_ref, o_ref, acc_ref, run_ref, *, tq, cpb):
    i = pl.program_id(2)
    q0 = pl.multiple_of(i * tq, tq)
    hms = _half_masks(BF16)
    heads = [(cc, p) for cc in range(cpb) for p in range(HEADS_PER_CHUNK)]
    qs = [q_ref[:, cc * LANES:(cc + 1) * LANES] * (hms[p] * SCALE) for cc, p in heads]
    upper = (_iota((2 * tq, tq), 0) % tq > _iota((2 * tq, tq), 1)).astype(BF16)

    def suffix_sum(x):
        hi, lo = _split(x)
        return _dot(jnp.concatenate([hi, lo], axis=1), upper)

    def block(kstart, diag):
        if diag:
            causal = _iota((tq, tq), 1) < _iota((tq, tq), 0)
        kjs = [k_ref[pl.ds(kstart, tq), cc * LANES:(cc + 1) * LANES] for cc in range(cpb)]
        vjs = [v_ref[pl.ds(kstart, tq), cc * LANES:(cc + 1) * LANES] for cc in range(cpb)]
        zs = [_dot_nt(qs[h], kjs[cc]) for h, (cc, p) in enumerate(heads)]
        lss, lks = [], []
        for z in zs:
            ls = jnp.minimum(z, 0.0) - jnp.log(1.0 + jnp.exp(-jnp.abs(z)))
            lk = ls - z
            if diag:
                lk = jnp.where(causal, lk, 0.0)
            lss.append(ls)
            lks.append(lk)
        sufs = [suffix_sum(lk) for lk in lks]
        ws = []
        for h in range(len(heads)):
            if diag:
                a = jnp.where(causal, jnp.exp(lss[h] + sufs[h]), 0.0)
            else:
                a = jnp.exp(lss[h] + sufs[h] + run_ref[h])
            ws.append(a.astype(BF16))
        pvs = [_dot(ws[h], vjs[cc]) for h, (cc, p) in enumerate(heads)]
        for h in range(len(heads)):
            rowsum = jnp.sum(lks[h], axis=-1, keepdims=True)
            if diag:
                acc_ref[h] = pvs[h]
                run_ref[h] = rowsum
            else:
                acc_ref[h] += pvs[h]
                run_ref[h] += rowsum

    block(q0, True)

    def weights_alive():
        run = run_ref[0]
        for h in range(1, len(heads)):
            run = jnp.maximum(run, run_ref[h])
        return jnp.max(run) > EXP_UNDERFLOW

    def body(carry):
        t, _ = carry
        block(pl.multiple_of((i - 1 - t) * tq, tq), False)
        return t + 1, weights_alive()

    lax.while_loop(lambda c: (c[0] < i) & c[1], body, (jnp.int32(0), weights_alive()))
    lane = _iota((1, LANES), 1)
    for cc in range(cpb):
        o = jnp.where(lane < HEAD_DIM, acc_ref[HEADS_PER_CHUNK * cc], acc_ref[HEADS_PER_CHUNK * cc + 1])
        o_ref[:, cc * LANES:(cc + 1) * LANES] = o.astype(o_ref.dtype)


def _stick_attention(q, k, v, tq=256, cpb=4):
    b, s, _ = q.shape
    wide = cpb * LANES
    n_heads = cpb * HEADS_PER_CHUNK
    spec_q = pl.BlockSpec((None, tq, wide), lambda bi, c, i: (bi, i, c))
    spec_kv = pl.BlockSpec((None, s, wide), lambda bi, c, i: (bi, 0, c))
    return pl.pallas_call(
        functools.partial(_stick_kernel, tq=tq, cpb=cpb),
        out_shape=jax.ShapeDtypeStruct((b, s, MIX), BF16),
        grid=(b, N_CHUNKS // cpb, s // tq),
        in_specs=[spec_q, spec_kv, spec_kv],
        out_specs=spec_q,
        scratch_shapes=[pltpu.VMEM((n_heads, tq, LANES), F32), pltpu.VMEM((n_heads, tq, 1), F32)],
        compiler_params=_cparams("parallel", "parallel", "parallel"),
        name="stick_breaking",
    )(q, k, v)


def _route(logits):
    lane = _iota(logits.shape, 1)
    lane_f = lane.astype(F32)
    ninf = -jnp.inf
    is_g = (lane >= N_EXPERTS) & (lane < N_EXPERTS + N_GROUPS)
    gmax = jnp.max(jnp.where(is_g, logits, ninf), axis=-1, keepdims=True)
    gidx = jnp.min(jnp.where(is_g & (logits == gmax), lane_f - N_EXPERTS, 1e9), axis=-1, keepdims=True)
    p_g = 1.0 / jnp.sum(jnp.where(is_g, jnp.exp(logits - gmax), 0.0), axis=-1, keepdims=True)
    in_grp = (lane < N_EXPERTS) & ((lane // EXPERTS_PER_GROUP).astype(F32) == gidx)
    le = jnp.where(in_grp, logits, ninf)
    m1 = jnp.max(le, axis=-1, keepdims=True)
    i1 = jnp.min(jnp.where(le == m1, lane_f, 1e9), axis=-1, keepdims=True)
    le2 = jnp.where(lane_f == i1, ninf, le)
    m2 = jnp.max(le2, axis=-1, keepdims=True)
    i2 = jnp.min(jnp.where(le2 == m2, lane_f, 1e9), axis=-1, keepdims=True)
    e2 = jnp.exp(m2 - m1)
    w1 = p_g / (1.0 + e2)
    w2 = p_g * e2 / (1.0 + e2)
    return jnp.where(lane_f == i1, w1, 0.0) + jnp.where(lane_f == i2, w2, 0.0)


def _gate_expand(branch):
    r = _iota((LANES, MIX), 0)
    col = _iota((LANES, MIX), 1)
    head = col // LANES + GROUP * ((col % LANES) // HEAD_DIM)
    return (r == 3 * head + branch).astype(BF16)


def _out_kernel(*refs, nsa):
    if nsa:
        (x_ref, o1_ref, ocmp_ref, oslc_ref, owin_ref, gates_ref, w1_ref, w2_ref,
         ng_ref, wr_ref, br_ref, x1_ref, h_ref, comb_ref) = refs
        g = gates_ref[...]
        o2 = (_dot_hilo(g, _gate_expand(0)) * ocmp_ref[...]
              + _dot_hilo(g, _gate_expand(1)) * oslc_ref[...]
              + _dot_hilo(g, _gate_expand(2)) * owin_ref[...]).astype(BF16)
    else:
        (x_ref, o1_ref, o2_ref, w1_ref, w2_ref,
         ng_ref, wr_ref, br_ref, x1_ref, h_ref, comb_ref) = refs
        o2 = o2_ref[...]
    x1 = x_ref[...] + _dot(o1_ref[...], w1_ref[...]) + _dot(o2, w2_ref[...])
    x1_ref[...] = x1
    ms = jnp.mean(x1 * x1, axis=-1, keepdims=True)
    h = x1 * lax.rsqrt(ms + EPS) * ng_ref[...]
    h_ref[...] = h.astype(BF16)
    h_hi, h_lo = _split(h)
    w_hi, w_lo = _split(wr_ref[...])
    both = _dot(h_hi, jnp.concatenate([w_hi, w_lo], axis=1))
    logits = both[:, :LANES] + (both[:, LANES:] + _dot(h_lo, w_hi)) + br_ref[...]
    comb_ref[...] = _route(logits)


def _out_proj_route(x2, attn, w1, w2, moe_g, w_route, b_route, tm=1024):
    t, d = x2.shape
    nsa = len(attn) > 2
    row = lambda width: pl.BlockSpec((tm, width), lambda i: (i, 0))
    full = lambda a: pl.BlockSpec(a.shape, lambda i: (0, 0))
    consts = [w1, w2, moe_g.reshape(1, d), w_route, b_route]
    return pl.pallas_call(
        functools.partial(_out_kernel, nsa=nsa),
        out_shape=[jax.ShapeDtypeStruct((t, d), F32), jax.ShapeDtypeStruct((t, d), BF16),
                   jax.ShapeDtypeStruct((t, LANES), F32)],
        grid=(t // tm,),
        in_specs=[row(d)] + [row(a.shape[1]) for a in attn] + [full(a) for a in consts],
        out_specs=[row(d), row(d), row(LANES)],
        compiler_params=_cparams("parallel"),
        name="out_proj_route",
    )(x2, *attn, *consts)


MOE_CHUNK = 256
MOE_FIRST_CHUNK = MOE_CHUNK + 32
SEG_ALIGN = 16


def _sorted_rows(tm):
    return -(-(tm + N_GROUPS * SEG_ALIGN) // LANES) * LANES


def _moe_sort(h_ref, comb_ref, hs_ref, cs_ref, pos_ref, acc_ref, seg_ref):
    tm = h_ref.shape[0]
    rows_s = hs_ref.shape[0]
    comb = comb_ref[...]
    used = jnp.where(comb > 0, 1.0, 0.0).astype(BF16)
    of_group = ((_iota((LANES, LANES), 0) // EXPERTS_PER_GROUP == _iota((LANES, LANES), 1))
                & (_iota((LANES, LANES), 0) < N_EXPERTS)).astype(BF16)
    member = jnp.where(_dot(used, of_group) > 0.5, 1.0, 0.0)
    member_t = member.T
    earlier = (_iota((tm, tm), 0) < _iota((tm, tm), 1)).astype(BF16)
    rank_t = _dot(member_t.astype(BF16), earlier)
    count = jnp.sum(member_t, axis=1, keepdims=True)
    counts = [count[g:g + 1, :] for g in range(N_GROUPS)]
    starts = [jnp.zeros((1, 1), F32)]
    for g in range(1, N_GROUPS):
        padded = jnp.floor((counts[g - 1] + (SEG_ALIGN - 1)) * (1.0 / SEG_ALIGN)) * SEG_ALIGN
        starts.append(starts[g - 1] + padded)
    grow = _iota((LANES, 1), 0)
    start_col = jnp.zeros((LANES, 1), F32)
    for g in range(1, N_GROUPS):
        start_col = jnp.where(grow == g, starts[g], start_col)
    pos_t = jnp.sum(member_t * (start_col + rank_t), axis=0, keepdims=True)
    used_rows = _sorted_rows(tm)
    place = jnp.where(_iota((used_rows, tm), 0).astype(F32) == pos_t, 1.0, 0.0).astype(BF16)
    hs_ref[:used_rows] = _dot(place, h_ref[...]).astype(BF16)
    hs_ref[used_rows:] = jnp.zeros((rows_s - used_rows, hs_ref.shape[1]), BF16)
    c_hi = comb.astype(BF16)
    rest = comb - c_hi.astype(F32)
    c_mid = rest.astype(BF16)
    c_lo = (rest - c_mid.astype(F32)).astype(BF16)
    cs_ref[:used_rows] = _dot(place, c_hi) + (_dot(place, c_mid) + _dot(place, c_lo))
    cs_ref[used_rows:] = jnp.zeros((rows_s - used_rows, LANES), F32)
    pos_ref[...] = jnp.broadcast_to(pos_t, (LANES, tm)).T
    acc_ref[...] = jnp.zeros(acc_ref.shape, F32)
    for g in range(N_GROUPS):
        seg_ref[g] = starts[g][0, 0].astype(jnp.int32)
        seg_ref[N_GROUPS + g] = counts[g][0, 0].astype(jnp.int32)


def _moe_kernel(h_ref, comb_ref, x1_ref, wg_ref, wu_ref, wd_ref, o_ref,
                hs_ref, cs_ref, pos_ref, acc_ref, seg_ref, *, per_step):
    step = pl.program_id(1)
    tm = h_ref.shape[0]
    rows_s = hs_ref.shape[0]

    @pl.when(step == 0)
    def _():
        _moe_sort(h_ref, comb_ref, hs_ref, cs_ref, pos_ref, acc_ref, seg_ref)

    first = step * per_step
    group = first // EXPERTS_PER_GROUP
    start = seg_ref[group]
    count = seg_ref[N_GROUPS + group]
    w_gate = [wg_ref[j] for j in range(per_step)]
    w_up = [wu_ref[j] for j in range(per_step)]
    w_down = jnp.concatenate([wd_ref[j] for j in range(per_step)], axis=0)
    sizes = [MOE_FIRST_CHUNK, MOE_CHUNK // 2, MOE_CHUNK // 2] + [MOE_CHUNK] * (tm // MOE_CHUNK - 2)
    assert sum(sizes) >= tm and max(sizes) <= MOE_FIRST_CHUNK
    begin = 0
    for size in sizes:
        @pl.when(begin < count)
        def _(begin=begin, size=size):
            off = pl.multiple_of(start + begin, SEG_ALIGN)
            hs = hs_ref[pl.ds(off, size), :]
            cs = cs_ref[pl.ds(off, size), :]
            lane = _iota((size, LANES), 1)
            acts = []
            for j in range(per_step):
                g = _dot(hs, w_gate[j])
                u = _dot(hs, w_up[j])
                c = jnp.sum(jnp.where(lane == first + j, cs, 0.0), axis=-1, keepdims=True)
                acts.append((c * (g * jax.nn.sigmoid(g) * u)).astype(BF16))
            acc_ref[pl.ds(off, size), :] += _dot(jnp.concatenate(acts, axis=1), w_down)
        begin += size

    @pl.when(step == pl.num_programs(1) - 1)
    def _():
        pos = pos_ref[...]
        used_rows = _sorted_rows(tm)
        back = jnp.concatenate(
            [jnp.where(pos == (_iota((tm, LANES), 1) + blk * LANES).astype(F32), 1.0, 0.0).astype(BF16)
             for blk in range(used_rows // LANES)], axis=1)
        o_ref[...] = x1_ref[...] + _dot(back, acc_ref[:used_rows].astype(BF16))


def _moe(h, comb, x1, w_gate, w_up, w_down, layer, tm=1024, per_step=4):
    t, d = h.shape
    _, n_exp, _, ff = w_gate.shape
    assert t % tm == 0 and EXPERTS_PER_GROUP % per_step == 0 and tm % MOE_CHUNK == 0
    rows_s = _sorted_rows(tm) + -(-MOE_FIRST_CHUNK // LANES) * LANES
    return pl.pallas_call(
        functools.partial(_moe_kernel, per_step=per_step),
        out_shape=jax.ShapeDtypeStruct((t, d), F32),
        grid=(t // tm, n_exp // per_step),
        scratch_shapes=[pltpu.VMEM((rows_s, d), BF16), pltpu.VMEM((rows_s, LANES), F32),
                        pltpu.VMEM((tm, LANES), F32), pltpu.VMEM((rows_s, d), F32),
                        pltpu.SMEM((2 * N_GROUPS,), jnp.int32)],
        in_specs=[
            pl.BlockSpec((tm, d), lambda i, e: (i, 0)),
            pl.BlockSpec((tm, LANES), lambda i, e: (i, 0)),
            pl.BlockSpec((tm, d), lambda i, e: (i, 0)),
            pl.BlockSpec((None, per_step, d, ff), lambda i, e: (layer, e, 0, 0)),
            pl.BlockSpec((None, per_step, d, ff), lambda i, e: (layer, e, 0, 0)),
            pl.BlockSpec((None, per_step, ff, d), lambda i, e: (layer, e, 0, 0)),
        ],
        out_specs=pl.BlockSpec((tm, d), lambda i, e: (i, 0)),
        compiler_params=_cparams("parallel", "arbitrary", vmem=MOE_VMEM_LIMIT),
        name="moe_experts",
    )(h, comb, x1, w_gate, w_up, w_down)


def _moe_block(x2, attn, w1, w2, moe_g, w_grp, b_grp, w_exp, b_exp, w_gate, w_up, w_down, layer):
    d = x2.shape[1]
    pad = LANES - N_EXPERTS - N_GROUPS
    w_route = jnp.concatenate([w_exp, w_grp, jnp.zeros((d, pad), F32)], axis=1)
    b_route = jnp.concatenate([b_exp, b_grp, jnp.zeros((pad,), F32)]).reshape(1, LANES)
    x1, h, comb = _out_proj_route(x2, attn, w1, w2, moe_g, w_route, b_route)
    return _moe(h, comb, x1, w_gate, w_up, w_down, layer)


def _even_attn(x2, b, s, norm_g, w_in, q_norm, k_norm, sink, w_out):
    w = jnp.concatenate([_gqa_cols(w_in[:, :MIX]), w_in[:, MIX:]], axis=1).astype(BF16)
    n = w.shape[1]
    kvw = N_KV * HEAD_DIM
    gain = jnp.concatenate([jnp.tile(q_norm, N_HEADS), jnp.tile(k_norm, N_KV),
                            jnp.ones((n - MIX - kvw,), F32)]).reshape(1, n)
    plan, col = [], 0
    for width, op in ((MIX, "norm"), (kvw, "norm"), (kvw, None), (MIX, None), (MIX, None), (MIX, None)):
        plan.append((col, width, op))
        col += width
    qa, ka, va, qb, kb, vb = _proj(x2, norm_g, w, gain, tuple(plan), [BF16] * 6)
    r3 = lambda a: a.reshape(b, s, a.shape[-1])
    slopes = _alibi(N_HEADS)
    oa = _band_attention(r3(qa), r3(ka), r3(va), window=A_WINDOW,
                         slopes=[slopes[h] for h in GQA_PERM], sink=sink[np.asarray(GQA_PERM)])
    ob = _stick_attention(r3(qb), r3(kb), r3(vb))
    w1 = _gqa_rows(w_out[:MIX]).astype(BF16)
    w2 = w_out[MIX:].astype(BF16)
    t = b * s
    return [oa.reshape(t, MIX), ob.reshape(t, MIX)], w1, w2


def _rank_rows(score, j, n, seg):
    row = _iota(score.shape, 0)
    rank = jnp.zeros(score.shape, F32)
    for jj in range(n):
        other = score[jj:jj + 1, :]
        for sgm in range(1, score.shape[0] // seg):
            other = jnp.where(row // seg == sgm, score[sgm * seg + jj:sgm * seg + jj + 1, :], other)
        beats = (other > score) | ((other == score) & (jj < j))
        rank = rank + jnp.where(beats, 1.0, 0.0)
    return rank


def _ones_beside(v, p):
    lane = _iota((1, LANES), 1)
    return jnp.where(lane // HEAD_DIM == p, v, jnp.ones_like(v))


def _normalize(acc):
    return acc / pltpu.roll(acc, HEAD_DIM, 1)


def _moba_kernel(slope_ref, q_ref, k_ref, v_ref, o_ref, kmean_ref, vt_ref, m_ref, acc_ref, *,
                 blk, nblk, topk, cpb):
    g = pl.program_id(1)
    i = pl.program_id(2)
    s_len = k_ref.shape[0]
    heads = [(cc, p) for cc in range(cpb) for p in range(HEADS_PER_CHUNK)]
    nh = len(heads)
    chunk = lambda cc: slice(cc * LANES, (cc + 1) * LANES)

    @pl.when(i == 0)
    def _():
        member = (_iota((LANES, s_len), 1) // blk == _iota((LANES, s_len), 0)).astype(BF16)
        for cc in range(cpb):
            kmean_ref[cc] = _dot(member, k_ref[:, chunk(cc)]) * (1.0 / blk)
            vt_ref[cc] = v_ref[:, chunk(cc)].astype(F32).T.astype(BF16)

    hms = _half_masks(BF16)
    q0 = pl.multiple_of(i * blk, blk)
    rel = (_iota((blk, blk), 0) - _iota((blk, blk), 1)).astype(F32)
    slopes = [slope_ref[HEADS_PER_CHUNK * (g * cpb + cc) + p] for cc, p in heads]
    kms = [_split(kmean_ref[cc]) for cc in range(cpb)]
    gates = [_dot_nt(kms[cc][0], q_ref[:, chunk(cc)] * hms[p]) + _dot_nt(kms[cc][1], q_ref[:, chunk(cc)] * hms[p])
             for cc, p in heads]
    lane = _iota((blk, LANES), 1)
    off_f = _iota((blk, LANES), 0).astype(F32)
    off_b = off_f.astype(BF16)
    spare = lambda p, n: (1 - p) * HEAD_DIM + n

    def with_key_lanes(kj, p, shift):
        return jnp.where(lane == spare(p, 0), 1.0,
                         jnp.where(lane == spare(p, 1), off_b,
                                   jnp.where(lane == spare(p, 2), shift.astype(BF16), kj)))

    qs = []
    for h, (cc, p) in enumerate(heads):
        extra = jnp.where(lane == spare(p, 0), -slopes[h] * off_f,
                          jnp.where((lane == spare(p, 1)) | (lane == spare(p, 2)), slopes[h], 0.0))
        qs.append(q_ref[:, chunk(cc)] * (hms[p] * SCALE) + extra.astype(BF16))
    zero = jnp.zeros((), F32)
    raw = [_dot_nt(with_key_lanes(k_ref[pl.ds(q0, blk), chunk(cc)], p, zero), qs[h])
           for h, (cc, p) in enumerate(heads)]
    nrow = -(-nblk // 8) * 8
    blk_id = _iota((nrow, blk), 0)
    skips = []
    for h in range(nh):
        gate = jnp.where(blk_id < i, gates[h][:nrow], NEG)
        rank = _rank_rows(gate, blk_id, nblk, nrow)
        skip_t = jnp.where((rank < topk) & (blk_id < i), 0.0, 1.0)
        skip_t = jnp.concatenate([skip_t, jnp.ones((LANES - nrow, blk), F32)], axis=0)
        skips.append(skip_t.T.astype(BF16))
    own_bias = jnp.where(rel <= 0, 0.0, NEG)
    vrow = _iota((LANES, blk), 0)

    def values_t(k0, cc, p):
        vt = vt_ref[cc, :, pl.ds(k0, blk)]
        return jnp.where(vrow // HEAD_DIM == p, vt, jnp.ones_like(vt))

    es = []
    for h in range(nh):
        s = raw[h] + own_bias
        m = jnp.max(s, axis=0, keepdims=True)
        m_ref[h] = m
        es.append(jnp.exp(s - m).astype(BF16))
    for h, (cc, p) in enumerate(heads):
        acc_ref[h] = _dot(values_t(q0, cc, p), es[h])

    def body(j, carry):
        k0 = pl.multiple_of(j * blk, blk)
        shift = ((j - i) * blk).astype(F32)
        to_lane = [((_iota((LANES, LANES), 0) == j) & (_iota((LANES, LANES), 1) == spare(p, 0))).astype(BF16)
                   for p in range(HEADS_PER_CHUNK)]
        skipped = [_dot(skips[h], to_lane[p]) for h, (cc, p) in enumerate(heads)]
        raw = [_dot_nt(with_key_lanes(k_ref[pl.ds(k0, blk), chunk(cc)], p, shift),
                       qs[h] + (skipped[h] * NEG).astype(BF16))
               for h, (cc, p) in enumerate(heads)]
        es, alphas = [], []
        for h in range(nh):
            s = raw[h]
            m_old = m_ref[h]
            m_new = jnp.maximum(m_old, jnp.max(s, axis=0, keepdims=True))
            m_ref[h] = m_new
            es.append(jnp.exp(s - m_new).astype(BF16))
            alphas.append(jnp.exp(m_old - m_new))
        pvs = [_dot(values_t(k0, cc, p), es[h]) for h, (cc, p) in enumerate(heads)]
        for h in range(nh):
            acc_ref[h] = alphas[h] * acc_ref[h] + pvs[h]
        return carry

    lax.fori_loop(0, i, body, 0)
    lane = _iota((1, LANES), 1)
    for cc in range(cpb):
        outs = []
        for p in range(HEADS_PER_CHUNK):
            acc = acc_ref[HEADS_PER_CHUNK * cc + p]
            den = acc[(1 - p) * HEAD_DIM:(1 - p) * HEAD_DIM + 1, :]
            outs.append((acc / den).T)
        o_ref[:, chunk(cc)] = jnp.where(lane < HEAD_DIM, outs[0], outs[1]).astype(o_ref.dtype)


def _moba_attention(q, k, v, slopes, cpb=4):
    b, s, _ = q.shape
    assert s % C_BLOCK == 0 and C_BLOCK <= 256 and all(_is_pow2(x) for x in slopes)
    nblk = s // C_BLOCK
    assert nblk <= LANES
    wide = cpb * LANES
    nh = cpb * HEADS_PER_CHUNK
    spec_q = pl.BlockSpec((None, C_BLOCK, wide), lambda bi, c, i, sl: (bi, i, c))
    spec_kv = pl.BlockSpec((None, s, wide), lambda bi, c, i, sl: (bi, 0, c))
    return pl.pallas_call(
        functools.partial(_moba_kernel, blk=C_BLOCK, nblk=nblk, topk=min(C_TOPK, nblk), cpb=cpb),
        out_shape=jax.ShapeDtypeStruct((b, s, MIX), BF16),
        grid_spec=pltpu.PrefetchScalarGridSpec(
            num_scalar_prefetch=1,
            grid=(b, N_CHUNKS // cpb, nblk),
            in_specs=[spec_q, spec_kv, spec_kv],
            out_specs=spec_q,
            scratch_shapes=[pltpu.VMEM((cpb, LANES, LANES), F32), pltpu.VMEM((cpb, LANES, s), BF16),
                            pltpu.VMEM((nh, 1, C_BLOCK), F32), pltpu.VMEM((nh, LANES, C_BLOCK), F32)],
        ),
        compiler_params=_cparams("parallel", "parallel", "arbitrary"),
        name="moba_attention",
    )(jnp.asarray(slopes, F32), q, k, v)


def _compress_kernel(xk_ref, xv_ref, wk_lo_ref, wk_hi_ref, wv_lo_ref, wv_hi_ref,
                     pk_lo_ref, pk_hi_ref, pv_lo_ref, pv_hi_ref, gain_ref, kc_ref, vc_ref):
    def compress(x_ref, w_lo_ref, w_hi_ref, p_lo_ref, p_hi_ref):
        x = x_ref[...]
        nrow = x.shape[0]
        first = _dot(x, w_lo_ref[...])
        second = pltpu.roll(_dot(x, w_hi_ref[...]), nrow - 1, 0)
        p_lo = jnp.broadcast_to(p_lo_ref[...], (8, p_lo_ref.shape[1]))
        p_hi = jnp.broadcast_to(p_hi_ref[...], (8, p_hi_ref.shape[1]))
        bias = _dot_hilo(p_lo, w_lo_ref[...]) + _dot_hilo(p_hi, w_hi_ref[...])
        return first + second + bias[0:1]

    kc = compress(xk_ref, wk_lo_ref, wk_hi_ref, pk_lo_ref, pk_hi_ref)
    kc = kc * lax.rsqrt(_head_mean_sq(kc) + EPS) * gain_ref[...]
    kc_ref[...] = kc.astype(kc_ref.dtype)
    vc_ref[...] = compress(xv_ref, wv_lo_ref, wv_hi_ref, pv_lo_ref, pv_hi_ref).astype(vc_ref.dtype)


def _compress_weights(pos, w):
    half = CMP_LEN // 2
    eye = jnp.eye(N_KV, dtype=F32)
    wd = jnp.einsum("gh,lde->lgdhe", eye, w).reshape(CMP_LEN, LANES, LANES)
    w_lo = wd[:half].reshape(half * LANES, LANES).astype(BF16)
    w_hi = wd[half:].reshape(half * LANES, LANES).astype(BF16)
    pt = jnp.tile(pos, (1, N_KV))
    return w_lo, w_hi, pt[:half].reshape(1, half * LANES), pt[half:].reshape(1, half * LANES)


def _compress(kcmp, vcmp, cmp_pos, cmp_w, k_gain):
    b, s, _ = kcmp.shape
    assert CMP_LEN == 2 * CMP_STRIDE and s % CMP_STRIDE == 0
    nrow = s // CMP_STRIDE
    wide = CMP_STRIDE * LANES
    xk = kcmp.reshape(b, nrow, wide)
    xv = vcmp.reshape(b, nrow, wide)
    wk = _compress_weights(cmp_pos[0], cmp_w[0])
    wv = _compress_weights(cmp_pos[1], cmp_w[1])
    consts = [wk[0], wk[1], wv[0], wv[1], wk[2], wk[3], wv[2], wv[3],
              jnp.tile(k_gain, N_KV).reshape(1, LANES)]
    spec_x = pl.BlockSpec((None, nrow, wide), lambda bi: (bi, 0, 0))
    spec_o = pl.BlockSpec((None, nrow, LANES), lambda bi: (bi, 0, 0))
    return pl.pallas_call(
        _compress_kernel,
        out_shape=[jax.ShapeDtypeStruct((b, nrow, LANES), BF16)] * 2,
        grid=(b,),
        in_specs=[spec_x, spec_x] + [pl.BlockSpec(a.shape, lambda bi: (0, 0)) for a in consts],
        out_specs=[spec_o, spec_o],
        compiler_params=_cparams("parallel"),
        name="nsa_compress",
    )(xk, xv, *consts)


def _nsa_cmp_kernel(q_ref, kc_ref, vc_ref, o_ref, sel_ref, *, tq, n_cmp, n_slc, topn, slopes):
    q0 = pl.program_id(1) * tq
    kc = kc_ref[...]
    vc = vc_ref[...]
    ncp = kc.shape[0]
    rows = N_CHUNKS * tq
    seg = LANES // N_KV
    qpos = q0 + _iota((rows, ncp), 0) % tq
    ncol = _iota((rows, ncp), 1)
    diff = qpos - (ncol * CMP_STRIDE + CMP_LEN - 1)
    mask = (diff >= 0) & (ncol < n_cmp)
    diff_f = diff.astype(F32)
    orow = _iota((LANES, ncp), 0)
    cst = _iota((LANES, ncp), 1) * CMP_STRIDE
    sst = (orow % seg) * SLC_BLOCK
    overlap = (cst < sst + SLC_BLOCK) & (cst + CMP_LEN > sst) & (orow % seg < n_slc)
    halves = range(HEADS_PER_CHUNK)
    hms = _half_masks(BF16)
    raw = [_dot_nt(_stack_group(q_ref, hms[p] * SCALE), kc) for p in halves]
    pcs = []
    for p in halves:
        slope = _per_chunk_rows(rows, tq, [slopes[2 * c + p] for c in range(N_CHUNKS)])
        sc = jnp.where(mask, raw[p] - slope * diff_f, NEG)
        mx = jnp.max(sc, axis=-1, keepdims=True)
        e = jnp.where(mask, jnp.exp(sc - mx), 0.0)
        den = jnp.sum(e, axis=-1, keepdims=True)
        pcs.append(e / jnp.where(den > 0, den, 1.0))
    _store_group(o_ref, [_dot(pcs[p].astype(BF16), vc) for p in halves], tq)
    p_slc = jnp.zeros((LANES, tq), F32)
    for p in halves:
        pg = pcs[p][0:tq]
        for c in range(1, N_CHUNKS):
            pg = pg + pcs[p][c * tq:(c + 1) * tq]
        ov = (overlap & (orow // seg == p)).astype(BF16)
        pg_hi, pg_lo = _split(pg)
        p_slc = p_slc + (_dot_nt(ov, pg_hi) + _dot_nt(ov, pg_lo))
    j = _iota((LANES, tq), 0) % seg
    cur = (q0 + _iota((LANES, tq), 1)) // SLC_BLOCK
    forced = (j == 0) | (j == cur) | (j == cur - 1)
    usable = (j <= cur) & (j < n_slc)
    score = jnp.where(usable, p_slc + jnp.where(forced, FORCE_BONUS, 0.0), NEG)
    rank = _rank_rows(score, j, n_slc, seg)
    sel_t = jnp.where((rank < topn) & usable, 1.0, 0.0)
    sel_ref[...] = sel_t.T.astype(sel_ref.dtype)


def _nsa_cmp(q, kc, vc, slopes, tq=128):
    b, s, _ = q.shape
    ncp = kc.shape[1]
    n_slc = s // SLC_BLOCK
    assert n_slc <= LANES // N_KV and ncp % LANES == 0
    kern = functools.partial(_nsa_cmp_kernel, tq=tq, n_cmp=ncp - 1, n_slc=n_slc,
                             topn=min(SLC_TOPN, n_slc), slopes=slopes)
    spec_c = pl.BlockSpec((None, ncp, LANES), lambda bi, n: (bi, 0, 0))
    return pl.pallas_call(
        kern,
        out_shape=[jax.ShapeDtypeStruct((b, s, MIX), BF16), jax.ShapeDtypeStruct((b, s, LANES), BF16)],
        grid=(b, s // tq),
        in_specs=[pl.BlockSpec((None, tq, MIX), lambda bi, n: (bi, n, 0)), spec_c, spec_c],
        out_specs=[pl.BlockSpec((None, tq, MIX), lambda bi, n: (bi, n, 0)),
                   pl.BlockSpec((None, tq, LANES), lambda bi, n: (bi, n, 0))],
        compiler_params=_cparams("parallel", "parallel"),
        name="nsa_compressed",
    )(q, kc, vc)


def _nsa_slc_kernel(q_ref, k_ref, v_ref, sel_ref, o_ref, vt_ref, m_ref, acc_ref, *, tq, tk, slopes):
    n = pl.program_id(1)
    q0 = n * tq
    rows = N_CHUNKS * tq
    seg = LANES // N_KV
    halves = range(HEADS_PER_CHUNK)
    skip = 1.0 - sel_ref[...]
    nsub = tk // SLC_BLOCK
    lane_q = _iota((tq, LANES), 1)
    row_q = _iota((tq, LANES), 0).astype(F32)
    lane_k = _iota((tk, LANES), 1)
    off_k = _iota((tk, LANES), 0)
    spare = lambda p, n: (1 - p) * HEAD_DIM + n
    hms = _half_masks(BF16)
    qs = []
    for p in halves:
        extra = jnp.concatenate(
            [jnp.where(lane_q == spare(p, 0), -slopes[2 * c + p] * row_q,
                       jnp.where((lane_q == spare(p, 1)) | (lane_q == spare(p, 2)), slopes[2 * c + p], 0.0))
             for c in range(N_CHUNKS)], axis=0)
        qs.append(_stack_group(q_ref, hms[p] * SCALE) + extra.astype(BF16))
    @pl.when(n == 0)
    def _():
        vt_ref[...] = v_ref[...].astype(F32).T.astype(BF16)

    for p in halves:
        m_ref[p] = jnp.full((1, rows), 0.1 * NEG, F32)
        acc_ref[p] = jnp.zeros((LANES, rows), F32)

    n_grp = 2
    grp = rows // n_grp
    lanes = lambda g: slice(g * grp, (g + 1) * grp)
    chains = [(p, g) for p in halves for g in range(n_grp)]

    def tile(kt, diagonal):
        k0 = pl.multiple_of(kt * tk, tk)
        kj = k_ref[pl.ds(k0, tk), :]
        vt = vt_ref[:, pl.ds(k0, tk)]
        shift = (k0 - q0).astype(F32).astype(BF16)
        raw = []
        for p in halves:
            in_sub = (lane_k >= spare(p, 3)) & (lane_k < spare(p, 3) + nsub)
            kp = jnp.where(lane_k == spare(p, 0), 1.0,
                           jnp.where(lane_k == spare(p, 1), off_k.astype(F32).astype(BF16),
                                     jnp.where(lane_k == spare(p, 2), shift,
                                               jnp.where(in_sub, jnp.where(off_k // SLC_BLOCK == lane_k - spare(p, 3),
                                                                           1.0, 0.0).astype(BF16), kj))))
            src = _iota((LANES, LANES), 0) - (p * seg + k0 // SLC_BLOCK)
            dst = _iota((LANES, LANES), 1) - spare(p, 3)
            to_lane = ((src == dst) & (dst >= 0) & (dst < nsub)).astype(BF16)
            bias = (_dot(skip, to_lane) * NEG).astype(BF16)
            qp = qs[p] + jnp.concatenate([bias] * N_CHUNKS, axis=0)
            raw += [_dot_nt(kp, qp[lanes(g)]) for g in range(n_grp)]
        if diagonal:
            future = jnp.where(_iota((tk, tq), 1) + (q0 - k0) >= _iota((tk, tq), 0), 0.0, NEG)
            future = jnp.concatenate([future] * (grp // tq), axis=1)
        es, alphas = [], []
        for p, g in chains:
            s = raw[p * n_grp + g] + future if diagonal else raw[p * n_grp + g]
            m_old = m_ref[p, :, lanes(g)]
            m_new = jnp.maximum(m_old, jnp.max(s, axis=0, keepdims=True))
            m_ref[p, :, lanes(g)] = m_new
            es.append(jnp.exp(s - m_new).astype(BF16))
            alphas.append(jnp.exp(m_old - m_new))
        row = _iota((LANES, tk), 0)
        vts = [jnp.where(row // HEAD_DIM == p, vt, jnp.ones_like(vt)) for p in halves]
        pvs = [_dot(vts[p], es[p * n_grp + g]) for p, g in chains]
        for (p, g), alpha, pv in zip(chains, alphas, pvs):
            acc_ref[p, :, lanes(g)] = alpha * acc_ref[p, :, lanes(g)] + pv

    def body(kt, carry):
        tile(kt, False)
        return carry

    n_past = q0 // tk
    lax.fori_loop(0, n_past, body, 0)
    tile(n_past, True)
    outs = []
    for p in halves:
        acc = acc_ref[p]
        den = acc[(1 - p) * HEAD_DIM:(1 - p) * HEAD_DIM + 1, :]
        outs.append((acc / den).T)
    _store_group(o_ref, outs, tq)


def _nsa_slc(q, k, v, sel, slopes, tq=256, tk=256):
    b, s, _ = q.shape
    assert tk % tq == 0 and tq % SLC_BLOCK == 0 and s % tk == 0 and tk <= 256
    assert all(_is_pow2(x) for x in slopes)
    rows = N_CHUNKS * tq
    spec_q = pl.BlockSpec((None, tq, MIX), lambda bi, n: (bi, n, 0))
    spec_kv = pl.BlockSpec((None, s, LANES), lambda bi, n: (bi, 0, 0))
    return pl.pallas_call(
        functools.partial(_nsa_slc_kernel, tq=tq, tk=tk, slopes=slopes),
        out_shape=jax.ShapeDtypeStruct((b, s, MIX), BF16),
        grid=(b, s // tq),
        in_specs=[spec_q, spec_kv, spec_kv, pl.BlockSpec((None, tq, LANES), lambda bi, n: (bi, n, 0))],
        out_specs=spec_q,
        scratch_shapes=[pltpu.VMEM((LANES, s), BF16), pltpu.VMEM((N_KV, 1, rows), F32),
                        pltpu.VMEM((N_KV, LANES, rows), F32)],
        compiler_params=_cparams("parallel", "arbitrary"),
        name="nsa_selected",
    )(q, k, v, sel)


def _odd_attn(x2, b, s, norm_g, w_in, c_q_norm, c_k_norm, d_q_norm, d_k_norm, cmp_pos, cmp_w, w_out):
    kvw = N_KV * HEAD_DIM
    n_gate = N_HEADS * 3
    qd0 = 3 * MIX
    w = jnp.concatenate([w_in[:, :qd0], _gqa_cols(w_in[:, qd0:qd0 + MIX]), w_in[:, qd0 + MIX:],
                         jnp.zeros((w_in.shape[0], LANES - n_gate), F32)], axis=1).astype(BF16)
    n = w.shape[1]
    ones = lambda width: jnp.ones((width,), F32)
    gain = jnp.concatenate([
        jnp.tile(c_q_norm, N_HEADS), jnp.tile(c_k_norm, N_HEADS), ones(MIX), jnp.tile(d_q_norm, N_HEADS),
        ones(2 * kvw), jnp.tile(d_k_norm[1], N_KV), ones(kvw), jnp.tile(d_k_norm[2], N_KV), ones(kvw),
        ones(LANES)]).reshape(1, n)
    plan, col = [], 0
    for width, op in ((MIX, "norm"), (MIX, "norm"), (MIX, None), (MIX, "norm"), (kvw, None), (kvw, None),
                      (kvw, "norm"), (kvw, None), (kvw, "norm"), (kvw, None), (LANES, "sigmoid")):
        plan.append((col, width, op))
        col += width
    outs = _proj(x2, norm_g, w, gain, tuple(plan), [BF16] * 10 + [F32])
    r3 = lambda a: a.reshape(b, s, a.shape[-1])
    qc, kc, vc, qd, kcmp, vcmp, kslc, vslc, kwin, vwin = [r3(a) for a in outs[:10]]
    gates = outs[10]
    slopes = _alibi(N_HEADS)
    gslopes = [slopes[h] for h in GQA_PERM]
    oc = _moba_attention(qc, kc, vc, slopes)
    k_cmp, v_cmp = _compress(kcmp, vcmp, cmp_pos, cmp_w, d_k_norm[0])
    o_cmp, sel = _nsa_cmp(qd, k_cmp, v_cmp, gslopes)
    o_slc = _nsa_slc(qd, kslc, vslc, sel, gslopes)
    o_win = _band_attention(qd, kwin, vwin, window=D_WINDOW, slopes=gslopes)
    t = b * s
    w1 = w_out[:MIX].astype(BF16)
    w2 = _gqa_rows(w_out[MIX:]).astype(BF16)
    flat = lambda a: a.reshape(t, a.shape[-1])
    return [flat(oc), flat(o_cmp), flat(o_slc), flat(o_win), gates], w1, w2


def kernel(x, ev_norm, ev_w_in, ev_q_norm, ev_k_norm, ev_sink, ev_w_out, od_norm, od_w_in, od_c_q_norm,
           od_c_k_norm, od_d_q_norm, od_d_k_norm, od_cmp_pos, od_cmp_w, od_w_out, moe_norm, moe_w_grp,
           moe_b_grp, moe_w_exp, moe_b_exp, moe_w_gate, moe_w_up, moe_w_down):
    b, s, d = x.shape
    x2 = x.reshape(b * s, d)
    depth = moe_norm.shape[0]
    moe_w_gate, moe_w_up, moe_w_down = (w.astype(BF16) for w in (moe_w_gate, moe_w_up, moe_w_down))
    for layer in range(depth):
        i = layer // 2
        if layer % 2 == 0:
            attn, w1, w2 = _even_attn(x2, b, s, ev_norm[i], ev_w_in[i], ev_q_norm[i], ev_k_norm[i],
                                      ev_sink[i], ev_w_out[i])
        else:
            attn, w1, w2 = _odd_attn(x2, b, s, od_norm[i], od_w_in[i], od_c_q_norm[i], od_c_k_norm[i],
                                     od_d_q_norm[i], od_d_k_norm[i], od_cmp_pos[i], od_cmp_w[i], od_w_out[i])
        x2 = _moe_block(x2, attn, w1, w2, moe_norm[layer], moe_w_grp[layer], moe_b_grp[layer],
                        moe_w_exp[layer], moe_b_exp[layer], moe_w_gate, moe_w_up, moe_w_down, layer)
    return x2.reshape(b, s, d)
```

```python
import functools
import math

import numpy as np
import jax
import jax.numpy as jnp
from jax import lax
from jax.experimental import pallas as pl
from jax.experimental.pallas import tpu as pltpu

F32 = jnp.float32
BF16 = jnp.bfloat16

LANES = 128
HEAD_DIM = 64
HEADS_PER_CHUNK = LANES // HEAD_DIM
D_MODEL = 1024
N_HEADS = 8
N_KV = 2
GROUP = N_HEADS // N_KV
N_CHUNKS = N_HEADS // HEADS_PER_CHUNK
MIX = N_HEADS * HEAD_DIM
SCALE = 1.0 / math.sqrt(HEAD_DIM)
EPS = 1e-6
NEG = -1e30
EXP_UNDERFLOW = -104.0

A_WINDOW = 128
C_BLOCK = 256
C_TOPK = 3
CMP_LEN = 32
CMP_STRIDE = 16
SLC_BLOCK = 64
SLC_TOPN = 16
D_WINDOW = 512
FORCE_BONUS = 1000.0

N_GROUPS = 4
EXPERTS_PER_GROUP = 8
N_EXPERTS = N_GROUPS * EXPERTS_PER_GROUP
EXPERT_FF = 256

VMEM_LIMIT = 48 * 1024 * 1024
MOE_VMEM_LIMIT = 56 * 1024 * 1024

GQA_PERM = tuple(h for c in range(N_CHUNKS) for h in (c, c + GROUP))
MHA_PERM = tuple(range(N_HEADS))


def _alibi(n_heads):
    return [float(2.0 ** (-8.0 * (i + 1) / n_heads)) for i in range(n_heads)]


def _is_pow2(x):
    return math.frexp(x)[0] == 0.5


def _gqa_cols(w):
    lead = w.shape[:-1]
    w = w.reshape(*lead, N_KV, GROUP, HEAD_DIM)
    return jnp.swapaxes(w, -3, -2).reshape(*lead, MIX)


def _gqa_rows(w):
    tail = w.shape[1:]
    return jnp.swapaxes(w.reshape(N_KV, GROUP, HEAD_DIM, *tail), 0, 1).reshape(MIX, *tail)


def _cparams(*sem, vmem=VMEM_LIMIT):
    return pltpu.CompilerParams(dimension_semantics=sem, vmem_limit_bytes=vmem)


def _dot(a, b):
    return jnp.dot(a, b, preferred_element_type=F32)


def _dot_nt(a, b):
    return lax.dot_general(a, b, (((1,), (1,)), ((), ())), preferred_element_type=F32)


def _split(x):
    hi = x.astype(BF16)
    lo = (x - hi.astype(F32)).astype(BF16)
    return hi, lo


def _dot_hilo(a, b):
    hi, lo = _split(a)
    return _dot(jnp.concatenate([hi, lo], axis=1), jnp.concatenate([b, b], axis=0))


def _iota(shape, dim):
    return lax.broadcasted_iota(jnp.int32, shape, dim)


def _half_masks(dtype):
    lane = _iota((1, LANES), 1)
    return [(lane // HEAD_DIM == p).astype(dtype) for p in range(HEADS_PER_CHUNK)]


def _head_mean_sq(y):
    w = y.shape[1]
    same = (_iota((w, w), 0) // HEAD_DIM == _iota((w, w), 1) // HEAD_DIM)
    return _dot((y * y).astype(BF16), same.astype(BF16)) * (1.0 / HEAD_DIM)


def _proj_kernel(x_ref, g_ref, w_ref, gain_ref, *out_refs, plan):
    x = x_ref[...]
    ms = jnp.mean(x * x, axis=-1, keepdims=True)
    xn = (x * lax.rsqrt(ms + EPS) * g_ref[...]).astype(BF16)
    for o_ref, (col0, width, op) in zip(out_refs, plan):
        for a in range(0, width, 2 * LANES):
            wd = min(2 * LANES, width - a)
            y = _dot(xn, w_ref[:, col0 + a:col0 + a + wd])
            if op == "norm":
                y = y * lax.rsqrt(_head_mean_sq(y) + EPS) * gain_ref[:, col0 + a:col0 + a + wd]
            elif op == "sigmoid":
                y = jax.nn.sigmoid(y)
            o_ref[:, a:a + wd] = y.astype(o_ref.dtype)


def _proj(x2, norm_g, w, gain, plan, out_dtypes, tm=1024):
    t, d = x2.shape
    n = w.shape[1]
    out_shape = [jax.ShapeDtypeStruct((t, width), dt) for (_, width, _), dt in zip(plan, out_dtypes)]
    return pl.pallas_call(
        functools.partial(_proj_kernel, plan=plan),
        out_shape=out_shape,
        grid=(t // tm,),
        in_specs=[
            pl.BlockSpec((tm, d), lambda i: (i, 0)),
            pl.BlockSpec((1, d), lambda i: (0, 0)),
            pl.BlockSpec((d, n), lambda i: (0, 0)),
            pl.BlockSpec((1, n), lambda i: (0, 0)),
        ],
        out_specs=[pl.BlockSpec((tm, width), lambda i: (i, 0)) for (_, width, _) in plan],
        compiler_params=_cparams("parallel"),
        name="norm_proj",
    )(x2, norm_g.reshape(1, d), w, gain)


def _stack_group(q_ref, halfmask):
    return jnp.concatenate(
        [q_ref[:, c * LANES:(c + 1) * LANES] * halfmask for c in range(N_CHUNKS)], axis=0)


def _per_chunk_rows(rows, tq, values):
    r = _iota((rows, 1), 0) // tq
    out = jnp.full((rows, 1), values[N_CHUNKS - 1], F32)
    for c in range(N_CHUNKS - 2, -1, -1):
        out = jnp.where(r == c, values[c], out)
    return out


def _per_chunk_lanes(rows, tq, values):
    r = _iota((1, rows), 1) // tq
    out = jnp.full((1, rows), values[N_CHUNKS - 1], F32)
    for c in range(N_CHUNKS - 2, -1, -1):
        out = jnp.where(r == c, values[c], out)
    return out


def _band_kernel(*refs, tq, window, wpad, slopes, has_sink):
    if has_sink:
        sink_ref, q_ref, k_ref, v_ref, o_ref, vt_ref = refs
    else:
        q_ref, k_ref, v_ref, o_ref, vt_ref = refs
    n = pl.program_id(1)

    @pl.when(n == 0)
    def _():
        vt_ref[...] = v_ref[...].astype(F32).T.astype(BF16)

    q0 = n * tq
    kw = tq + wpad
    kstart = pl.multiple_of(jnp.maximum(q0 - wpad, 0), LANES)
    ks = k_ref[pl.ds(kstart, kw), :]
    vt = vt_ref[:, pl.ds(kstart, kw)]
    rows = N_CHUNKS * tq
    halves = range(HEADS_PER_CHUNK)
    hms = _half_masks(BF16)
    spare = lambda p, j: (1 - p) * HEAD_DIM + j
    lane_q = _iota((tq, LANES), 1)
    row_q = _iota((tq, LANES), 0).astype(F32)
    lane_k = _iota((kw, LANES), 1)
    off_k = _iota((kw, LANES), 0)
    fine = (off_k % LANES).astype(F32).astype(BF16)
    coarse = ((off_k // LANES) * LANES + (kstart - q0)).astype(F32).astype(BF16)
    diff = _iota((kw, tq), 1) + (q0 - kstart) - _iota((kw, tq), 0)
    band = jnp.where((diff >= 0) & (diff < window), 0.0, NEG)
    band = jnp.concatenate([band] * N_CHUNKS, axis=1)
    raw = []
    for p in halves:
        extra = jnp.concatenate(
            [jnp.where(lane_q == spare(p, 0), -slopes[2 * c + p] * row_q,
                       jnp.where((lane_q == spare(p, 1)) | (lane_q == spare(p, 2)), slopes[2 * c + p], 0.0))
             for c in range(N_CHUNKS)], axis=0)
        qp = _stack_group(q_ref, hms[p] * SCALE) + extra.astype(BF16)
        kp = jnp.where(lane_k == spare(p, 0), 1.0,
                       jnp.where(lane_k == spare(p, 1), fine, jnp.where(lane_k == spare(p, 2), coarse, ks)))
        raw.append(_dot_nt(kp, qp))
    es, sinks = [], []
    for p in halves:
        s = raw[p] + band
        mx = jnp.max(s, axis=0, keepdims=True)
        if has_sink:
            sk = _per_chunk_lanes(rows, tq, [sink_ref[2 * c + p] for c in range(N_CHUNKS)])
            mx = jnp.maximum(mx, sk)
            sinks.append(jnp.exp(sk - mx))
        es.append(jnp.exp(s - mx).astype(BF16))
    vrow = _iota((LANES, kw), 0)
    outs = []
    for p in halves:
        acc = _dot(jnp.where(vrow // HEAD_DIM == p, vt, jnp.ones_like(vt)), es[p])
        den = acc[(1 - p) * HEAD_DIM:(1 - p) * HEAD_DIM + 1, :]
        if has_sink:
            den = den + sinks[p]
        outs.append((acc / den).T)
    _store_group(o_ref, outs, tq)


def _store_group(o_ref, os_, tq):
    lane = _iota((1, LANES), 1)
    for c in range(N_CHUNKS):
        o = jnp.where(lane < HEAD_DIM, os_[0][c * tq:(c + 1) * tq], os_[1][c * tq:(c + 1) * tq])
        o_ref[:, c * LANES:(c + 1) * LANES] = o.astype(o_ref.dtype)


def _band_attention(q, k, v, *, window, slopes, sink=None, tq=256):
    b, s, _ = q.shape
    wpad = -(-window // LANES) * LANES
    assert s >= tq + wpad and s % tq == 0 and tq <= 256 and all(_is_pow2(x) for x in slopes)
    kern = functools.partial(_band_kernel, tq=tq, window=window, wpad=wpad,
                             slopes=slopes, has_sink=sink is not None)
    in_specs = [
        pl.BlockSpec((None, tq, MIX), lambda bi, n: (bi, n, 0)),
        pl.BlockSpec((None, s, LANES), lambda bi, n: (bi, 0, 0)),
        pl.BlockSpec((None, s, LANES), lambda bi, n: (bi, 0, 0)),
    ]
    args = [q, k, v]
    if sink is not None:
        in_specs = [pl.BlockSpec(memory_space=pltpu.SMEM)] + in_specs
        args = [sink] + args
    return pl.pallas_call(
        kern,
        out_shape=jax.ShapeDtypeStruct((b, s, MIX), BF16),
        grid=(b, s // tq),
        in_specs=in_specs,
        out_specs=pl.BlockSpec((None, tq, MIX), lambda bi, n: (bi, n, 0)),
        scratch_shapes=[pltpu.VMEM((LANES, s), BF16)],
        compiler_params=_cparams("parallel", "arbitrary"),
        name="band_attention",
    )(*args)


def _stick_kernel(q_ref, k_ref, v_ref, o_ref, acc_ref, run_ref, *, tq, cpb):
    i = pl.program_id(2)
    q0 = pl.multiple_of(i * tq, tq)
    hms = _half_masks(BF16)
    heads = [(cc, p) for cc in range(cpb) for p in range(HEADS_PER_CHUNK)]
    qs = [q_ref[:, cc * LANES:(cc + 1) * LANES] * (hms[p] * SCALE) for cc, p in heads]
    upper = (_iota((2 * tq, tq), 0) % tq > _iota((2 * tq, tq), 1)).astype(BF16)

    def suffix_sum(x):
        hi, lo = _split(x)
        return _dot(jnp.concatenate([hi, lo], axis=1), upper)

    def block(kstart, diag):
        if diag:
            causal = _iota((tq, tq), 1) < _iota((tq, tq), 0)
        kjs = [k_ref[pl.ds(kstart, tq), cc * LANES:(cc + 1) * LANES] for cc in range(cpb)]
        vjs = [v_ref[pl.ds(kstart, tq), cc * LANES:(cc + 1) * LANES] for cc in range(cpb)]
        zs = [_dot_nt(qs[h], kjs[cc]) for h, (cc, p) in enumerate(heads)]
        lss, lks = [], []
        for z in zs:
            ls = jnp.minimum(z, 0.0) - jnp.log(1.0 + jnp.exp(-jnp.abs(z)))
            lk = ls - z
            if diag:
                lk = jnp.where(causal, lk, 0.0)
            lss.append(ls)
            lks.append(lk)
        sufs = [suffix_sum(lk) for lk in lks]
        ws = []
        for h in range(len(heads)):
            if diag:
                a = jnp.where(causal, jnp.exp(lss[h] + sufs[h]), 0.0)
            else:
                a = jnp.exp(lss[h] + sufs[h] + run_ref[h])
            ws.append(a.astype(BF16))
        pvs = [_dot(ws[h], vjs[cc]) for h, (cc, p) in enumerate(heads)]
        for h in range(len(heads)):
            rowsum = jnp.sum(lks[h], axis=-1, keepdims=True)
            if diag:
                acc_ref[h] = pvs[h]
                run_ref[h] = rowsum
            else:
                acc_ref[h] += pvs[h]
                run_ref[h] += rowsum

    block(q0, True)

    def weights_alive():
        run = run_ref[0]
        for h in range(1, len(heads)):
            run = jnp.maximum(run, run_ref[h])
        return jnp.max(run) > EXP_UNDERFLOW

    def body(carry):
        t, _ = carry
        block(pl.multiple_of((i - 1 - t) * tq, tq), False)
        return t + 1, weights_alive()

    lax.while_loop(lambda c: (c[0] < i) & c[1], body, (jnp.int32(0), weights_alive()))
    lane = _iota((1, LANES), 1)
    for cc in range(cpb):
        o = jnp.where(lane < HEAD_DIM, acc_ref[HEADS_PER_CHUNK * cc], acc_ref[HEADS_PER_CHUNK * cc + 1])
        o_ref[:, cc * LANES:(cc + 1) * LANES] = o.astype(o_ref.dtype)


def _stick_attention(q, k, v, tq=256, cpb=4):
    b, s, _ = q.shape
    wide = cpb * LANES
    n_heads = cpb * HEADS_PER_CHUNK
    spec_q = pl.BlockSpec((None, tq, wide), lambda bi, c, i: (bi, i, c))
    spec_kv = pl.BlockSpec((None, s, wide), lambda bi, c, i: (bi, 0, c))
    return pl.pallas_call(
        functools.partial(_stick_kernel, tq=tq, cpb=cpb),
        out_shape=jax.ShapeDtypeStruct((b, s, MIX), BF16),
        grid=(b, N_CHUNKS // cpb, s // tq),
        in_specs=[spec_q, spec_kv, spec_kv],
        out_specs=spec_q,
        scratch_shapes=[pltpu.VMEM((n_heads, tq, LANES), F32), pltpu.VMEM((n_heads, tq, 1), F32)],
        compiler_params=_cparams("parallel", "parallel", "parallel"),
        name="stick_breaking",
    )(q, k, v)


def _route(logits):
    lane = _iota(logits.shape, 1)
    lane_f = lane.astype(F32)
    ninf = -jnp.inf
    is_g = (lane >= N_EXPERTS) & (lane < N_EXPERTS + N_GROUPS)
    gmax = jnp.max(jnp.where(is_g, logits, ninf), axis=-1, keepdims=True)
    gidx = jnp.min(jnp.where(is_g & (logits == gmax), lane_f - N_EXPERTS, 1e9), axis=-1, keepdims=True)
    p_g = 1.0 / jnp.sum(jnp.where(is_g, jnp.exp(logits - gmax), 0.0), axis=-1, keepdims=True)
    in_grp = (lane < N_EXPERTS) & ((lane // EXPERTS_PER_GROUP).astype(F32) == gidx)
    le = jnp.where(in_grp, logits, ninf)
    m1 = jnp.max(le, axis=-1, keepdims=True)
    i1 = jnp.min(jnp.where(le == m1, lane_f, 1e9), axis=-1, keepdims=True)
    le2 = jnp.where(lane_f == i1, ninf, le)
    m2 = jnp.max(le2, axis=-1, keepdims=True)
    i2 = jnp.min(jnp.where(le2 == m2, lane_f, 1e9), axis=-1, keepdims=True)
    e2 = jnp.exp(m2 - m1)
    w1 = p_g / (1.0 + e2)
    w2 = p_g * e2 / (1.0 + e2)
    return jnp.where(lane_f == i1, w1, 0.0) + jnp.where(lane_f == i2, w2, 0.0)


def _gate_expand(branch):
    r = _iota((LANES, MIX), 0)
    col = _iota((LANES, MIX), 1)
    head = col // LANES + GROUP * ((col % LANES) // HEAD_DIM)
    return (r == 3 * head + branch).astype(BF16)


def _out_kernel(*refs, nsa):
    if nsa:
        (x_ref, o1_ref, ocmp_ref, oslc_ref, owin_ref, gates_ref, w1_ref, w2_ref,
         ng_ref, wr_ref, br_ref, x1_ref, h_ref, comb_ref) = refs
        g = gates_ref[...]
        o2 = (_dot_hilo(g, _gate_expand(0)) * ocmp_ref[...]
              + _dot_hilo(g, _gate_expand(1)) * oslc_ref[...]
              + _dot_hilo(g, _gate_expand(2)) * owin_ref[...]).astype(BF16)
    else:
        (x_ref, o1_ref, o2_ref, w1_ref, w2_ref,
         ng_ref, wr_ref, br_ref, x1_ref, h_ref, comb_ref) = refs
        o2 = o2_ref[...]
    x1 = x_ref[...] + _dot(o1_ref[...], w1_ref[...]) + _dot(o2, w2_ref[...])
    x1_ref[...] = x1
    ms = jnp.mean(x1 * x1, axis=-1, keepdims=True)
    h = x1 * lax.rsqrt(ms + EPS) * ng_ref[...]
    h_ref[...] = h.astype(BF16)
    h_hi, h_lo = _split(h)
    w_hi, w_lo = _split(wr_ref[...])
    both = _dot(h_hi, jnp.concatenate([w_hi, w_lo], axis=1))
    logits = both[:, :LANES] + (both[:, LANES:] + _dot(h_lo, w_hi)) + br_ref[...]
    comb_ref[...] = _route(logits)


def _out_proj_route(x2, attn, w1, w2, moe_g, w_route, b_route, tm=1024):
    t, d = x2.shape
    nsa = len(attn) > 2
    row = lambda width: pl.BlockSpec((tm, width), lambda i: (i, 0))
    full = lambda a: pl.BlockSpec(a.shape, lambda i: (0, 0))
    consts = [w1, w2, moe_g.reshape(1, d), w_route, b_route]
    return pl.pallas_call(
        functools.partial(_out_kernel, nsa=nsa),
        out_shape=[jax.ShapeDtypeStruct((t, d), F32), jax.ShapeDtypeStruct((t, d), BF16),
                   jax.ShapeDtypeStruct((t, LANES), F32)],
        grid=(t // tm,),
        in_specs=[row(d)] + [row(a.shape[1]) for a in attn] + [full(a) for a in consts],
        out_specs=[row(d), row(d), row(LANES)],
        compiler_params=_cparams("parallel"),
        name="out_proj_route",
    )(x2, *attn, *consts)


MOE_CHUNK = 256
MOE_FIRST_CHUNK = MOE_CHUNK + 32
SEG_ALIGN = 16


def _sorted_rows(tm):
    return -(-(tm + N_GROUPS * SEG_ALIGN) // LANES) * LANES


def _moe_sort(h_ref, comb_ref, hs_ref, cs_ref, pos_ref, acc_ref, seg_ref):
    tm = h_ref.shape[0]
    rows_s = hs_ref.shape[0]
    comb = comb_ref[...]
    used = jnp.where(comb > 0, 1.0, 0.0).astype(BF16)
    of_group = ((_iota((LANES, LANES), 0) // EXPERTS_PER_GROUP == _iota((LANES, LANES), 1))
                & (_iota((LANES, LANES), 0) < N_EXPERTS)).astype(BF16)
    member = jnp.where(_dot(used, of_group) > 0.5, 1.0, 0.0)
    member_t = member.T
    earlier = (_iota((tm, tm), 0) < _iota((tm, tm), 1)).astype(BF16)
    rank_t = _dot(member_t.astype(BF16), earlier)
    count = jnp.sum(member_t, axis=1, keepdims=True)
    counts = [count[g:g + 1, :] for g in range(N_GROUPS)]
    starts = [jnp.zeros((1, 1), F32)]
    for g in range(1, N_GROUPS):
        padded = jnp.floor((counts[g - 1] + (SEG_ALIGN - 1)) * (1.0 / SEG_ALIGN)) * SEG_ALIGN
        starts.append(starts[g - 1] + padded)
    grow = _iota((LANES, 1), 0)
    start_col = jnp.zeros((LANES, 1), F32)
    for g in range(1, N_GROUPS):
        start_col = jnp.where(grow == g, starts[g], start_col)
    pos_t = jnp.sum(member_t * (start_col + rank_t), axis=0, keepdims=True)
    used_rows = _sorted_rows(tm)
    place = jnp.where(_iota((used_rows, tm), 0).astype(F32) == pos_t, 1.0, 0.0).astype(BF16)
    hs_ref[:used_rows] = _dot(place, h_ref[...]).astype(BF16)
    hs_ref[used_rows:] = jnp.zeros((rows_s - used_rows, hs_ref.shape[1]), BF16)
    c_hi = comb.astype(BF16)
    rest = comb - c_hi.astype(F32)
    c_mid = rest.astype(BF16)
    c_lo = (rest - c_mid.astype(F32)).astype(BF16)
    cs_ref[:used_rows] = _dot(place, c_hi) + (_dot(place, c_mid) + _dot(place, c_lo))
    cs_ref[used_rows:] = jnp.zeros((rows_s - used_rows, LANES), F32)
    pos_ref[...] = jnp.broadcast_to(pos_t, (LANES, tm)).T
    acc_ref[...] = jnp.zeros(acc_ref.shape, F32)
    for g in range(N_GROUPS):
        seg_ref[g] = starts[g][0, 0].astype(jnp.int32)
        seg_ref[N_GROUPS + g] = counts[g][0, 0].astype(jnp.int32)


def _moe_kernel(h_ref, comb_ref, x1_ref, wg_ref, wu_ref, wd_ref, o_ref,
                hs_ref, cs_ref, pos_ref, acc_ref, seg_ref, *, per_step):
    step = pl.program_id(1)
    tm = h_ref.shape[0]
    rows_s = hs_ref.shape[0]

    @pl.when(step == 0)
    def _():
        _moe_sort(h_ref, comb_ref, hs_ref, cs_ref, pos_ref, acc_ref, seg_ref)

    first = step * per_step
    group = first // EXPERTS_PER_GROUP
    start = seg_ref[group]
    count = seg_ref[N_GROUPS + group]
    sizes = [MOE_FIRST_CHUNK, MOE_CHUNK // 2, MOE_CHUNK // 2] + [MOE_CHUNK] * (tm // MOE_CHUNK - 2)
    assert sum(sizes) >= tm and max(sizes) <= MOE_FIRST_CHUNK
    begin = 0
    for size in sizes:
        @pl.when(begin < count)
        def _(begin=begin, size=size):
            off = pl.multiple_of(start + begin, SEG_ALIGN)
            hs = hs_ref[pl.ds(off, size), :]
            cs = cs_ref[pl.ds(off, size), :]
            lane = _iota((size, LANES), 1)
            acts = []
            for j in range(per_step):
                g = _dot(hs, wg_ref[j])
                u = _dot(hs, wu_ref[j])
                c = jnp.sum(jnp.where(lane == first + j, cs, 0.0), axis=-1, keepdims=True)
                acts.append((c * (g * jax.nn.sigmoid(g) * u)).astype(BF16))
            w_down = wd_ref[...].reshape(per_step * wd_ref.shape[1], wd_ref.shape[2])
            acc_ref[pl.ds(off, size), :] += _dot(jnp.concatenate(acts, axis=1), w_down)
        begin += size

    @pl.when(step == pl.num_programs(1) - 1)
    def _():
        pos = pos_ref[...]
        used_rows = _sorted_rows(tm)
        back = jnp.concatenate(
            [jnp.where(pos == (_iota((tm, LANES), 1) + blk * LANES).astype(F32), 1.0, 0.0).astype(BF16)
             for blk in range(used_rows // LANES)], axis=1)
        o_ref[...] = x1_ref[...] + _dot(back, acc_ref[:used_rows].astype(BF16))


def _moe(h, comb, x1, w_gate, w_up, w_down, layer, tm=1024, per_step=4):
    t, d = h.shape
    _, n_exp, _, ff = w_gate.shape
    assert t % tm == 0 and EXPERTS_PER_GROUP % per_step == 0 and tm % MOE_CHUNK == 0
    rows_s = _sorted_rows(tm) + -(-MOE_FIRST_CHUNK // LANES) * LANES
    return pl.pallas_call(
        functools.partial(_moe_kernel, per_step=per_step),
        out_shape=jax.ShapeDtypeStruct((t, d), F32),
        grid=(t // tm, n_exp // per_step),
        scratch_shapes=[pltpu.VMEM((rows_s, d), BF16), pltpu.VMEM((rows_s, LANES), F32),
                        pltpu.VMEM((tm, LANES), F32), pltpu.VMEM((rows_s, d), F32),
                        pltpu.SMEM((2 * N_GROUPS,), jnp.int32)],
        in_specs=[
            pl.BlockSpec((tm, d), lambda i, e: (i, 0)),
            pl.BlockSpec((tm, LANES), lambda i, e: (i, 0)),
            pl.BlockSpec((tm, d), lambda i, e: (i, 0)),
            pl.BlockSpec((None, per_step, d, ff), lambda i, e: (layer, e, 0, 0)),
            pl.BlockSpec((None, per_step, d, ff), lambda i, e: (layer, e, 0, 0)),
            pl.BlockSpec((None, per_step, ff, d), lambda i, e: (layer, e, 0, 0)),
        ],
        out_specs=pl.BlockSpec((tm, d), lambda i, e: (i, 0)),
        compiler_params=_cparams("parallel", "arbitrary", vmem=MOE_VMEM_LIMIT),
        name="moe_experts",
    )(h, comb, x1, w_gate, w_up, w_down)


def _moe_block(x2, attn, w1, w2, moe_g, w_grp, b_grp, w_exp, b_exp, w_gate, w_up, w_down, layer):
    d = x2.shape[1]
    pad = LANES - N_EXPERTS - N_GROUPS
    w_route = jnp.concatenate([w_exp, w_grp, jnp.zeros((d, pad), F32)], axis=1)
    b_route = jnp.concatenate([b_exp, b_grp, jnp.zeros((pad,), F32)]).reshape(1, LANES)
    x1, h, comb = _out_proj_route(x2, attn, w1, w2, moe_g, w_route, b_route)
    return _moe(h, comb, x1, w_gate, w_up, w_down, layer)


def _even_attn(x2, b, s, norm_g, w_in, q_norm, k_norm, sink, w_out):
    w = jnp.concatenate([_gqa_cols(w_in[:, :MIX]), w_in[:, MIX:]], axis=1).astype(BF16)
    n = w.shape[1]
    kvw = N_KV * HEAD_DIM
    gain = jnp.concatenate([jnp.tile(q_norm, N_HEADS), jnp.tile(k_norm, N_KV),
                            jnp.ones((n - MIX - kvw,), F32)]).reshape(1, n)
    plan, col = [], 0
    for width, op in ((MIX, "norm"), (kvw, "norm"), (kvw, None), (MIX, None), (MIX, None), (MIX, None)):
        plan.append((col, width, op))
        col += width
    qa, ka, va, qb, kb, vb = _proj(x2, norm_g, w, gain, tuple(plan), [BF16] * 6)
    r3 = lambda a: a.reshape(b, s, a.shape[-1])
    slopes = _alibi(N_HEADS)
    oa = _band_attention(r3(qa), r3(ka), r3(va), window=A_WINDOW,
                         slopes=[slopes[h] for h in GQA_PERM], sink=sink[np.asarray(GQA_PERM)])
    ob = _stick_attention(r3(qb), r3(kb), r3(vb))
    w1 = _gqa_rows(w_out[:MIX]).astype(BF16)
    w2 = w_out[MIX:].astype(BF16)
    t = b * s
    return [oa.reshape(t, MIX), ob.reshape(t, MIX)], w1, w2


def _rank_rows(score, j, n, seg):
    row = _iota(score.shape, 0)
    rank = jnp.zeros(score.shape, F32)
    for jj in range(n):
        other = score[jj:jj + 1, :]
        for sgm in range(1, score.shape[0] // seg):
            other = jnp.where(row // seg == sgm, score[sgm * seg + jj:sgm * seg + jj + 1, :], other)
        beats = (other > score) | ((other == score) & (jj < j))
        rank = rank + jnp.where(beats, 1.0, 0.0)
    return rank


def _ones_beside(v, p):
    lane = _iota((1, LANES), 1)
    return jnp.where(lane // HEAD_DIM == p, v, jnp.ones_like(v))


def _normalize(acc):
    return acc / pltpu.roll(acc, HEAD_DIM, 1)


def _moba_kernel(slope_ref, q_ref, k_ref, v_ref, o_ref, kmean_ref, vt_ref, m_ref, acc_ref, *,
                 blk, nblk, topk, cpb):
    g = pl.program_id(1)
    i = pl.program_id(2)
    s_len = k_ref.shape[0]
    heads = [(cc, p) for cc in range(cpb) for p in range(HEADS_PER_CHUNK)]
    nh = len(heads)
    chunk = lambda cc: slice(cc * LANES, (cc + 1) * LANES)

    @pl.when(i == 0)
    def _():
        member = (_iota((LANES, s_len), 1) // blk == _iota((LANES, s_len), 0)).astype(BF16)
        for cc in range(cpb):
            kmean_ref[cc] = _dot(member, k_ref[:, chunk(cc)]) * (1.0 / blk)
            vt_ref[cc] = v_ref[:, chunk(cc)].astype(F32).T.astype(BF16)

    hms = _half_masks(BF16)
    q0 = pl.multiple_of(i * blk, blk)
    rel = (_iota((blk, blk), 0) - _iota((blk, blk), 1)).astype(F32)
    slopes = [slope_ref[HEADS_PER_CHUNK * (g * cpb + cc) + p] for cc, p in heads]
    kms = [_split(kmean_ref[cc]) for cc in range(cpb)]
    gates = [_dot_nt(kms[cc][0], q_ref[:, chunk(cc)] * hms[p]) + _dot_nt(kms[cc][1], q_ref[:, chunk(cc)] * hms[p])
             for cc, p in heads]
    lane = _iota((blk, LANES), 1)
    off_f = _iota((blk, LANES), 0).astype(F32)
    off_b = off_f.astype(BF16)
    spare = lambda p, n: (1 - p) * HEAD_DIM + n

    def with_key_lanes(kj, p, shift):
        return jnp.where(lane == spare(p, 0), 1.0,
                         jnp.where(lane == spare(p, 1), off_b,
                                   jnp.where(lane == spare(p, 2), shift.astype(BF16), kj)))

    qs = []
    for h, (cc, p) in enumerate(heads):
        extra = jnp.where(lane == spare(p, 0), -slopes[h] * off_f,
                          jnp.where((lane == spare(p, 1)) | (lane == spare(p, 2)), slopes[h], 0.0))
        qs.append(q_ref[:, chunk(cc)] * (hms[p] * SCALE) + extra.astype(BF16))
    zero = jnp.zeros((), F32)
    raw = [_dot_nt(with_key_lanes(k_ref[pl.ds(q0, blk), chunk(cc)], p, zero), qs[h])
           for h, (cc, p) in enumerate(heads)]
    nrow = -(-nblk // 8) * 8
    blk_id = _iota((nrow, blk), 0)
    skips = []
    for h in range(nh):
        gate = jnp.where(blk_id < i, gates[h][:nrow], NEG)
        rank = _rank_rows(gate, blk_id, nblk, nrow)
        skip_t = jnp.where((rank < topk) & (blk_id < i), 0.0, 1.0)
        skip_t = jnp.concatenate([skip_t, jnp.ones((LANES - nrow, blk), F32)], axis=0)
        skips.append(skip_t.T.astype(BF16))
    own_bias = jnp.where(rel <= 0, 0.0, NEG)
    vrow = _iota((LANES, blk), 0)

    def values_t(k0, cc, p):
        vt = vt_ref[cc, :, pl.ds(k0, blk)]
        return jnp.where(vrow // HEAD_DIM == p, vt, jnp.ones_like(vt))

    es = []
    for h in range(nh):
        s = raw[h] + own_bias
        m = jnp.max(s, axis=0, keepdims=True)
        m_ref[h] = m
        es.append(jnp.exp(s - m).astype(BF16))
    for h, (cc, p) in enumerate(heads):
        acc_ref[h] = _dot(values_t(q0, cc, p), es[h])

    def body(j, carry):
        k0 = pl.multiple_of(j * blk, blk)
        shift = ((j - i) * blk).astype(F32)
        to_lane = [((_iota((LANES, LANES), 0) == j) & (_iota((LANES, LANES), 1) == spare(p, 0))).astype(BF16)
                   for p in range(HEADS_PER_CHUNK)]
        skipped = [_dot(skips[h], to_lane[p]) for h, (cc, p) in enumerate(heads)]
        raw = [_dot_nt(with_key_lanes(k_ref[pl.ds(k0, blk), chunk(cc)], p, shift),
                       qs[h] + (skipped[h] * NEG).astype(BF16))
               for h, (cc, p) in enumerate(heads)]
        es, alphas = [], []
        for h in range(nh):
            s = raw[h]
            m_old = m_ref[h]
            m_new = jnp.maximum(m_old, jnp.max(s, axis=0, keepdims=True))
            m_ref[h] = m_new
            es.append(jnp.exp(s - m_new).astype(BF16))
            alphas.append(jnp.exp(m_old - m_new))
        pvs = [_dot(values_t(k0, cc, p), es[h]) for h, (cc, p) in enumerate(heads)]
        for h in range(nh):
            acc_ref[h] = alphas[h] * acc_ref[h] + pvs[h]
        return carry

    lax.fori_loop(0, i, body, 0)
    lane = _iota((1, LANES), 1)
    for cc in range(cpb):
        outs = []
        for p in range(HEADS_PER_CHUNK):
            acc = acc_ref[HEADS_PER_CHUNK * cc + p]
            den = acc[(1 - p) * HEAD_DIM:(1 - p) * HEAD_DIM + 1, :]
            outs.append((acc / den).T)
        o_ref[:, chunk(cc)] = jnp.where(lane < HEAD_DIM, outs[0], outs[1]).astype(o_ref.dtype)


def _moba_attention(q, k, v, slopes, cpb=4):
    b, s, _ = q.shape
    assert s % C_BLOCK == 0 and C_BLOCK <= 256 and all(_is_pow2(x) for x in slopes)
    nblk = s // C_BLOCK
    assert nblk <= LANES
    wide = cpb * LANES
    nh = cpb * HEADS_PER_CHUNK
    spec_q = pl.BlockSpec((None, C_BLOCK, wide), lambda bi, c, i, sl: (bi, i, c))
    spec_kv = pl.BlockSpec((None, s, wide), lambda bi, c, i, sl: (bi, 0, c))
    return pl.pallas_call(
        functools.partial(_moba_kernel, blk=C_BLOCK, nblk=nblk, topk=min(C_TOPK, nblk), cpb=cpb),
        out_shape=jax.ShapeDtypeStruct((b, s, MIX), BF16),
        grid_spec=pltpu.PrefetchScalarGridSpec(
            num_scalar_prefetch=1,
            grid=(b, N_CHUNKS // cpb, nblk),
            in_specs=[spec_q, spec_kv, spec_kv],
            out_specs=spec_q,
            scratch_shapes=[pltpu.VMEM((cpb, LANES, LANES), F32), pltpu.VMEM((cpb, LANES, s), BF16),
                            pltpu.VMEM((nh, 1, C_BLOCK), F32), pltpu.VMEM((nh, LANES, C_BLOCK), F32)],
        ),
        compiler_params=_cparams("parallel", "parallel", "arbitrary"),
        name="moba_attention",
    )(jnp.asarray(slopes, F32), q, k, v)


def _compress_kernel(xk_ref, xv_ref, wk_lo_ref, wk_hi_ref, wv_lo_ref, wv_hi_ref,
                     pk_lo_ref, pk_hi_ref, pv_lo_ref, pv_hi_ref, gain_ref, kc_ref, vc_ref):
    def compress(x_ref, w_lo_ref, w_hi_ref, p_lo_ref, p_hi_ref):
        x = x_ref[...]
        nrow = x.shape[0]
        first = _dot(x, w_lo_ref[...])
        second = pltpu.roll(_dot(x, w_hi_ref[...]), nrow - 1, 0)
        p_lo = jnp.broadcast_to(p_lo_ref[...], (8, p_lo_ref.shape[1]))
        p_hi = jnp.broadcast_to(p_hi_ref[...], (8, p_hi_ref.shape[1]))
        bias = _dot_hilo(p_lo, w_lo_ref[...]) + _dot_hilo(p_hi, w_hi_ref[...])
        return first + second + bias[0:1]

    kc = compress(xk_ref, wk_lo_ref, wk_hi_ref, pk_lo_ref, pk_hi_ref)
    kc = kc * lax.rsqrt(_head_mean_sq(kc) + EPS) * gain_ref[...]
    kc_ref[...] = kc.astype(kc_ref.dtype)
    vc_ref[...] = compress(xv_ref, wv_lo_ref, wv_hi_ref, pv_lo_ref, pv_hi_ref).astype(vc_ref.dtype)


def _compress_weights(pos, w):
    half = CMP_LEN // 2
    eye = jnp.eye(N_KV, dtype=F32)
    wd = jnp.einsum("gh,lde->lgdhe", eye, w).reshape(CMP_LEN, LANES, LANES)
    w_lo = wd[:half].reshape(half * LANES, LANES).astype(BF16)
    w_hi = wd[half:].reshape(half * LANES, LANES).astype(BF16)
    pt = jnp.tile(pos, (1, N_KV))
    return w_lo, w_hi, pt[:half].reshape(1, half * LANES), pt[half:].reshape(1, half * LANES)


def _compress(kcmp, vcmp, cmp_pos, cmp_w, k_gain):
    b, s, _ = kcmp.shape
    assert CMP_LEN == 2 * CMP_STRIDE and s % CMP_STRIDE == 0
    nrow = s // CMP_STRIDE
    wide = CMP_STRIDE * LANES
    xk = kcmp.reshape(b, nrow, wide)
    xv = vcmp.reshape(b, nrow, wide)
    wk = _compress_weights(cmp_pos[0], cmp_w[0])
    wv = _compress_weights(cmp_pos[1], cmp_w[1])
    consts = [wk[0], wk[1], wv[0], wv[1], wk[2], wk[3], wv[2], wv[3],
              jnp.tile(k_gain, N_KV).reshape(1, LANES)]
    spec_x = pl.BlockSpec((None, nrow, wide), lambda bi: (bi, 0, 0))
    spec_o = pl.BlockSpec((None, nrow, LANES), lambda bi: (bi, 0, 0))
    return pl.pallas_call(
        _compress_kernel,
        out_shape=[jax.ShapeDtypeStruct((b, nrow, LANES), BF16)] * 2,
        grid=(b,),
        in_specs=[spec_x, spec_x] + [pl.BlockSpec(a.shape, lambda bi: (0, 0)) for a in consts],
        out_specs=[spec_o, spec_o],
        compiler_params=_cparams("parallel"),
        name="nsa_compress",
    )(xk, xv, *consts)


def _nsa_cmp_kernel(q_ref, kc_ref, vc_ref, o_ref, sel_ref, *, tq, n_cmp, n_slc, topn, slopes):
    q0 = pl.program_id(1) * tq
    kc = kc_ref[...]
    vc = vc_ref[...]
    ncp = kc.shape[0]
    rows = N_CHUNKS * tq
    seg = LANES // N_KV
    qpos = q0 + _iota((rows, ncp), 0) % tq
    ncol = _iota((rows, ncp), 1)
    diff = qpos - (ncol * CMP_STRIDE + CMP_LEN - 1)
    mask = (diff >= 0) & (ncol < n_cmp)
    diff_f = diff.astype(F32)
    orow = _iota((LANES, ncp), 0)
    cst = _iota((LANES, ncp), 1) * CMP_STRIDE
    sst = (orow % seg) * SLC_BLOCK
    overlap = (cst < sst + SLC_BLOCK) & (cst + CMP_LEN > sst) & (orow % seg < n_slc)
    halves = range(HEADS_PER_CHUNK)
    hms = _half_masks(BF16)
    raw = [_dot_nt(_stack_group(q_ref, hms[p] * SCALE), kc) for p in halves]
    pcs = []
    for p in halves:
        slope = _per_chunk_rows(rows, tq, [slopes[2 * c + p] for c in range(N_CHUNKS)])
        sc = jnp.where(mask, raw[p] - slope * diff_f, NEG)
        mx = jnp.max(sc, axis=-1, keepdims=True)
        e = jnp.where(mask, jnp.exp(sc - mx), 0.0)
        den = jnp.sum(e, axis=-1, keepdims=True)
        pcs.append(e / jnp.where(den > 0, den, 1.0))
    _store_group(o_ref, [_dot(pcs[p].astype(BF16), vc) for p in halves], tq)
    p_slc = jnp.zeros((LANES, tq), F32)
    for p in halves:
        pg = pcs[p][0:tq]
        for c in range(1, N_CHUNKS):
            pg = pg + pcs[p][c * tq:(c + 1) * tq]
        ov = (overlap & (orow // seg == p)).astype(BF16)
        pg_hi, pg_lo = _split(pg)
        p_slc = p_slc + (_dot_nt(ov, pg_hi) + _dot_nt(ov, pg_lo))
    j = _iota((LANES, tq), 0) % seg
    cur = (q0 + _iota((LANES, tq), 1)) // SLC_BLOCK
    forced = (j == 0) | (j == cur) | (j == cur - 1)
    usable = (j <= cur) & (j < n_slc)
    score = jnp.where(usable, p_slc + jnp.where(forced, FORCE_BONUS, 0.0), NEG)
    rank = _rank_rows(score, j, n_slc, seg)
    sel_t = jnp.where((rank < topn) & usable, 1.0, 0.0)
    sel_ref[...] = sel_t.T.astype(sel_ref.dtype)


def _nsa_cmp(q, kc, vc, slopes, tq=128):
    b, s, _ = q.shape
    ncp = kc.shape[1]
    n_slc = s // SLC_BLOCK
    assert n_slc <= LANES // N_KV and ncp % LANES == 0
    kern = functools.partial(_nsa_cmp_kernel, tq=tq, n_cmp=ncp - 1, n_slc=n_slc,
                             topn=min(SLC_TOPN, n_slc), slopes=slopes)
    spec_c = pl.BlockSpec((None, ncp, LANES), lambda bi, n: (bi, 0, 0))
    return pl.pallas_call(
        kern,
        out_shape=[jax.ShapeDtypeStruct((b, s, MIX), BF16), jax.ShapeDtypeStruct((b, s, LANES), BF16)],
        grid=(b, s // tq),
        in_specs=[pl.BlockSpec((None, tq, MIX), lambda bi, n: (bi, n, 0)), spec_c, spec_c],
        out_specs=[pl.BlockSpec((None, tq, MIX), lambda bi, n: (bi, n, 0)),
                   pl.BlockSpec((None, tq, LANES), lambda bi, n: (bi, n, 0))],
        compiler_params=_cparams("parallel", "parallel"),
        name="nsa_compressed",
    )(q, kc, vc)


def _nsa_slc_kernel(q_ref, k_ref, v_ref, sel_ref, o_ref, vt_ref, m_ref, acc_ref, *, tq, tk, slopes):
    n = pl.program_id(1)
    q0 = n * tq
    rows = N_CHUNKS * tq
    seg = LANES // N_KV
    halves = range(HEADS_PER_CHUNK)
    skip = 1.0 - sel_ref[...]
    nsub = tk // SLC_BLOCK
    lane_q = _iota((tq, LANES), 1)
    row_q = _iota((tq, LANES), 0).astype(F32)
    lane_k = _iota((tk, LANES), 1)
    off_k = _iota((tk, LANES), 0)
    spare = lambda p, n: (1 - p) * HEAD_DIM + n
    hms = _half_masks(BF16)
    qs = []
    for p in halves:
        extra = jnp.concatenate(
            [jnp.where(lane_q == spare(p, 0), -slopes[2 * c + p] * row_q,
                       jnp.where((lane_q == spare(p, 1)) | (lane_q == spare(p, 2)), slopes[2 * c + p], 0.0))
             for c in range(N_CHUNKS)], axis=0)
        qs.append(_stack_group(q_ref, hms[p] * SCALE) + extra.astype(BF16))
    @pl.when(n == 0)
    def _():
        vt_ref[...] = v_ref[...].astype(F32).T.astype(BF16)

    for p in halves:
        m_ref[p] = jnp.full((1, rows), 0.1 * NEG, F32)
        acc_ref[p] = jnp.zeros((LANES, rows), F32)

    n_grp = 4
    grp = rows // n_grp
    lanes = lambda g: slice(g * grp, (g + 1) * grp)
    chains = [(p, g) for p in halves for g in range(n_grp)]

    def tile(kt, diagonal):
        k0 = pl.multiple_of(kt * tk, tk)
        kj = k_ref[pl.ds(k0, tk), :]
        vt = vt_ref[:, pl.ds(k0, tk)]
        shift = (k0 - q0).astype(F32).astype(BF16)
        raw = []
        for p in halves:
            in_sub = (lane_k >= spare(p, 3)) & (lane_k < spare(p, 3) + nsub)
            kp = jnp.where(lane_k == spare(p, 0), 1.0,
                           jnp.where(lane_k == spare(p, 1), off_k.astype(F32).astype(BF16),
                                     jnp.where(lane_k == spare(p, 2), shift,
                                               jnp.where(in_sub, jnp.where(off_k // SLC_BLOCK == lane_k - spare(p, 3),
                                                                           1.0, 0.0).astype(BF16), kj))))
            src = _iota((LANES, LANES), 0) - (p * seg + k0 // SLC_BLOCK)
            dst = _iota((LANES, LANES), 1) - spare(p, 3)
            to_lane = ((src == dst) & (dst >= 0) & (dst < nsub)).astype(BF16)
            bias = (_dot(skip, to_lane) * NEG).astype(BF16)
            qp = qs[p] + jnp.concatenate([bias] * N_CHUNKS, axis=0)
            raw += [_dot_nt(kp, qp[lanes(g)]) for g in range(n_grp)]
        if diagonal:
            future = jnp.where(_iota((tk, tq), 1) + (q0 - k0) >= _iota((tk, tq), 0), 0.0, NEG)
            future = jnp.concatenate([future] * (grp // tq), axis=1)
        es, alphas = [], []
        for p, g in chains:
            s = raw[p * n_grp + g] + future if diagonal else raw[p * n_grp + g]
            m_old = m_ref[p, :, lanes(g)]
            m_new = jnp.maximum(m_old, jnp.max(s, axis=0, keepdims=True))
            m_ref[p, :, lanes(g)] = m_new
            es.append(jnp.exp(s - m_new).astype(BF16))
            alphas.append(jnp.exp(m_old - m_new))
        row = _iota((LANES, tk), 0)
        vts = [jnp.where(row // HEAD_DIM == p, vt, jnp.ones_like(vt)) for p in halves]
        pvs = [_dot(vts[p], es[p * n_grp + g]) for p, g in chains]
        for (p, g), alpha, pv in zip(chains, alphas, pvs):
            acc_ref[p, :, lanes(g)] = alpha * acc_ref[p, :, lanes(g)] + pv

    def body(kt, carry):
        tile(kt, False)
        return carry

    n_past = q0 // tk
    lax.fori_loop(0, n_past, body, 0)
    tile(n_past, True)
    outs = []
    for p in halves:
        acc = acc_ref[p]
        den = acc[(1 - p) * HEAD_DIM:(1 - p) * HEAD_DIM + 1, :]
        outs.append((acc / den).T)
    _store_group(o_ref, outs, tq)


def _nsa_slc(q, k, v, sel, slopes, tq=256, tk=256):
    b, s, _ = q.shape
    assert tk % tq == 0 and tq % SLC_BLOCK == 0 and s % tk == 0 and tk <= 256
    assert all(_is_pow2(x) for x in slopes)
    rows = N_CHUNKS * tq
    spec_q = pl.BlockSpec((None, tq, MIX), lambda bi, n: (bi, n, 0))
    spec_kv = pl.BlockSpec((None, s, LANES), lambda bi, n: (bi, 0, 0))
    return pl.pallas_call(
        functools.partial(_nsa_slc_kernel, tq=tq, tk=tk, slopes=slopes),
        out_shape=jax.ShapeDtypeStruct((b, s, MIX), BF16),
        grid=(b, s // tq),
        in_specs=[spec_q, spec_kv, spec_kv, pl.BlockSpec((None, tq, LANES), lambda bi, n: (bi, n, 0))],
        out_specs=spec_q,
        scratch_shapes=[pltpu.VMEM((LANES, s), BF16), pltpu.VMEM((N_KV, 1, rows), F32),
                        pltpu.VMEM((N_KV, LANES, rows), F32)],
        compiler_params=_cparams("parallel", "arbitrary"),
        name="nsa_selected",
    )(q, k, v, sel)


def _odd_attn(x2, b, s, norm_g, w_in, c_q_norm, c_k_norm, d_q_norm, d_k_norm, cmp_pos, cmp_w, w_out):
    kvw = N_KV * HEAD_DIM
    n_gate = N_HEADS * 3
    qd0 = 3 * MIX
    w = jnp.concatenate([w_in[:, :qd0], _gqa_cols(w_in[:, qd0:qd0 + MIX]), w_in[:, qd0 + MIX:],
                         jnp.zeros((w_in.shape[0], LANES - n_gate), F32)], axis=1).astype(BF16)
    n = w.shape[1]
    ones = lambda width: jnp.ones((width,), F32)
    gain = jnp.concatenate([
        jnp.tile(c_q_norm, N_HEADS), jnp.tile(c_k_norm, N_HEADS), ones(MIX), jnp.tile(d_q_norm, N_HEADS),
        ones(2 * kvw), jnp.tile(d_k_norm[1], N_KV), ones(kvw), jnp.tile(d_k_norm[2], N_KV), ones(kvw),
        ones(LANES)]).reshape(1, n)
    plan, col = [], 0
    for width, op in ((MIX, "norm"), (MIX, "norm"), (MIX, None), (MIX, "norm"), (kvw, None), (kvw, None),
                      (kvw, "norm"), (kvw, None), (kvw, "norm"), (kvw, None), (LANES, "sigmoid")):
        plan.append((col, width, op))
        col += width
    outs = _proj(x2, norm_g, w, gain, tuple(plan), [BF16] * 10 + [F32])
    r3 = lambda a: a.reshape(b, s, a.shape[-1])
    qc, kc, vc, qd, kcmp, vcmp, kslc, vslc, kwin, vwin = [r3(a) for a in outs[:10]]
    gates = outs[10]
    slopes = _alibi(N_HEADS)
    gslopes = [slopes[h] for h in GQA_PERM]
    oc = _moba_attention(qc, kc, vc, slopes)
    k_cmp, v_cmp = _compress(kcmp, vcmp, cmp_pos, cmp_w, d_k_norm[0])
    o_cmp, sel = _nsa_cmp(qd, k_cmp, v_cmp, gslopes)
    o_slc = _nsa_slc(qd, kslc, vslc, sel, gslopes)
    o_win = _band_attention(qd, kwin, vwin, window=D_WINDOW, slopes=gslopes)
    t = b * s
    w1 = w_out[:MIX].astype(BF16)
    w2 = _gqa_rows(w_out[MIX:]).astype(BF16)
    flat = lambda a: a.reshape(t, a.shape[-1])
    return [flat(oc), flat(o_cmp), flat(o_slc), flat(o_win), gates], w1, w2


def kernel(x, ev_norm, ev_w_in, ev_q_norm, ev_k_norm, ev_sink, ev_w_out, od_norm, od_w_in, od_c_q_norm,
           od_c_k_norm, od_d_q_norm, od_d_k_norm, od_cmp_pos, od_cmp_w, od_w_out, moe_norm, moe_w_grp,
           moe_b_grp, moe_w_exp, moe_b_exp, moe_w_gate, moe_w_up, moe_w_down):
    b, s, d = x.shape
    x2 = x.reshape(b * s, d)
    depth = moe_norm.shape[0]
    moe_w_gate, moe_w_up, moe_w_down = (w.astype(BF16) for w in (moe_w_gate, moe_w_up, moe_w_down))
    for layer in range(depth):
        i = layer // 2
        if layer % 2 == 0:
            attn, w1, w2 = _even_attn(x2, b, s, ev_norm[i], ev_w_in[i], ev_q_norm[i], ev_k_norm[i],
                                      ev_sink[i], ev_w_out[i])
        else:
            attn, w1, w2 = _odd_attn(x2, b, s, od_norm[i], od_w_in[i], od_c_q_norm[i], od_c_k_norm[i],
                                     od_d_q_norm[i], od_d_k_norm[i], od_cmp_pos[i], od_cmp_w[i], od_w_out[i])
        x2 = _moe_block(x2, attn, w1, w2, moe_norm[layer], moe_w_grp[layer], moe_b_grp[layer],
                        moe_w_exp[layer], moe_b_exp[layer], moe_w_gate, moe_w_up, moe_w_down, layer)
    return x2.reshape(b, s, d)
```

```python
import functools
import math

import numpy as np
import jax
import jax.numpy as jnp
from jax import lax
from jax.experimental import pallas as pl
from jax.experimental.pallas import tpu as pltpu

F32 = jnp.float32
BF16 = jnp.bfloat16

LANES = 128
HEAD_DIM = 64
HEADS_PER_CHUNK = LANES // HEAD_DIM
N_HEADS = 8
N_KV = 2
GROUP = N_HEADS // N_KV
N_CHUNKS = N_HEADS // HEADS_PER_CHUNK
MIX = N_HEADS * HEAD_DIM
SCALE = 1.0 / math.sqrt(HEAD_DIM)
EPS = 1e-6
NEG = -1e30
EXP_UNDERFLOW = -104.0

A_WINDOW = 128
C_BLOCK = 256
C_TOPK = 3
CMP_LEN = 32
CMP_STRIDE = 16
SLC_BLOCK = 64
SLC_TOPN = 16
D_WINDOW = 512
FORCE_BONUS = 1000.0

N_GROUPS = 4
EXPERTS_PER_GROUP = 8
N_EXPERTS = N_GROUPS * EXPERTS_PER_GROUP

VMEM_LIMIT = 48 * 1024 * 1024
MOE_VMEM_LIMIT = 56 * 1024 * 1024

GQA_PERM = tuple(h for c in range(N_CHUNKS) for h in (c, c + GROUP))


def _alibi(n_heads):
    return [float(2.0 ** (-8.0 * (i + 1) / n_heads)) for i in range(n_heads)]


def _is_pow2(x):
    return math.frexp(x)[0] == 0.5


def _gqa_cols(w):
    lead = w.shape[:-1]
    w = w.reshape(*lead, N_KV, GROUP, HEAD_DIM)
    return jnp.swapaxes(w, -3, -2).reshape(*lead, MIX)


def _gqa_rows(w):
    tail = w.shape[1:]
    return jnp.swapaxes(w.reshape(N_KV, GROUP, HEAD_DIM, *tail), 0, 1).reshape(MIX, *tail)


def _cparams(*sem, vmem=VMEM_LIMIT):
    return pltpu.CompilerParams(dimension_semantics=sem, vmem_limit_bytes=vmem)


def _dot(a, b):
    return jnp.dot(a, b, preferred_element_type=F32)


def _dot_nt(a, b):
    return lax.dot_general(a, b, (((1,), (1,)), ((), ())), preferred_element_type=F32)


def _split(x):
    hi = x.astype(BF16)
    lo = (x - hi.astype(F32)).astype(BF16)
    return hi, lo


def _dot_hilo(a, b):
    hi, lo = _split(a)
    return _dot(jnp.concatenate([hi, lo], axis=1), jnp.concatenate([b, b], axis=0))


def _iota(shape, dim):
    return lax.broadcasted_iota(jnp.int32, shape, dim)


def _half_masks(dtype):
    lane = _iota((1, LANES), 1)
    return [(lane // HEAD_DIM == p).astype(dtype) for p in range(HEADS_PER_CHUNK)]


def _head_mean_sq(y):
    w = y.shape[1]
    same = (_iota((w, w), 0) // HEAD_DIM == _iota((w, w), 1) // HEAD_DIM)
    return _dot((y * y).astype(BF16), same.astype(BF16)) * (1.0 / HEAD_DIM)


def _proj_kernel(x_ref, g_ref, w_ref, gain_ref, *out_refs, plan):
    x = x_ref[...]
    ms = jnp.mean(x * x, axis=-1, keepdims=True)
    xn = (x * lax.rsqrt(ms + EPS) * g_ref[...]).astype(BF16)
    for o_ref, (col0, width, op) in zip(out_refs, plan):
        for a in range(0, width, 2 * LANES):
            wd = min(2 * LANES, width - a)
            y = _dot(xn, w_ref[:, col0 + a:col0 + a + wd])
            if op == "norm":
                y = y * lax.rsqrt(_head_mean_sq(y) + EPS) * gain_ref[:, col0 + a:col0 + a + wd]
            elif op == "sigmoid":
                y = jax.nn.sigmoid(y)
            o_ref[:, a:a + wd] = y.astype(o_ref.dtype)


def _proj(x2, norm_g, w, gain, plan, out_dtypes, tm=1024):
    t, d = x2.shape
    n = w.shape[1]
    out_shape = [jax.ShapeDtypeStruct((t, width), dt) for (_, width, _), dt in zip(plan, out_dtypes)]
    return pl.pallas_call(
        functools.partial(_proj_kernel, plan=plan),
        out_shape=out_shape,
        grid=(t // tm,),
        in_specs=[
            pl.BlockSpec((tm, d), lambda i: (i, 0)),
            pl.BlockSpec((1, d), lambda i: (0, 0)),
            pl.BlockSpec((d, n), lambda i: (0, 0)),
            pl.BlockSpec((1, n), lambda i: (0, 0)),
        ],
        out_specs=[pl.BlockSpec((tm, width), lambda i: (i, 0)) for (_, width, _) in plan],
        compiler_params=_cparams("parallel"),
        name="norm_proj",
    )(x2, norm_g.reshape(1, d), w, gain)


def _stack_group(q_ref, halfmask):
    return jnp.concatenate(
        [q_ref[:, c * LANES:(c + 1) * LANES] * halfmask for c in range(N_CHUNKS)], axis=0)


def _per_chunk_rows(rows, tq, values):
    r = _iota((rows, 1), 0) // tq
    out = jnp.full((rows, 1), values[N_CHUNKS - 1], F32)
    for c in range(N_CHUNKS - 2, -1, -1):
        out = jnp.where(r == c, values[c], out)
    return out


def _per_chunk_lanes(rows, tq, values):
    r = _iota((1, rows), 1) // tq
    out = jnp.full((1, rows), values[N_CHUNKS - 1], F32)
    for c in range(N_CHUNKS - 2, -1, -1):
        out = jnp.where(r == c, values[c], out)
    return out


def _band_kernel(*refs, tq, window, wpad, slopes, has_sink):
    if has_sink:
        sink_ref, q_ref, k_ref, v_ref, o_ref, vt_ref = refs
    else:
        q_ref, k_ref, v_ref, o_ref, vt_ref = refs
    n = pl.program_id(1)

    @pl.when(n == 0)
    def _():
        vt_ref[...] = v_ref[...].astype(F32).T.astype(BF16)

    q0 = n * tq
    kw = tq + wpad
    kstart = pl.multiple_of(jnp.maximum(q0 - wpad, 0), LANES)
    ks = k_ref[pl.ds(kstart, kw), :]
    vt = vt_ref[:, pl.ds(kstart, kw)]
    rows = N_CHUNKS * tq
    halves = range(HEADS_PER_CHUNK)
    hms = _half_masks(BF16)
    spare = lambda p, j: (1 - p) * HEAD_DIM + j
    lane_q = _iota((tq, LANES), 1)
    row_q = _iota((tq, LANES), 0).astype(F32)
    lane_k = _iota((kw, LANES), 1)
    off_k = _iota((kw, LANES), 0)
    fine = (off_k % LANES).astype(F32).astype(BF16)
    coarse = ((off_k // LANES) * LANES + (kstart - q0)).astype(F32).astype(BF16)
    diff = _iota((kw, tq), 1) + (q0 - kstart) - _iota((kw, tq), 0)
    band = jnp.where((diff >= 0) & (diff < window), 0.0, NEG)
    band = jnp.concatenate([band] * N_CHUNKS, axis=1)
    raw = []
    for p in halves:
        extra = jnp.concatenate(
            [jnp.where(lane_q == spare(p, 0), -slopes[2 * c + p] * row_q,
                       jnp.where((lane_q == spare(p, 1)) | (lane_q == spare(p, 2)), slopes[2 * c + p], 0.0))
             for c in range(N_CHUNKS)], axis=0)
        qp = _stack_group(q_ref, hms[p] * SCALE) + extra.astype(BF16)
        kp = jnp.where(lane_k == spare(p, 0), 1.0,
                       jnp.where(lane_k == spare(p, 1), fine, jnp.where(lane_k == spare(p, 2), coarse, ks)))
        raw.append(_dot_nt(kp, qp))
    es, sinks = [], []
    for p in halves:
        s = raw[p] + band
        mx = jnp.max(s, axis=0, keepdims=True)
        if has_sink:
            sk = _per_chunk_lanes(rows, tq, [sink_ref[2 * c + p] for c in range(N_CHUNKS)])
            mx = jnp.maximum(mx, sk)
            sinks.append(jnp.exp(sk - mx))
        es.append(jnp.exp(s - mx).astype(BF16))
    vrow = _iota((LANES, kw), 0)
    outs = []
    for p in halves:
        acc = _dot(jnp.where(vrow // HEAD_DIM == p, vt, jnp.ones_like(vt)), es[p])
        den = acc[(1 - p) * HEAD_DIM:(1 - p) * HEAD_DIM + 1, :]
        if has_sink:
            den = den + sinks[p]
        outs.append((acc / den).T)
    _store_group(o_ref, outs, tq)


def _store_group(o_ref, os_, tq):
    lane = _iota((1, LANES), 1)
    for c in range(N_CHUNKS):
        o = jnp.where(lane < HEAD_DIM, os_[0][c * tq:(c + 1) * tq], os_[1][c * tq:(c + 1) * tq])
        o_ref[:, c * LANES:(c + 1) * LANES] = o.astype(o_ref.dtype)


def _band_attention(q, k, v, *, window, slopes, sink=None, tq=256):
    b, s, _ = q.shape
    wpad = -(-window // LANES) * LANES
    assert s >= tq + wpad and s % tq == 0 and tq <= 256 and all(_is_pow2(x) for x in slopes)
    kern = functools.partial(_band_kernel, tq=tq, window=window, wpad=wpad,
                             slopes=slopes, has_sink=sink is not None)
    in_specs = [
        pl.BlockSpec((None, tq, MIX), lambda bi, n: (bi, n, 0)),
        pl.BlockSpec((None, s, LANES), lambda bi, n: (bi, 0, 0)),
        pl.BlockSpec((None, s, LANES), lambda bi, n: (bi, 0, 0)),
    ]
    args = [q, k, v]
    if sink is not None:
        in_specs = [pl.BlockSpec(memory_space=pltpu.SMEM)] + in_specs
        args = [sink] + args
    return pl.pallas_call(
        kern,
        out_shape=jax.ShapeDtypeStruct((b, s, MIX), BF16),
        grid=(b, s // tq),
        in_specs=in_specs,
        out_specs=pl.BlockSpec((None, tq, MIX), lambda bi, n: (bi, n, 0)),
        scratch_shapes=[pltpu.VMEM((LANES, s), BF16)],
        compiler_params=_cparams("parallel", "arbitrary"),
        name="band_attention",
    )(*args)


def _stick_kernel(q_ref, k_ref, v_ref, o_ref, acc_ref, run_ref, *, tq, cpb):
    i = pl.program_id(2)
    q0 = pl.multiple_of(i * tq, tq)
    hms = _half_masks(BF16)
    heads = [(cc, p) for cc in range(cpb) for p in range(HEADS_PER_CHUNK)]
    qs = [q_ref[:, cc * LANES:(cc + 1) * LANES] * (hms[p] * SCALE) for cc, p in heads]
    upper = (_iota((2 * tq, tq), 0) % tq > _iota((2 * tq, tq), 1)).astype(BF16)

    def suffix_sum(x):
        hi, lo = _split(x)
        return _dot(jnp.concatenate([hi, lo], axis=1), upper)

    def block(kstart, diag):
        if diag:
            causal = _iota((tq, tq), 1) < _iota((tq, tq), 0)
        kjs = [k_ref[pl.ds(kstart, tq), cc * LANES:(cc + 1) * LANES] for cc in range(cpb)]
        vjs = [v_ref[pl.ds(kstart, tq), cc * LANES:(cc + 1) * LANES] for cc in range(cpb)]
        zs = [_dot_nt(qs[h], kjs[cc]) for h, (cc, p) in enumerate(heads)]
        lss, lks = [], []
        for z in zs:
            ls = jnp.minimum(z, 0.0) - jnp.log(1.0 + jnp.exp(-jnp.abs(z)))
            lk = ls - z
            if diag:
                lk = jnp.where(causal, lk, 0.0)
            lss.append(ls)
            lks.append(lk)
        sufs = [suffix_sum(lk) for lk in lks]
        ws = []
        for h in range(len(heads)):
            if diag:
                a = jnp.where(causal, jnp.exp(lss[h] + sufs[h]), 0.0)
            else:
                a = jnp.exp(lss[h] + sufs[h] + run_ref[h])
            ws.append(a.astype(BF16))
        pvs = [_dot(ws[h], vjs[cc]) for h, (cc, p) in enumerate(heads)]
        for h in range(len(heads)):
            rowsum = jnp.sum(lks[h], axis=-1, keepdims=True)
            if diag:
                acc_ref[h] = pvs[h]
                run_ref[h] = rowsum
            else:
                acc_ref[h] += pvs[h]
                run_ref[h] += rowsum

    block(q0, True)

    def weights_alive():
        run = run_ref[0]
        for h in range(1, len(heads)):
            run = jnp.maximum(run, run_ref[h])
        return jnp.max(run) > EXP_UNDERFLOW

    def body(carry):
        t, _ = carry
        block(pl.multiple_of((i - 1 - t) * tq, tq), False)
        return t + 1, weights_alive()

    lax.while_loop(lambda c: (c[0] < i) & c[1], body, (jnp.int32(0), weights_alive()))
    lane = _iota((1, LANES), 1)
    for cc in range(cpb):
        o = jnp.where(lane < HEAD_DIM, acc_ref[HEADS_PER_CHUNK * cc], acc_ref[HEADS_PER_CHUNK * cc + 1])
        o_ref[:, cc * LANES:(cc + 1) * LANES] = o.astype(o_ref.dtype)


def _stick_attention(q, k, v, tq=256, cpb=4):
    b, s, _ = q.shape
    wide = cpb * LANES
    n_heads = cpb * HEADS_PER_CHUNK
    spec_q = pl.BlockSpec((None, tq, wide), lambda bi, c, i: (bi, i, c))
    spec_kv = pl.BlockSpec((None, s, wide), lambda bi, c, i: (bi, 0, c))
    return pl.pallas_call(
        functools.partial(_stick_kernel, tq=tq, cpb=cpb),
        out_shape=jax.ShapeDtypeStruct((b, s, MIX), BF16),
        grid=(b, N_CHUNKS // cpb, s // tq),
        in_specs=[spec_q, spec_kv, spec_kv],
        out_specs=spec_q,
        scratch_shapes=[pltpu.VMEM((n_heads, tq, LANES), F32), pltpu.VMEM((n_heads, tq, 1), F32)],
        compiler_params=_cparams("parallel", "parallel", "parallel"),
        name="stick_breaking",
    )(q, k, v)


def _route(logits):
    lane = _iota(logits.shape, 1)
    lane_f = lane.astype(F32)
    ninf = -jnp.inf
    is_g = (lane >= N_EXPERTS) & (lane < N_EXPERTS + N_GROUPS)
    gmax = jnp.max(jnp.where(is_g, logits, ninf), axis=-1, keepdims=True)
    gidx = jnp.min(jnp.where(is_g & (logits == gmax), lane_f - N_EXPERTS, 1e9), axis=-1, keepdims=True)
    p_g = 1.0 / jnp.sum(jnp.where(is_g, jnp.exp(logits - gmax), 0.0), axis=-1, keepdims=True)
    in_grp = (lane < N_EXPERTS) & ((lane // EXPERTS_PER_GROUP).astype(F32) == gidx)
    le = jnp.where(in_grp, logits, ninf)
    m1 = jnp.max(le, axis=-1, keepdims=True)
    i1 = jnp.min(jnp.where(le == m1, lane_f, 1e9), axis=-1, keepdims=True)
    le2 = jnp.where(lane_f == i1, ninf, le)
    m2 = jnp.max(le2, axis=-1, keepdims=True)
    i2 = jnp.min(jnp.where(le2 == m2, lane_f, 1e9), axis=-1, keepdims=True)
    e2 = jnp.exp(m2 - m1)
    w1 = p_g / (1.0 + e2)
    w2 = p_g * e2 / (1.0 + e2)
    return jnp.where(lane_f == i1, w1, 0.0) + jnp.where(lane_f == i2, w2, 0.0)


def _gate_expand(branch):
    r = _iota((LANES, MIX), 0)
    col = _iota((LANES, MIX), 1)
    head = col // LANES + GROUP * ((col % LANES) // HEAD_DIM)
    return (r == 3 * head + branch).astype(BF16)


def _out_kernel(*refs, nsa):
    if nsa:
        (x_ref, o1_ref, ocmp_ref, oslc_ref, owin_ref, gates_ref, w1_ref, w2_ref,
         ng_ref, wr_ref, br_ref, x1_ref, h_ref, comb_ref) = refs
        g = gates_ref[...]
        o2 = (_dot_hilo(g, _gate_expand(0)) * ocmp_ref[...]
              + _dot_hilo(g, _gate_expand(1)) * oslc_ref[...]
              + _dot_hilo(g, _gate_expand(2)) * owin_ref[...]).astype(BF16)
    else:
        (x_ref, o1_ref, o2_ref, w1_ref, w2_ref,
         ng_ref, wr_ref, br_ref, x1_ref, h_ref, comb_ref) = refs
        o2 = o2_ref[...]
    x1 = x_ref[...] + _dot(o1_ref[...], w1_ref[...]) + _dot(o2, w2_ref[...])
    x1_ref[...] = x1
    ms = jnp.mean(x1 * x1, axis=-1, keepdims=True)
    h = x1 * lax.rsqrt(ms + EPS) * ng_ref[...]
    h_ref[...] = h.astype(BF16)
    h_hi, h_lo = _split(h)
    w_hi, w_lo = _split(wr_ref[...])
    both = _dot(h_hi, jnp.concatenate([w_hi, w_lo], axis=1))
    logits = both[:, :LANES] + (both[:, LANES:] + _dot(h_lo, w_hi)) + br_ref[...]
    comb_ref[...] = _route(logits)


def _out_proj_route(x2, attn, w1, w2, moe_g, w_route, b_route, tm=1024):
    t, d = x2.shape
    nsa = len(attn) > 2
    row = lambda width: pl.BlockSpec((tm, width), lambda i: (i, 0))
    full = lambda a: pl.BlockSpec(a.shape, lambda i: (0, 0))
    consts = [w1, w2, moe_g.reshape(1, d), w_route, b_route]
    return pl.pallas_call(
        functools.partial(_out_kernel, nsa=nsa),
        out_shape=[jax.ShapeDtypeStruct((t, d), F32), jax.ShapeDtypeStruct((t, d), BF16),
                   jax.ShapeDtypeStruct((t, LANES), F32)],
        grid=(t // tm,),
        in_specs=[row(d)] + [row(a.shape[1]) for a in attn] + [full(a) for a in consts],
        out_specs=[row(d), row(d), row(LANES)],
        compiler_params=_cparams("parallel"),
        name="out_proj_route",
    )(x2, *attn, *consts)


MOE_CHUNK = 256
MOE_FIRST_CHUNK = MOE_CHUNK + 32
SEG_ALIGN = 16


def _sorted_rows(tm):
    return -(-(tm + N_GROUPS * SEG_ALIGN) // LANES) * LANES


def _moe_sort(h_ref, comb_ref, hs_ref, cs_ref, pos_ref, acc_ref, seg_ref):
    tm = h_ref.shape[0]
    rows_s = hs_ref.shape[0]
    comb = comb_ref[...]
    used = jnp.where(comb > 0, 1.0, 0.0).astype(BF16)
    of_group = ((_iota((LANES, LANES), 0) // EXPERTS_PER_GROUP == _iota((LANES, LANES), 1))
                & (_iota((LANES, LANES), 0) < N_EXPERTS)).astype(BF16)
    member = jnp.where(_dot(used, of_group) > 0.5, 1.0, 0.0)
    member_t = member.T
    earlier = (_iota((tm, tm), 0) < _iota((tm, tm), 1)).astype(BF16)
    rank_t = _dot(member_t.astype(BF16), earlier)
    count = jnp.sum(member_t, axis=1, keepdims=True)
    counts = [count[g:g + 1, :] for g in range(N_GROUPS)]
    starts = [jnp.zeros((1, 1), F32)]
    for g in range(1, N_GROUPS):
        padded = jnp.floor((counts[g - 1] + (SEG_ALIGN - 1)) * (1.0 / SEG_ALIGN)) * SEG_ALIGN
        starts.append(starts[g - 1] + padded)
    grow = _iota((LANES, 1), 0)
    start_col = jnp.zeros((LANES, 1), F32)
    for g in range(1, N_GROUPS):
        start_col = jnp.where(grow == g, starts[g], start_col)
    pos_t = jnp.sum(member_t * (start_col + rank_t), axis=0, keepdims=True)
    used_rows = _sorted_rows(tm)
    place = jnp.where(_iota((used_rows, tm), 0).astype(F32) == pos_t, 1.0, 0.0).astype(BF16)
    hs_ref[:used_rows] = _dot(place, h_ref[...]).astype(BF16)
    hs_ref[used_rows:] = jnp.zeros((rows_s - used_rows, hs_ref.shape[1]), BF16)
    c_hi = comb.astype(BF16)
    rest = comb - c_hi.astype(F32)
    c_mid = rest.astype(BF16)
    c_lo = (rest - c_mid.astype(F32)).astype(BF16)
    cs_ref[:used_rows] = _dot(place, c_hi) + (_dot(place, c_mid) + _dot(place, c_lo))
    cs_ref[used_rows:] = jnp.zeros((rows_s - used_rows, LANES), F32)
    pos_ref[...] = jnp.broadcast_to(pos_t, (LANES, tm)).T
    acc_ref[...] = jnp.zeros(acc_ref.shape, F32)
    for g in range(N_GROUPS):
        seg_ref[g] = starts[g][0, 0].astype(jnp.int32)
        seg_ref[N_GROUPS + g] = counts[g][0, 0].astype(jnp.int32)


def _moe_kernel(h_ref, comb_ref, x1_ref, wg_ref, wu_ref, wd_ref, o_ref,
                hs_ref, cs_ref, pos_ref, acc_ref, seg_ref, *, per_step):
    step = pl.program_id(1)
    tm = h_ref.shape[0]
    rows_s = hs_ref.shape[0]

    @pl.when(step == 0)
    def _():
        _moe_sort(h_ref, comb_ref, hs_ref, cs_ref, pos_ref, acc_ref, seg_ref)

    first = step * per_step
    group = first // EXPERTS_PER_GROUP
    start = seg_ref[group]
    count = seg_ref[N_GROUPS + group]
    sizes = [MOE_FIRST_CHUNK, MOE_CHUNK // 2, MOE_CHUNK // 2] + [MOE_CHUNK] * (tm // MOE_CHUNK - 2)
    assert sum(sizes) >= tm and max(sizes) <= MOE_FIRST_CHUNK
    begin = 0
    for size in sizes:
        @pl.when(begin < count)
        def _(begin=begin, size=size):
            off = pl.multiple_of(start + begin, SEG_ALIGN)
            hs = hs_ref[pl.ds(off, size), :]
            cs = cs_ref[pl.ds(off, size), :]
            lane = _iota((size, LANES), 1)
            acts = []
            for j in range(per_step):
                g = _dot(hs, wg_ref[j])
                u = _dot(hs, wu_ref[j])
                c = jnp.sum(jnp.where(lane == first + j, cs, 0.0), axis=-1, keepdims=True)
                acts.append((c * (g * jax.nn.sigmoid(g) * u)).astype(BF16))
            w_down = wd_ref[...].reshape(per_step * wd_ref.shape[1], wd_ref.shape[2])
            acc_ref[pl.ds(off, size), :] += _dot(jnp.concatenate(acts, axis=1), w_down)
        begin += size

    @pl.when(step == pl.num_programs(1) - 1)
    def _():
        pos = pos_ref[...]
        used_rows = _sorted_rows(tm)
        back = jnp.concatenate(
            [jnp.where(pos == (_iota((tm, LANES), 1) + blk * LANES).astype(F32), 1.0, 0.0).astype(BF16)
             for blk in range(used_rows // LANES)], axis=1)
        o_ref[...] = x1_ref[...] + _dot(back, acc_ref[:used_rows].astype(BF16))


def _moe(h, comb, x1, w_gate, w_up, w_down, layer, tm=1024, per_step=4):
    t, d = h.shape
    _, n_exp, _, ff = w_gate.shape
    assert t % tm == 0 and EXPERTS_PER_GROUP % per_step == 0 and tm % MOE_CHUNK == 0
    rows_s = _sorted_rows(tm) + -(-MOE_FIRST_CHUNK // LANES) * LANES
    return pl.pallas_call(
        functools.partial(_moe_kernel, per_step=per_step),
        out_shape=jax.ShapeDtypeStruct((t, d), F32),
        grid=(t // tm, n_exp // per_step),
        scratch_shapes=[pltpu.VMEM((rows_s, d), BF16), pltpu.VMEM((rows_s, LANES), F32),
                        pltpu.VMEM((tm, LANES), F32), pltpu.VMEM((rows_s, d), F32),
                        pltpu.SMEM((2 * N_GROUPS,), jnp.int32)],
        in_specs=[
            pl.BlockSpec((tm, d), lambda i, e: (i, 0)),
            pl.BlockSpec((tm, LANES), lambda i, e: (i, 0)),
            pl.BlockSpec((tm, d), lambda i, e: (i, 0)),
            pl.BlockSpec((None, per_step, d, ff), lambda i, e: (layer, e, 0, 0)),
            pl.BlockSpec((None, per_step, d, ff), lambda i, e: (layer, e, 0, 0)),
            pl.BlockSpec((None, per_step, ff, d), lambda i, e: (layer, e, 0, 0)),
        ],
        out_specs=pl.BlockSpec((tm, d), lambda i, e: (i, 0)),
        compiler_params=_cparams("parallel", "arbitrary", vmem=MOE_VMEM_LIMIT),
        name="moe_experts",
    )(h, comb, x1, w_gate, w_up, w_down)


def _moe_block(x2, attn, w1, w2, moe_g, w_grp, b_grp, w_exp, b_exp, w_gate, w_up, w_down, layer):
    d = x2.shape[1]
    pad = LANES - N_EXPERTS - N_GROUPS
    w_route = jnp.concatenate([w_exp, w_grp, jnp.zeros((d, pad), F32)], axis=1)
    b_route = jnp.concatenate([b_exp, b_grp, jnp.zeros((pad,), F32)]).reshape(1, LANES)
    x1, h, comb = _out_proj_route(x2, attn, w1, w2, moe_g, w_route, b_route)
    return _moe(h, comb, x1, w_gate, w_up, w_down, layer)


def _even_attn(x2, b, s, norm_g, w_in, q_norm, k_norm, sink, w_out):
    w = jnp.concatenate([_gqa_cols(w_in[:, :MIX]), w_in[:, MIX:]], axis=1).astype(BF16)
    n = w.shape[1]
    kvw = N_KV * HEAD_DIM
    gain = jnp.concatenate([jnp.tile(q_norm, N_HEADS), jnp.tile(k_norm, N_KV),
                            jnp.ones((n - MIX - kvw,), F32)]).reshape(1, n)
    plan, col = [], 0
    for width, op in ((MIX, "norm"), (kvw, "norm"), (kvw, None), (MIX, None), (MIX, None), (MIX, None)):
        plan.append((col, width, op))
        col += width
    qa, ka, va, qb, kb, vb = _proj(x2, norm_g, w, gain, tuple(plan), [BF16] * 6)
    r3 = lambda a: a.reshape(b, s, a.shape[-1])
    slopes = _alibi(N_HEADS)
    oa = _band_attention(r3(qa), r3(ka), r3(va), window=A_WINDOW,
                         slopes=[slopes[h] for h in GQA_PERM], sink=sink[np.asarray(GQA_PERM)])
    ob = _stick_attention(r3(qb), r3(kb), r3(vb))
    w1 = _gqa_rows(w_out[:MIX]).astype(BF16)
    w2 = w_out[MIX:].astype(BF16)
    t = b * s
    return [oa.reshape(t, MIX), ob.reshape(t, MIX)], w1, w2


def _rank_rows(score, j, n, seg):
    row = _iota(score.shape, 0)
    rank = jnp.zeros(score.shape, F32)
    for jj in range(n):
        other = score[jj:jj + 1, :]
        for sgm in range(1, score.shape[0] // seg):
            other = jnp.where(row // seg == sgm, score[sgm * seg + jj:sgm * seg + jj + 1, :], other)
        beats = (other > score) | ((other == score) & (jj < j))
        rank = rank + jnp.where(beats, 1.0, 0.0)
    return rank


def _moba_kernel(slope_ref, q_ref, k_ref, v_ref, o_ref, kmean_ref, vt_ref, m_ref, acc_ref, *,
                 blk, nblk, topk, cpb):
    g = pl.program_id(1)
    i = pl.program_id(2)
    s_len = k_ref.shape[0]
    heads = [(cc, p) for cc in range(cpb) for p in range(HEADS_PER_CHUNK)]
    nh = len(heads)
    chunk = lambda cc: slice(cc * LANES, (cc + 1) * LANES)

    @pl.when(i == 0)
    def _():
        member = (_iota((LANES, s_len), 1) // blk == _iota((LANES, s_len), 0)).astype(BF16)
        for cc in range(cpb):
            kmean_ref[cc] = _dot(member, k_ref[:, chunk(cc)]) * (1.0 / blk)
            vt_ref[cc] = v_ref[:, chunk(cc)].astype(F32).T.astype(BF16)

    hms = _half_masks(BF16)
    q0 = pl.multiple_of(i * blk, blk)
    rel = (_iota((blk, blk), 0) - _iota((blk, blk), 1)).astype(F32)
    slopes = [slope_ref[HEADS_PER_CHUNK * (g * cpb + cc) + p] for cc, p in heads]
    kms = [_split(kmean_ref[cc]) for cc in range(cpb)]
    gates = [_dot_nt(kms[cc][0], q_ref[:, chunk(cc)] * hms[p]) + _dot_nt(kms[cc][1], q_ref[:, chunk(cc)] * hms[p])
             for cc, p in heads]
    lane = _iota((blk, LANES), 1)
    off_f = _iota((blk, LANES), 0).astype(F32)
    off_b = off_f.astype(BF16)
    spare = lambda p, n: (1 - p) * HEAD_DIM + n

    def with_key_lanes(kj, p, shift):
        return jnp.where(lane == spare(p, 0), 1.0,
                         jnp.where(lane == spare(p, 1), off_b,
                                   jnp.where(lane == spare(p, 2), shift.astype(BF16), kj)))

    qs = []
    for h, (cc, p) in enumerate(heads):
        extra = jnp.where(lane == spare(p, 0), -slopes[h] * off_f,
                          jnp.where((lane == spare(p, 1)) | (lane == spare(p, 2)), slopes[h], 0.0))
        qs.append(q_ref[:, chunk(cc)] * (hms[p] * SCALE) + extra.astype(BF16))
    zero = jnp.zeros((), F32)
    raw = [_dot_nt(with_key_lanes(k_ref[pl.ds(q0, blk), chunk(cc)], p, zero), qs[h])
           for h, (cc, p) in enumerate(heads)]
    nrow = -(-nblk // 8) * 8
    blk_id = _iota((nrow, blk), 0)
    skips = []
    for h in range(nh):
        gate = jnp.where(blk_id < i, gates[h][:nrow], NEG)
        rank = _rank_rows(gate, blk_id, nblk, nrow)
        skip_t = jnp.where((rank < topk) & (blk_id < i), 0.0, 1.0)
        skip_t = jnp.concatenate([skip_t, jnp.ones((LANES - nrow, blk), F32)], axis=0)
        skips.append(skip_t.T.astype(BF16))
    own_bias = jnp.where(rel <= 0, 0.0, NEG)
    vrow = _iota((LANES, blk), 0)

    def values_t(k0, cc, p):
        vt = vt_ref[cc, :, pl.ds(k0, blk)]
        return jnp.where(vrow // HEAD_DIM == p, vt, jnp.ones_like(vt))

    es = []
    for h in range(nh):
        s = raw[h] + own_bias
        m = jnp.max(s, axis=0, keepdims=True)
        m_ref[h] = m
        es.append(jnp.exp(s - m).astype(BF16))
    for h, (cc, p) in enumerate(heads):
        acc_ref[h] = _dot(values_t(q0, cc, p), es[h])

    def body(j, carry):
        k0 = pl.multiple_of(j * blk, blk)
        shift = ((j - i) * blk).astype(F32)
        to_lane = [((_iota((LANES, LANES), 0) == j) & (_iota((LANES, LANES), 1) == spare(p, 0))).astype(BF16)
                   for p in range(HEADS_PER_CHUNK)]
        skipped = [_dot(skips[h], to_lane[p]) for h, (cc, p) in enumerate(heads)]
        raw = [_dot_nt(with_key_lanes(k_ref[pl.ds(k0, blk), chunk(cc)], p, shift),
                       qs[h] + (skipped[h] * NEG).astype(BF16))
               for h, (cc, p) in enumerate(heads)]
        es, alphas = [], []
        for h in range(nh):
            s = raw[h]
            m_old = m_ref[h]
            m_new = jnp.maximum(m_old, jnp.max(s, axis=0, keepdims=True))
            m_ref[h] = m_new
            es.append(jnp.exp(s - m_new).astype(BF16))
            alphas.append(jnp.exp(m_old - m_new))
        pvs = [_dot(values_t(k0, cc, p), es[h]) for h, (cc, p) in enumerate(heads)]
        for h in range(nh):
            acc_ref[h] = alphas[h] * acc_ref[h] + pvs[h]
        return carry

    lax.fori_loop(0, i, body, 0)
    lane = _iota((1, LANES), 1)
    for cc in range(cpb):
        outs = []
        for p in range(HEADS_PER_CHUNK):
            acc = acc_ref[HEADS_PER_CHUNK * cc + p]
            den = acc[(1 - p) * HEAD_DIM:(1 - p) * HEAD_DIM + 1, :]
            outs.append((acc / den).T)
        o_ref[:, chunk(cc)] = jnp.where(lane < HEAD_DIM, outs[0], outs[1]).astype(o_ref.dtype)


def _moba_attention(q, k, v, slopes, cpb=4):
    b, s, _ = q.shape
    assert s % C_BLOCK == 0 and C_BLOCK <= 256 and all(_is_pow2(x) for x in slopes)
    nblk = s // C_BLOCK
    assert nblk <= LANES
    wide = cpb * LANES
    nh = cpb * HEADS_PER_CHUNK
    spec_q = pl.BlockSpec((None, C_BLOCK, wide), lambda bi, c, i, sl: (bi, i, c))
    spec_kv = pl.BlockSpec((None, s, wide), lambda bi, c, i, sl: (bi, 0, c))
    return pl.pallas_call(
        functools.partial(_moba_kernel, blk=C_BLOCK, nblk=nblk, topk=min(C_TOPK, nblk), cpb=cpb),
        out_shape=jax.ShapeDtypeStruct((b, s, MIX), BF16),
        grid_spec=pltpu.PrefetchScalarGridSpec(
            num_scalar_prefetch=1,
            grid=(b, N_CHUNKS // cpb, nblk),
            in_specs=[spec_q, spec_kv, spec_kv],
            out_specs=spec_q,
            scratch_shapes=[pltpu.VMEM((cpb, LANES, LANES), F32), pltpu.VMEM((cpb, LANES, s), BF16),
                            pltpu.VMEM((nh, 1, C_BLOCK), F32), pltpu.VMEM((nh, LANES, C_BLOCK), F32)],
        ),
        compiler_params=_cparams("parallel", "parallel", "arbitrary"),
        name="moba_attention",
    )(jnp.asarray(slopes, F32), q, k, v)


def _compress_kernel(xk_ref, xv_ref, wk_lo_ref, wk_hi_ref, wv_lo_ref, wv_hi_ref,
                     pk_lo_ref, pk_hi_ref, pv_lo_ref, pv_hi_ref, gain_ref, kc_ref, vc_ref):
    def compress(x_ref, w_lo_ref, w_hi_ref, p_lo_ref, p_hi_ref):
        x = x_ref[...]
        nrow = x.shape[0]
        first = _dot(x, w_lo_ref[...])
        second = pltpu.roll(_dot(x, w_hi_ref[...]), nrow - 1, 0)
        p_lo = jnp.broadcast_to(p_lo_ref[...], (8, p_lo_ref.shape[1]))
        p_hi = jnp.broadcast_to(p_hi_ref[...], (8, p_hi_ref.shape[1]))
        bias = _dot_hilo(p_lo, w_lo_ref[...]) + _dot_hilo(p_hi, w_hi_ref[...])
        return first + second + bias[0:1]

    kc = compress(xk_ref, wk_lo_ref, wk_hi_ref, pk_lo_ref, pk_hi_ref)
    kc = kc * lax.rsqrt(_head_mean_sq(kc) + EPS) * gain_ref[...]
    kc_ref[...] = kc.astype(kc_ref.dtype)
    vc_ref[...] = compress(xv_ref, wv_lo_ref, wv_hi_ref, pv_lo_ref, pv_hi_ref).astype(vc_ref.dtype)


def _compress_weights(pos, w):
    half = CMP_LEN // 2
    eye = jnp.eye(N_KV, dtype=F32)
    wd = jnp.einsum("gh,lde->lgdhe", eye, w).reshape(CMP_LEN, LANES, LANES)
    w_lo = wd[:half].reshape(half * LANES, LANES).astype(BF16)
    w_hi = wd[half:].reshape(half * LANES, LANES).astype(BF16)
    pt = jnp.tile(pos, (1, N_KV))
    return w_lo, w_hi, pt[:half].reshape(1, half * LANES), pt[half:].reshape(1, half * LANES)


def _compress(kcmp, vcmp, cmp_pos, cmp_w, k_gain):
    b, s, _ = kcmp.shape
    assert CMP_LEN == 2 * CMP_STRIDE and s % CMP_STRIDE == 0
    nrow = s // CMP_STRIDE
    wide = CMP_STRIDE * LANES
    xk = kcmp.reshape(b, nrow, wide)
    xv = vcmp.reshape(b, nrow, wide)
    wk = _compress_weights(cmp_pos[0], cmp_w[0])
    wv = _compress_weights(cmp_pos[1], cmp_w[1])
    consts = [wk[0], wk[1], wv[0], wv[1], wk[2], wk[3], wv[2], wv[3],
              jnp.tile(k_gain, N_KV).reshape(1, LANES)]
    spec_x = pl.BlockSpec((None, nrow, wide), lambda bi: (bi, 0, 0))
    spec_o = pl.BlockSpec((None, nrow, LANES), lambda bi: (bi, 0, 0))
    return pl.pallas_call(
        _compress_kernel,
        out_shape=[jax.ShapeDtypeStruct((b, nrow, LANES), BF16)] * 2,
        grid=(b,),
        in_specs=[spec_x, spec_x] + [pl.BlockSpec(a.shape, lambda bi: (0, 0)) for a in consts],
        out_specs=[spec_o, spec_o],
        compiler_params=_cparams("parallel"),
        name="nsa_compress",
    )(xk, xv, *consts)


def _nsa_cmp_kernel(q_ref, kc_ref, vc_ref, o_ref, sel_ref, *, tq, n_cmp, n_slc, topn, slopes):
    q0 = pl.program_id(1) * tq
    kc = kc_ref[...]
    vc = vc_ref[...]
    ncp = kc.shape[0]
    rows = N_CHUNKS * tq
    seg = LANES // N_KV
    qpos = q0 + _iota((rows, ncp), 0) % tq
    ncol = _iota((rows, ncp), 1)
    diff = qpos - (ncol * CMP_STRIDE + CMP_LEN - 1)
    mask = (diff >= 0) & (ncol < n_cmp)
    diff_f = diff.astype(F32)
    orow = _iota((LANES, ncp), 0)
    cst = _iota((LANES, ncp), 1) * CMP_STRIDE
    sst = (orow % seg) * SLC_BLOCK
    overlap = (cst < sst + SLC_BLOCK) & (cst + CMP_LEN > sst) & (orow % seg < n_slc)
    halves = range(HEADS_PER_CHUNK)
    hms = _half_masks(BF16)
    raw = [_dot_nt(_stack_group(q_ref, hms[p] * SCALE), kc) for p in halves]
    pcs = []
    for p in halves:
        slope = _per_chunk_rows(rows, tq, [slopes[2 * c + p] for c in range(N_CHUNKS)])
        sc = jnp.where(mask, raw[p] - slope * diff_f, NEG)
        mx = jnp.max(sc, axis=-1, keepdims=True)
        e = jnp.where(mask, jnp.exp(sc - mx), 0.0)
        den = jnp.sum(e, axis=-1, keepdims=True)
        pcs.append(e / jnp.where(den > 0, den, 1.0))
    _store_group(o_ref, [_dot(pcs[p].astype(BF16), vc) for p in halves], tq)
    p_slc = jnp.zeros((LANES, tq), F32)
    for p in halves:
        pg = pcs[p][0:tq]
        for c in range(1, N_CHUNKS):
            pg = pg + pcs[p][c * tq:(c + 1) * tq]
        ov = (overlap & (orow // seg == p)).astype(BF16)
        pg_hi, pg_lo = _split(pg)
        p_slc = p_slc + (_dot_nt(ov, pg_hi) + _dot_nt(ov, pg_lo))
    j = _iota((LANES, tq), 0) % seg
    cur = (q0 + _iota((LANES, tq), 1)) // SLC_BLOCK
    forced = (j == 0) | (j == cur) | (j == cur - 1)
    usable = (j <= cur) & (j < n_slc)
    score = jnp.where(usable, p_slc + jnp.where(forced, FORCE_BONUS, 0.0), NEG)
    rank = _rank_rows(score, j, n_slc, seg)
    sel_t = jnp.where((rank < topn) & usable, 1.0, 0.0)
    sel_ref[...] = sel_t.T.astype(sel_ref.dtype)


def _nsa_cmp(q, kc, vc, slopes, tq=128):
    b, s, _ = q.shape
    ncp = kc.shape[1]
    n_slc = s // SLC_BLOCK
    assert n_slc <= LANES // N_KV and ncp % LANES == 0
    kern = functools.partial(_nsa_cmp_kernel, tq=tq, n_cmp=ncp - 1, n_slc=n_slc,
                             topn=min(SLC_TOPN, n_slc), slopes=slopes)
    spec_c = pl.BlockSpec((None, ncp, LANES), lambda bi, n: (bi, 0, 0))
    return pl.pallas_call(
        kern,
        out_shape=[jax.ShapeDtypeStruct((b, s, MIX), BF16), jax.ShapeDtypeStruct((b, s, LANES), BF16)],
        grid=(b, s // tq),
        in_specs=[pl.BlockSpec((None, tq, MIX), lambda bi, n: (bi, n, 0)), spec_c, spec_c],
        out_specs=[pl.BlockSpec((None, tq, MIX), lambda bi, n: (bi, n, 0)),
                   pl.BlockSpec((None, tq, LANES), lambda bi, n: (bi, n, 0))],
        compiler_params=_cparams("parallel", "parallel"),
        name="nsa_compressed",
    )(q, kc, vc)


def _nsa_slc_kernel(q_ref, k_ref, v_ref, sel_ref, o_ref, vt_ref, m_ref, acc_ref, *, tq, tk, slopes):
    n = pl.program_id(1)
    q0 = n * tq
    rows = N_CHUNKS * tq
    seg = LANES // N_KV
    halves = range(HEADS_PER_CHUNK)
    skip = 1.0 - sel_ref[...]
    nsub = tk // SLC_BLOCK
    lane_q = _iota((tq, LANES), 1)
    row_q = _iota((tq, LANES), 0).astype(F32)
    lane_k = _iota((tk, LANES), 1)
    off_k = _iota((tk, LANES), 0)
    spare = lambda p, n: (1 - p) * HEAD_DIM + n
    hms = _half_masks(BF16)
    qs = []
    for p in halves:
        extra = jnp.concatenate(
            [jnp.where(lane_q == spare(p, 0), -slopes[2 * c + p] * row_q,
                       jnp.where((lane_q == spare(p, 1)) | (lane_q == spare(p, 2)), slopes[2 * c + p], 0.0))
             for c in range(N_CHUNKS)], axis=0)
        qs.append(_stack_group(q_ref, hms[p] * SCALE) + extra.astype(BF16))
    @pl.when(n == 0)
    def _():
        vt_ref[...] = v_ref[...].astype(F32).T.astype(BF16)

    for p in halves:
        m_ref[p] = jnp.full((1, rows), 0.1 * NEG, F32)
        acc_ref[p] = jnp.zeros((LANES, rows), F32)

    n_grp = 4
    grp = rows // n_grp
    lanes = lambda g: slice(g * grp, (g + 1) * grp)
    chains = [(p, g) for p in halves for g in range(n_grp)]

    def tile(kt, diagonal):
        k0 = pl.multiple_of(kt * tk, tk)
        kj = k_ref[pl.ds(k0, tk), :]
        vt = vt_ref[:, pl.ds(k0, tk)]
        shift = (k0 - q0).astype(F32).astype(BF16)
        raw = []
        for p in halves:
            in_sub = (lane_k >= spare(p, 3)) & (lane_k < spare(p, 3) + nsub)
            kp = jnp.where(lane_k == spare(p, 0), 1.0,
                           jnp.where(lane_k == spare(p, 1), off_k.astype(F32).astype(BF16),
                                     jnp.where(lane_k == spare(p, 2), shift,
                                               jnp.where(in_sub, jnp.where(off_k // SLC_BLOCK == lane_k - spare(p, 3),
                                                                           1.0, 0.0).astype(BF16), kj))))
            src = _iota((LANES, LANES), 0) - (p * seg + k0 // SLC_BLOCK)
            dst = _iota((LANES, LANES), 1) - spare(p, 3)
            to_lane = ((src == dst) & (dst >= 0) & (dst < nsub)).astype(BF16)
            bias = (_dot(skip, to_lane) * NEG).astype(BF16)
            qp = qs[p] + jnp.concatenate([bias] * N_CHUNKS, axis=0)
            raw += [_dot_nt(kp, qp[lanes(g)]) for g in range(n_grp)]
        if diagonal:
            future = jnp.where(_iota((tk, tq), 1) + (q0 - k0) >= _iota((tk, tq), 0), 0.0, NEG)
            future = jnp.concatenate([future] * (grp // tq), axis=1)
        es, alphas = [], []
        for p, g in chains:
            s = raw[p * n_grp + g] + future if diagonal else raw[p * n_grp + g]
            m_old = m_ref[p, :, lanes(g)]
            m_new = jnp.maximum(m_old, jnp.max(s, axis=0, keepdims=True))
            m_ref[p, :, lanes(g)] = m_new
            es.append(jnp.exp(s - m_new).astype(BF16))
            alphas.append(jnp.exp(m_old - m_new))
        row = _iota((LANES, tk), 0)
        vts = [jnp.where(row // HEAD_DIM == p, vt, jnp.ones_like(vt)) for p in halves]
        pvs = [_dot(vts[p], es[p * n_grp + g]) for p, g in chains]
        for (p, g), alpha, pv in zip(chains, alphas, pvs):
            acc_ref[p, :, lanes(g)] = alpha * acc_ref[p, :, lanes(g)] + pv

    def body(kt, carry):
        tile(kt, False)
        return carry

    n_past = q0 // tk
    lax.fori_loop(0, n_past, body, 0)
    tile(n_past, True)
    outs = []
    for p in halves:
        acc = acc_ref[p]
        den = acc[(1 - p) * HEAD_DIM:(1 - p) * HEAD_DIM + 1, :]
        outs.append((acc / den).T)
    _store_group(o_ref, outs, tq)


def _nsa_slc(q, k, v, sel, slopes, tq=256, tk=256):
    b, s, _ = q.shape
    assert tk % tq == 0 and tq % SLC_BLOCK == 0 and s % tk == 0 and tk <= 256
    assert all(_is_pow2(x) for x in slopes)
    rows = N_CHUNKS * tq
    spec_q = pl.BlockSpec((None, tq, MIX), lambda bi, n: (bi, n, 0))
    spec_kv = pl.BlockSpec((None, s, LANES), lambda bi, n: (bi, 0, 0))
    return pl.pallas_call(
        functools.partial(_nsa_slc_kernel, tq=tq, tk=tk, slopes=slopes),
        out_shape=jax.ShapeDtypeStruct((b, s, MIX), BF16),
        grid=(b, s // tq),
        in_specs=[spec_q, spec_kv, spec_kv, pl.BlockSpec((None, tq, LANES), lambda bi, n: (bi, n, 0))],
        out_specs=spec_q,
        scratch_shapes=[pltpu.VMEM((LANES, s), BF16), pltpu.VMEM((N_KV, 1, rows), F32),
                        pltpu.VMEM((N_KV, LANES, rows), F32)],
        compiler_params=_cparams("parallel", "arbitrary"),
        name="nsa_selected",
    )(q, k, v, sel)


def _odd_attn(x2, b, s, norm_g, w_in, c_q_norm, c_k_norm, d_q_norm, d_k_norm, cmp_pos, cmp_w, w_out):
    kvw = N_KV * HEAD_DIM
    n_gate = N_HEADS * 3
    qd0 = 3 * MIX
    w = jnp.concatenate([w_in[:, :qd0], _gqa_cols(w_in[:, qd0:qd0 + MIX]), w_in[:, qd0 + MIX:],
                         jnp.zeros((w_in.shape[0], LANES - n_gate), F32)], axis=1).astype(BF16)
    n = w.shape[1]
    ones = lambda width: jnp.ones((width,), F32)
    gain = jnp.concatenate([
        jnp.tile(c_q_norm, N_HEADS), jnp.tile(c_k_norm, N_HEADS), ones(MIX), jnp.tile(d_q_norm, N_HEADS),
        ones(2 * kvw), jnp.tile(d_k_norm[1], N_KV), ones(kvw), jnp.tile(d_k_norm[2], N_KV), ones(kvw),
        ones(LANES)]).reshape(1, n)
    plan, col = [], 0
    for width, op in ((MIX, "norm"), (MIX, "norm"), (MIX, None), (MIX, "norm"), (kvw, None), (kvw, None),
                      (kvw, "norm"), (kvw, None), (kvw, "norm"), (kvw, None), (LANES, "sigmoid")):
        plan.append((col, width, op))
        col += width
    outs = _proj(x2, norm_g, w, gain, tuple(plan), [BF16] * 10 + [F32])
    r3 = lambda a: a.reshape(b, s, a.shape[-1])
    qc, kc, vc, qd, kcmp, vcmp, kslc, vslc, kwin, vwin = [r3(a) for a in outs[:10]]
    gates = outs[10]
    slopes = _alibi(N_HEADS)
    gslopes = [slopes[h] for h in GQA_PERM]
    oc = _moba_attention(qc, kc, vc, slopes)
    k_cmp, v_cmp = _compress(kcmp, vcmp, cmp_pos, cmp_w, d_k_norm[0])
    o_cmp, sel = _nsa_cmp(qd, k_cmp, v_cmp, gslopes)
    o_slc = _nsa_slc(qd, kslc, vslc, sel, gslopes)
    o_win = _band_attention(qd, kwin, vwin, window=D_WINDOW, slopes=gslopes)
    t = b * s
    w1 = w_out[:MIX].astype(BF16)
    w2 = _gqa_rows(w_out[MIX:]).astype(BF16)
    flat = lambda a: a.reshape(t, a.shape[-1])
    return [flat(oc), flat(o_cmp), flat(o_slc), flat(o_win), gates], w1, w2


def kernel(x, ev_norm, ev_w_in, ev_q_norm, ev_k_norm, ev_sink, ev_w_out, od_norm, od_w_in, od_c_q_norm,
           od_c_k_norm, od_d_q_norm, od_d_k_norm, od_cmp_pos, od_cmp_w, od_w_out, moe_norm, moe_w_grp,
           moe_b_grp, moe_w_exp, moe_b_exp, moe_w_gate, moe_w_up, moe_w_down):
    b, s, d = x.shape
    x2 = x.reshape(b * s, d)
    depth = moe_norm.shape[0]
    moe_w_gate, moe_w_up, moe_w_down = (w.astype(BF16) for w in (moe_w_gate, moe_w_up, moe_w_down))
    for layer in range(depth):
        i = layer // 2
        if layer % 2 == 0:
            attn, w1, w2 = _even_attn(x2, b, s, ev_norm[i], ev_w_in[i], ev_q_norm[i], ev_k_norm[i],
                                      ev_sink[i], ev_w_out[i])
        else:
            attn, w1, w2 = _odd_attn(x2, b, s, od_norm[i], od_w_in[i], od_c_q_norm[i], od_c_k_norm[i],
                                     od_d_q_norm[i], od_d_k_norm[i], od_cmp_pos[i], od_cmp_w[i], od_w_out[i])
        x2 = _moe_block(x2, attn, w1, w2, moe_norm[layer], moe_w_grp[layer], moe_b_grp[layer],
                        moe_w_exp[layer], moe_b_exp[layer], moe_w_gate, moe_w_up, moe_w_down, layer)
    return x2.reshape(b, s, d)
```

```python
import functools
import math

import numpy as np
import jax
import jax.numpy as jnp
from jax import lax
from jax.experimental import pallas as pl
from jax.experimental.pallas import tpu as pltpu

F32 = jnp.float32
BF16 = jnp.bfloat16

LANES = 128
HEAD_DIM = 64
HEADS_PER_CHUNK = LANES // HEAD_DIM
N_HEADS = 8
N_KV = 2
GROUP = N_HEADS // N_KV
N_CHUNKS = N_HEADS // HEADS_PER_CHUNK
MIX = N_HEADS * HEAD_DIM
SCALE = 1.0 / math.sqrt(HEAD_DIM)
EPS = 1e-6
NEG = -1e30
EXP_UNDERFLOW = -104.0

A_WINDOW = 128
C_BLOCK = 256
C_TOPK = 3
CMP_LEN = 32
CMP_STRIDE = 16
SLC_BLOCK = 64
SLC_TOPN = 16
D_WINDOW = 512
FORCE_BONUS = 1000.0

N_GROUPS = 4
EXPERTS_PER_GROUP = 8
N_EXPERTS = N_GROUPS * EXPERTS_PER_GROUP

VMEM_LIMIT = 48 * 1024 * 1024
MOE_VMEM_LIMIT = 56 * 1024 * 1024

GQA_PERM = tuple(h for c in range(N_CHUNKS) for h in (c, c + GROUP))


def _alibi(n_heads):
    return [float(2.0 ** (-8.0 * (i + 1) / n_heads)) for i in range(n_heads)]


def _is_pow2(x):
    return math.frexp(x)[0] == 0.5


def _gqa_cols(w):
    lead = w.shape[:-1]
    w = w.reshape(*lead, N_KV, GROUP, HEAD_DIM)
    return jnp.swapaxes(w, -3, -2).reshape(*lead, MIX)


def _gqa_rows(w):
    tail = w.shape[1:]
    return jnp.swapaxes(w.reshape(N_KV, GROUP, HEAD_DIM, *tail), 0, 1).reshape(MIX, *tail)


def _cparams(*sem, vmem=VMEM_LIMIT):
    return pltpu.CompilerParams(dimension_semantics=sem, vmem_limit_bytes=vmem)


def _dot(a, b):
    return jnp.dot(a, b, preferred_element_type=F32)


def _dot_nt(a, b):
    return lax.dot_general(a, b, (((1,), (1,)), ((), ())), preferred_element_type=F32)


def _split(x):
    hi = x.astype(BF16)
    lo = (x - hi.astype(F32)).astype(BF16)
    return hi, lo


def _dot_hilo(a, b):
    hi, lo = _split(a)
    return _dot(jnp.concatenate([hi, lo], axis=1), jnp.concatenate([b, b], axis=0))


def _iota(shape, dim):
    return lax.broadcasted_iota(jnp.int32, shape, dim)


def _half_masks(dtype):
    lane = _iota((1, LANES), 1)
    return [(lane // HEAD_DIM == p).astype(dtype) for p in range(HEADS_PER_CHUNK)]


def _head_mean_sq(y):
    w = y.shape[1]
    same = (_iota((w, w), 0) // HEAD_DIM == _iota((w, w), 1) // HEAD_DIM)
    return _dot((y * y).astype(BF16), same.astype(BF16)) * (1.0 / HEAD_DIM)


def _proj_kernel(x_ref, g_ref, w_ref, gain_ref, *out_refs, plan):
    x = x_ref[...]
    ms = jnp.mean(x * x, axis=-1, keepdims=True)
    xn = (x * lax.rsqrt(ms + EPS) * g_ref[...]).astype(BF16)
    for o_ref, (col0, width, op) in zip(out_refs, plan):
        for a in range(0, width, 2 * LANES):
            wd = min(2 * LANES, width - a)
            y = _dot(xn, w_ref[:, col0 + a:col0 + a + wd])
            if op == "norm":
                y = y * lax.rsqrt(_head_mean_sq(y) + EPS) * gain_ref[:, col0 + a:col0 + a + wd]
            elif op == "sigmoid":
                y = jax.nn.sigmoid(y)
            o_ref[:, a:a + wd] = y.astype(o_ref.dtype)


def _proj(x2, norm_g, w, gain, plan, out_dtypes, tm=1024):
    t, d = x2.shape
    n = w.shape[1]
    out_shape = [jax.ShapeDtypeStruct((t, width), dt) for (_, width, _), dt in zip(plan, out_dtypes)]
    return pl.pallas_call(
        functools.partial(_proj_kernel, plan=plan),
        out_shape=out_shape,
        grid=(t // tm,),
        in_specs=[
            pl.BlockSpec((tm, d), lambda i: (i, 0)),
            pl.BlockSpec((1, d), lambda i: (0, 0)),
            pl.BlockSpec((d, n), lambda i: (0, 0)),
            pl.BlockSpec((1, n), lambda i: (0, 0)),
        ],
        out_specs=[pl.BlockSpec((tm, width), lambda i: (i, 0)) for (_, width, _) in plan],
        compiler_params=_cparams("parallel"),
        name="norm_proj",
    )(x2, norm_g.reshape(1, d), w, gain)


def _stack_group(q_ref, halfmask):
    return jnp.concatenate(
        [q_ref[:, c * LANES:(c + 1) * LANES] * halfmask for c in range(N_CHUNKS)], axis=0)


def _per_chunk_rows(rows, tq, values):
    r = _iota((rows, 1), 0) // tq
    out = jnp.full((rows, 1), values[N_CHUNKS - 1], F32)
    for c in range(N_CHUNKS - 2, -1, -1):
        out = jnp.where(r == c, values[c], out)
    return out


def _per_chunk_lanes(rows, tq, values):
    r = _iota((1, rows), 1) // tq
    out = jnp.full((1, rows), values[N_CHUNKS - 1], F32)
    for c in range(N_CHUNKS - 2, -1, -1):
        out = jnp.where(r == c, values[c], out)
    return out


def _band_kernel(*refs, tq, window, wpad, slopes, has_sink):
    if has_sink:
        sink_ref, q_ref, k_ref, v_ref, o_ref, vt_ref = refs
    else:
        q_ref, k_ref, v_ref, o_ref, vt_ref = refs
    n = pl.program_id(1)

    @pl.when(n == 0)
    def _():
        vt_ref[...] = v_ref[...].astype(F32).T.astype(BF16)

    q0 = n * tq
    kw = tq + wpad
    kstart = pl.multiple_of(jnp.maximum(q0 - wpad, 0), LANES)
    ks = k_ref[pl.ds(kstart, kw), :]
    vt = vt_ref[:, pl.ds(kstart, kw)]
    rows = N_CHUNKS * tq
    halves = range(HEADS_PER_CHUNK)
    hms = _half_masks(BF16)
    spare = lambda p, j: (1 - p) * HEAD_DIM + j
    lane_q = _iota((tq, LANES), 1)
    row_q = _iota((tq, LANES), 0).astype(F32)
    lane_k = _iota((kw, LANES), 1)
    off_k = _iota((kw, LANES), 0)
    fine = (off_k % LANES).astype(F32).astype(BF16)
    coarse = ((off_k // LANES) * LANES + (kstart - q0)).astype(F32).astype(BF16)
    diff = _iota((kw, tq), 1) + (q0 - kstart) - _iota((kw, tq), 0)
    band = jnp.where((diff >= 0) & (diff < window), 0.0, NEG)
    band = jnp.concatenate([band] * N_CHUNKS, axis=1)
    raw = []
    for p in halves:
        extra = jnp.concatenate(
            [jnp.where(lane_q == spare(p, 0), -slopes[2 * c + p] * row_q,
                       jnp.where((lane_q == spare(p, 1)) | (lane_q == spare(p, 2)), slopes[2 * c + p], 0.0))
             for c in range(N_CHUNKS)], axis=0)
        qp = _stack_group(q_ref, hms[p] * SCALE) + extra.astype(BF16)
        kp = jnp.where(lane_k == spare(p, 0), 1.0,
                       jnp.where(lane_k == spare(p, 1), fine, jnp.where(lane_k == spare(p, 2), coarse, ks)))
        raw.append(_dot_nt(kp, qp))
    es, sinks = [], []
    for p in halves:
        s = raw[p] + band
        mx = jnp.max(s, axis=0, keepdims=True)
        if has_sink:
            sk = _per_chunk_lanes(rows, tq, [sink_ref[2 * c + p] for c in range(N_CHUNKS)])
            mx = jnp.maximum(mx, sk)
            sinks.append(jnp.exp(sk - mx))
        es.append(jnp.exp(s - mx).astype(BF16))
    vrow = _iota((LANES, kw), 0)
    outs = []
    for p in halves:
        acc = _dot(jnp.where(vrow // HEAD_DIM == p, vt, jnp.ones_like(vt)), es[p])
        den = acc[(1 - p) * HEAD_DIM:(1 - p) * HEAD_DIM + 1, :]
        if has_sink:
            den = den + sinks[p]
        outs.append((acc / den).T)
    _store_group(o_ref, outs, tq)


def _store_group(o_ref, os_, tq):
    lane = _iota((1, LANES), 1)
    for c in range(N_CHUNKS):
        o = jnp.where(lane < HEAD_DIM, os_[0][c * tq:(c + 1) * tq], os_[1][c * tq:(c + 1) * tq])
        o_ref[:, c * LANES:(c + 1) * LANES] = o.astype(o_ref.dtype)


def _band_attention(q, k, v, *, window, slopes, sink=None, tq=256):
    b, s, _ = q.shape
    wpad = -(-window // LANES) * LANES
    assert s >= tq + wpad and s % tq == 0 and tq <= 256 and all(_is_pow2(x) for x in slopes)
    kern = functools.partial(_band_kernel, tq=tq, window=window, wpad=wpad,
                             slopes=slopes, has_sink=sink is not None)
    in_specs = [
        pl.BlockSpec((None, tq, MIX), lambda bi, n: (bi, n, 0)),
        pl.BlockSpec((None, s, LANES), lambda bi, n: (bi, 0, 0)),
        pl.BlockSpec((None, s, LANES), lambda bi, n: (bi, 0, 0)),
    ]
    args = [q, k, v]
    if sink is not None:
        in_specs = [pl.BlockSpec(memory_space=pltpu.SMEM)] + in_specs
        args = [sink] + args
    return pl.pallas_call(
        kern,
        out_shape=jax.ShapeDtypeStruct((b, s, MIX), BF16),
        grid=(b, s // tq),
        in_specs=in_specs,
        out_specs=pl.BlockSpec((None, tq, MIX), lambda bi, n: (bi, n, 0)),
        scratch_shapes=[pltpu.VMEM((LANES, s), BF16)],
        compiler_params=_cparams("parallel", "arbitrary"),
        name="band_attention",
    )(*args)


def _stick_kernel(q_ref, k_ref, v_ref, o_ref, acc_ref, run_ref, *, tq, cpb):
    i = pl.program_id(2)
    q0 = pl.multiple_of(i * tq, tq)
    hms = _half_masks(BF16)
    heads = [(cc, p) for cc in range(cpb) for p in range(HEADS_PER_CHUNK)]
    qs = [q_ref[:, cc * LANES:(cc + 1) * LANES] * (hms[p] * SCALE) for cc, p in heads]
    upper = (_iota((2 * tq, tq), 0) % tq > _iota((2 * tq, tq), 1)).astype(BF16)

    def suffix_sum(x):
        hi, lo = _split(x)
        return _dot(jnp.concatenate([hi, lo], axis=1), upper)

    def block(kstart, diag):
        if diag:
            causal = _iota((tq, tq), 1) < _iota((tq, tq), 0)
        kjs = [k_ref[pl.ds(kstart, tq), cc * LANES:(cc + 1) * LANES] for cc in range(cpb)]
        vjs = [v_ref[pl.ds(kstart, tq), cc * LANES:(cc + 1) * LANES] for cc in range(cpb)]
        zs = [_dot_nt(qs[h], kjs[cc]) for h, (cc, p) in enumerate(heads)]
        lss, lks = [], []
        for z in zs:
            ls = jnp.minimum(z, 0.0) - jnp.log(1.0 + jnp.exp(-jnp.abs(z)))
            lk = ls - z
            if diag:
                lk = jnp.where(causal, lk, 0.0)
            lss.append(ls)
            lks.append(lk)
        sufs = [suffix_sum(lk) for lk in lks]
        ws = []
        for h in range(len(heads)):
            if diag:
                a = jnp.where(causal, jnp.exp(lss[h] + sufs[h]), 0.0)
            else:
                a = jnp.exp(lss[h] + sufs[h] + run_ref[h])
            ws.append(a.astype(BF16))
        pvs = [_dot(ws[h], vjs[cc]) for h, (cc, p) in enumerate(heads)]
        for h in range(len(heads)):
            rowsum = jnp.sum(lks[h], axis=-1, keepdims=True)
            if diag:
                acc_ref[h] = pvs[h]
                run_ref[h] = rowsum
            else:
                acc_ref[h] += pvs[h]
                run_ref[h] += rowsum

    block(q0, True)

    def weights_alive():
        run = run_ref[0]
        for h in range(1, len(heads)):
            run = jnp.maximum(run, run_ref[h])
        return jnp.max(run) > EXP_UNDERFLOW

    def body(carry):
        t, _ = carry
        block(pl.multiple_of((i - 1 - t) * tq, tq), False)
        return t + 1, weights_alive()

    lax.while_loop(lambda c: (c[0] < i) & c[1], body, (jnp.int32(0), weights_alive()))
    lane = _iota((1, LANES), 1)
    for cc in range(cpb):
        o = jnp.where(lane < HEAD_DIM, acc_ref[HEADS_PER_CHUNK * cc], acc_ref[HEADS_PER_CHUNK * cc + 1])
        o_ref[:, cc * LANES:(cc + 1) * LANES] = o.astype(o_ref.dtype)


def _stick_attention(q, k, v, tq=256, cpb=4):
    b, s, _ = q.shape
    wide = cpb * LANES
    n_heads = cpb * HEADS_PER_CHUNK
    spec_q = pl.BlockSpec((None, tq, wide), lambda bi, c, i: (bi, i, c))
    spec_kv = pl.BlockSpec((None, s, wide), lambda bi, c, i: (bi, 0, c))
    return pl.pallas_call(
        functools.partial(_stick_kernel, tq=tq, cpb=cpb),
        out_shape=jax.ShapeDtypeStruct((b, s, MIX), BF16),
        grid=(b, N_CHUNKS // cpb, s // tq),
        in_specs=[spec_q, spec_kv, spec_kv],
        out_specs=spec_q,
        scratch_shapes=[pltpu.VMEM((n_heads, tq, LANES), F32), pltpu.VMEM((n_heads, tq, 1), F32)],
        compiler_params=_cparams("parallel", "parallel", "parallel"),
        name="stick_breaking",
    )(q, k, v)


def _route(logits):
    lane = _iota(logits.shape, 1)
    lane_f = lane.astype(F32)
    ninf = -jnp.inf
    is_g = (lane >= N_EXPERTS) & (lane < N_EXPERTS + N_GROUPS)
    gmax = jnp.max(jnp.where(is_g, logits, ninf), axis=-1, keepdims=True)
    gidx = jnp.min(jnp.where(is_g & (logits == gmax), lane_f - N_EXPERTS, 1e9), axis=-1, keepdims=True)
    p_g = 1.0 / jnp.sum(jnp.where(is_g, jnp.exp(logits - gmax), 0.0), axis=-1, keepdims=True)
    in_grp = (lane < N_EXPERTS) & ((lane // EXPERTS_PER_GROUP).astype(F32) == gidx)
    le = jnp.where(in_grp, logits, ninf)
    m1 = jnp.max(le, axis=-1, keepdims=True)
    i1 = jnp.min(jnp.where(le == m1, lane_f, 1e9), axis=-1, keepdims=True)
    le2 = jnp.where(lane_f == i1, ninf, le)
    m2 = jnp.max(le2, axis=-1, keepdims=True)
    i2 = jnp.min(jnp.where(le2 == m2, lane_f, 1e9), axis=-1, keepdims=True)
    e2 = jnp.exp(m2 - m1)
    w1 = p_g / (1.0 + e2)
    w2 = p_g * e2 / (1.0 + e2)
    return jnp.where(lane_f == i1, w1, 0.0) + jnp.where(lane_f == i2, w2, 0.0)


def _gate_expand(branch):
    r = _iota((LANES, MIX), 0)
    col = _iota((LANES, MIX), 1)
    head = col // LANES + GROUP * ((col % LANES) // HEAD_DIM)
    return (r == 3 * head + branch).astype(BF16)


def _out_kernel(*refs, nsa):
    if nsa:
        (x_ref, o1_ref, ocmp_ref, oslc_ref, owin_ref, gates_ref, w1_ref, w2_ref,
         ng_ref, wr_ref, br_ref, x1_ref, h_ref, comb_ref) = refs
        g = gates_ref[...]
        o2 = (_dot_hilo(g, _gate_expand(0)) * ocmp_ref[...]
              + _dot_hilo(g, _gate_expand(1)) * oslc_ref[...]
              + _dot_hilo(g, _gate_expand(2)) * owin_ref[...]).astype(BF16)
    else:
        (x_ref, o1_ref, o2_ref, w1_ref, w2_ref,
         ng_ref, wr_ref, br_ref, x1_ref, h_ref, comb_ref) = refs
        o2 = o2_ref[...]
    x1 = x_ref[...] + _dot(o1_ref[...], w1_ref[...]) + _dot(o2, w2_ref[...])
    x1_ref[...] = x1
    ms = jnp.mean(x1 * x1, axis=-1, keepdims=True)
    h = x1 * lax.rsqrt(ms + EPS) * ng_ref[...]
    h_ref[...] = h.astype(BF16)
    h_hi, h_lo = _split(h)
    w_hi, w_lo = _split(wr_ref[...])
    both = _dot(h_hi, jnp.concatenate([w_hi, w_lo], axis=1))
    logits = both[:, :LANES] + (both[:, LANES:] + _dot(h_lo, w_hi)) + br_ref[...]
    comb_ref[...] = _route(logits)


def _out_proj_route(x2, attn, w1, w2, moe_g, w_route, b_route, tm=1024):
    t, d = x2.shape
    nsa = len(attn) > 2
    row = lambda width: pl.BlockSpec((tm, width), lambda i: (i, 0))
    full = lambda a: pl.BlockSpec(a.shape, lambda i: (0, 0))
    consts = [w1, w2, moe_g.reshape(1, d), w_route, b_route]
    return pl.pallas_call(
        functools.partial(_out_kernel, nsa=nsa),
        out_shape=[jax.ShapeDtypeStruct((t, d), F32), jax.ShapeDtypeStruct((t, d), BF16),
                   jax.ShapeDtypeStruct((t, LANES), F32)],
        grid=(t // tm,),
        in_specs=[row(d)] + [row(a.shape[1]) for a in attn] + [full(a) for a in consts],
        out_specs=[row(d), row(d), row(LANES)],
        compiler_params=_cparams("parallel"),
        name="out_proj_route",
    )(x2, *attn, *consts)


MOE_CHUNK = 256
MOE_FIRST_CHUNK = MOE_CHUNK + 32
SEG_ALIGN = 16


def _sorted_rows(tm):
    return -(-(tm + N_GROUPS * SEG_ALIGN) // LANES) * LANES


def _moe_sort(h_ref, comb_ref, hs_ref, cs_ref, pos_ref, acc_ref, seg_ref):
    tm = h_ref.shape[0]
    rows_s = hs_ref.shape[0]
    comb = comb_ref[...]
    used = jnp.where(comb > 0, 1.0, 0.0).astype(BF16)
    of_group = ((_iota((LANES, LANES), 0) // EXPERTS_PER_GROUP == _iota((LANES, LANES), 1))
                & (_iota((LANES, LANES), 0) < N_EXPERTS)).astype(BF16)
    member = jnp.where(_dot(used, of_group) > 0.5, 1.0, 0.0)
    member_t = member.T
    earlier = (_iota((tm, tm), 0) < _iota((tm, tm), 1)).astype(BF16)
    rank_t = _dot(member_t.astype(BF16), earlier)
    count = jnp.sum(member_t, axis=1, keepdims=True)
    counts = [count[g:g + 1, :] for g in range(N_GROUPS)]
    starts = [jnp.zeros((1, 1), F32)]
    for g in range(1, N_GROUPS):
        padded = jnp.floor((counts[g - 1] + (SEG_ALIGN - 1)) * (1.0 / SEG_ALIGN)) * SEG_ALIGN
        starts.append(starts[g - 1] + padded)
    grow = _iota((LANES, 1), 0)
    start_col = jnp.zeros((LANES, 1), F32)
    for g in range(1, N_GROUPS):
        start_col = jnp.where(grow == g, starts[g], start_col)
    pos_t = jnp.sum(member_t * (start_col + rank_t), axis=0, keepdims=True)
    used_rows = _sorted_rows(tm)
    place = jnp.where(_iota((used_rows, tm), 0).astype(F32) == pos_t, 1.0, 0.0).astype(BF16)
    hs_ref[:used_rows] = _dot(place, h_ref[...]).astype(BF16)
    hs_ref[used_rows:] = jnp.zeros((rows_s - used_rows, hs_ref.shape[1]), BF16)
    c_hi = comb.astype(BF16)
    rest = comb - c_hi.astype(F32)
    c_mid = rest.astype(BF16)
    c_lo = (rest - c_mid.astype(F32)).astype(BF16)
    cs_ref[:used_rows] = _dot(place, c_hi) + (_dot(place, c_mid) + _dot(place, c_lo))
    cs_ref[used_rows:] = jnp.zeros((rows_s - used_rows, LANES), F32)
    pos_ref[...] = jnp.broadcast_to(pos_t, (LANES, tm)).T
    acc_ref[...] = jnp.zeros(acc_ref.shape, F32)
    for g in range(N_GROUPS):
        seg_ref[g] = starts[g][0, 0].astype(jnp.int32)
        seg_ref[N_GROUPS + g] = counts[g][0, 0].astype(jnp.int32)


def _moe_kernel(h_ref, comb_ref, x1_ref, wg_ref, wu_ref, wd_ref, o_ref,
                hs_ref, cs_ref, pos_ref, acc_ref, seg_ref, *, per_step):
    step = pl.program_id(1)
    tm = h_ref.shape[0]
    rows_s = hs_ref.shape[0]

    @pl.when(step == 0)
    def _():
        _moe_sort(h_ref, comb_ref, hs_ref, cs_ref, pos_ref, acc_ref, seg_ref)

    first = step * per_step
    group = first // EXPERTS_PER_GROUP
    start = seg_ref[group]
    count = seg_ref[N_GROUPS + group]
    sizes = [MOE_FIRST_CHUNK, MOE_CHUNK // 2, MOE_CHUNK // 2] + [MOE_CHUNK] * (tm // MOE_CHUNK - 2)
    assert sum(sizes) >= tm and max(sizes) <= MOE_FIRST_CHUNK
    begin = 0
    for size in sizes:
        @pl.when(begin < count)
        def _(begin=begin, size=size):
            off = pl.multiple_of(start + begin, SEG_ALIGN)
            hs = hs_ref[pl.ds(off, size), :]
            cs = cs_ref[pl.ds(off, size), :]
            lane = _iota((size, LANES), 1)
            acts = []
            for j in range(per_step):
                g = _dot(hs, wg_ref[j])
                u = _dot(hs, wu_ref[j])
                c = jnp.sum(jnp.where(lane == first + j, cs, 0.0), axis=-1, keepdims=True)
                acts.append((c * (g * jax.nn.sigmoid(g) * u)).astype(BF16))
            w_down = wd_ref[...].reshape(per_step * wd_ref.shape[1], wd_ref.shape[2])
            acc_ref[pl.ds(off, size), :] += _dot(jnp.concatenate(acts, axis=1), w_down)
        begin += size

    @pl.when(step == pl.num_programs(1) - 1)
    def _():
        pos = pos_ref[...]
        used_rows = _sorted_rows(tm)
        back = jnp.concatenate(
            [jnp.where(pos == (_iota((tm, LANES), 1) + blk * LANES).astype(F32), 1.0, 0.0).astype(BF16)
             for blk in range(used_rows // LANES)], axis=1)
        o_ref[...] = x1_ref[...] + _dot(back, acc_ref[:used_rows].astype(BF16))


def _moe(h, comb, x1, w_gate, w_up, w_down, layer, tm=1024, per_step=4):
    t, d = h.shape
    _, n_exp, _, ff = w_gate.shape
    assert t % tm == 0 and EXPERTS_PER_GROUP % per_step == 0 and tm % MOE_CHUNK == 0
    rows_s = _sorted_rows(tm) + -(-MOE_FIRST_CHUNK // LANES) * LANES
    return pl.pallas_call(
        functools.partial(_moe_kernel, per_step=per_step),
        out_shape=jax.ShapeDtypeStruct((t, d), F32),
        grid=(t // tm, n_exp // per_step),
        scratch_shapes=[pltpu.VMEM((rows_s, d), BF16), pltpu.VMEM((rows_s, LANES), F32),
                        pltpu.VMEM((tm, LANES), F32), pltpu.VMEM((rows_s, d), F32),
                        pltpu.SMEM((2 * N_GROUPS,), jnp.int32)],
        in_specs=[
            pl.BlockSpec((tm, d), lambda i, e: (i, 0)),
            pl.BlockSpec((tm, LANES), lambda i, e: (i, 0)),
            pl.BlockSpec((tm, d), lambda i, e: (i, 0)),
            pl.BlockSpec((None, per_step, d, ff), lambda i, e: (layer, e, 0, 0)),
            pl.BlockSpec((None, per_step, d, ff), lambda i, e: (layer, e, 0, 0)),
            pl.BlockSpec((None, per_step, ff, d), lambda i, e: (layer, e, 0, 0)),
        ],
        out_specs=pl.BlockSpec((tm, d), lambda i, e: (i, 0)),
        compiler_params=_cparams("parallel", "arbitrary", vmem=MOE_VMEM_LIMIT),
        name="moe_experts",
    )(h, comb, x1, w_gate, w_up, w_down)


def _moe_block(x2, attn, w1, w2, moe_g, w_grp, b_grp, w_exp, b_exp, w_gate, w_up, w_down, layer):
    d = x2.shape[1]
    pad = LANES - N_EXPERTS - N_GROUPS
    w_route = jnp.concatenate([w_exp, w_grp, jnp.zeros((d, pad), F32)], axis=1)
    b_route = jnp.concatenate([b_exp, b_grp, jnp.zeros((pad,), F32)]).reshape(1, LANES)
    x1, h, comb = _out_proj_route(x2, attn, w1, w2, moe_g, w_route, b_route)
    return _moe(h, comb, x1, w_gate, w_up, w_down, layer)


def _even_attn(x2, b, s, norm_g, w_in, q_norm, k_norm, sink, w_out):
    w = jnp.concatenate([_gqa_cols(w_in[:, :MIX]), w_in[:, MIX:]], axis=1).astype(BF16)
    n = w.shape[1]
    kvw = N_KV * HEAD_DIM
    gain = jnp.concatenate([jnp.tile(q_norm, N_HEADS), jnp.tile(k_norm, N_KV),
                            jnp.ones((n - MIX - kvw,), F32)]).reshape(1, n)
    plan, col = [], 0
    for width, op in ((MIX, "norm"), (kvw, "norm"), (kvw, None), (MIX, None), (MIX, None), (MIX, None)):
        plan.append((col, width, op))
        col += width
    qa, ka, va, qb, kb, vb = _proj(x2, norm_g, w, gain, tuple(plan), [BF16] * 6)
    r3 = lambda a: a.reshape(b, s, a.shape[-1])
    slopes = _alibi(N_HEADS)
    oa = _band_attention(r3(qa), r3(ka), r3(va), window=A_WINDOW,
                         slopes=[slopes[h] for h in GQA_PERM], sink=sink[np.asarray(GQA_PERM)])
    ob = _stick_attention(r3(qb), r3(kb), r3(vb))
    w1 = _gqa_rows(w_out[:MIX]).astype(BF16)
    w2 = w_out[MIX:].astype(BF16)
    t = b * s
    return [oa.reshape(t, MIX), ob.reshape(t, MIX)], w1, w2


def _rank_rows(score, j, n, seg):
    row = _iota(score.shape, 0)
    rank = jnp.zeros(score.shape, F32)
    for jj in range(n):
        other = score[jj:jj + 1, :]
        for sgm in range(1, score.shape[0] // seg):
            other = jnp.where(row // seg == sgm, score[sgm * seg + jj:sgm * seg + jj + 1, :], other)
        beats = (other > score) | ((other == score) & (jj < j))
        rank = rank + jnp.where(beats, 1.0, 0.0)
    return rank


def _moba_kernel(slope_ref, q_ref, k_ref, v_ref, o_ref, kmean_ref, vt_ref, m_ref, acc_ref, *,
                 blk, nblk, topk, cpb):
    g = pl.program_id(1)
    i = pl.program_id(2)
    s_len = k_ref.shape[0]
    heads = [(cc, p) for cc in range(cpb) for p in range(HEADS_PER_CHUNK)]
    nh = len(heads)
    chunk = lambda cc: slice(cc * LANES, (cc + 1) * LANES)

    @pl.when(i == 0)
    def _():
        member = (_iota((LANES, s_len), 1) // blk == _iota((LANES, s_len), 0)).astype(BF16)
        for cc in range(cpb):
            kmean_ref[cc] = _dot(member, k_ref[:, chunk(cc)]) * (1.0 / blk)
            vt_ref[cc] = v_ref[:, chunk(cc)].astype(F32).T.astype(BF16)

    hms = _half_masks(BF16)
    q0 = pl.multiple_of(i * blk, blk)
    rel = (_iota((blk, blk), 0) - _iota((blk, blk), 1)).astype(F32)
    slopes = [slope_ref[HEADS_PER_CHUNK * (g * cpb + cc) + p] for cc, p in heads]
    kms = [_split(kmean_ref[cc]) for cc in range(cpb)]
    gates = [_dot_nt(kms[cc][0], q_ref[:, chunk(cc)] * hms[p]) + _dot_nt(kms[cc][1], q_ref[:, chunk(cc)] * hms[p])
             for cc, p in heads]
    lane = _iota((blk, LANES), 1)
    off_f = _iota((blk, LANES), 0).astype(F32)
    spare = lambda p, n: (1 - p) * HEAD_DIM + n

    def with_key_lanes(kj, p, shift):
        lane = _iota((blk, LANES), 1)
        off_b = _iota((blk, LANES), 0).astype(F32).astype(BF16)
        return jnp.where(lane == spare(p, 0), 1.0,
                         jnp.where(lane == spare(p, 1), off_b,
                                   jnp.where(lane == spare(p, 2), shift.astype(BF16), kj)))

    qs = []
    for h, (cc, p) in enumerate(heads):
        extra = jnp.where(lane == spare(p, 0), -slopes[h] * off_f,
                          jnp.where((lane == spare(p, 1)) | (lane == spare(p, 2)), slopes[h], 0.0))
        qs.append(q_ref[:, chunk(cc)] * (hms[p] * SCALE) + extra.astype(BF16))
    zero = jnp.zeros((), F32)
    raw = [_dot_nt(with_key_lanes(k_ref[pl.ds(q0, blk), chunk(cc)], p, zero), qs[h])
           for h, (cc, p) in enumerate(heads)]
    nrow = -(-nblk // 8) * 8
    blk_id = _iota((nrow, blk), 0)
    skips = []
    for h in range(nh):
        gate = jnp.where(blk_id < i, gates[h][:nrow], NEG)
        rank = _rank_rows(gate, blk_id, nblk, nrow)
        skip_t = jnp.where((rank < topk) & (blk_id < i), 0.0, 1.0)
        skip_t = jnp.concatenate([skip_t, jnp.ones((LANES - nrow, blk), F32)], axis=0)
        skips.append(skip_t.T.astype(BF16))
    own_bias = jnp.where(rel <= 0, 0.0, NEG)
    vrow = _iota((LANES, blk), 0)

    def values_t(k0, cc, p):
        vt = vt_ref[cc, :, pl.ds(k0, blk)]
        return jnp.where(vrow // HEAD_DIM == p, vt, jnp.ones_like(vt))

    es = []
    for h in range(nh):
        s = raw[h] + own_bias
        m = jnp.max(s, axis=0, keepdims=True)
        m_ref[h] = m
        es.append(jnp.exp(s - m).astype(BF16))
    for h, (cc, p) in enumerate(heads):
        acc_ref[h] = _dot(values_t(q0, cc, p), es[h])

    def body(j, carry):
        k0 = pl.multiple_of(j * blk, blk)
        shift = ((j - i) * blk).astype(F32)
        to_lane = [((_iota((LANES, LANES), 0) == j) & (_iota((LANES, LANES), 1) == spare(p, 0))).astype(BF16)
                   for p in range(HEADS_PER_CHUNK)]
        skipped = [_dot(skips[h], to_lane[p]) for h, (cc, p) in enumerate(heads)]
        raw = [_dot_nt(with_key_lanes(k_ref[pl.ds(k0, blk), chunk(cc)], p, shift),
                       qs[h] + (skipped[h] * NEG).astype(BF16))
               for h, (cc, p) in enumerate(heads)]
        es, alphas = [], []
        for h in range(nh):
            s = raw[h]
            m_old = m_ref[h]
            m_new = jnp.maximum(m_old, jnp.max(s, axis=0, keepdims=True))
            m_ref[h] = m_new
            es.append(jnp.exp(s - m_new).astype(BF16))
            alphas.append(jnp.exp(m_old - m_new))
        pvs = [_dot(values_t(k0, cc, p), es[h]) for h, (cc, p) in enumerate(heads)]
        for h in range(nh):
            acc_ref[h] = alphas[h] * acc_ref[h] + pvs[h]
        return carry

    lax.fori_loop(0, i, body, 0)
    lane = _iota((1, LANES), 1)
    for cc in range(cpb):
        outs = []
        for p in range(HEADS_PER_CHUNK):
            acc = acc_ref[HEADS_PER_CHUNK * cc + p]
            den = acc[(1 - p) * HEAD_DIM:(1 - p) * HEAD_DIM + 1, :]
            outs.append((acc / den).T)
        o_ref[:, chunk(cc)] = jnp.where(lane < HEAD_DIM, outs[0], outs[1]).astype(o_ref.dtype)


def _moba_attention(q, k, v, slopes, cpb=4):
    b, s, _ = q.shape
    assert s % C_BLOCK == 0 and C_BLOCK <= 256 and all(_is_pow2(x) for x in slopes)
    nblk = s // C_BLOCK
    assert nblk <= LANES
    wide = cpb * LANES
    nh = cpb * HEADS_PER_CHUNK
    spec_q = pl.BlockSpec((None, C_BLOCK, wide), lambda bi, c, i, sl: (bi, i, c))
    spec_kv = pl.BlockSpec((None, s, wide), lambda bi, c, i, sl: (bi, 0, c))
    return pl.pallas_call(
        functools.partial(_moba_kernel, blk=C_BLOCK, nblk=nblk, topk=min(C_TOPK, nblk), cpb=cpb),
        out_shape=jax.ShapeDtypeStruct((b, s, MIX), BF16),
        grid_spec=pltpu.PrefetchScalarGridSpec(
            num_scalar_prefetch=1,
            grid=(b, N_CHUNKS // cpb, nblk),
            in_specs=[spec_q, spec_kv, spec_kv],
            out_specs=spec_q,
            scratch_shapes=[pltpu.VMEM((cpb, LANES, LANES), F32), pltpu.VMEM((cpb, LANES, s), BF16),
                            pltpu.VMEM((nh, 1, C_BLOCK), F32), pltpu.VMEM((nh, LANES, C_BLOCK), F32)],
        ),
        compiler_params=_cparams("parallel", "parallel", "arbitrary"),
        name="moba_attention",
    )(jnp.asarray(slopes, F32), q, k, v)


def _compress_kernel(xk_ref, xv_ref, wk_lo_ref, wk_hi_ref, wv_lo_ref, wv_hi_ref,
                     pk_lo_ref, pk_hi_ref, pv_lo_ref, pv_hi_ref, gain_ref, kc_ref, vc_ref):
    def compress(x_ref, w_lo_ref, w_hi_ref, p_lo_ref, p_hi_ref):
        x = x_ref[...]
        nrow = x.shape[0]
        first = _dot(x, w_lo_ref[...])
        second = pltpu.roll(_dot(x, w_hi_ref[...]), nrow - 1, 0)
        p_lo = jnp.broadcast_to(p_lo_ref[...], (8, p_lo_ref.shape[1]))
        p_hi = jnp.broadcast_to(p_hi_ref[...], (8, p_hi_ref.shape[1]))
        bias = _dot_hilo(p_lo, w_lo_ref[...]) + _dot_hilo(p_hi, w_hi_ref[...])
        return first + second + bias[0:1]

    kc = compress(xk_ref, wk_lo_ref, wk_hi_ref, pk_lo_ref, pk_hi_ref)
    kc = kc * lax.rsqrt(_head_mean_sq(kc) + EPS) * gain_ref[...]
    kc_ref[...] = kc.astype(kc_ref.dtype)
    vc_ref[...] = compress(xv_ref, wv_lo_ref, wv_hi_ref, pv_lo_ref, pv_hi_ref).astype(vc_ref.dtype)


def _compress_weights(pos, w):
    half = CMP_LEN // 2
    eye = jnp.eye(N_KV, dtype=F32)
    wd = jnp.einsum("gh,lde->lgdhe", eye, w).reshape(CMP_LEN, LANES, LANES)
    w_lo = wd[:half].reshape(half * LANES, LANES).astype(BF16)
    w_hi = wd[half:].reshape(half * LANES, LANES).astype(BF16)
    pt = jnp.tile(pos, (1, N_KV))
    return w_lo, w_hi, pt[:half].reshape(1, half * LANES), pt[half:].reshape(1, half * LANES)


def _compress(kcmp, vcmp, cmp_pos, cmp_w, k_gain):
    b, s, _ = kcmp.shape
    assert CMP_LEN == 2 * CMP_STRIDE and s % CMP_STRIDE == 0
    nrow = s // CMP_STRIDE
    wide = CMP_STRIDE * LANES
    xk = kcmp.reshape(b, nrow, wide)
    xv = vcmp.reshape(b, nrow, wide)
    wk = _compress_weights(cmp_pos[0], cmp_w[0])
    wv = _compress_weights(cmp_pos[1], cmp_w[1])
    consts = [wk[0], wk[1], wv[0], wv[1], wk[2], wk[3], wv[2], wv[3],
              jnp.tile(k_gain, N_KV).reshape(1, LANES)]
    spec_x = pl.BlockSpec((None, nrow, wide), lambda bi: (bi, 0, 0))
    spec_o = pl.BlockSpec((None, nrow, LANES), lambda bi: (bi, 0, 0))
    return pl.pallas_call(
        _compress_kernel,
        out_shape=[jax.ShapeDtypeStruct((b, nrow, LANES), BF16)] * 2,
        grid=(b,),
        in_specs=[spec_x, spec_x] + [pl.BlockSpec(a.shape, lambda bi: (0, 0)) for a in consts],
        out_specs=[spec_o, spec_o],
        compiler_params=_cparams("parallel"),
        name="nsa_compress",
    )(xk, xv, *consts)


def _nsa_cmp_kernel(q_ref, kc_ref, vc_ref, o_ref, sel_ref, *, tq, n_cmp, n_slc, topn, slopes):
    q0 = pl.program_id(1) * tq
    kc = kc_ref[...]
    vc = vc_ref[...]
    ncp = kc.shape[0]
    rows = N_CHUNKS * tq
    seg = LANES // N_KV
    qpos = q0 + _iota((rows, ncp), 0) % tq
    ncol = _iota((rows, ncp), 1)
    diff = qpos - (ncol * CMP_STRIDE + CMP_LEN - 1)
    mask = (diff >= 0) & (ncol < n_cmp)
    diff_f = diff.astype(F32)
    orow = _iota((LANES, ncp), 0)
    cst = _iota((LANES, ncp), 1) * CMP_STRIDE
    sst = (orow % seg) * SLC_BLOCK
    overlap = (cst < sst + SLC_BLOCK) & (cst + CMP_LEN > sst) & (orow % seg < n_slc)
    halves = range(HEADS_PER_CHUNK)
    hms = _half_masks(BF16)
    raw = [_dot_nt(_stack_group(q_ref, hms[p] * SCALE), kc) for p in halves]
    pcs = []
    for p in halves:
        slope = _per_chunk_rows(rows, tq, [slopes[2 * c + p] for c in range(N_CHUNKS)])
        sc = jnp.where(mask, raw[p] - slope * diff_f, NEG)
        mx = jnp.max(sc, axis=-1, keepdims=True)
        e = jnp.where(mask, jnp.exp(sc - mx), 0.0)
        den = jnp.sum(e, axis=-1, keepdims=True)
        pcs.append(e / jnp.where(den > 0, den, 1.0))
    _store_group(o_ref, [_dot(pcs[p].astype(BF16), vc) for p in halves], tq)
    p_slc = jnp.zeros((LANES, tq), F32)
    for p in halves:
        pg = pcs[p][0:tq]
        for c in range(1, N_CHUNKS):
            pg = pg + pcs[p][c * tq:(c + 1) * tq]
        ov = (overlap & (orow // seg == p)).astype(BF16)
        pg_hi, pg_lo = _split(pg)
        p_slc = p_slc + (_dot_nt(ov, pg_hi) + _dot_nt(ov, pg_lo))
    j = _iota((LANES, tq), 0) % seg
    cur = (q0 + _iota((LANES, tq), 1)) // SLC_BLOCK
    forced = (j == 0) | (j == cur) | (j == cur - 1)
    usable = (j <= cur) & (j < n_slc)
    score = jnp.where(usable, p_slc + jnp.where(forced, FORCE_BONUS, 0.0), NEG)
    rank = _rank_rows(score, j, n_slc, seg)
    sel_t = jnp.where((rank < topn) & usable, 1.0, 0.0)
    sel_ref[...] = sel_t.T.astype(sel_ref.dtype)


def _nsa_cmp(q, kc, vc, slopes, tq=128):
    b, s, _ = q.shape
    ncp = kc.shape[1]
    n_slc = s // SLC_BLOCK
    assert n_slc <= LANES // N_KV and ncp % LANES == 0
    kern = functools.partial(_nsa_cmp_kernel, tq=tq, n_cmp=ncp - 1, n_slc=n_slc,
                             topn=min(SLC_TOPN, n_slc), slopes=slopes)
    spec_c = pl.BlockSpec((None, ncp, LANES), lambda bi, n: (bi, 0, 0))
    return pl.pallas_call(
        kern,
        out_shape=[jax.ShapeDtypeStruct((b, s, MIX), BF16), jax.ShapeDtypeStruct((b, s, LANES), BF16)],
        grid=(b, s // tq),
        in_specs=[pl.BlockSpec((None, tq, MIX), lambda bi, n: (bi, n, 0)), spec_c, spec_c],
        out_specs=[pl.BlockSpec((None, tq, MIX), lambda bi, n: (bi, n, 0)),
                   pl.BlockSpec((None, tq, LANES), lambda bi, n: (bi, n, 0))],
        compiler_params=_cparams("parallel", "parallel"),
        name="nsa_compressed",
    )(q, kc, vc)


def _nsa_slc_kernel(q_ref, k_ref, v_ref, sel_ref, o_ref, vt_ref, m_ref, acc_ref, *, tq, tk, slopes):
    n = pl.program_id(1)
    q0 = n * tq
    rows = N_CHUNKS * tq
    seg = LANES // N_KV
    halves = range(HEADS_PER_CHUNK)
    skip = 1.0 - sel_ref[...]
    nsub = tk // SLC_BLOCK
    lane_q = _iota((tq, LANES), 1)
    row_q = _iota((tq, LANES), 0).astype(F32)
    spare = lambda p, n: (1 - p) * HEAD_DIM + n
    hms = _half_masks(BF16)
    qs = []
    for p in halves:
        extra = jnp.concatenate(
            [jnp.where(lane_q == spare(p, 0), -slopes[2 * c + p] * row_q,
                       jnp.where((lane_q == spare(p, 1)) | (lane_q == spare(p, 2)), slopes[2 * c + p], 0.0))
             for c in range(N_CHUNKS)], axis=0)
        qs.append(_stack_group(q_ref, hms[p] * SCALE) + extra.astype(BF16))
    @pl.when(n == 0)
    def _():
        vt_ref[...] = v_ref[...].astype(F32).T.astype(BF16)

    for p in halves:
        m_ref[p] = jnp.full((1, rows), 0.1 * NEG, F32)
        acc_ref[p] = jnp.zeros((LANES, rows), F32)

    n_grp = 4
    grp = rows // n_grp
    lanes = lambda g: slice(g * grp, (g + 1) * grp)
    chains = [(p, g) for p in halves for g in range(n_grp)]

    def tile(kt, diagonal):
        k0 = pl.multiple_of(kt * tk, tk)
        kj = k_ref[pl.ds(k0, tk), :]
        vt = vt_ref[:, pl.ds(k0, tk)]
        shift = (k0 - q0).astype(F32).astype(BF16)
        lane_k = _iota((tk, LANES), 1)
        off_k = _iota((tk, LANES), 0)
        raw = []
        for p in halves:
            in_sub = (lane_k >= spare(p, 3)) & (lane_k < spare(p, 3) + nsub)
            kp = jnp.where(lane_k == spare(p, 0), 1.0,
                           jnp.where(lane_k == spare(p, 1), off_k.astype(F32).astype(BF16),
                                     jnp.where(lane_k == spare(p, 2), shift,
                                               jnp.where(in_sub, jnp.where(off_k // SLC_BLOCK == lane_k - spare(p, 3),
                                                                           1.0, 0.0).astype(BF16), kj))))
            src = _iota((LANES, LANES), 0) - (p * seg + k0 // SLC_BLOCK)
            dst = _iota((LANES, LANES), 1) - spare(p, 3)
            to_lane = ((src == dst) & (dst >= 0) & (dst < nsub)).astype(BF16)
            bias = (_dot(skip, to_lane) * NEG).astype(BF16)
            qp = qs[p] + jnp.concatenate([bias] * N_CHUNKS, axis=0)
            raw += [_dot_nt(kp, qp[lanes(g)]) for g in range(n_grp)]
        if diagonal:
            future = jnp.where(_iota((tk, tq), 1) + (q0 - k0) >= _iota((tk, tq), 0), 0.0, NEG)
            future = jnp.concatenate([future] * (grp // tq), axis=1)
        es, alphas = [], []
        for p, g in chains:
            s = raw[p * n_grp + g] + future if diagonal else raw[p * n_grp + g]
            m_old = m_ref[p, :, lanes(g)]
            m_new = jnp.maximum(m_old, jnp.max(s, axis=0, keepdims=True))
            m_ref[p, :, lanes(g)] = m_new
            es.append(jnp.exp(s - m_new).astype(BF16))
            alphas.append(jnp.exp(m_old - m_new))
        row = _iota((LANES, tk), 0)
        vts = [jnp.where(row // HEAD_DIM == p, vt, jnp.ones_like(vt)) for p in halves]
        pvs = [_dot(vts[p], es[p * n_grp + g]) for p, g in chains]
        for (p, g), alpha, pv in zip(chains, alphas, pvs):
            acc_ref[p, :, lanes(g)] = alpha * acc_ref[p, :, lanes(g)] + pv

    def body(kt, carry):
        tile(kt, False)
        return carry

    n_past = q0 // tk
    lax.fori_loop(0, n_past, body, 0)
    tile(n_past, True)
    outs = []
    for p in halves:
        acc = acc_ref[p]
        den = acc[(1 - p) * HEAD_DIM:(1 - p) * HEAD_DIM + 1, :]
        outs.append((acc / den).T)
    _store_group(o_ref, outs, tq)


def _nsa_slc(q, k, v, sel, slopes, tq=256, tk=256):
    b, s, _ = q.shape
    assert tk % tq == 0 and tq % SLC_BLOCK == 0 and s % tk == 0 and tk <= 256
    assert all(_is_pow2(x) for x in slopes)
    rows = N_CHUNKS * tq
    spec_q = pl.BlockSpec((None, tq, MIX), lambda bi, n: (bi, n, 0))
    spec_kv = pl.BlockSpec((None, s, LANES), lambda bi, n: (bi, 0, 0))
    return pl.pallas_call(
        functools.partial(_nsa_slc_kernel, tq=tq, tk=tk, slopes=slopes),
        out_shape=jax.ShapeDtypeStruct((b, s, MIX), BF16),
        grid=(b, s // tq),
        in_specs=[spec_q, spec_kv, spec_kv, pl.BlockSpec((None, tq, LANES), lambda bi, n: (bi, n, 0))],
        out_specs=spec_q,
        scratch_shapes=[pltpu.VMEM((LANES, s), BF16), pltpu.VMEM((N_KV, 1, rows), F32),
                        pltpu.VMEM((N_KV, LANES, rows), F32)],
        compiler_params=_cparams("parallel", "arbitrary"),
        name="nsa_selected",
    )(q, k, v, sel)


def _odd_attn(x2, b, s, norm_g, w_in, c_q_norm, c_k_norm, d_q_norm, d_k_norm, cmp_pos, cmp_w, w_out):
    kvw = N_KV * HEAD_DIM
    n_gate = N_HEADS * 3
    qd0 = 3 * MIX
    w = jnp.concatenate([w_in[:, :qd0], _gqa_cols(w_in[:, qd0:qd0 + MIX]), w_in[:, qd0 + MIX:],
                         jnp.zeros((w_in.shape[0], LANES - n_gate), F32)], axis=1).astype(BF16)
    n = w.shape[1]
    ones = lambda width: jnp.ones((width,), F32)
    gain = jnp.concatenate([
        jnp.tile(c_q_norm, N_HEADS), jnp.tile(c_k_norm, N_HEADS), ones(MIX), jnp.tile(d_q_norm, N_HEADS),
        ones(2 * kvw), jnp.tile(d_k_norm[1], N_KV), ones(kvw), jnp.tile(d_k_norm[2], N_KV), ones(kvw),
        ones(LANES)]).reshape(1, n)
    plan, col = [], 0
    for width, op in ((MIX, "norm"), (MIX, "norm"), (MIX, None), (MIX, "norm"), (kvw, None), (kvw, None),
                      (kvw, "norm"), (kvw, None), (kvw, "norm"), (kvw, None), (LANES, "sigmoid")):
        plan.append((col, width, op))
        col += width
    outs = _proj(x2, norm_g, w, gain, tuple(plan), [BF16] * 10 + [F32])
    r3 = lambda a: a.reshape(b, s, a.shape[-1])
    qc, kc, vc, qd, kcmp, vcmp, kslc, vslc, kwin, vwin = [r3(a) for a in outs[:10]]
    gates = outs[10]
    slopes = _alibi(N_HEADS)
    gslopes = [slopes[h] for h in GQA_PERM]
    oc = _moba_attention(qc, kc, vc, slopes)
    k_cmp, v_cmp = _compress(kcmp, vcmp, cmp_pos, cmp_w, d_k_norm[0])
    o_cmp, sel = _nsa_cmp(qd, k_cmp, v_cmp, gslopes)
    o_slc = _nsa_slc(qd, kslc, vslc, sel, gslopes)
    o_win = _band_attention(qd, kwin, vwin, window=D_WINDOW, slopes=gslopes)
    t = b * s
    w1 = w_out[:MIX].astype(BF16)
    w2 = _gqa_rows(w_out[MIX:]).astype(BF16)
    flat = lambda a: a.reshape(t, a.shape[-1])
    return [flat(oc), flat(o_cmp), flat(o_slc), flat(o_win), gates], w1, w2


def kernel(x, ev_norm, ev_w_in, ev_q_norm, ev_k_norm, ev_sink, ev_w_out, od_norm, od_w_in, od_c_q_norm,
           od_c_k_norm, od_d_q_norm, od_d_k_norm, od_cmp_pos, od_cmp_w, od_w_out, moe_norm, moe_w_grp,
           moe_b_grp, moe_w_exp, moe_b_exp, moe_w_gate, moe_w_up, moe_w_down):
    b, s, d = x.shape
    x2 = x.reshape(b * s, d)
    depth = moe_norm.shape[0]
    moe_w_gate, moe_w_up, moe_w_down = (w.astype(BF16) for w in (moe_w_gate, moe_w_up, moe_w_down))
    for layer in range(depth):
        i = layer // 2
        if layer % 2 == 0:
            attn, w1, w2 = _even_attn(x2, b, s, ev_norm[i], ev_w_in[i], ev_q_norm[i], ev_k_norm[i],
                                      ev_sink[i], ev_w_out[i])
        else:
            attn, w1, w2 = _odd_attn(x2, b, s, od_norm[i], od_w_in[i], od_c_q_norm[i], od_c_k_norm[i],
                                     od_d_q_norm[i], od_d_k_norm[i], od_cmp_pos[i], od_cmp_w[i], od_w_out[i])
        x2 = _moe_block(x2, attn, w1, w2, moe_norm[layer], moe_w_grp[layer], moe_b_grp[layer],
                        moe_w_exp[layer], moe_b_exp[layer], moe_w_gate, moe_w_up, moe_w_down, layer)
    return x2.reshape(b, s, d)
```

```python
import functools
import math

import numpy as np
import jax
import jax.numpy as jnp
from jax import lax
from jax.experimental import pallas as pl
from jax.experimental.pallas import tpu as pltpu

F32 = jnp.float32
BF16 = jnp.bfloat16

LANES = 128
HEAD_DIM = 64
HEADS_PER_CHUNK = LANES // HEAD_DIM
N_HEADS = 8
N_KV = 2
GROUP = N_HEADS // N_KV
N_CHUNKS = N_HEADS // HEADS_PER_CHUNK
MIX = N_HEADS * HEAD_DIM
SCALE = 1.0 / math.sqrt(HEAD_DIM)
EPS = 1e-6
NEG = -1e30
EXP_UNDERFLOW = -104.0

A_WINDOW = 128
C_BLOCK = 256
C_TOPK = 3
CMP_LEN = 32
CMP_STRIDE = 16
SLC_BLOCK = 64
SLC_TOPN = 16
D_WINDOW = 512
FORCE_BONUS = 1000.0

N_GROUPS = 4
EXPERTS_PER_GROUP = 8
N_EXPERTS = N_GROUPS * EXPERTS_PER_GROUP

VMEM_LIMIT = 48 * 1024 * 1024
MOE_VMEM_LIMIT = 56 * 1024 * 1024

GQA_PERM = tuple(h for c in range(N_CHUNKS) for h in (c, c + GROUP))


def _alibi(n_heads):
    return [float(2.0 ** (-8.0 * (i + 1) / n_heads)) for i in range(n_heads)]


def _is_pow2(x):
    return math.frexp(x)[0] == 0.5


def _gqa_cols(w):
    lead = w.shape[:-1]
    w = w.reshape(*lead, N_KV, GROUP, HEAD_DIM)
    return jnp.swapaxes(w, -3, -2).reshape(*lead, MIX)


def _gqa_rows(w):
    tail = w.shape[1:]
    return jnp.swapaxes(w.reshape(N_KV, GROUP, HEAD_DIM, *tail), 0, 1).reshape(MIX, *tail)


def _cparams(*sem, vmem=VMEM_LIMIT):
    return pltpu.CompilerParams(dimension_semantics=sem, vmem_limit_bytes=vmem)


def _dot(a, b):
    return jnp.dot(a, b, preferred_element_type=F32)


def _dot_nt(a, b):
    return lax.dot_general(a, b, (((1,), (1,)), ((), ())), preferred_element_type=F32)


def _split(x):
    hi = x.astype(BF16)
    lo = (x - hi.astype(F32)).astype(BF16)
    return hi, lo


def _dot_hilo(a, b):
    hi, lo = _split(a)
    return _dot(jnp.concatenate([hi, lo], axis=1), jnp.concatenate([b, b], axis=0))


def _iota(shape, dim):
    return lax.broadcasted_iota(jnp.int32, shape, dim)


def _half_masks(dtype):
    lane = _iota((1, LANES), 1)
    return [(lane // HEAD_DIM == p).astype(dtype) for p in range(HEADS_PER_CHUNK)]


def _head_mean_sq(y):
    w = y.shape[1]
    same = (_iota((w, w), 0) // HEAD_DIM == _iota((w, w), 1) // HEAD_DIM)
    return _dot((y * y).astype(BF16), same.astype(BF16)) * (1.0 / HEAD_DIM)


def _proj_kernel(x_ref, g_ref, w_ref, gain_ref, *out_refs, plan):
    x = x_ref[...]
    ms = jnp.mean(x * x, axis=-1, keepdims=True)
    xn = (x * lax.rsqrt(ms + EPS) * g_ref[...]).astype(BF16)
    for o_ref, (col0, width, op) in zip(out_refs, plan):
        for a in range(0, width, 2 * LANES):
            wd = min(2 * LANES, width - a)
            y = _dot(xn, w_ref[:, col0 + a:col0 + a + wd])
            if op == "norm":
                y = y * lax.rsqrt(_head_mean_sq(y) + EPS) * gain_ref[:, col0 + a:col0 + a + wd]
            elif op == "sigmoid":
                y = jax.nn.sigmoid(y)
            o_ref[:, a:a + wd] = y.astype(o_ref.dtype)


def _proj(x2, norm_g, w, gain, plan, out_dtypes, tm=1024):
    t, d = x2.shape
    n = w.shape[1]
    out_shape = [jax.ShapeDtypeStruct((t, width), dt) for (_, width, _), dt in zip(plan, out_dtypes)]
    return pl.pallas_call(
        functools.partial(_proj_kernel, plan=plan),
        out_shape=out_shape,
        grid=(t // tm,),
        in_specs=[
            pl.BlockSpec((tm, d), lambda i: (i, 0)),
            pl.BlockSpec((1, d), lambda i: (0, 0)),
            pl.BlockSpec((d, n), lambda i: (0, 0)),
            pl.BlockSpec((1, n), lambda i: (0, 0)),
        ],
        out_specs=[pl.BlockSpec((tm, width), lambda i: (i, 0)) for (_, width, _) in plan],
        compiler_params=_cparams("parallel"),
        name="norm_proj",
    )(x2, norm_g.reshape(1, d), w, gain)


def _stack_group(q_ref, halfmask):
    return jnp.concatenate(
        [q_ref[:, c * LANES:(c + 1) * LANES] * halfmask for c in range(N_CHUNKS)], axis=0)


def _per_chunk_rows(rows, tq, values):
    r = _iota((rows, 1), 0) // tq
    out = jnp.full((rows, 1), values[N_CHUNKS - 1], F32)
    for c in range(N_CHUNKS - 2, -1, -1):
        out = jnp.where(r == c, values[c], out)
    return out


def _per_chunk_lanes(rows, tq, values):
    r = _iota((1, rows), 1) // tq
    out = jnp.full((1, rows), values[N_CHUNKS - 1], F32)
    for c in range(N_CHUNKS - 2, -1, -1):
        out = jnp.where(r == c, values[c], out)
    return out


def _band_kernel(*refs, tq, window, wpad, slopes, has_sink):
    if has_sink:
        sink_ref, q_ref, k_ref, v_ref, o_ref, vt_ref = refs
    else:
        q_ref, k_ref, v_ref, o_ref, vt_ref = refs
    n = pl.program_id(1)

    @pl.when(n == 0)
    def _():
        vt_ref[...] = v_ref[...].astype(F32).T.astype(BF16)

    q0 = n * tq
    kw = tq + wpad
    kstart = pl.multiple_of(jnp.maximum(q0 - wpad, 0), LANES)
    ks = k_ref[pl.ds(kstart, kw), :]
    vt = vt_ref[:, pl.ds(kstart, kw)]
    rows = N_CHUNKS * tq
    halves = range(HEADS_PER_CHUNK)
    hms = _half_masks(BF16)
    spare = lambda p, j: (1 - p) * HEAD_DIM + j
    lane_q = _iota((tq, LANES), 1)
    row_q = _iota((tq, LANES), 0).astype(F32)
    lane_k = _iota((kw, LANES), 1)
    off_k = _iota((kw, LANES), 0)
    fine = (off_k % LANES).astype(F32).astype(BF16)
    coarse = ((off_k // LANES) * LANES + (kstart - q0)).astype(F32).astype(BF16)
    diff = _iota((kw, tq), 1) + (q0 - kstart) - _iota((kw, tq), 0)
    band = jnp.where((diff >= 0) & (diff < window), 0.0, NEG)
    band = jnp.concatenate([band] * N_CHUNKS, axis=1)
    raw = []
    for p in halves:
        extra = jnp.concatenate(
            [jnp.where(lane_q == spare(p, 0), -slopes[2 * c + p] * row_q,
                       jnp.where((lane_q == spare(p, 1)) | (lane_q == spare(p, 2)), slopes[2 * c + p], 0.0))
             for c in range(N_CHUNKS)], axis=0)
        qp = _stack_group(q_ref, hms[p] * SCALE) + extra.astype(BF16)
        kp = jnp.where(lane_k == spare(p, 0), 1.0,
                       jnp.where(lane_k == spare(p, 1), fine, jnp.where(lane_k == spare(p, 2), coarse, ks)))
        raw.append(_dot_nt(kp, qp))
    es, sinks = [], []
    for p in halves:
        s = raw[p] + band
        mx = jnp.max(s, axis=0, keepdims=True)
        if has_sink:
            sk = _per_chunk_lanes(rows, tq, [sink_ref[2 * c + p] for c in range(N_CHUNKS)])
            mx = jnp.maximum(mx, sk)
            sinks.append(jnp.exp(sk - mx))
        es.append(jnp.exp(s - mx).astype(BF16))
    vrow = _iota((LANES, kw), 0)
    outs = []
    for p in halves:
        acc = _dot(jnp.where(vrow // HEAD_DIM == p, vt, jnp.ones_like(vt)), es[p])
        den = acc[(1 - p) * HEAD_DIM:(1 - p) * HEAD_DIM + 1, :]
        if has_sink:
            den = den + sinks[p]
        outs.append((acc / den).T)
    _store_group(o_ref, outs, tq)


def _store_group(o_ref, os_, tq):
    lane = _iota((1, LANES), 1)
    for c in range(N_CHUNKS):
        o = jnp.where(lane < HEAD_DIM, os_[0][c * tq:(c + 1) * tq], os_[1][c * tq:(c + 1) * tq])
        o_ref[:, c * LANES:(c + 1) * LANES] = o.astype(o_ref.dtype)


def _band_attention(q, k, v, *, window, slopes, sink=None, tq=256):
    b, s, _ = q.shape
    wpad = -(-window // LANES) * LANES
    assert s >= tq + wpad and s % tq == 0 and tq <= 256 and all(_is_pow2(x) for x in slopes)
    kern = functools.partial(_band_kernel, tq=tq, window=window, wpad=wpad,
                             slopes=slopes, has_sink=sink is not None)
    in_specs = [
        pl.BlockSpec((None, tq, MIX), lambda bi, n: (bi, n, 0)),
        pl.BlockSpec((None, s, LANES), lambda bi, n: (bi, 0, 0)),
        pl.BlockSpec((None, s, LANES), lambda bi, n: (bi, 0, 0)),
    ]
    args = [q, k, v]
    if sink is not None:
        in_specs = [pl.BlockSpec(memory_space=pltpu.SMEM)] + in_specs
        args = [sink] + args
    return pl.pallas_call(
        kern,
        out_shape=jax.ShapeDtypeStruct((b, s, MIX), BF16),
        grid=(b, s // tq),
        in_specs=in_specs,
        out_specs=pl.BlockSpec((None, tq, MIX), lambda bi, n: (bi, n, 0)),
        scratch_shapes=[pltpu.VMEM((LANES, s), BF16)],
        compiler_params=_cparams("parallel", "arbitrary"),
        name="band_attention",
    )(*args)


def _stick_kernel(q_ref, k_ref, v_ref, o_ref, acc_ref, run_ref, *, tq, cpb):
    i = pl.program_id(2)
    q0 = pl.multiple_of(i * tq, tq)
    hms = _half_masks(BF16)
    heads = [(cc, p) for cc in range(cpb) for p in range(HEADS_PER_CHUNK)]
    qs = [q_ref[:, cc * LANES:(cc + 1) * LANES] * (hms[p] * SCALE) for cc, p in heads]
    upper = (_iota((2 * tq, tq), 0) % tq > _iota((2 * tq, tq), 1)).astype(BF16)

    def suffix_sum(x):
        hi, lo = _split(x)
        return _dot(jnp.concatenate([hi, lo], axis=1), upper)

    def block(kstart, diag):
        if diag:
            causal = _iota((tq, tq), 1) < _iota((tq, tq), 0)
        kjs = [k_ref[pl.ds(kstart, tq), cc * LANES:(cc + 1) * LANES] for cc in range(cpb)]
        vjs = [v_ref[pl.ds(kstart, tq), cc * LANES:(cc + 1) * LANES] for cc in range(cpb)]
        zs = [_dot_nt(qs[h], kjs[cc]) for h, (cc, p) in enumerate(heads)]
        lss, lks = [], []
        for z in zs:
            ls = jnp.minimum(z, 0.0) - jnp.log(1.0 + jnp.exp(-jnp.abs(z)))
            lk = ls - z
            if diag:
                lk = jnp.where(causal, lk, 0.0)
            lss.append(ls)
            lks.append(lk)
        sufs = [suffix_sum(lk) for lk in lks]
        ws = []
        for h in range(len(heads)):
            if diag:
                a = jnp.where(causal, jnp.exp(lss[h] + sufs[h]), 0.0)
            else:
                a = jnp.exp(lss[h] + sufs[h] + run_ref[h])
            ws.append(a.astype(BF16))
        pvs = [_dot(ws[h], vjs[cc]) for h, (cc, p) in enumerate(heads)]
        for h in range(len(heads)):
            rowsum = jnp.sum(lks[h], axis=-1, keepdims=True)
            if diag:
                acc_ref[h] = pvs[h]
                run_ref[h] = rowsum
            else:
                acc_ref[h] += pvs[h]
                run_ref[h] += rowsum

    block(q0, True)

    def weights_alive():
        run = run_ref[0]
        for h in range(1, len(heads)):
            run = jnp.maximum(run, run_ref[h])
        return jnp.max(run) > EXP_UNDERFLOW

    def body(carry):
        t, _ = carry
        block(pl.multiple_of((i - 1 - t) * tq, tq), False)
        return t + 1, weights_alive()

    lax.while_loop(lambda c: (c[0] < i) & c[1], body, (jnp.int32(0), weights_alive()))
    lane = _iota((1, LANES), 1)
    for cc in range(cpb):
        o = jnp.where(lane < HEAD_DIM, acc_ref[HEADS_PER_CHUNK * cc], acc_ref[HEADS_PER_CHUNK * cc + 1])
        o_ref[:, cc * LANES:(cc + 1) * LANES] = o.astype(o_ref.dtype)


def _stick_attention(q, k, v, tq=256, cpb=4):
    b, s, _ = q.shape
    wide = cpb * LANES
    n_heads = cpb * HEADS_PER_CHUNK
    spec_q = pl.BlockSpec((None, tq, wide), lambda bi, c, i: (bi, i, c))
    spec_kv = pl.BlockSpec((None, s, wide), lambda bi, c, i: (bi, 0, c))
    return pl.pallas_call(
        functools.partial(_stick_kernel, tq=tq, cpb=cpb),
        out_shape=jax.ShapeDtypeStruct((b, s, MIX), BF16),
        grid=(b, N_CHUNKS // cpb, s // tq),
        in_specs=[spec_q, spec_kv, spec_kv],
        out_specs=spec_q,
        scratch_shapes=[pltpu.VMEM((n_heads, tq, LANES), F32), pltpu.VMEM((n_heads, tq, 1), F32)],
        compiler_params=_cparams("parallel", "parallel", "parallel"),
        name="stick_breaking",
    )(q, k, v)


def _route(logits):
    lane = _iota(logits.shape, 1)
    lane_f = lane.astype(F32)
    ninf = -jnp.inf
    is_g = (lane >= N_EXPERTS) & (lane < N_EXPERTS + N_GROUPS)
    gmax = jnp.max(jnp.where(is_g, logits, ninf), axis=-1, keepdims=True)
    gidx = jnp.min(jnp.where(is_g & (logits == gmax), lane_f - N_EXPERTS, 1e9), axis=-1, keepdims=True)
    p_g = 1.0 / jnp.sum(jnp.where(is_g, jnp.exp(logits - gmax), 0.0), axis=-1, keepdims=True)
    in_grp = (lane < N_EXPERTS) & ((lane // EXPERTS_PER_GROUP).astype(F32) == gidx)
    le = jnp.where(in_grp, logits, ninf)
    m1 = jnp.max(le, axis=-1, keepdims=True)
    i1 = jnp.min(jnp.where(le == m1, lane_f, 1e9), axis=-1, keepdims=True)
    le2 = jnp.where(lane_f == i1, ninf, le)
    m2 = jnp.max(le2, axis=-1, keepdims=True)
    i2 = jnp.min(jnp.where(le2 == m2, lane_f, 1e9), axis=-1, keepdims=True)
    e2 = jnp.exp(m2 - m1)
    w1 = p_g / (1.0 + e2)
    w2 = p_g * e2 / (1.0 + e2)
    return jnp.where(lane_f == i1, w1, 0.0) + jnp.where(lane_f == i2, w2, 0.0)


def _gate_expand(branch):
    r = _iota((LANES, MIX), 0)
    col = _iota((LANES, MIX), 1)
    head = col // LANES + GROUP * ((col % LANES) // HEAD_DIM)
    return (r == 3 * head + branch).astype(BF16)


def _out_kernel(*refs, nsa):
    if nsa:
        (x_ref, o1_ref, ocmp_ref, oslc_ref, owin_ref, gates_ref, w1_ref, w2_ref,
         ng_ref, wr_ref, br_ref, x1_ref, h_ref, comb_ref) = refs
        g = gates_ref[...]
        o2 = (_dot_hilo(g, _gate_expand(0)) * ocmp_ref[...]
              + _dot_hilo(g, _gate_expand(1)) * oslc_ref[...]
              + _dot_hilo(g, _gate_expand(2)) * owin_ref[...]).astype(BF16)
    else:
        (x_ref, o1_ref, o2_ref, w1_ref, w2_ref,
         ng_ref, wr_ref, br_ref, x1_ref, h_ref, comb_ref) = refs
        o2 = o2_ref[...]
    x1 = x_ref[...] + _dot(o1_ref[...], w1_ref[...]) + _dot(o2, w2_ref[...])
    x1_ref[...] = x1
    ms = jnp.mean(x1 * x1, axis=-1, keepdims=True)
    h = x1 * lax.rsqrt(ms + EPS) * ng_ref[...]
    h_ref[...] = h.astype(BF16)
    h_hi, h_lo = _split(h)
    w_hi, w_lo = _split(wr_ref[...])
    both = _dot(h_hi, jnp.concatenate([w_hi, w_lo], axis=1))
    logits = both[:, :LANES] + (both[:, LANES:] + _dot(h_lo, w_hi)) + br_ref[...]
    comb_ref[...] = _route(logits)


def _out_proj_route(x2, attn, w1, w2, moe_g, w_route, b_route, tm=1024):
    t, d = x2.shape
    nsa = len(attn) > 2
    row = lambda width: pl.BlockSpec((tm, width), lambda i: (i, 0))
    full = lambda a: pl.BlockSpec(a.shape, lambda i: (0, 0))
    consts = [w1, w2, moe_g.reshape(1, d), w_route, b_route]
    return pl.pallas_call(
        functools.partial(_out_kernel, nsa=nsa),
        out_shape=[jax.ShapeDtypeStruct((t, d), F32), jax.ShapeDtypeStruct((t, d), BF16),
                   jax.ShapeDtypeStruct((t, LANES), F32)],
        grid=(t // tm,),
        in_specs=[row(d)] + [row(a.shape[1]) for a in attn] + [full(a) for a in consts],
        out_specs=[row(d), row(d), row(LANES)],
        compiler_params=_cparams("parallel"),
        name="out_proj_route",
    )(x2, *attn, *consts)


MOE_CHUNK = 256
MOE_FIRST_CHUNK = MOE_CHUNK + 32
SEG_ALIGN = 16


def _sorted_rows(tm):
    return -(-(tm + N_GROUPS * SEG_ALIGN) // LANES) * LANES


def _moe_sort(h_ref, comb_ref, hs_ref, cs_ref, pos_ref, acc_ref, seg_ref):
    tm = h_ref.shape[0]
    rows_s = hs_ref.shape[0]
    comb = comb_ref[...]
    used = jnp.where(comb > 0, 1.0, 0.0).astype(BF16)
    of_group = ((_iota((LANES, LANES), 0) // EXPERTS_PER_GROUP == _iota((LANES, LANES), 1))
                & (_iota((LANES, LANES), 0) < N_EXPERTS)).astype(BF16)
    member = jnp.where(_dot(used, of_group) > 0.5, 1.0, 0.0)
    member_t = member.T
    earlier = (_iota((tm, tm), 0) < _iota((tm, tm), 1)).astype(BF16)
    rank_t = _dot(member_t.astype(BF16), earlier)
    count = jnp.sum(member_t, axis=1, keepdims=True)
    counts = [count[g:g + 1, :] for g in range(N_GROUPS)]
    starts = [jnp.zeros((1, 1), F32)]
    for g in range(1, N_GROUPS):
        padded = jnp.floor((counts[g - 1] + (SEG_ALIGN - 1)) * (1.0 / SEG_ALIGN)) * SEG_ALIGN
        starts.append(starts[g - 1] + padded)
    grow = _iota((LANES, 1), 0)
    start_col = jnp.zeros((LANES, 1), F32)
    for g in range(1, N_GROUPS):
        start_col = jnp.where(grow == g, starts[g], start_col)
    pos_t = jnp.sum(member_t * (start_col + rank_t), axis=0, keepdims=True)
    used_rows = _sorted_rows(tm)
    place = jnp.where(_iota((used_rows, tm), 0).astype(F32) == pos_t, 1.0, 0.0).astype(BF16)
    hs_ref[:used_rows] = _dot(place, h_ref[...]).astype(BF16)
    hs_ref[used_rows:] = jnp.zeros((rows_s - used_rows, hs_ref.shape[1]), BF16)
    c_hi = comb.astype(BF16)
    rest = comb - c_hi.astype(F32)
    c_mid = rest.astype(BF16)
    c_lo = (rest - c_mid.astype(F32)).astype(BF16)
    cs_ref[:used_rows] = _dot(place, c_hi) + (_dot(place, c_mid) + _dot(place, c_lo))
    cs_ref[used_rows:] = jnp.zeros((rows_s - used_rows, LANES), F32)
    pos_ref[...] = jnp.broadcast_to(pos_t, (LANES, tm)).T
    acc_ref[...] = jnp.zeros(acc_ref.shape, F32)
    for g in range(N_GROUPS):
        seg_ref[g] = starts[g][0, 0].astype(jnp.int32)
        seg_ref[N_GROUPS + g] = counts[g][0, 0].astype(jnp.int32)


def _moe_kernel(h_ref, comb_ref, x1_ref, wg_ref, wu_ref, wd_ref, o_ref,
                hs_ref, cs_ref, pos_ref, acc_ref, seg_ref, *, per_step):
    step = pl.program_id(1)
    tm = h_ref.shape[0]
    rows_s = hs_ref.shape[0]

    @pl.when(step == 0)
    def _():
        _moe_sort(h_ref, comb_ref, hs_ref, cs_ref, pos_ref, acc_ref, seg_ref)

    first = step * per_step
    group = first // EXPERTS_PER_GROUP
    start = seg_ref[group]
    count = seg_ref[N_GROUPS + group]
    sizes = [MOE_FIRST_CHUNK, MOE_CHUNK // 2, MOE_CHUNK // 2] + [MOE_CHUNK] * (tm // MOE_CHUNK - 2)
    assert sum(sizes) >= tm and max(sizes) <= MOE_FIRST_CHUNK
    begin = 0
    for size in sizes:
        @pl.when(begin < count)
        def _(begin=begin, size=size):
            off = pl.multiple_of(start + begin, SEG_ALIGN)
            hs = hs_ref[pl.ds(off, size), :]
            cs = cs_ref[pl.ds(off, size), :]
            lane = _iota((size, LANES), 1)
            acts = []
            for j in range(per_step):
                g = _dot(hs, wg_ref[j])
                u = _dot(hs, wu_ref[j])
                c = jnp.sum(jnp.where(lane == first + j, cs, 0.0), axis=-1, keepdims=True)
                acts.append((c * (g * jax.nn.sigmoid(g) * u)).astype(BF16))
            w_down = wd_ref[...].reshape(per_step * wd_ref.shape[1], wd_ref.shape[2])
            acc_ref[pl.ds(off, size), :] += _dot(jnp.concatenate(acts, axis=1), w_down)
        begin += size

    @pl.when(step == pl.num_programs(1) - 1)
    def _():
        pos = pos_ref[...]
        used_rows = _sorted_rows(tm)
        back = jnp.concatenate(
            [jnp.where(pos == (_iota((tm, LANES), 1) + blk * LANES).astype(F32), 1.0, 0.0).astype(BF16)
             for blk in range(used_rows // LANES)], axis=1)
        o_ref[...] = x1_ref[...] + _dot(back, acc_ref[:used_rows].astype(BF16))


def _moe(h, comb, x1, w_gate, w_up, w_down, layer, tm=1024, per_step=4):
    t, d = h.shape
    _, n_exp, _, ff = w_gate.shape
    assert t % tm == 0 and EXPERTS_PER_GROUP % per_step == 0 and tm % MOE_CHUNK == 0
    rows_s = _sorted_rows(tm) + -(-MOE_FIRST_CHUNK // LANES) * LANES
    return pl.pallas_call(
        functools.partial(_moe_kernel, per_step=per_step),
        out_shape=jax.ShapeDtypeStruct((t, d), F32),
        grid=(t // tm, n_exp // per_step),
        scratch_shapes=[pltpu.VMEM((rows_s, d), BF16), pltpu.VMEM((rows_s, LANES), F32),
                        pltpu.VMEM((tm, LANES), F32), pltpu.VMEM((rows_s, d), F32),
                        pltpu.SMEM((2 * N_GROUPS,), jnp.int32)],
        in_specs=[
            pl.BlockSpec((tm, d), lambda i, e: (i, 0)),
            pl.BlockSpec((tm, LANES), lambda i, e: (i, 0)),
            pl.BlockSpec((tm, d), lambda i, e: (i, 0)),
            pl.BlockSpec((None, per_step, d, ff), lambda i, e: (layer, e, 0, 0)),
            pl.BlockSpec((None, per_step, d, ff), lambda i, e: (layer, e, 0, 0)),
            pl.BlockSpec((None, per_step, ff, d), lambda i, e: (layer, e, 0, 0)),
        ],
        out_specs=pl.BlockSpec((tm, d), lambda i, e: (i, 0)),
        compiler_params=_cparams("parallel", "arbitrary", vmem=MOE_VMEM_LIMIT),
        name="moe_experts",
    )(h, comb, x1, w_gate, w_up, w_down)


def _moe_block(x2, attn, w1, w2, moe_g, w_grp, b_grp, w_exp, b_exp, w_gate, w_up, w_down, layer):
    d = x2.shape[1]
    pad = LANES - N_EXPERTS - N_GROUPS
    w_route = jnp.concatenate([w_exp, w_grp, jnp.zeros((d, pad), F32)], axis=1)
    b_route = jnp.concatenate([b_exp, b_grp, jnp.zeros((pad,), F32)]).reshape(1, LANES)
    x1, h, comb = _out_proj_route(x2, attn, w1, w2, moe_g, w_route, b_route)
    return _moe(h, comb, x1, w_gate, w_up, w_down, layer)


def _even_attn(x2, b, s, norm_g, w_in, q_norm, k_norm, sink, w_out):
    w = jnp.concatenate([_gqa_cols(w_in[:, :MIX]), w_in[:, MIX:]], axis=1).astype(BF16)
    n = w.shape[1]
    kvw = N_KV * HEAD_DIM
    gain = jnp.concatenate([jnp.tile(q_norm, N_HEADS), jnp.tile(k_norm, N_KV),
                            jnp.ones((n - MIX - kvw,), F32)]).reshape(1, n)
    plan, col = [], 0
    for width, op in ((MIX, "norm"), (kvw, "norm"), (kvw, None), (MIX, None), (MIX, None), (MIX, None)):
        plan.append((col, width, op))
        col += width
    qa, ka, va, qb, kb, vb = _proj(x2, norm_g, w, gain, tuple(plan), [BF16] * 6)
    r3 = lambda a: a.reshape(b, s, a.shape[-1])
    slopes = _alibi(N_HEADS)
    oa = _band_attention(r3(qa), r3(ka), r3(va), window=A_WINDOW,
                         slopes=[slopes[h] for h in GQA_PERM], sink=sink[np.asarray(GQA_PERM)])
    ob = _stick_attention(r3(qb), r3(kb), r3(vb))
    w1 = _gqa_rows(w_out[:MIX]).astype(BF16)
    w2 = w_out[MIX:].astype(BF16)
    t = b * s
    return [oa.reshape(t, MIX), ob.reshape(t, MIX)], w1, w2


def _rank_rows(score, j, n, seg):
    row = _iota(score.shape, 0)
    rank = jnp.zeros(score.shape, F32)
    for jj in range(n):
        other = score[jj:jj + 1, :]
        for sgm in range(1, score.shape[0] // seg):
            other = jnp.where(row // seg == sgm, score[sgm * seg + jj:sgm * seg + jj + 1, :], other)
        beats = (other > score) | ((other == score) & (jj < j))
        rank = rank + jnp.where(beats, 1.0, 0.0)
    return rank


def _moba_kernel(slope_ref, q_ref, k_ref, v_ref, o_ref, kmean_ref, vt_ref, m_ref, acc_ref, *,
                 blk, nblk, topk, cpb):
    g = pl.program_id(1)
    i = pl.program_id(2)
    s_len = k_ref.shape[0]
    heads = [(cc, p) for cc in range(cpb) for p in range(HEADS_PER_CHUNK)]
    nh = len(heads)
    chunk = lambda cc: slice(cc * LANES, (cc + 1) * LANES)

    @pl.when(i == 0)
    def _():
        member = (_iota((LANES, s_len), 1) // blk == _iota((LANES, s_len), 0)).astype(BF16)
        for cc in range(cpb):
            kmean_ref[cc] = _dot(member, k_ref[:, chunk(cc)]) * (1.0 / blk)
            vt_ref[cc] = v_ref[:, chunk(cc)].astype(F32).T.astype(BF16)

    hms = _half_masks(BF16)
    q0 = pl.multiple_of(i * blk, blk)
    rel = (_iota((blk, blk), 0) - _iota((blk, blk), 1)).astype(F32)
    slopes = [slope_ref[HEADS_PER_CHUNK * (g * cpb + cc) + p] for cc, p in heads]
    kms = [_split(kmean_ref[cc]) for cc in range(cpb)]
    gates = [_dot_nt(kms[cc][0], q_ref[:, chunk(cc)] * hms[p]) + _dot_nt(kms[cc][1], q_ref[:, chunk(cc)] * hms[p])
             for cc, p in heads]
    lane = _iota((blk, LANES), 1)
    off_f = _iota((blk, LANES), 0).astype(F32)
    spare = lambda p, n: (1 - p) * HEAD_DIM + n

    def with_key_lanes(kj, p, shift):
        lane = _iota((blk, LANES), 1)
        off_b = _iota((blk, LANES), 0).astype(F32).astype(BF16)
        return jnp.where(lane == spare(p, 0), 1.0,
                         jnp.where(lane == spare(p, 1), off_b,
                                   jnp.where(lane == spare(p, 2), shift.astype(BF16), kj)))

    qs = []
    for h, (cc, p) in enumerate(heads):
        extra = jnp.where(lane == spare(p, 0), -slopes[h] * off_f,
                          jnp.where((lane == spare(p, 1)) | (lane == spare(p, 2)), slopes[h], 0.0))
        qs.append(q_ref[:, chunk(cc)] * (hms[p] * SCALE) + extra.astype(BF16))
    zero = jnp.zeros((), F32)
    raw = [_dot_nt(with_key_lanes(k_ref[pl.ds(q0, blk), chunk(cc)], p, zero), qs[h])
           for h, (cc, p) in enumerate(heads)]
    nrow = -(-nblk // 8) * 8
    blk_id = _iota((nrow, blk), 0)
    skips = []
    for h in range(nh):
        gate = jnp.where(blk_id < i, gates[h][:nrow], NEG)
        rank = _rank_rows(gate, blk_id, nblk, nrow)
        skip_t = jnp.where((rank < topk) & (blk_id < i), 0.0, 1.0)
        skip_t = jnp.concatenate([skip_t, jnp.ones((LANES - nrow, blk), F32)], axis=0)
        skips.append(skip_t.T.astype(BF16))
    own_bias = jnp.where(rel <= 0, 0.0, NEG)
    vrow = _iota((LANES, blk), 0)

    def values_t(k0, cc, p):
        vt = vt_ref[cc, :, pl.ds(k0, blk)]
        return jnp.where(vrow // HEAD_DIM == p, vt, jnp.ones_like(vt))

    es = []
    for h in range(nh):
        s = raw[h] + own_bias
        m = jnp.max(s, axis=0, keepdims=True)
        m_ref[h] = m
        es.append(jnp.exp(s - m).astype(BF16))
    for h, (cc, p) in enumerate(heads):
        acc_ref[h] = _dot(values_t(q0, cc, p), es[h])

    def body(j, carry):
        k0 = pl.multiple_of(j * blk, blk)
        shift = ((j - i) * blk).astype(F32)
        to_lane = [((_iota((LANES, LANES), 0) == j) & (_iota((LANES, LANES), 1) == spare(p, 0))).astype(BF16)
                   for p in range(HEADS_PER_CHUNK)]
        skipped = [_dot(skips[h], to_lane[p]) for h, (cc, p) in enumerate(heads)]
        raw = [_dot_nt(with_key_lanes(k_ref[pl.ds(k0, blk), chunk(cc)], p, shift),
                       qs[h] + (skipped[h] * NEG).astype(BF16))
               for h, (cc, p) in enumerate(heads)]
        es, alphas = [], []
        for h in range(nh):
            s = raw[h]
            m_old = m_ref[h]
            m_new = jnp.maximum(m_old, jnp.max(s, axis=0, keepdims=True))
            m_ref[h] = m_new
            es.append(jnp.exp(s - m_new).astype(BF16))
            alphas.append(jnp.exp(m_old - m_new))
        pvs = [_dot(values_t(k0, cc, p), es[h]) for h, (cc, p) in enumerate(heads)]
        for h in range(nh):
            acc_ref[h] = alphas[h] * acc_ref[h] + pvs[h]
        return carry

    lax.fori_loop(0, i, body, 0)
    lane = _iota((1, LANES), 1)
    for cc in range(cpb):
        outs = []
        for p in range(HEADS_PER_CHUNK):
            acc = acc_ref[HEADS_PER_CHUNK * cc + p]
            den = acc[(1 - p) * HEAD_DIM:(1 - p) * HEAD_DIM + 1, :]
            outs.append((acc / den).T)
        o_ref[:, chunk(cc)] = jnp.where(lane < HEAD_DIM, outs[0], outs[1]).astype(o_ref.dtype)


def _moba_attention(q, k, v, slopes, cpb=4):
    b, s, _ = q.shape
    assert s % C_BLOCK == 0 and C_BLOCK <= 256 and all(_is_pow2(x) for x in slopes)
    nblk = s // C_BLOCK
    assert nblk <= LANES
    wide = cpb * LANES
    nh = cpb * HEADS_PER_CHUNK
    spec_q = pl.BlockSpec((None, C_BLOCK, wide), lambda bi, c, i, sl: (bi, i, c))
    spec_kv = pl.BlockSpec((None, s, wide), lambda bi, c, i, sl: (bi, 0, c))
    return pl.pallas_call(
        functools.partial(_moba_kernel, blk=C_BLOCK, nblk=nblk, topk=min(C_TOPK, nblk), cpb=cpb),
        out_shape=jax.ShapeDtypeStruct((b, s, MIX), BF16),
        grid_spec=pltpu.PrefetchScalarGridSpec(
            num_scalar_prefetch=1,
            grid=(b, N_CHUNKS // cpb, nblk),
            in_specs=[spec_q, spec_kv, spec_kv],
            out_specs=spec_q,
            scratch_shapes=[pltpu.VMEM((cpb, LANES, LANES), F32), pltpu.VMEM((cpb, LANES, s), BF16),
                            pltpu.VMEM((nh, 1, C_BLOCK), F32), pltpu.VMEM((nh, LANES, C_BLOCK), F32)],
        ),
        compiler_params=_cparams("parallel", "parallel", "arbitrary"),
        name="moba_attention",
    )(jnp.asarray(slopes, F32), q, k, v)


def _compress_kernel(xk_ref, xv_ref, wk_lo_ref, wk_hi_ref, wv_lo_ref, wv_hi_ref,
                     pk_lo_ref, pk_hi_ref, pv_lo_ref, pv_hi_ref, gain_ref, kc_ref, vc_ref):
    def compress(x_ref, w_lo_ref, w_hi_ref, p_lo_ref, p_hi_ref):
        x = x_ref[...]
        nrow = x.shape[0]
        first = _dot(x, w_lo_ref[...])
        second = pltpu.roll(_dot(x, w_hi_ref[...]), nrow - 1, 0)
        p_lo = jnp.broadcast_to(p_lo_ref[...], (8, p_lo_ref.shape[1]))
        p_hi = jnp.broadcast_to(p_hi_ref[...], (8, p_hi_ref.shape[1]))
        bias = _dot_hilo(p_lo, w_lo_ref[...]) + _dot_hilo(p_hi, w_hi_ref[...])
        return first + second + bias[0:1]

    kc = compress(xk_ref, wk_lo_ref, wk_hi_ref, pk_lo_ref, pk_hi_ref)
    kc = kc * lax.rsqrt(_head_mean_sq(kc) + EPS) * gain_ref[...]
    kc_ref[...] = kc.astype(kc_ref.dtype)
    vc_ref[...] = compress(xv_ref, wv_lo_ref, wv_hi_ref, pv_lo_ref, pv_hi_ref).astype(vc_ref.dtype)


def _compress_weights(pos, w):
    half = CMP_LEN // 2
    eye = jnp.eye(N_KV, dtype=F32)
    wd = jnp.einsum("gh,lde->lgdhe", eye, w).reshape(CMP_LEN, LANES, LANES)
    w_lo = wd[:half].reshape(half * LANES, LANES).astype(BF16)
    w_hi = wd[half:].reshape(half * LANES, LANES).astype(BF16)
    pt = jnp.tile(pos, (1, N_KV))
    return w_lo, w_hi, pt[:half].reshape(1, half * LANES), pt[half:].reshape(1, half * LANES)


def _compress(kcmp, vcmp, cmp_pos, cmp_w, k_gain):
    b, s, _ = kcmp.shape
    assert CMP_LEN == 2 * CMP_STRIDE and s % CMP_STRIDE == 0
    nrow = s // CMP_STRIDE
    wide = CMP_STRIDE * LANES
    xk = kcmp.reshape(b, nrow, wide)
    xv = vcmp.reshape(b, nrow, wide)
    wk = _compress_weights(cmp_pos[0], cmp_w[0])
    wv = _compress_weights(cmp_pos[1], cmp_w[1])
    consts = [wk[0], wk[1], wv[0], wv[1], wk[2], wk[3], wv[2], wv[3],
              jnp.tile(k_gain, N_KV).reshape(1, LANES)]
    spec_x = pl.BlockSpec((None, nrow, wide), lambda bi: (bi, 0, 0))
    spec_o = pl.BlockSpec((None, nrow, LANES), lambda bi: (bi, 0, 0))
    return pl.pallas_call(
        _compress_kernel,
        out_shape=[jax.ShapeDtypeStruct((b, nrow, LANES), BF16)] * 2,
        grid=(b,),
        in_specs=[spec_x, spec_x] + [pl.BlockSpec(a.shape, lambda bi: (0, 0)) for a in consts],
        out_specs=[spec_o, spec_o],
        compiler_params=_cparams("parallel"),
        name="nsa_compress",
    )(xk, xv, *consts)


def _nsa_cmp_kernel(q_ref, kc_ref, vc_ref, o_ref, sel_ref, *, tq, n_cmp, n_slc, topn, slopes):
    q0 = pl.program_id(1) * tq
    kc = kc_ref[...]
    vc = vc_ref[...]
    ncp = kc.shape[0]
    rows = N_CHUNKS * tq
    seg = LANES // N_KV
    qpos = q0 + _iota((rows, ncp), 0) % tq
    ncol = _iota((rows, ncp), 1)
    diff = qpos - (ncol * CMP_STRIDE + CMP_LEN - 1)
    mask = (diff >= 0) & (ncol < n_cmp)
    diff_f = diff.astype(F32)
    orow = _iota((LANES, ncp), 0)
    cst = _iota((LANES, ncp), 1) * CMP_STRIDE
    sst = (orow % seg) * SLC_BLOCK
    overlap = (cst < sst + SLC_BLOCK) & (cst + CMP_LEN > sst) & (orow % seg < n_slc)
    halves = range(HEADS_PER_CHUNK)
    hms = _half_masks(BF16)
    raw = [_dot_nt(_stack_group(q_ref, hms[p] * SCALE), kc) for p in halves]
    pcs = []
    for p in halves:
        slope = _per_chunk_rows(rows, tq, [slopes[2 * c + p] for c in range(N_CHUNKS)])
        sc = jnp.where(mask, raw[p] - slope * diff_f, NEG)
        mx = jnp.max(sc, axis=-1, keepdims=True)
        e = jnp.where(mask, jnp.exp(sc - mx), 0.0)
        den = jnp.sum(e, axis=-1, keepdims=True)
        pcs.append(e / jnp.where(den > 0, den, 1.0))
    _store_group(o_ref, [_dot(pcs[p].astype(BF16), vc) for p in halves], tq)
    p_slc = jnp.zeros((LANES, tq), F32)
    for p in halves:
        pg = pcs[p][0:tq]
        for c in range(1, N_CHUNKS):
            pg = pg + pcs[p][c * tq:(c + 1) * tq]
        ov = (overlap & (orow // seg == p)).astype(BF16)
        pg_hi, pg_lo = _split(pg)
        p_slc = p_slc + (_dot_nt(ov, pg_hi) + _dot_nt(ov, pg_lo))
    j = _iota((LANES, tq), 0) % seg
    cur = (q0 + _iota((LANES, tq), 1)) // SLC_BLOCK
    forced = (j == 0) | (j == cur) | (j == cur - 1)
    usable = (j <= cur) & (j < n_slc)
    score = jnp.where(usable, p_slc + jnp.where(forced, FORCE_BONUS, 0.0), NEG)
    rank = _rank_rows(score, j, n_slc, seg)
    sel_t = jnp.where((rank < topn) & usable, 1.0, 0.0)
    sel_ref[...] = sel_t.T.astype(sel_ref.dtype)


def _nsa_cmp(q, kc, vc, slopes, tq=256):
    b, s, _ = q.shape
    ncp = kc.shape[1]
    n_slc = s // SLC_BLOCK
    assert n_slc <= LANES // N_KV and ncp % LANES == 0
    kern = functools.partial(_nsa_cmp_kernel, tq=tq, n_cmp=ncp - 1, n_slc=n_slc,
                             topn=min(SLC_TOPN, n_slc), slopes=slopes)
    spec_c = pl.BlockSpec((None, ncp, LANES), lambda bi, n: (bi, 0, 0))
    return pl.pallas_call(
        kern,
        out_shape=[jax.ShapeDtypeStruct((b, s, MIX), BF16), jax.ShapeDtypeStruct((b, s, LANES), BF16)],
        grid=(b, s // tq),
        in_specs=[pl.BlockSpec((None, tq, MIX), lambda bi, n: (bi, n, 0)), spec_c, spec_c],
        out_specs=[pl.BlockSpec((None, tq, MIX), lambda bi, n: (bi, n, 0)),
                   pl.BlockSpec((None, tq, LANES), lambda bi, n: (bi, n, 0))],
        compiler_params=_cparams("parallel", "parallel"),
        name="nsa_compressed",
    )(q, kc, vc)


def _nsa_slc_kernel(q_ref, k_ref, v_ref, sel_ref, o_ref, vt_ref, m_ref, acc_ref, *, tq, tk, slopes):
    n = pl.program_id(1)
    q0 = n * tq
    rows = N_CHUNKS * tq
    seg = LANES // N_KV
    halves = range(HEADS_PER_CHUNK)
    skip = 1.0 - sel_ref[...]
    nsub = tk // SLC_BLOCK
    lane_q = _iota((tq, LANES), 1)
    row_q = _iota((tq, LANES), 0).astype(F32)
    spare = lambda p, n: (1 - p) * HEAD_DIM + n
    hms = _half_masks(BF16)
    qs = []
    for p in halves:
        extra = jnp.concatenate(
            [jnp.where(lane_q == spare(p, 0), -slopes[2 * c + p] * row_q,
                       jnp.where((lane_q == spare(p, 1)) | (lane_q == spare(p, 2)), slopes[2 * c + p], 0.0))
             for c in range(N_CHUNKS)], axis=0)
        qs.append(_stack_group(q_ref, hms[p] * SCALE) + extra.astype(BF16))
    @pl.when(n == 0)
    def _():
        vt_ref[...] = v_ref[...].astype(F32).T.astype(BF16)

    for p in halves:
        m_ref[p] = jnp.full((1, rows), 0.1 * NEG, F32)
        acc_ref[p] = jnp.zeros((LANES, rows), F32)

    n_grp = 4
    grp = rows // n_grp
    lanes = lambda g: slice(g * grp, (g + 1) * grp)
    chains = [(p, g) for p in halves for g in range(n_grp)]

    def tile(kt, diagonal):
        k0 = pl.multiple_of(kt * tk, tk)
        kj = k_ref[pl.ds(k0, tk), :]
        vt = vt_ref[:, pl.ds(k0, tk)]
        shift = (k0 - q0).astype(F32).astype(BF16)
        lane_k = _iota((tk, LANES), 1)
        off_k = _iota((tk, LANES), 0)
        raw = []
        for p in halves:
            in_sub = (lane_k >= spare(p, 3)) & (lane_k < spare(p, 3) + nsub)
            kp = jnp.where(lane_k == spare(p, 0), 1.0,
                           jnp.where(lane_k == spare(p, 1), off_k.astype(F32).astype(BF16),
                                     jnp.where(lane_k == spare(p, 2), shift,
                                               jnp.where(in_sub, jnp.where(off_k // SLC_BLOCK == lane_k - spare(p, 3),
                                                                           1.0, 0.0).astype(BF16), kj))))
            src = _iota((LANES, LANES), 0) - (p * seg + k0 // SLC_BLOCK)
            dst = _iota((LANES, LANES), 1) - spare(p, 3)
            to_lane = ((src == dst) & (dst >= 0) & (dst < nsub)).astype(BF16)
            bias = (_dot(skip, to_lane) * NEG).astype(BF16)
            qp = qs[p] + jnp.concatenate([bias] * N_CHUNKS, axis=0)
            raw += [_dot_nt(kp, qp[lanes(g)]) for g in range(n_grp)]
        if diagonal:
            future = jnp.where(_iota((tk, tq), 1) + (q0 - k0) >= _iota((tk, tq), 0), 0.0, NEG)
            future = jnp.concatenate([future] * (grp // tq), axis=1)
        es, alphas = [], []
        for p, g in chains:
            s = raw[p * n_grp + g] + future if diagonal else raw[p * n_grp + g]
            m_old = m_ref[p, :, lanes(g)]
            m_new = jnp.maximum(m_old, jnp.max(s, axis=0, keepdims=True))
            m_ref[p, :, lanes(g)] = m_new
            es.append(jnp.exp(s - m_new).astype(BF16))
            alphas.append(jnp.exp(m_old - m_new))
        row = _iota((LANES, tk), 0)
        vts = [jnp.where(row // HEAD_DIM == p, vt, jnp.ones_like(vt)) for p in halves]
        pvs = [_dot(vts[p], es[p * n_grp + g]) for p, g in chains]
        for (p, g), alpha, pv in zip(chains, alphas, pvs):
            acc_ref[p, :, lanes(g)] = alpha * acc_ref[p, :, lanes(g)] + pv

    def body(kt, carry):
        tile(kt, False)
        return carry

    n_past = q0 // tk
    lax.fori_loop(0, n_past, body, 0)
    tile(n_past, True)
    outs = []
    for p in halves:
        acc = acc_ref[p]
        den = acc[(1 - p) * HEAD_DIM:(1 - p) * HEAD_DIM + 1, :]
        outs.append((acc / den).T)
    _store_group(o_ref, outs, tq)


def _nsa_slc(q, k, v, sel, slopes, tq=256, tk=256):
    b, s, _ = q.shape
    assert tk % tq == 0 and tq % SLC_BLOCK == 0 and s % tk == 0 and tk <= 256
    assert all(_is_pow2(x) for x in slopes)
    rows = N_CHUNKS * tq
    spec_q = pl.BlockSpec((None, tq, MIX), lambda bi, n: (bi, n, 0))
    spec_kv = pl.BlockSpec((None, s, LANES), lambda bi, n: (bi, 0, 0))
    return pl.pallas_call(
        functools.partial(_nsa_slc_kernel, tq=tq, tk=tk, slopes=slopes),
        out_shape=jax.ShapeDtypeStruct((b, s, MIX), BF16),
        grid=(b, s // tq),
        in_specs=[spec_q, spec_kv, spec_kv, pl.BlockSpec((None, tq, LANES), lambda bi, n: (bi, n, 0))],
        out_specs=spec_q,
        scratch_shapes=[pltpu.VMEM((LANES, s), BF16), pltpu.VMEM((N_KV, 1, rows), F32),
                        pltpu.VMEM((N_KV, LANES, rows), F32)],
        compiler_params=_cparams("parallel", "arbitrary"),
        name="nsa_selected",
    )(q, k, v, sel)


def _odd_attn(x2, b, s, norm_g, w_in, c_q_norm, c_k_norm, d_q_norm, d_k_norm, cmp_pos, cmp_w, w_out):
    kvw = N_KV * HEAD_DIM
    n_gate = N_HEADS * 3
    qd0 = 3 * MIX
    w = jnp.concatenate([w_in[:, :qd0], _gqa_cols(w_in[:, qd0:qd0 + MIX]), w_in[:, qd0 + MIX:],
                         jnp.zeros((w_in.shape[0], LANES - n_gate), F32)], axis=1).astype(BF16)
    n = w.shape[1]
    ones = lambda width: jnp.ones((width,), F32)
    gain = jnp.concatenate([
        jnp.tile(c_q_norm, N_HEADS), jnp.tile(c_k_norm, N_HEADS), ones(MIX), jnp.tile(d_q_norm, N_HEADS),
        ones(2 * kvw), jnp.tile(d_k_norm[1], N_KV), ones(kvw), jnp.tile(d_k_norm[2], N_KV), ones(kvw),
        ones(LANES)]).reshape(1, n)
    plan, col = [], 0
    for width, op in ((MIX, "norm"), (MIX, "norm"), (MIX, None), (MIX, "norm"), (kvw, None), (kvw, None),
                      (kvw, "norm"), (kvw, None), (kvw, "norm"), (kvw, None), (LANES, "sigmoid")):
        plan.append((col, width, op))
        col += width
    outs = _proj(x2, norm_g, w, gain, tuple(plan), [BF16] * 10 + [F32])
    r3 = lambda a: a.reshape(b, s, a.shape[-1])
    qc, kc, vc, qd, kcmp, vcmp, kslc, vslc, kwin, vwin = [r3(a) for a in outs[:10]]
    gates = outs[10]
    slopes = _alibi(N_HEADS)
    gslopes = [slopes[h] for h in GQA_PERM]
    oc = _moba_attention(qc, kc, vc, slopes)
    k_cmp, v_cmp = _compress(kcmp, vcmp, cmp_pos, cmp_w, d_k_norm[0])
    o_cmp, sel = _nsa_cmp(qd, k_cmp, v_cmp, gslopes)
    o_slc = _nsa_slc(qd, kslc, vslc, sel, gslopes)
    o_win = _band_attention(qd, kwin, vwin, window=D_WINDOW, slopes=gslopes)
    t = b * s
    w1 = w_out[:MIX].astype(BF16)
    w2 = _gqa_rows(w_out[MIX:]).astype(BF16)
    flat = lambda a: a.reshape(t, a.shape[-1])
    return [flat(oc), flat(o_cmp), flat(o_slc), flat(o_win), gates], w1, w2


def kernel(x, ev_norm, ev_w_in, ev_q_norm, ev_k_norm, ev_sink, ev_w_out, od_norm, od_w_in, od_c_q_norm,
           od_c_k_norm, od_d_q_norm, od_d_k_norm, od_cmp_pos, od_cmp_w, od_w_out, moe_norm, moe_w_grp,
           moe_b_grp, moe_w_exp, moe_b_exp, moe_w_gate, moe_w_up, moe_w_down):
    b, s, d = x.shape
    x2 = x.reshape(b * s, d)
    depth = moe_norm.shape[0]
    moe_w_gate, moe_w_up, moe_w_down = (w.astype(BF16) for w in (moe_w_gate, moe_w_up, moe_w_down))
    for layer in range(depth):
        i = layer // 2
        if layer % 2 == 0:
            attn, w1, w2 = _even_attn(x2, b, s, ev_norm[i], ev_w_in[i], ev_q_norm[i], ev_k_norm[i],
                                      ev_sink[i], ev_w_out[i])
        else:
            attn, w1, w2 = _odd_attn(x2, b, s, od_norm[i], od_w_in[i], od_c_q_norm[i], od_c_k_norm[i],
                                     od_d_q_norm[i], od_d_k_norm[i], od_cmp_pos[i], od_cmp_w[i], od_w_out[i])
        x2 = _moe_block(x2, attn, w1, w2, moe_norm[layer], moe_w_grp[layer], moe_b_grp[layer],
                        moe_w_exp[layer], moe_b_exp[layer], moe_w_gate, moe_w_up, moe_w_down, layer)
    return x2.reshape(b, s, d)
```
